```python
import math
import jax, jax.numpy as jnp
from jax import lax
import numpy as np

D_MODEL = 2048
BATCH = 4
SEQ = 8192
DEPTH = 2
DEC_BATCH = 32
DEC_SEQ = 32
PAST_LEN = 4096

CHUNK = 64
QBLK = 128
EPS = 1e-6
ROPE_THETA = 500000.0
HEAD_DIM = 128
MIX_WIDTH = D_MODEL // 2
N_BRANCH = 3
A_HEADS = MIX_WIDTH // HEAD_DIM
A_KV = 2
A_ROT = HEAD_DIM // 4
IDX_HEADS = 8
IDX_DIM = 64
IDX_ROT = IDX_DIM // 4
TOPK_MAX = 256
B_HEADS = MIX_WIDTH // HEAD_DIM
BAND_CHUNKS = 8
REL_CLIP = 256
C_WIDTH = MIX_WIDTH
C_GROUP = 16
C_GROUPS = C_WIDTH // C_GROUP
C_STATE = 64
D_FF = ((8 * D_MODEL // 3 + 255) // 256) * 256
CONV_W = 3
ATTN_SCALE = HEAD_DIM ** -0.5
IDX_SCALE = (IDX_DIM * IDX_HEADS) ** -0.5
IN_SIZES = (A_HEADS * HEAD_DIM, A_KV * HEAD_DIM, A_KV * HEAD_DIM, IDX_HEADS * IDX_DIM, IDX_DIM, IDX_HEADS,
            B_HEADS * HEAD_DIM, B_HEADS * HEAD_DIM, B_HEADS * HEAD_DIM, C_WIDTH, N_BRANCH * D_MODEL)
N_IN = sum(IN_SIZES)

kernel_name = 'hybrid_streaming_encoder_step'


def rmsnorm(x, g):
    xf = x.astype(jnp.float32)
    y = xf * lax.rsqrt(jnp.mean(xf * xf, axis=-1, keepdims=True) + EPS)
    return (y * g.astype(jnp.float32)).astype(x.dtype)


def rope(x, pos, rot):
    half = rot // 2
    inv = jnp.float32(ROPE_THETA) ** (-jnp.arange(half, dtype=jnp.float32) / half)
    ang = pos.astype(jnp.float32)[:, None] * inv[None, :]
    cos = jnp.cos(ang)[:, None, :]
    sin = jnp.sin(ang)[:, None, :]
    xf = x.astype(jnp.float32)
    x1 = xf[..., :half]
    x2 = xf[..., half:rot]
    out = jnp.concatenate([x1 * cos - x2 * sin, x2 * cos + x1 * sin, xf[..., rot:]], axis=-1)
    return out.astype(x.dtype)


def project(h, p, pos):
    b, t, _ = h.shape
    offs = np.cumsum(IN_SIZES)[:-1].tolist()
    qa, ka, va, qi, ki, wi, qb, kb, vb, uc, g = jnp.split(h @ p['w_in'], offs, axis=-1)
    qa = rope(qa.reshape(b, t, A_HEADS, HEAD_DIM), pos, A_ROT)
    ka = rope(ka.reshape(b, t, A_KV, HEAD_DIM), pos, A_ROT)
    va = va.reshape(b, t, A_KV, HEAD_DIM)
    qi = rope(qi.reshape(b, t, IDX_HEADS, IDX_DIM), pos, IDX_ROT)
    ki = rope(ki.reshape(b, t, 1, IDX_DIM), pos, IDX_ROT)[:, :, 0]
    qb = qb.reshape(b, t, B_HEADS, HEAD_DIM)
    kb = kb.reshape(b, t, B_HEADS, HEAD_DIM)
    vb = vb.reshape(b, t, B_HEADS, HEAD_DIM)
    uc = uc.reshape(b, t, C_GROUPS, C_GROUP)
    g = g.reshape(b, t, N_BRANCH, D_MODEL)
    return qa, ka, va, qi, ki, wi, qb, kb, vb, uc, g


def dsa_attend(q, qi, wi, qpos, k, v, kidx, kpos, topk):
    b, tq = q.shape[:2]
    s = jnp.einsum('bthd,bsd->bths', qi.astype(jnp.float32), kidx.astype(jnp.float32))
    score = jnp.einsum('bth,bths->bts', wi.astype(jnp.float32), jax.nn.relu(s)) * IDX_SCALE
    vis = kpos[None, :] < ((qpos // CHUNK + 1) * CHUNK)[:, None]
    score = jnp.where(vis[None], score, -jnp.inf)
    top_val, top_idx = lax.top_k(score, topk)
    valid = jnp.isfinite(top_val)
    gather = jax.vmap(lambda rows, idx: rows[idx])
    k_sel = gather(k, top_idx)
    v_sel = gather(v, top_idx)
    qg = q.reshape(b, tq, A_KV, A_HEADS // A_KV, HEAD_DIM)
    logits = jnp.einsum('btngd,btsnd->btngs', qg, k_sel).astype(jnp.float32) * ATTN_SCALE
    logits = jnp.where(valid[:, :, None, None, :], logits, -jnp.inf)
    w = jax.nn.softmax(logits, axis=-1).astype(v.dtype)
    o = jnp.einsum('btngs,btsnd->btngd', w, v_sel)
    return o.reshape(b, tq, A_HEADS * HEAD_DIM)


def dsa_prompt(qa, qi, wi, ka, va, ki, pos):
    b, t = qa.shape[:2]
    nq = t // QBLK
    topk = min(TOPK_MAX, t // 4)

    def blocks(a):
        return a.reshape((b, nq, QBLK) + a.shape[2:]).swapaxes(0, 1)

    def one(args):
        q_blk, qi_blk, wi_blk, pos_blk = args
        return dsa_attend(q_blk, qi_blk, wi_blk, pos_blk, ka, va, ki, pos, topk)

    o = lax.map(one, (blocks(qa), blocks(qi), blocks(wi), pos.reshape(nq, QBLK)))
    return o.swapaxes(0, 1).reshape(b, t, A_HEADS * HEAD_DIM)


def rel_attend(q, k, v, qpos, kpos, table):
    rel = jnp.clip(qpos[:, None] - kpos[None, :], -REL_CLIP, REL_CLIP) + REL_CLIP
    bias = table[:, rel].astype(jnp.float32)
    qc = qpos[:, None] // CHUNK
    kc = kpos[None, :] // CHUNK
    vis = (kpos[None, :] >= 0) & (kc <= qc) & (kc >= qc - BAND_CHUNKS)
    logits = jnp.einsum('qhd,khd->hqk', q, k).astype(jnp.float32) * ATTN_SCALE + bias
    logits = jnp.where(vis[None], logits, -jnp.inf)
    w = jax.nn.softmax(logits, axis=-1).astype(v.dtype)
    return jnp.einsum('hqk,khd->qhd', w, v).reshape(q.shape[0], -1)


def band_prompt_one(q, k, v, pos, table):
    t, nh, dh = q.shape
    nc = t // CHUNK
    width = (BAND_CHUNKS + 1) * CHUNK

    def band(a):
        ac = a.reshape(nc, CHUNK, nh, dh)
        ap = jnp.pad(ac, ((BAND_CHUNKS, 0), (0, 0), (0, 0), (0, 0)))
        return jnp.stack([ap[j:j + nc] for j in range(BAND_CHUNKS + 1)], axis=1).reshape(nc, width, nh, dh)

    kpos = (jnp.arange(nc)[:, None] - BAND_CHUNKS) * CHUNK + jnp.arange(width)[None, :]
    out = jax.vmap(rel_attend, in_axes=(0, 0, 0, 0, 0, None))(
        q.reshape(nc, CHUNK, nh, dh), band(k), band(v), pos.reshape(nc, CHUNK), kpos, table)
    return out.reshape(t, nh * dh)


def s5_discretize(p):
    a_re = p['a_re'].astype(jnp.float32)
    a_im = p['a_im'].astype(jnp.float32)
    dt = jnp.exp(p['log_dt'].astype(jnp.float32))[:, None]
    mag = jnp.exp(dt * a_re)
    abar_re = mag * jnp.cos(dt * a_im)
    abar_im = mag * jnp.sin(dt * a_im)
    n_re = abar_re - 1.0
    n_im = abar_im
    den = a_re * a_re + a_im * a_im
    c_re = (n_re * a_re + n_im * a_im) / den
    c_im = (n_im * a_re - n_re * a_im) / den
    b_re = p['b_re'].astype(jnp.float32)
    b_im = p['b_im'].astype(jnp.float32)
    bbar_re = c_re[..., None] * b_re - c_im[..., None] * b_im
    bbar_im = c_re[..., None] * b_im + c_im[..., None] * b_re
    return abar_re, abar_im, bbar_re, bbar_im


def cplx_combine(e1, e2):
    a1r, a1i, b1r, b1i = e1
    a2r, a2i, b2r, b2i = e2
    return (a1r * a2r - a1i * a2i, a1r * a2i + a1i * a2r,
            a2r * b1r - a2i * b1i + b2r, a2r * b1i + a2i * b1r + b2i)


def s5_scan(u, h0_re, h0_im, disc, c_re, c_im, d):
    abar_re, abar_im, bbar_re, bbar_im = disc
    t = u.shape[0]
    bu_re = jnp.einsum('tgc,gpc->tgp', u, bbar_re)
    bu_im = jnp.einsum('tgc,gpc->tgp', u, bbar_im)
    a_re = jnp.broadcast_to(abar_re, (t,) + abar_re.shape)
    a_im = jnp.broadcast_to(abar_im, (t,) + abar_im.shape)
    cum_re, cum_im, h_re, h_im = lax.associative_scan(cplx_combine, (a_re, a_im, bu_re, bu_im), axis=0)
    hr = h_re + cum_re * h0_re - cum_im * h0_im
    hi = h_im + cum_re * h0_im + cum_im * h0_re
    y = jnp.einsum('tgp,gcp->tgc', hr, c_re) - jnp.einsum('tgp,gcp->tgc', hi, c_im) + d * u
    return y, hr[-1], hi[-1]


def s5_glu(y, p):
    z = jax.nn.gelu(y)
    return z * jax.nn.sigmoid(z @ p['w_glu'] + p['b_glu'])


def merge_branches(oa, ob, oc, g, p):
    o = jnp.stack([oa, ob, oc], axis=2)
    br = jnp.einsum('btnm,nmd->btnd', o, p['w_branch'])
    merged = jnp.sum(jax.nn.sigmoid(g) * br, axis=2)
    return merged @ p['w_out']


def conv_ffn(h, buf, p):
    u, v = jnp.split(h @ p['w_ffn_in'], 2, axis=-1)
    t = u.shape[1]
    up = jnp.concatenate([buf.astype(u.dtype), u], axis=1)
    c = p['b_conv']
    for j in range(CONV_W):
        c = c + up[:, j:j + t] * p['w_conv'][j]
    out = (jax.nn.gelu(c) * v) @ p['w_down']
    return out, up[:, t:]


def ssm_out_params(p):
    return p['c_re'].astype(jnp.float32), p['c_im'].astype(jnp.float32), p['d'].astype(jnp.float32)


def layer_prompt(x, pos, p):
    b, t, _ = x.shape
    h = rmsnorm(x, p['norm1'])
    qa, ka, va, qi, ki, wi, qb, kb, vb, uc, g = project(h, p, pos)
    oa = dsa_prompt(qa, qi, wi, ka, va, ki, pos)
    ob = lax.map(lambda a: band_prompt_one(a[0], a[1], a[2], pos, p['rel_bias']), (qb, kb, vb))
    disc = s5_discretize(p)
    c_re, c_im, d = ssm_out_params(p)
    h0 = jnp.zeros((C_GROUPS, C_STATE), jnp.float32)
    yc, hr, hi = lax.map(lambda u: s5_scan(u, h0, h0, disc, c_re, c_im, d), uc.astype(jnp.float32))
    oc = s5_glu(yc.reshape(b, t, C_WIDTH).astype(x.dtype), p)
    x = x + merge_branches(oa, ob, oc, g, p)
    buf0 = jnp.zeros((b, CONV_W - 1, D_FF), x.dtype)
    f, buf = conv_ffn(rmsnorm(x, p['norm2']), buf0, p)
    x = x + f
    nb = min(BAND_CHUNKS * CHUNK, t)
    return x, (ka, va, ki, kb[:, t - nb:], vb[:, t - nb:], hr, hi, buf)


def layer_sample(x, pos, cache, p):
    cak, cav, caki, cbk, cbv, scr, sci, sconv = cache
    b, t, _ = x.shape
    h = rmsnorm(x, p['norm1'])
    qa, ka, va, qi, ki, wi, qb, kb, vb, uc, g = project(h, p, pos)
    k_all = jnp.concatenate([cak.astype(ka.dtype), ka], axis=1)
    v_all = jnp.concatenate([cav.astype(va.dtype), va], axis=1)
    ki_all = jnp.concatenate([caki.astype(ki.dtype), ki], axis=1)
    n_keys = k_all.shape[1]
    oa = dsa_attend(qa, qi, wi, pos, k_all, v_all, ki_all, jnp.arange(n_keys), min(TOPK_MAX, n_keys // 4))
    past = cak.shape[1]
    nbc = cbk.shape[1]
    kb_all = jnp.concatenate([cbk.astype(kb.dtype), kb], axis=1)
    vb_all = jnp.concatenate([cbv.astype(vb.dtype), vb], axis=1)
    kpos_b = jnp.concatenate([past - nbc + jnp.arange(nbc), pos])
    ob = jax.vmap(rel_attend, in_axes=(0, 0, 0, None, None, None))(qb, kb_all, vb_all, pos, kpos_b, p['rel_bias'])
    disc = s5_discretize(p)
    c_re, c_im, d = ssm_out_params(p)
    yc, hr, hi = jax.vmap(lambda u, r, i: s5_scan(u, r, i, disc, c_re, c_im, d))(
        uc.astype(jnp.float32), scr.astype(jnp.float32), sci.astype(jnp.float32))
    oc = s5_glu(yc.reshape(b, t, C_WIDTH).astype(x.dtype), p)
    x = x + merge_branches(oa, ob, oc, g, p)
    f, buf = conv_ffn(rmsnorm(x, p['norm2']), sconv, p)
    x = x + f
    return x, (ka, va, ki, kb, vb, hr, hi, buf)


def setup_inputs(seed: int = 0) -> dict:
    key = jax.random.key(seed)
    ks = iter(jax.random.split(key, 40))
    f32 = jnp.float32

    def nrm(shape, scale):
        return jax.random.normal(next(ks), shape, f32) * scale

    nb_cache = min(BAND_CHUNKS * CHUNK, PAST_LEN)
    G, P, GC = C_GROUPS, C_STATE, C_GROUP
    n_idx = jnp.arange(P, dtype=f32)
    return {
        'x_prompt': nrm((BATCH, SEQ, D_MODEL), 1.0),
        'x_sample': nrm((DEC_BATCH, DEC_SEQ, D_MODEL), 1.0),
        'cache_a_k': nrm((DEPTH, DEC_BATCH, PAST_LEN, A_KV, HEAD_DIM), 1.0),
        'cache_a_v': nrm((DEPTH, DEC_BATCH, PAST_LEN, A_KV, HEAD_DIM), 1.0),
        'cache_a_kidx': nrm((DEPTH, DEC_BATCH, PAST_LEN, IDX_DIM), 1.0),
        'cache_b_k': nrm((DEPTH, DEC_BATCH, nb_cache, B_HEADS, HEAD_DIM), 1.0),
        'cache_b_v': nrm((DEPTH, DEC_BATCH, nb_cache, B_HEADS, HEAD_DIM), 1.0),
        'state_c_re': nrm((DEPTH, DEC_BATCH, G, P), 0.1),
        'state_c_im': nrm((DEPTH, DEC_BATCH, G, P), 0.1),
        'state_ffn_conv': nrm((DEPTH, DEC_BATCH, CONV_W - 1, D_FF), 1.0),
        'norm1_g': 1.0 + nrm((DEPTH, D_MODEL), 0.01),
        'w_in': nrm((DEPTH, D_MODEL, N_IN), D_MODEL ** -0.5),
        'rel_bias': nrm((DEPTH, B_HEADS, 2 * REL_CLIP + 1), 0.1),
        'ssm_a_re': -0.5 + nrm((DEPTH, G, P), 0.01),
        'ssm_a_im': jnp.pi * n_idx + nrm((DEPTH, G, P), 0.01),
        'ssm_log_dt': jax.random.uniform(next(ks), (DEPTH, G), f32, math.log(1e-3), math.log(1e-1)),
        'ssm_b_re': nrm((DEPTH, G, P, GC), GC ** -0.5),
        'ssm_b_im': nrm((DEPTH, G, P, GC), GC ** -0.5),
        'ssm_c_re': nrm((DEPTH, G, GC, P), P ** -0.5),
        'ssm_c_im': nrm((DEPTH, G, GC, P), P ** -0.5),
        'ssm_d': nrm((DEPTH, G, GC), 0.3),
        'w_glu': nrm((DEPTH, C_WIDTH, C_WIDTH), C_WIDTH ** -0.5),
        'b_glu': nrm((DEPTH, C_WIDTH), 0.01),
        'w_branch': nrm((DEPTH, N_BRANCH, MIX_WIDTH, D_MODEL), MIX_WIDTH ** -0.5),
        'w_out': nrm((DEPTH, D_MODEL, D_MODEL), D_MODEL ** -0.5),
        'norm2_g': 1.0 + nrm((DEPTH, D_MODEL), 0.01),
        'w_ffn_in': nrm((DEPTH, D_MODEL, 2 * D_FF), D_MODEL ** -0.5),
        'w_ffn_conv': nrm((DEPTH, CONV_W, D_FF), CONV_W ** -0.5),
        'b_ffn_conv': nrm((DEPTH, D_FF), 0.01),
        'w_ffn_down': nrm((DEPTH, D_FF, D_MODEL), D_FF ** -0.5),
        'normf_g': 1.0 + nrm((D_MODEL,), 0.01),
    }


def reference(x_prompt, x_sample, cache_a_k, cache_a_v, cache_a_kidx, cache_b_k, cache_b_v, state_c_re, state_c_im,
              state_ffn_conv, norm1_g, w_in, rel_bias, ssm_a_re, ssm_a_im, ssm_log_dt, ssm_b_re, ssm_b_im, ssm_c_re,
              ssm_c_im, ssm_d, w_glu, b_glu, w_branch, w_out, norm2_g, w_ffn_in, w_ffn_conv, b_ffn_conv, w_ffn_down,
              normf_g):
    pos_p = jnp.arange(x_prompt.shape[1])
    pos_s = cache_a_k.shape[2] + jnp.arange(x_sample.shape[1])
    xp = x_prompt
    xs = x_sample
    st_p = [[] for _ in range(8)]
    st_s = [[] for _ in range(8)]
    for l in range(DEPTH):
        p = {'norm1': norm1_g[l], 'w_in': w_in[l], 'rel_bias': rel_bias[l], 'a_re': ssm_a_re[l],
             'a_im': ssm_a_im[l], 'log_dt': ssm_log_dt[l], 'b_re': ssm_b_re[l], 'b_im': ssm_b_im[l],
             'c_re': ssm_c_re[l], 'c_im': ssm_c_im[l], 'd': ssm_d[l], 'w_glu': w_glu[l], 'b_glu': b_glu[l],
             'w_branch': w_branch[l], 'w_out': w_out[l], 'norm2': norm2_g[l], 'w_ffn_in': w_ffn_in[l],
             'w_conv': w_ffn_conv[l], 'b_conv': b_ffn_conv[l], 'w_down': w_ffn_down[l]}
        xp, sp = layer_prompt(xp, pos_p, p)
        cache_l = (cache_a_k[l], cache_a_v[l], cache_a_kidx[l], cache_b_k[l], cache_b_v[l],
                   state_c_re[l], state_c_im[l], state_ffn_conv[l])
        xs, ss = layer_sample(xs, pos_s, cache_l, p)
        for i in range(8):
            st_p[i].append(sp[i])
            st_s[i].append(ss[i])
    y_prompt = rmsnorm(xp, normf_g)
    y_sample = rmsnorm(xs, normf_g)
    new_a_k_p = jnp.stack(st_p[0])
    new_a_v_p = jnp.stack(st_p[1])
    new_a_kidx_p = jnp.stack(st_p[2])
    new_b_k_p = jnp.stack(st_p[3])
    new_b_v_p = jnp.stack(st_p[4])
    new_c_re_p = jnp.stack(st_p[5])
    new_c_im_p = jnp.stack(st_p[6])
    new_ffn_conv_p = jnp.stack(st_p[7])
    new_a_k_s = jnp.stack(st_s[0])
    new_a_v_s = jnp.stack(st_s[1])
    new_a_kidx_s = jnp.stack(st_s[2])
    new_b_k_s = jnp.stack(st_s[3])
    new_b_v_s = jnp.stack(st_s[4])
    new_c_re_s = jnp.stack(st_s[5])
    new_c_im_s = jnp.stack(st_s[6])
    new_ffn_conv_s = jnp.stack(st_s[7])
    return (y_prompt, y_sample, new_a_k_p, new_a_v_p, new_a_kidx_p, new_b_k_p, new_b_v_p, new_c_re_p, new_c_im_p,
            new_ffn_conv_p, new_a_k_s, new_a_v_s, new_a_kidx_s, new_b_k_s, new_b_v_s, new_c_re_s, new_c_im_s,
            new_ffn_conv_s)
```

```python
import functools
import math

import jax
import jax.numpy as jnp
import numpy as np
from jax import lax
from jax.experimental import pallas as pl
from jax.experimental.pallas import tpu as pltpu

F32 = jnp.float32
BF16 = jnp.bfloat16
I32 = jnp.int32

CHUNK = 64
EPS = 1e-6
ROPE_THETA = 500000.0
HEAD_DIM = 128
A_KV = 2
A_GROUP = 4
A_ROT = HEAD_DIM // 4
IDX_HEADS = 8
IDX_DIM = 64
IDX_ROT = IDX_DIM // 4
TOPK_MAX = 256
BAND_CHUNKS = 8
REL_CLIP = 256
C_GROUP = 16
C_STATE = 64
CONV_W = 3
N_BRANCH = 3
ATTN_SCALE = HEAD_DIM ** -0.5
IDX_SCALE = (IDX_DIM * IDX_HEADS) ** -0.5

LANES = 128
SUBLANES = 8
VMEM_LIMIT = 56 * 1024 * 1024

NEG_BIG = -1e30
KEY_NEG_INF = -2139095041
KEY_POS_INF = 2139095040
INT32_MIN = -2147483648

SSM_L = 16
DSA_TK = 512
DSA_QB = 128
BAND_QB = 128


def _cparams(sem):
    return pltpu.CompilerParams(dimension_semantics=sem, vmem_limit_bytes=VMEM_LIMIT)


def _gelu(x):
    return 0.5 * x * (1.0 + jnp.tanh(math.sqrt(2.0 / math.pi) * (x + 0.044715 * (x * x * x))))


def _sigmoid(x):
    return 1.0 / (1.0 + jnp.exp(-x))


def _rms(x, g):
    ms = jnp.mean(x * x, axis=-1, keepdims=True)
    return (x * lax.rsqrt(ms + EPS)) * g


def _proj_kernel(x_ref, g_ref, w_ref, o_ref, h_ref):
    @pl.when(pl.program_id(1) == 0)
    def _():
        h_ref[...] = _rms(x_ref[...], g_ref[...]).astype(BF16)

    o_ref[...] = jnp.dot(h_ref[...], w_ref[...], preferred_element_type=F32)


def _proj(x, g, w, tm, tn):
    t, d = x.shape
    n = w.shape[1]
    return pl.pallas_call(
        _proj_kernel,
        grid=(t // tm, n // tn),
        in_specs=[pl.BlockSpec((tm, d), lambda i, j: (i, 0)),
                  pl.BlockSpec((1, d), lambda i, j: (0, 0)),
                  pl.BlockSpec((d, tn), lambda i, j: (0, j))],
        out_specs=pl.BlockSpec((tm, tn), lambda i, j: (i, j)),
        out_shape=jax.ShapeDtypeStruct((t, n), F32),
        scratch_shapes=[pltpu.VMEM((tm, d), BF16)],
        compiler_params=_cparams(("arbitrary", "arbitrary")),
        name="proj",
    )(x, g.reshape(1, d), w)


def _rot(x, c, sa, sb, half):
    n = x.shape[-1]
    return x * c + pltpu.roll(x, n - half, 1) * sa + pltpu.roll(x, half, 1) * sb


def _rope_kernel(qa_ref, ka_ref, va_ref, qi_ref, kw_ref, tab_ref,
                 qa_o, ka_o, kab_o, vab_o, qi_o, kw_o, kib_o):
    ca, saa, sba = tab_ref[0], tab_ref[1], tab_ref[2]
    ci, sai, sbi = tab_ref[3], tab_ref[4], tab_ref[5]
    ck, sak, sbk = tab_ref[6], tab_ref[7], tab_ref[8]
    for h in range(qa_ref.shape[1] // LANES):
        sl = slice(h * LANES, (h + 1) * LANES)
        qa_o[:, sl] = _rot(qa_ref[:, sl], ca, saa, sba, A_ROT // 2).astype(BF16)
    for h in range(ka_ref.shape[1] // LANES):
        sl = slice(h * LANES, (h + 1) * LANES)
        k = _rot(ka_ref[:, sl], ca, saa, sba, A_ROT // 2)
        ka_o[:, sl] = k
        kab_o[:, sl] = k.astype(BF16)
    vab_o[...] = va_ref[...].astype(BF16)
    for h in range(qi_ref.shape[1] // LANES):
        sl = slice(h * LANES, (h + 1) * LANES)
        qi_o[:, sl] = _rot(qi_ref[:, sl], ci, sai, sbi, IDX_ROT // 2).astype(BF16)
    kw = _rot(kw_ref[...], ck, sak, sbk, IDX_ROT // 2)
    kw_o[...] = kw
    kib_o[...] = kw.astype(BF16)


def _rope(p, tabs, cols, tm):
    t = p.shape[0]
    n_tab_blocks = tabs.shape[1] // tm
    d_qa, d_kv, d_qi = A_KV * A_GROUP * HEAD_DIM, A_KV * HEAD_DIM, IDX_HEADS * IDX_DIM

    def col(width, off):
        assert off % width == 0
        return pl.BlockSpec((tm, width), lambda i: (i, off // width))

    def out(width):
        return pl.BlockSpec((tm, width), lambda i: (i, 0))

    return pl.pallas_call(
        _rope_kernel,
        grid=(t // tm,),
        in_specs=[col(d_qa, cols["qa"]), col(d_kv, cols["ka"]), col(d_kv, cols["va"]),
                  col(d_qi, cols["qi"]), col(LANES, cols["kw"]),
                  pl.BlockSpec((9, tm, LANES), lambda i: (0, i % n_tab_blocks, 0))],
        out_specs=[out(d_qa), out(d_kv), out(d_kv), out(d_kv), out(d_qi), out(LANES), out(LANES)],
        out_shape=[jax.ShapeDtypeStruct((t, d_qa), BF16),
                   jax.ShapeDtypeStruct((t, d_kv), F32),
                   jax.ShapeDtypeStruct((t, d_kv), BF16),
                   jax.ShapeDtypeStruct((t, d_kv), BF16),
                   jax.ShapeDtypeStruct((t, d_qi), BF16),
                   jax.ShapeDtypeStruct((t, LANES), F32),
                   jax.ShapeDtypeStruct((t, LANES), BF16)],
        compiler_params=_cparams(("arbitrary",)),
        name="rope",
    )(p, p, p, p, p, tabs)


def _rope_tables(pos):
    pos = pos.astype(F32)[:, None]

    def cs(rot):
        half = rot // 2
        inv = jnp.float32(ROPE_THETA) ** (-jnp.arange(half, dtype=F32) / half)
        ang = pos * inv[None, :]
        return jnp.cos(ang), jnp.sin(ang)

    def tabs(rot, width):
        half = rot // 2
        c, s = cs(rot)
        n = pos.shape[0]
        one = jnp.ones((n, width - rot), F32)
        zero = jnp.zeros((n, width - rot), F32)
        zh = jnp.zeros((n, half), F32)
        return (jnp.concatenate([c, c, one], 1), jnp.concatenate([-s, zh, zero], 1),
                jnp.concatenate([zh, s, zero], 1))

    ca, saa, sba = tabs(A_ROT, HEAD_DIM)
    c64, sa64, sb64 = tabs(IDX_ROT, IDX_DIM)
    n = pos.shape[0]
    one64, zero64 = jnp.ones((n, IDX_DIM), F32), jnp.zeros((n, IDX_DIM), F32)
    return jnp.stack([ca, saa, sba,
                      jnp.concatenate([c64, c64], 1), jnp.concatenate([sa64, sa64], 1),
                      jnp.concatenate([sb64, sb64], 1),
                      jnp.concatenate([c64, one64], 1), jnp.concatenate([sa64, zero64], 1),
                      jnp.concatenate([sb64, zero64], 1)])


def _dsa_kernel(qaT_ref, qiT_ref, wiT_ref, k_ref, vT_ref, ki_ref, o_ref, key_ref, acc_ref,
                *, s_valid, q_pos0, topk):
    qb, tk = DSA_QB, DSA_TK
    j = pl.program_id(1)
    q_first = q_pos0 + j * qb
    qpos = q_first + lax.broadcasted_iota(I32, (1, qb), 1)
    assert CHUNK == 64 and tk == 512
    n_adm = jnp.minimum(((qpos >> 6) + 1) * CHUNK, s_valid)
    n_max = jnp.minimum((((q_first + qb - 1) >> 6) + 1) * CHUNK, s_valid)
    nkt = (n_max + tk - 1) >> 9

    qi = qiT_ref[0]
    rhs = jnp.concatenate([qi[h * IDX_DIM:(h + 1) * IDX_DIM, :] for h in range(IDX_HEADS)], axis=1)
    rhs = jnp.concatenate([rhs, jnp.zeros((LANES - IDX_DIM, IDX_HEADS * qb), BF16)], axis=0)
    wi = wiT_ref[0]

    def p1(kt, carry):
        off = pl.multiple_of(kt * tk, tk)
        kid = ki_ref[0, pl.ds(off, tk), :]
        s_all = jnp.dot(kid, rhs, preferred_element_type=F32)
        acc = wi[0:1, :] * jnp.maximum(s_all[:, 0:qb], 0.0)
        for h in range(1, IDX_HEADS):
            acc = acc + wi[h:h + 1, :] * jnp.maximum(s_all[:, h * qb:(h + 1) * qb], 0.0)
        score = acc * IDX_SCALE
        kpos = off + lax.broadcasted_iota(I32, (tk, 1), 0)
        score = jnp.where(kpos < n_adm, score, -jnp.inf)
        u = pltpu.bitcast(score, I32)
        key_ref[pl.ds(off, tk), :] = u ^ (lax.shift_right_arithmetic(u, 31) & 0x7FFFFFFF)
        return carry

    lax.fori_loop(0, nkt, p1, 0)

    def count_ge(cand):
        def body(kt, cnt):
            blk = key_ref[pl.ds(pl.multiple_of(kt * tk, tk), tk), :]
            m = jnp.where(blk >= cand, 1, 0).astype(I32)
            return cnt + jnp.sum(m.reshape(tk // SUBLANES, SUBLANES, qb), axis=0)

        cnt8 = lax.fori_loop(0, nkt, body, jnp.zeros((SUBLANES, qb), I32))
        return jnp.sum(cnt8, axis=0, keepdims=True)

    def bit_body(i, prefix):
        cand_u = prefix | lax.shift_left(jnp.int32(1), 31 - i)
        cnt = count_ge(cand_u ^ INT32_MIN)
        return jnp.where(cnt >= topk, cand_u, prefix)

    prefix = lax.fori_loop(0, 32, bit_body, jnp.zeros((1, qb), I32))
    thr = prefix ^ INT32_MIN
    n_gt = count_ge(thr + 1)
    n_ge = count_ge(thr)
    take_eq = topk - n_gt
    cut_ties = jnp.max(jnp.where((n_ge > topk) & (thr > KEY_NEG_INF), 1, 0)) > 0

    @pl.when(cut_ties)
    def _():
        r = lax.broadcasted_iota(I32, (tk, tk), 0)
        c = lax.broadcasted_iota(I32, (tk, tk), 1)
        tri = jnp.where(c <= r, 1.0, 0.0).astype(BF16)
        take = take_eq.astype(F32)

        def body(kt, seen):
            off = pl.multiple_of(kt * tk, tk)
            blk = key_ref[pl.ds(off, tk), :]
            eq = blk == thr
            incl = jnp.dot(tri, jnp.where(eq, 1.0, 0.0).astype(BF16), preferred_element_type=F32)
            drop = eq & (seen + incl > take)
            key_ref[pl.ds(off, tk), :] = jnp.where(drop, thr - 1, blk)
            return seen + incl[tk - 1:tk, :]

        lax.fori_loop(0, nkt, body, jnp.zeros((1, qb), F32))

    thr_lo = jnp.maximum(thr, KEY_NEG_INF + 1)
    qa = qaT_ref[0]
    ng = A_GROUP * qb
    qg = [jnp.concatenate([qa[(g * A_GROUP + hh) * HEAD_DIM:(g * A_GROUP + hh + 1) * HEAD_DIM, :]
                           for hh in range(A_GROUP)], axis=1) for g in range(A_KV)]
    acc_ref[...] = jnp.zeros(acc_ref.shape, F32)

    def p3(kt, carry):
        off = pl.multiple_of(kt * tk, tk)
        blk = key_ref[pl.ds(off, tk), :]
        mask1 = jnp.where((blk >= thr_lo) & (blk < KEY_POS_INF), 0.0, NEG_BIG)
        mask = jnp.concatenate([mask1] * A_GROUP, axis=1)
        new = []
        for g in range(A_KV):
            m, l = carry[2 * g], carry[2 * g + 1]
            kg = k_ref[0, pl.ds(off, tk), g * HEAD_DIM:(g + 1) * HEAD_DIM]
            s = jnp.dot(kg, qg[g], preferred_element_type=F32) * ATTN_SCALE + mask
            m_new = jnp.maximum(m, jnp.max(s, axis=0, keepdims=True))
            alpha = jnp.exp(m - m_new)
            p = jnp.exp(s - m_new)
            l = alpha * l + jnp.sum(p, axis=0, keepdims=True)
            vg = vT_ref[0, g * HEAD_DIM:(g + 1) * HEAD_DIM, pl.ds(off, tk)]
            acc_ref[g] = alpha * acc_ref[g] + jnp.dot(vg, p.astype(BF16), preferred_element_type=F32)
            new += [m_new, l]
        return tuple(new)

    init = (jnp.full((1, ng), NEG_BIG, F32), jnp.zeros((1, ng), F32)) * A_KV
    fin = lax.fori_loop(0, nkt, p3, init)
    for g in range(A_KV):
        o = acc_ref[g] / fin[2 * g + 1]
        for hh in range(A_GROUP):
            h = g * A_GROUP + hh
            o_ref[0, h * HEAD_DIM:(h + 1) * HEAD_DIM, :] = o[:, hh * qb:(hh + 1) * qb].astype(o_ref.dtype)


def _dsa(qaT, qiT, wiT, k, vT, ki, *, s_valid, q_pos0, topk):
    b, dq, sq = qaT.shape
    s_pad = k.shape[1]
    assert s_pad % DSA_TK == 0 and sq % DSA_QB == 0 and topk <= DSA_TK
    kern = functools.partial(_dsa_kernel, s_valid=s_valid, q_pos0=q_pos0, topk=topk)
    return pl.pallas_call(
        kern,
        grid=(b, sq // DSA_QB),
        in_specs=[pl.BlockSpec((1, dq, DSA_QB), lambda i, j: (i, 0, j)),
                  pl.BlockSpec((1, qiT.shape[1], DSA_QB), lambda i, j: (i, 0, j)),
                  pl.BlockSpec((1, IDX_HEADS, DSA_QB), lambda i, j: (i, 0, j)),
                  pl.BlockSpec((1, s_pad, k.shape[2]), lambda i, j: (i, 0, 0)),
                  pl.BlockSpec((1, vT.shape[1], s_pad), lambda i, j: (i, 0, 0)),
                  pl.BlockSpec((1, s_pad, LANES), lambda i, j: (i, 0, 0))],
        out_specs=pl.BlockSpec((1, dq, DSA_QB), lambda i, j: (i, 0, j)),
        out_shape=jax.ShapeDtypeStruct((b, dq, sq), BF16),
        scratch_shapes=[pltpu.VMEM((s_pad, DSA_QB), I32),
                        pltpu.VMEM((A_KV, HEAD_DIM, A_GROUP * DSA_QB), F32)],
        compiler_params=_cparams(("arbitrary", "arbitrary")),
        name="dsa",
    )(qaT, qiT, wiT, k, vT, ki)


def _band_heads(q, k_tiles, v_tiles, bias_ref, tile_ok, o_ref):
    nh = q.shape[1] // HEAD_DIM
    for h in range(nh):
        sl = slice(h * HEAD_DIM, (h + 1) * HEAD_DIM)
        qh = q[:, sl].astype(BF16)
        logits = []
        for i, kt in enumerate(k_tiles):
            s = lax.dot_general(qh, kt[:, sl].astype(BF16), (((1,), (1,)), ((), ())),
                                preferred_element_type=F32) * ATTN_SCALE
            if tile_ok[i] is not None:
                s = jnp.where(tile_ok[i], s, NEG_BIG)
            logits.append(s)
        s = jnp.concatenate(logits, axis=1) + bias_ref[h]
        m = jnp.max(s, axis=1, keepdims=True)
        p = jnp.exp(s - m)
        w = (p / jnp.sum(p, axis=1, keepdims=True)).astype(BF16)
        o = None
        off = 0
        for vt in v_tiles:
            n = vt.shape[0]
            t = jnp.dot(w[:, off:off + n], vt[:, sl].astype(BF16), preferred_element_type=F32)
            o = t if o is None else o + t
            off += n
        o_ref[:, sl] = o.astype(o_ref.dtype)


def _band_prompt_kernel(*refs, n_tiles):
    q_ref = refs[0]
    k_refs = refs[1:1 + n_tiles]
    v_refs = refs[1 + n_tiles:1 + 2 * n_tiles]
    bias_ref = refs[1 + 2 * n_tiles]
    o_ref = refs[2 + 2 * n_tiles]
    j = pl.program_id(1)
    tile_ok = [(j - (n_tiles - 1) + i) >= 0 for i in range(n_tiles - 1)] + [None]
    _band_heads(q_ref[...], [r[...] for r in k_refs], [r[...] for r in v_refs], bias_ref, tile_ok, o_ref)


def _band_prompt(p, bias, cols, batch, seq):
    qb = BAND_QB
    n_tiles = BAND_CHUNKS * CHUNK // qb + 1
    width = bias.shape[0] * HEAD_DIM
    nq = seq // qb
    cq, ck, cv = cols["qb"] // width, cols["kb"] // width, cols["vb"] // width
    assert cols["qb"] % width == 0 and cols["kb"] % width == 0 and cols["vb"] % width == 0

    def kv_spec(c, i):
        return pl.BlockSpec((qb, width), lambda b, j: (b * nq + jnp.maximum(j - (n_tiles - 1) + i, 0), c))

    return pl.pallas_call(
        functools.partial(_band_prompt_kernel, n_tiles=n_tiles),
        grid=(batch, nq),
        in_specs=([pl.BlockSpec((qb, width), lambda b, j: (b * nq + j, cq))]
                  + [kv_spec(ck, i) for i in range(n_tiles)]
                  + [kv_spec(cv, i) for i in range(n_tiles)]
                  + [pl.BlockSpec(bias.shape, lambda b, j: (0, 0, 0))]),
        out_specs=pl.BlockSpec((qb, width), lambda b, j: (b * nq + j, 0)),
        out_shape=jax.ShapeDtypeStruct((batch * seq, width), BF16),
        compiler_params=_cparams(("arbitrary", "arbitrary")),
        name="band_prompt",
    )(*([p] * (1 + 2 * n_tiles)), bias)


def _band_sample_kernel(q_ref, k_ref, v_ref, ck_ref, cv_ref, bias_ref, o_ref):
    _band_heads(q_ref[...], [ck_ref[0], k_ref[...]], [cv_ref[0], v_ref[...]], bias_ref, [None, None], o_ref)


def _band_sample(p, cache_k, cache_v, bias, cols, batch, t):
    width = bias.shape[0] * HEAD_DIM
    nbc = cache_k.shape[1]
    cq, ck, cv = cols["qb"] // width, cols["kb"] // width, cols["vb"] // width
    return pl.pallas_call(
        _band_sample_kernel,
        grid=(batch,),
        in_specs=[pl.BlockSpec((t, width), lambda b: (b, cq)),
                  pl.BlockSpec((t, width), lambda b: (b, ck)),
                  pl.BlockSpec((t, width), lambda b: (b, cv)),
                  pl.BlockSpec((1, nbc, width), lambda b: (b, 0, 0)),
                  pl.BlockSpec((1, nbc, width), lambda b: (b, 0, 0)),
                  pl.BlockSpec(bias.shape, lambda b: (0, 0, 0))],
        out_specs=pl.BlockSpec((t, width), lambda b: (b, 0)),
        out_shape=jax.ShapeDtypeStruct((batch * t, width), BF16),
        compiler_params=_cparams(("arbitrary",)),
        name="band_sample",
    )(p, p, p, cache_k, cache_v, bias)


def _band_bias(table, qpos, kpos):
    rel = jnp.clip(qpos[:, None] - kpos[None, :], -REL_CLIP, REL_CLIP) + REL_CLIP
    qc = qpos[:, None] // CHUNK
    kc = kpos[None, :] // CHUNK
    vis = (kc <= qc) & (kc >= qc - BAND_CHUNKS)
    return jnp.where(vis[None], table[:, rel].astype(F32), NEG_BIG)


def _ssm_kernel(u_ref, t_ref, p_ref, q_ref, a_ref, h0_ref, y_ref, hout_ref, hre_ref, him_ref, *, nc, bp):
    u0, u1 = u_ref[0], u_ref[1]
    hre_ref[...] = (jnp.dot(u0, p_ref[0, 0], preferred_element_type=F32)
                    + jnp.dot(u1, p_ref[1, 0], preferred_element_type=F32))
    him_ref[...] = (jnp.dot(u0, p_ref[0, 1], preferred_element_type=F32)
                    + jnp.dot(u1, p_ref[1, 1], preferred_element_type=F32))
    a_re, a_im = a_ref[0, 0:1, :], a_ref[0, 1:2, :]

    def step(c, h):
        h_re, h_im = h
        rows = pl.ds(pl.multiple_of(c * bp, bp), bp)
        s_re, s_im = hre_ref[rows, :], him_ref[rows, :]
        hre_ref[rows, :] = h_re
        him_ref[rows, :] = h_im
        return (a_re * h_re - a_im * h_im + s_re, a_re * h_im + a_im * h_re + s_im)

    h_re, h_im = lax.fori_loop(0, nc, step, (h0_ref[0, 0], h0_ref[0, 1]))
    hout_ref[0, 0] = h_re
    hout_ref[0, 1] = h_im
    hs_re, hs_im = hre_ref[...].astype(BF16), him_ref[...].astype(BF16)
    for g, u in enumerate((u0, u1)):
        y_ref[g] = (jnp.dot(u, t_ref[g], preferred_element_type=F32)
                    + jnp.dot(hs_re, q_ref[g, 0], preferred_element_type=F32)
                    + jnp.dot(hs_im, q_ref[g, 1], preferred_element_type=F32))


def _ssm(u, tmat, pmat, qmat, apow, h0, nc, bp):
    g, rows, w = u.shape
    return pl.pallas_call(
        functools.partial(_ssm_kernel, nc=nc, bp=bp),
        grid=(g // 2,),
        in_specs=[pl.BlockSpec((2, rows, w), lambda i: (i, 0, 0)),
                  pl.BlockSpec((2, w, w), lambda i: (i, 0, 0)),
                  pl.BlockSpec((2, 2, w, LANES), lambda i: (i, 0, 0, 0)),
                  pl.BlockSpec((2, 2, LANES, w), lambda i: (i, 0, 0, 0)),
                  pl.BlockSpec((1, 2, LANES), lambda i: (i, 0, 0)),
                  pl.BlockSpec((1, 2, bp, LANES), lambda i: (i, 0, 0, 0))],
        out_specs=[pl.BlockSpec((2, rows, w), lambda i: (i, 0, 0)),
                   pl.BlockSpec((1, 2, bp, LANES), lambda i: (i, 0, 0, 0))],
        out_shape=[jax.ShapeDtypeStruct((g, rows, w), F32),
                   jax.ShapeDtypeStruct((g // 2, 2, bp, LANES), F32)],
        scratch_shapes=[pltpu.VMEM((rows, LANES), F32), pltpu.VMEM((rows, LANES), F32)],
        compiler_params=_cparams(("arbitrary",)),
        name="ssm",
    )(u, tmat, pmat, qmat, apow, h0)


def _ssm_weights(a_re, a_im, log_dt, b_re, b_im, c_re, c_im, d):
    hp = lax.Precision.HIGHEST
    g, p = a_re.shape
    gc = b_re.shape[2]
    L = SSM_L
    a_re, a_im = a_re.astype(F32), a_im.astype(F32)
    dt = jnp.exp(log_dt.astype(F32))[:, None]
    mag = jnp.exp(dt * a_re)
    ab_re, ab_im = mag * jnp.cos(dt * a_im), mag * jnp.sin(dt * a_im)
    n_re = ab_re - 1.0
    n_im = ab_im
    den = a_re * a_re + a_im * a_im
    cc_re = (n_re * a_re + n_im * a_im) / den
    cc_im = (n_im * a_re - n_re * a_im) / den
    b_re, b_im = b_re.astype(F32), b_im.astype(F32)
    bb_re = cc_re[..., None] * b_re - cc_im[..., None] * b_im
    bb_im = cc_re[..., None] * b_im + cc_im[..., None] * b_re
    c_re, c_im = c_re.astype(F32), c_im.astype(F32)

    def cmul(x, y):
        return x[0] * y[0] - x[1] * y[1], x[0] * y[1] + x[1] * y[0]

    rep = lambda a: jnp.broadcast_to(a[None], (L,) + a.shape)
    pw_re, pw_im = lax.associative_scan(cmul, (rep(ab_re), rep(ab_im)), axis=0)
    pw_re = jnp.concatenate([jnp.ones((1, g, p), F32), pw_re], axis=0)
    pw_im = jnp.concatenate([jnp.zeros((1, g, p), F32), pw_im], axis=0)
    ca_re = c_re[None] * pw_re[:, :, None, :] - c_im[None] * pw_im[:, :, None, :]
    ca_im = c_re[None] * pw_im[:, :, None, :] + c_im[None] * pw_re[:, :, None, :]
    taps = (jnp.einsum("tgop,gpi->tgoi", ca_re[:L], bb_re, precision=hp)
            - jnp.einsum("tgop,gpi->tgoi", ca_im[:L], bb_im, precision=hp))
    taps = taps.at[0].add(jax.vmap(jnp.diag)(d.astype(F32)))
    s_idx = jnp.arange(L)[:, None]
    t_idx = jnp.arange(L)[None, :]
    lag = t_idx - s_idx
    tt = jnp.where((lag >= 0)[:, :, None, None, None], taps[jnp.clip(lag, 0, L - 1)], 0.0)
    tmat = tt.transpose(2, 0, 4, 1, 3).reshape(g, L * gc, L * gc)
    rev = L - 1 - jnp.arange(L)
    pin_re = pw_re[rev][:, :, :, None] * bb_re[None] - pw_im[rev][:, :, :, None] * bb_im[None]
    pin_im = pw_re[rev][:, :, :, None] * bb_im[None] + pw_im[rev][:, :, :, None] * bb_re[None]
    to_p = lambda a: a.transpose(1, 0, 3, 2).reshape(g, L * gc, p)
    to_q = lambda a: a.transpose(1, 3, 0, 2).reshape(g, p, L * gc)
    half = jnp.arange(g) % 2

    def lane_pair(x, axis):
        pad = [(0, 0)] * x.ndim
        lo = jnp.pad(x, pad[:axis] + [(0, p)] + pad[axis + 1:])
        hi = jnp.pad(x, pad[:axis] + [(p, 0)] + pad[axis + 1:])
        sel = half.reshape((g,) + (1,) * (x.ndim - 1)) == 0
        return jnp.where(sel, lo, hi)

    pmat = jnp.stack([lane_pair(to_p(pin_re), 2), lane_pair(to_p(pin_im), 2)], axis=1)
    qmat = jnp.stack([lane_pair(to_q(ca_re[1:]), 1), lane_pair(-to_q(ca_im[1:]), 1)], axis=1)
    apow = jnp.stack([pw_re[L].reshape(g // 2, 2 * p), pw_im[L].reshape(g // 2, 2 * p)], axis=1)
    return tmat.astype(BF16), pmat.astype(BF16), qmat.astype(BF16), apow


def _ssm_apply(uc, h0_re, h0_im, weights):
    tmat, pmat, qmat, apow = weights
    b, t, width = uc.shape
    g = tmat.shape[0]
    gc = width // g
    p = C_STATE
    L = SSM_L
    nc = t // L
    bp = -(-b // SUBLANES) * SUBLANES
    u = uc.reshape(b, nc, L, g, gc).transpose(3, 1, 0, 2, 4)
    u = jnp.pad(u, ((0, 0), (0, 0), (0, bp - b), (0, 0), (0, 0))).reshape(g, nc * bp, L * gc).astype(BF16)

    def pack(h):
        h = jnp.pad(h.astype(F32), ((0, bp - b), (0, 0), (0, 0)))
        return h.reshape(bp, g // 2, 2 * p).transpose(1, 0, 2)

    h0 = jnp.stack([pack(h0_re), pack(h0_im)], axis=1)
    y, hout = _ssm(u, tmat, pmat, qmat, apow, h0, nc, bp)
    y = y.reshape(g, nc, bp, L, gc)[:, :, :b].transpose(2, 1, 3, 0, 4).reshape(b * t, width)

    def unpack(h):
        return h.transpose(1, 0, 2).reshape(bp, g, p)[:b]

    return y, unpack(hout[:, 0]), unpack(hout[:, 1])


def _glu_kernel(y_ref, w_ref, b_ref, o_ref):
    z = _gelu(y_ref[...])
    a = jnp.dot(z.astype(BF16), w_ref[...], preferred_element_type=F32) + b_ref[...]
    o_ref[...] = (z * _sigmoid(a)).astype(o_ref.dtype)


def _glu(y, w, b, tm):
    t, n = y.shape
    return pl.pallas_call(
        _glu_kernel,
        grid=(t // tm,),
        in_specs=[pl.BlockSpec((tm, n), lambda i: (i, 0)),
                  pl.BlockSpec((n, n), lambda i: (0, 0)),
                  pl.BlockSpec((1, n), lambda i: (0, 0))],
        out_specs=pl.BlockSpec((tm, n), lambda i: (i, 0)),
        out_shape=jax.ShapeDtypeStruct((t, n), BF16),
        compiler_params=_cparams(("arbitrary",)),
        name="glu",
    )(y, w, b.reshape(1, n))


def _merge_kernel(oa_ref, ob_ref, oc_ref, ga_ref, gb_ref, gc_ref, w_ref, o_ref):
    acc = None
    for n, (o, g) in enumerate(((oa_ref, ga_ref), (ob_ref, gb_ref), (oc_ref, gc_ref))):
        br = jnp.dot(o[...], w_ref[n], preferred_element_type=F32)
        t = _sigmoid(g[...]) * br
        acc = t if acc is None else acc + t
    o_ref[...] = acc.astype(o_ref.dtype)


def _merge(oa, ob, oc, p, w, g_col, tm, tn):
    t, m = oa.shape
    d = w.shape[2]
    assert g_col % tn == 0 and d % tn == 0

    def gspec(n):
        return pl.BlockSpec((tm, tn), lambda i, j: (i, (g_col + n * d) // tn + j))

    ospec = pl.BlockSpec((tm, m), lambda i, j: (i, 0))
    return pl.pallas_call(
        _merge_kernel,
        grid=(t // tm, d // tn),
        in_specs=[ospec, ospec, ospec, gspec(0), gspec(1), gspec(2),
                  pl.BlockSpec((N_BRANCH, m, tn), lambda i, j: (0, 0, j))],
        out_specs=pl.BlockSpec((tm, tn), lambda i, j: (i, j)),
        out_shape=jax.ShapeDtypeStruct((t, d), BF16),
        compiler_params=_cparams(("arbitrary", "arbitrary")),
        name="merge",
    )(oa, ob, oc, p, p, p, w)


def _mmres_kernel(a_ref, w_ref, r_ref, o_ref):
    o_ref[...] = r_ref[...] + jnp.dot(a_ref[...], w_ref[...], preferred_element_type=F32)


def _mmres(a, w, res, tm, tn):
    t, k = a.shape
    n = w.shape[1]
    return pl.pallas_call(
        _mmres_kernel,
        grid=(t // tm, n // tn),
        in_specs=[pl.BlockSpec((tm, k), lambda i, j: (i, 0)),
                  pl.BlockSpec((k, tn), lambda i, j: (0, j)),
                  pl.BlockSpec((tm, tn), lambda i, j: (i, j))],
        out_specs=pl.BlockSpec((tm, tn), lambda i, j: (i, j)),
        out_shape=jax.ShapeDtypeStruct((t, n), F32),
        compiler_params=_cparams(("arbitrary", "arbitrary")),
        name="mmres",
    )(a, w, res)


def _ffn_in_kernel(x_ref, g_ref, wu_ref, wv_ref, wc_ref, bc_ref, st_ref, o_ref, tail_ref,
                   h_ref, up_ref, carry_ref, *, ns, ts, tiles_per_seq):
    i, j = pl.program_id(0), pl.program_id(1)
    pad = SUBLANES

    @pl.when(j == 0)
    def _():
        h_ref[...] = _rms(x_ref[...], g_ref[...]).astype(BF16)

    h = h_ref[...]
    u = jnp.dot(h, wu_ref[...], preferred_element_type=F32)
    v = jnp.dot(h, wv_ref[...], preferred_element_type=F32)
    tn = u.shape[1]
    up_ref[:, pad:, :] = u.reshape(ns, ts, tn)
    first = (i % tiles_per_seq) == 0

    @pl.when(first)
    def _():
        up_ref[:, pad - 2:pad, :] = st_ref[...]

    @pl.when(jnp.logical_not(first))
    def _():
        up_ref[:, pad - 2:pad, :] = carry_ref[j]

    last2 = up_ref[:, ts + pad - 2:ts + pad, :]
    carry_ref[j] = last2
    tail_ref[0] = last2
    c = bc_ref[...].reshape(1, 1, tn)
    for tap in range(CONV_W):
        c = c + up_ref[:, pad - 2 + tap:pad - 2 + tap + ts, :] * wc_ref[tap:tap + 1, :].reshape(1, 1, tn)
    o_ref[...] = (_gelu(c).reshape(ns * ts, tn) * v).astype(o_ref.dtype)


def _ffn_in(x, g, w, wc, bc, state, ns, ts, tn):
    t, d = x.shape
    f = w.shape[1] // 2
    tm = ns * ts
    nj = f // tn
    seq_len = t // state.shape[0]
    tiles_per_seq = max(seq_len // tm, 1)
    kern = functools.partial(_ffn_in_kernel, ns=ns, ts=ts, tiles_per_seq=tiles_per_seq)
    return pl.pallas_call(
        kern,
        grid=(t // tm, nj),
        in_specs=[pl.BlockSpec((tm, d), lambda i, j: (i, 0)),
                  pl.BlockSpec((1, d), lambda i, j: (0, 0)),
                  pl.BlockSpec((d, tn), lambda i, j: (0, j)),
                  pl.BlockSpec((d, tn), lambda i, j: (0, nj + j)),
                  pl.BlockSpec((CONV_W, tn), lambda i, j: (0, j)),
                  pl.BlockSpec((1, tn), lambda i, j: (0, j)),
                  pl.BlockSpec((ns, CONV_W - 1, tn), lambda i, j: (i // tiles_per_seq, 0, j))],
        out_specs=[pl.BlockSpec((tm, tn), lambda i, j: (i, j)),
                   pl.BlockSpec((1, ns, CONV_W - 1, tn), lambda i, j: (i, 0, 0, j))],
        out_shape=[jax.ShapeDtypeStruct((t, f), BF16),
                   jax.ShapeDtypeStruct((t // tm, ns, CONV_W - 1, f), F32)],
        scratch_shapes=[pltpu.VMEM((tm, d), BF16),
                        pltpu.VMEM((ns, ts + SUBLANES, tn), F32),
                        pltpu.VMEM((nj, ns, CONV_W - 1, tn), F32)],
        compiler_params=_cparams(("arbitrary", "arbitrary")),
        name="ffn_in",
    )(x, g.reshape(1, d), w, w, wc, bc.reshape(1, f), state)


def _norm_kernel(x_ref, g_ref, o_ref):
    o_ref[...] = _rms(x_ref[...], g_ref[...])


def _norm(x, g, tm):
    t, d = x.shape
    return pl.pallas_call(
        _norm_kernel,
        grid=(t // tm,),
        in_specs=[pl.BlockSpec((tm, d), lambda i: (i, 0)), pl.BlockSpec((1, d), lambda i: (0, 0))],
        out_specs=pl.BlockSpec((tm, d), lambda i: (i, 0)),
        out_shape=jax.ShapeDtypeStruct((t, d), F32),
        compiler_params=_cparams(("arbitrary",)),
        name="final_norm",
    )(x, g.reshape(1, d))


def _row_tile(t, cap=1024):
    tm = min(t, cap)
    while t % tm:
        tm //= 2
    return tm


def _pack_w_in(w_in, d_model):
    mix = d_model // 2
    sizes = dict(qa=mix, ka=A_KV * HEAD_DIM, va=A_KV * HEAD_DIM, qi=IDX_HEADS * IDX_DIM, ki=IDX_DIM,
                 wi=IDX_HEADS, qb=mix, kb=mix, vb=mix, uc=mix, g=N_BRANCH * d_model)
    src_order = ["qa", "ka", "va", "qi", "ki", "wi", "qb", "kb", "vb", "uc", "g"]
    src, off = {}, 0
    for name in src_order:
        src[name] = (off, sizes[name])
        off += sizes[name]
    assert off == w_in.shape[1]
    dst_order = ["qa", "qb", "kb", "vb", "uc", "g", "qi", "ka", "va", "ki", "wi"]
    cols, parts, off = {}, [], 0
    for name in dst_order:
        s, n = src[name]
        cols[name] = off
        parts.append(w_in[:, s:s + n])
        off += n
    cols["kw"] = cols["ki"]
    tn = 512
    total = -(-off // tn) * tn
    parts.append(jnp.zeros((w_in.shape[0], total - off), w_in.dtype))
    return jnp.concatenate(parts, axis=1).astype(BF16), cols, tn


def _to_heads_T(x, b, t):
    return x.reshape(b, t, x.shape[1]).transpose(0, 2, 1)


def _pad_axis(x, axis, size):
    pad = [(0, 0)] * x.ndim
    pad[axis] = (0, size - x.shape[axis])
    return jnp.pad(x, pad)


def _layer(x, pos, prm, cache, is_prompt):
    b, t, d = x.shape
    mix = d // 2
    xt = x.reshape(b * t, d)
    tm = _row_tile(b * t)
    cols = prm["cols"]
    p = _proj(xt, prm["norm1"], prm["w_in"], tm, prm["w_in_tn"])

    tabs = _rope_tables(pos)
    tm_r = _row_tile(t) if is_prompt else tm
    if not is_prompt:
        tabs = jnp.tile(tabs, (1, b, 1))
    qa, ka, kab, vab, qi, kw, kib = _rope(p, tabs, cols, tm_r)
    if is_prompt:
        sq, s_valid, q_pos0 = t, t, 0
        k_all, v_all, ki_all = kab.reshape(b, t, -1), vab.reshape(b, t, -1), kib.reshape(b, t, LANES)
    else:
        cak, cav, caki = cache[0], cache[1], cache[2]
        past = cak.shape[1]
        sq, s_valid, q_pos0 = DSA_QB, past + t, past
        k_all = jnp.concatenate([cak.reshape(b, past, -1).astype(BF16), kab.reshape(b, t, -1)], axis=1)
        v_all = jnp.concatenate([cav.reshape(b, past, -1).astype(BF16), vab.reshape(b, t, -1)], axis=1)
        ki_new = kib.reshape(b, t, LANES)
        ki_all = jnp.concatenate([_pad_axis(caki.astype(BF16), 2, LANES), ki_new], axis=1)
    s_pad = -(-s_valid // DSA_TK) * DSA_TK
    k_all = _pad_axis(k_all, 1, s_pad)
    vT = _pad_axis(v_all, 1, s_pad).transpose(0, 2, 1)
    ki_all = _pad_axis(ki_all, 1, s_pad)
    qaT = _pad_axis(_to_heads_T(qa, b, t), 2, sq)
    qiT = _pad_axis(_to_heads_T(qi, b, t), 2, sq)
    wiT = _pad_axis(_to_heads_T(kw[:, IDX_DIM:IDX_DIM + IDX_HEADS], b, t), 2, sq)
    topk = min(TOPK_MAX, s_valid // 4)
    oaT = _dsa(qaT, qiT, wiT, k_all, vT, ki_all, s_valid=s_valid, q_pos0=q_pos0, topk=topk)
    oa = oaT[:, :, :t].transpose(0, 2, 1).reshape(b * t, mix)

    nh = mix // HEAD_DIM
    if is_prompt:
        n_tiles = BAND_CHUNKS * CHUNK // BAND_QB + 1
        ql = jnp.arange(BAND_QB) + BAND_CHUNKS * CHUNK
        kl = jnp.arange(n_tiles * BAND_QB)
        bias = _band_bias(prm["rel_bias"], ql, kl)
        ob = _band_prompt(p, bias, cols, b, t)
    else:
        cbk, cbv = cache[3], cache[4]
        nbc = cbk.shape[1]
        past = cache[0].shape[1]
        kpos = jnp.concatenate([past - nbc + jnp.arange(nbc), pos])
        bias = _band_bias(prm["rel_bias"], pos, kpos)
        ob = _band_sample(p, cbk.reshape(b, nbc, nh * HEAD_DIM), cbv.reshape(b, nbc, nh * HEAD_DIM),
                          bias, cols, b, t)

    uc = p[:, cols["uc"]:cols["uc"] + mix].reshape(b, t, mix)
    g_ssm = mix // C_GROUP
    if is_prompt:
        h0_re = jnp.zeros((b, g_ssm, C_STATE), F32)
        h0_im = h0_re
    else:
        h0_re, h0_im = cache[5], cache[6]
    yc, hr, hi = _ssm_apply(uc, h0_re, h0_im, prm["ssm"])
    oc = _glu(yc, prm["w_glu"], prm["b_glu"], tm)

    merged = _merge(oa, ob, oc, p, prm["w_branch"], cols["g"], tm, 512)
    x1 = _mmres(merged, prm["w_out"], xt, tm, 1024)

    f = prm["w_conv"].shape[1]
    if is_prompt:
        state = jnp.zeros((b, CONV_W - 1, f), F32)
        ns, ts = 1, _row_tile(t)
    else:
        state = cache[7].astype(F32)
        ns, ts = b, t
    act, tails = _ffn_in(x1, prm["norm2"], prm["w_ffn_in"], prm["w_conv"], prm["b_conv"], state, ns, ts, 512)
    x2 = _mmres(act, prm["w_down"], x1, _row_tile(b * t, 512), 512)
    if is_prompt:
        buf = tails.reshape(b, t // ts, CONV_W - 1, f)[:, -1]
    else:
        buf = tails[0]

    ka4 = ka.reshape(b, t, A_KV, HEAD_DIM)
    va4 = p[:, cols["va"]:cols["va"] + A_KV * HEAD_DIM].reshape(b, t, A_KV, HEAD_DIM)
    ki3 = kw[:, :IDX_DIM].reshape(b, t, IDX_DIM)
    kb4 = p[:, cols["kb"]:cols["kb"] + mix].reshape(b, t, nh, HEAD_DIM)
    vb4 = p[:, cols["vb"]:cols["vb"] + mix].reshape(b, t, nh, HEAD_DIM)
    if is_prompt:
        nb = min(BAND_CHUNKS * CHUNK, t)
        kb4, vb4 = kb4[:, t - nb:], vb4[:, t - nb:]
    return x2.reshape(b, t, d), (ka4, va4, ki3, kb4, vb4, hr, hi, buf)


def kernel(x_prompt, x_sample, cache_a_k, cache_a_v, cache_a_kidx, cache_b_k, cache_b_v, state_c_re, state_c_im,
           state_ffn_conv, norm1_g, w_in, rel_bias, ssm_a_re, ssm_a_im, ssm_log_dt, ssm_b_re, ssm_b_im, ssm_c_re,
           ssm_c_im, ssm_d, w_glu, b_glu, w_branch, w_out, norm2_g, w_ffn_in, w_ffn_conv, b_ffn_conv, w_ffn_down,
           normf_g):
    depth = w_in.shape[0]
    d = x_prompt.shape[2]
    pos_p = jnp.arange(x_prompt.shape[1])
    pos_s = cache_a_k.shape[2] + jnp.arange(x_sample.shape[1])
    xp, xs = x_prompt, x_sample
    st_p, st_s = [], []
    for l in range(depth):
        w_in_l, cols, tn = _pack_w_in(w_in[l], d)
        prm = dict(norm1=norm1_g[l], w_in=w_in_l, cols=cols, w_in_tn=tn, rel_bias=rel_bias[l],
                   ssm=_ssm_weights(ssm_a_re[l], ssm_a_im[l], ssm_log_dt[l], ssm_b_re[l], ssm_b_im[l],
                                    ssm_c_re[l], ssm_c_im[l], ssm_d[l]),
                   w_glu=w_glu[l].astype(BF16), b_glu=b_glu[l], w_branch=w_branch[l].astype(BF16),
                   w_out=w_out[l].astype(BF16), norm2=norm2_g[l], w_ffn_in=w_ffn_in[l].astype(BF16),
                   w_conv=w_ffn_conv[l], b_conv=b_ffn_conv[l], w_down=w_ffn_down[l].astype(BF16))
        xp, sp = _layer(xp, pos_p, prm, None, True)
        cache_l = (cache_a_k[l], cache_a_v[l], cache_a_kidx[l], cache_b_k[l], cache_b_v[l],
                   state_c_re[l], state_c_im[l], state_ffn_conv[l])
        xs, ss = _layer(xs, pos_s, prm, cache_l, False)
        st_p.append(sp)
        st_s.append(ss)
    bp, tp, _ = xp.shape
    bs, tsq, _ = xs.shape
    y_prompt = _norm(xp.reshape(bp * tp, d), normf_g, _row_tile(bp * tp)).reshape(bp, tp, d)
    y_sample = _norm(xs.reshape(bs * tsq, d), normf_g, _row_tile(bs * tsq)).reshape(bs, tsq, d)
    outs_p = [jnp.stack([s[i] for s in st_p]) for i in range(8)]
    outs_s = [jnp.stack([s[i] for s in st_s]) for i in range(8)]
    return (y_prompt, y_sample, *outs_p, *outs_s)
```

```python
import functools
import math

import jax
import jax.numpy as jnp
import numpy as np
from jax import lax
from jax.experimental import pallas as pl
from jax.experimental.pallas import tpu as pltpu

F32 = jnp.float32
BF16 = jnp.bfloat16
I32 = jnp.int32
I16 = jnp.int16

CHUNK = 64
EPS = 1e-6
ROPE_THETA = 500000.0
HEAD_DIM = 128
A_KV = 2
A_GROUP = 4
A_ROT = HEAD_DIM // 4
IDX_HEADS = 8
IDX_DIM = 64
IDX_ROT = IDX_DIM // 4
TOPK_MAX = 256
BAND_CHUNKS = 8
REL_CLIP = 256
C_GROUP = 16
C_STATE = 64
CONV_W = 3
N_BRANCH = 3
ATTN_SCALE = HEAD_DIM ** -0.5
IDX_SCALE = (IDX_DIM * IDX_HEADS) ** -0.5

LANES = 128
SUBLANES = 8
VMEM_LIMIT = 56 * 1024 * 1024

NEG_BIG = -1e30
KEY_NEG_INF = -2139095041
KEY_POS_INF = 2139095040
INT32_MIN = -2147483648

SSM_L = SUBLANES
SSM_GB = LANES // C_GROUP
DSA_TK = 512
DSA_QB = 128
BAND_QB = 128


def _cparams(sem):
    return pltpu.CompilerParams(dimension_semantics=sem, vmem_limit_bytes=VMEM_LIMIT)


def _gelu(x):
    return 0.5 * x * (1.0 + jnp.tanh(math.sqrt(2.0 / math.pi) * (x + 0.044715 * (x * x * x))))


def _sigmoid(x):
    return 1.0 / (1.0 + jnp.exp(-x))


def _rms(x, g):
    ms = jnp.mean(x * x, axis=-1, keepdims=True)
    return (x * lax.rsqrt(ms + EPS)) * g


def _proj_kernel(x_ref, g_ref, w_ref, o_ref, h_ref):
    @pl.when(pl.program_id(1) == 0)
    def _():
        h_ref[...] = _rms(x_ref[...], g_ref[...]).astype(BF16)

    o_ref[...] = jnp.dot(h_ref[...], w_ref[...], preferred_element_type=F32)


def _proj(x, g, w, tm, tn):
    t, d = x.shape
    n = w.shape[1]
    return pl.pallas_call(
        _proj_kernel,
        grid=(t // tm, n // tn),
        in_specs=[pl.BlockSpec((tm, d), lambda i, j: (i, 0)),
                  pl.BlockSpec((1, d), lambda i, j: (0, 0)),
                  pl.BlockSpec((d, tn), lambda i, j: (0, j))],
        out_specs=pl.BlockSpec((tm, tn), lambda i, j: (i, j)),
        out_shape=jax.ShapeDtypeStruct((t, n), F32),
        scratch_shapes=[pltpu.VMEM((tm, d), BF16)],
        compiler_params=_cparams(("arbitrary", "arbitrary")),
        name="proj",
    )(x, g.reshape(1, d), w)


def _rot(x, c, sa, sb, half):
    n = x.shape[-1]
    return x * c + pltpu.roll(x, n - half, 1) * sa + pltpu.roll(x, half, 1) * sb


def _rope_kernel(qa_ref, ka_ref, va_ref, qi_ref, kw_ref, tab_ref,
                 qa_o, ka_o, kab_o, vab_o, qi_o, kw_o, kib_o):
    ca, saa, sba = tab_ref[0], tab_ref[1], tab_ref[2]
    ci, sai, sbi = tab_ref[3], tab_ref[4], tab_ref[5]
    ck, sak, sbk = tab_ref[6], tab_ref[7], tab_ref[8]
    for h in range(qa_ref.shape[1] // LANES):
        sl = slice(h * LANES, (h + 1) * LANES)
        qa_o[:, sl] = _rot(qa_ref[:, sl], ca, saa, sba, A_ROT // 2).astype(BF16)
    for h in range(ka_ref.shape[1] // LANES):
        sl = slice(h * LANES, (h + 1) * LANES)
        k = _rot(ka_ref[:, sl], ca, saa, sba, A_ROT // 2)
        ka_o[:, sl] = k
        kab_o[:, sl] = k.astype(BF16)
    vab_o[...] = va_ref[...].astype(BF16)
    for h in range(qi_ref.shape[1] // LANES):
        sl = slice(h * LANES, (h + 1) * LANES)
        qi_o[:, sl] = _rot(qi_ref[:, sl], ci, sai, sbi, IDX_ROT // 2).astype(BF16)
    kw = _rot(kw_ref[...], ck, sak, sbk, IDX_ROT // 2)
    kw_o[...] = kw
    kib_o[...] = kw.astype(BF16)


def _rope(p, tabs, cols, tm):
    t = p.shape[0]
    n_tab_blocks = tabs.shape[1] // tm
    d_qa, d_kv, d_qi = A_KV * A_GROUP * HEAD_DIM, A_KV * HEAD_DIM, IDX_HEADS * IDX_DIM

    def col(width, off):
        assert off % width == 0
        return pl.BlockSpec((tm, width), lambda i: (i, off // width))

    def out(width):
        return pl.BlockSpec((tm, width), lambda i: (i, 0))

    return pl.pallas_call(
        _rope_kernel,
        grid=(t // tm,),
        in_specs=[col(d_qa, cols["qa"]), col(d_kv, cols["ka"]), col(d_kv, cols["va"]),
                  col(d_qi, cols["qi"]), col(LANES, cols["kw"]),
                  pl.BlockSpec((9, tm, LANES), lambda i: (0, i % n_tab_blocks, 0))],
        out_specs=[out(d_qa), out(d_kv), out(d_kv), out(d_kv), out(d_qi), out(LANES), out(LANES)],
        out_shape=[jax.ShapeDtypeStruct((t, d_qa), BF16),
                   jax.ShapeDtypeStruct((t, d_kv), F32),
                   jax.ShapeDtypeStruct((t, d_kv), BF16),
                   jax.ShapeDtypeStruct((t, d_kv), BF16),
                   jax.ShapeDtypeStruct((t, d_qi), BF16),
                   jax.ShapeDtypeStruct((t, LANES), F32),
                   jax.ShapeDtypeStruct((t, LANES), BF16)],
        compiler_params=_cparams(("arbitrary",)),
        name="rope",
    )(p, p, p, p, p, tabs)


def _rope_tables(pos):
    pos = pos.astype(F32)[:, None]

    def cs(rot):
        half = rot // 2
        inv = jnp.float32(ROPE_THETA) ** (-jnp.arange(half, dtype=F32) / half)
        ang = pos * inv[None, :]
        return jnp.cos(ang), jnp.sin(ang)

    def tabs(rot, width):
        half = rot // 2
        c, s = cs(rot)
        n = pos.shape[0]
        one = jnp.ones((n, width - rot), F32)
        zero = jnp.zeros((n, width - rot), F32)
        zh = jnp.zeros((n, half), F32)
        return (jnp.concatenate([c, c, one], 1), jnp.concatenate([-s, zh, zero], 1),
                jnp.concatenate([zh, s, zero], 1))

    ca, saa, sba = tabs(A_ROT, HEAD_DIM)
    c64, sa64, sb64 = tabs(IDX_ROT, IDX_DIM)
    n = pos.shape[0]
    one64, zero64 = jnp.ones((n, IDX_DIM), F32), jnp.zeros((n, IDX_DIM), F32)
    return jnp.stack([ca, saa, sba,
                      jnp.concatenate([c64, c64], 1), jnp.concatenate([sa64, sa64], 1),
                      jnp.concatenate([sb64, sb64], 1),
                      jnp.concatenate([c64, one64], 1), jnp.concatenate([sa64, zero64], 1),
                      jnp.concatenate([sb64, zero64], 1)])


def _dsa_kernel(qaT_ref, qiT_ref, wiT_ref, k_ref, vT_ref, ki_ref, o_ref, key_ref, acc_ref, half_ref,
                *, s_valid, q_pos0, topk):
    qb, tk = DSA_QB, DSA_TK
    j = pl.program_id(1)
    q_first = q_pos0 + j * qb
    qpos = q_first + lax.broadcasted_iota(I32, (1, qb), 1)
    assert CHUNK == 64 and tk == 512
    n_adm = jnp.minimum(((qpos >> 6) + 1) * CHUNK, s_valid)
    n_max = jnp.minimum((((q_first + qb - 1) >> 6) + 1) * CHUNK, s_valid)
    nkt = (n_max + tk - 1) >> 9

    qi = qiT_ref[0]
    rhs = jnp.concatenate([qi[h * IDX_DIM:(h + 1) * IDX_DIM, :] for h in range(IDX_HEADS)], axis=1)
    rhs = jnp.concatenate([rhs, jnp.zeros((LANES - IDX_DIM, IDX_HEADS * qb), BF16)], axis=0)
    wi = wiT_ref[0]

    def p1(kt, carry):
        off = pl.multiple_of(kt * tk, tk)
        kid = ki_ref[0, pl.ds(off, tk), :]
        s_all = jnp.dot(kid, rhs, preferred_element_type=F32)
        acc = wi[0:1, :] * jnp.maximum(s_all[:, 0:qb], 0.0)
        for h in range(1, IDX_HEADS):
            acc = acc + wi[h:h + 1, :] * jnp.maximum(s_all[:, h * qb:(h + 1) * qb], 0.0)
        score = acc * IDX_SCALE
        kpos = off + lax.broadcasted_iota(I32, (tk, 1), 0)
        score = jnp.where(kpos < n_adm, score, -jnp.inf)
        u = pltpu.bitcast(score, I32)
        key = u ^ (lax.shift_right_arithmetic(u, 31) & 0x7FFFFFFF)
        key_ref[pl.ds(off, tk), :] = key
        half_ref[pl.ds(off, tk), :] = lax.shift_right_arithmetic(key, 16).astype(I16)
        return carry

    lax.fori_loop(0, nkt, p1, 0)

    def count_ge(cand):
        def body(kt, cnt):
            blk = key_ref[pl.ds(pl.multiple_of(kt * tk, tk), tk), :]
            m = jnp.where(blk >= cand, 1, 0).astype(I32)
            return cnt + jnp.sum(m.reshape(tk // SUBLANES, SUBLANES, qb), axis=0)

        cnt8 = lax.fori_loop(0, nkt, body, jnp.zeros((SUBLANES, qb), I32))
        return jnp.sum(cnt8, axis=0, keepdims=True)

    rows16 = 2 * SUBLANES

    def count_half_ge(cand):
        c16 = cand.astype(I16)

        def body(kt, cnt):
            blk = half_ref[pl.ds(pl.multiple_of(kt * tk, tk), tk), :]
            m = jnp.where(blk >= c16, jnp.int16(1), jnp.int16(0))
            parts = [m[r * rows16:(r + 1) * rows16, :] for r in range(tk // rows16)]
            while len(parts) > 1:
                parts = [a + b for a, b in zip(parts[::2], parts[1::2])]
            return cnt + parts[0]

        cnt = lax.fori_loop(0, nkt, body, jnp.zeros((rows16, qb), I16))
        return jnp.sum(cnt.astype(I32), axis=0, keepdims=True)

    def kth_half(k_target):
        def bit_body(i, prefix):
            cand_u = prefix | lax.shift_left(jnp.int32(1), 15 - i)
            cnt = count_half_ge(cand_u - 32768)
            return jnp.where(cnt >= k_target, cand_u, prefix)

        return lax.fori_loop(0, 16, bit_body, jnp.zeros((1, qb), I32)) - 32768

    assert key_ref.shape[0] // rows16 <= 32767
    hi = kth_half(topk)
    n_gt_hi = jnp.where(hi >= 32767, 0, count_half_ge(jnp.minimum(hi + 1, 32767)))

    def low_halves(kt, carry):
        rows = pl.ds(pl.multiple_of(kt * tk, tk), tk)
        key = key_ref[rows, :]
        lo = (key & 0xFFFF) - 32768
        half_ref[rows, :] = jnp.where(lax.shift_right_arithmetic(key, 16) == hi, lo, -32768).astype(I16)
        return carry

    lax.fori_loop(0, nkt, low_halves, 0)
    lo = kth_half(topk - n_gt_hi)
    thr = hi * 65536 + (lo + 32768)
    n_gt = count_ge(thr + 1)
    n_ge = count_ge(thr)
    take_eq = topk - n_gt
    cut_ties = jnp.max(jnp.where((n_ge > topk) & (thr > KEY_NEG_INF), 1, 0)) > 0

    @pl.when(cut_ties)
    def _():
        r = lax.broadcasted_iota(I32, (tk, tk), 0)
        c = lax.broadcasted_iota(I32, (tk, tk), 1)
        tri = jnp.where(c <= r, 1.0, 0.0).astype(BF16)
        take = take_eq.astype(F32)

        def body(kt, seen):
            off = pl.multiple_of(kt * tk, tk)
            blk = key_ref[pl.ds(off, tk), :]
            eq = blk == thr
            incl = jnp.dot(tri, jnp.where(eq, 1.0, 0.0).astype(BF16), preferred_element_type=F32)
            drop = eq & (seen + incl > take)
            key_ref[pl.ds(off, tk), :] = jnp.where(drop, thr - 1, blk)
            return seen + incl[tk - 1:tk, :]

        lax.fori_loop(0, nkt, body, jnp.zeros((1, qb), F32))

    thr_lo = jnp.maximum(thr, KEY_NEG_INF + 1)
    qa = qaT_ref[0]
    ng = A_GROUP * qb
    qg = [jnp.concatenate([qa[(g * A_GROUP + hh) * HEAD_DIM:(g * A_GROUP + hh + 1) * HEAD_DIM, :]
                           for hh in range(A_GROUP)], axis=1) for g in range(A_KV)]
    acc_ref[...] = jnp.zeros(acc_ref.shape, F32)

    c_exp = ATTN_SCALE * math.log2(math.e)

    def p3(kt, carry):
        off = pl.multiple_of(kt * tk, tk)
        blk = key_ref[pl.ds(off, tk), :]
        mask1 = jnp.where((blk >= thr_lo) & (blk < KEY_POS_INF), 0.0, NEG_BIG)
        mask = jnp.concatenate([mask1] * A_GROUP, axis=1)
        new = []
        for g in range(A_KV):
            m, l = carry[2 * g], carry[2 * g + 1]
            kg = k_ref[0, pl.ds(off, tk), g * HEAD_DIM:(g + 1) * HEAD_DIM]
            s = jnp.dot(kg, qg[g], preferred_element_type=F32) + mask
            m_new = jnp.maximum(m, jnp.max(s, axis=0, keepdims=True))
            alpha = jnp.exp2((m - m_new) * c_exp)
            p = jnp.exp2((s - m_new) * c_exp)
            l = alpha * l + jnp.sum(p, axis=0, keepdims=True)
            vg = vT_ref[0, g * HEAD_DIM:(g + 1) * HEAD_DIM, pl.ds(off, tk)]
            acc_ref[g] = alpha * acc_ref[g] + jnp.dot(vg, p.astype(BF16), preferred_element_type=F32)
            new += [m_new, l]
        return tuple(new)

    init = (jnp.full((1, ng), NEG_BIG, F32), jnp.zeros((1, ng), F32)) * A_KV
    fin = lax.fori_loop(0, nkt, p3, init)
    for g in range(A_KV):
        o = acc_ref[g] / fin[2 * g + 1]
        for hh in range(A_GROUP):
            h = g * A_GROUP + hh
            o_ref[0, h * HEAD_DIM:(h + 1) * HEAD_DIM, :] = o[:, hh * qb:(hh + 1) * qb].astype(o_ref.dtype)


def _dsa(qaT, qiT, wiT, k, vT, ki, *, s_valid, q_pos0, topk):
    b, dq, sq = qaT.shape
    s_pad = k.shape[1]
    assert s_pad % DSA_TK == 0 and sq % DSA_QB == 0 and topk <= DSA_TK
    kern = functools.partial(_dsa_kernel, s_valid=s_valid, q_pos0=q_pos0, topk=topk)
    return pl.pallas_call(
        kern,
        grid=(b, sq // DSA_QB),
        in_specs=[pl.BlockSpec((1, dq, DSA_QB), lambda i, j: (i, 0, j)),
                  pl.BlockSpec((1, qiT.shape[1], DSA_QB), lambda i, j: (i, 0, j)),
                  pl.BlockSpec((1, IDX_HEADS, DSA_QB), lambda i, j: (i, 0, j)),
                  pl.BlockSpec((1, s_pad, k.shape[2]), lambda i, j: (i, 0, 0)),
                  pl.BlockSpec((1, vT.shape[1], s_pad), lambda i, j: (i, 0, 0)),
                  pl.BlockSpec((1, s_pad, LANES), lambda i, j: (i, 0, 0))],
        out_specs=pl.BlockSpec((1, dq, DSA_QB), lambda i, j: (i, 0, j)),
        out_shape=jax.ShapeDtypeStruct((b, dq, sq), BF16),
        scratch_shapes=[pltpu.VMEM((s_pad, DSA_QB), I32),
                        pltpu.VMEM((A_KV, HEAD_DIM, A_GROUP * DSA_QB), F32),
                        pltpu.VMEM((s_pad, DSA_QB), jnp.int16)],
        compiler_params=_cparams(("arbitrary", "arbitrary")),
        name="dsa",
    )(qaT, qiT, wiT, k, vT, ki)


def _band_heads(q, k_tiles, v_tiles, bias_ref, tile_ok, o_ref):
    nh = q.shape[1] // HEAD_DIM
    for h in range(nh):
        sl = slice(h * HEAD_DIM, (h + 1) * HEAD_DIM)
        qh = q[:, sl].astype(BF16)
        logits = []
        for i, kt in enumerate(k_tiles):
            s = lax.dot_general(qh, kt[:, sl].astype(BF16), (((1,), (1,)), ((), ())),
                                preferred_element_type=F32) * ATTN_SCALE
            if tile_ok[i] is not None:
                s = jnp.where(tile_ok[i], s, NEG_BIG)
            logits.append(s)
        s = jnp.concatenate(logits, axis=1) + bias_ref[h]
        m = jnp.max(s, axis=1, keepdims=True)
        p = jnp.exp(s - m)
        w = (p / jnp.sum(p, axis=1, keepdims=True)).astype(BF16)
        o = None
        off = 0
        for vt in v_tiles:
            n = vt.shape[0]
            t = jnp.dot(w[:, off:off + n], vt[:, sl].astype(BF16), preferred_element_type=F32)
            o = t if o is None else o + t
            off += n
        o_ref[:, sl] = o.astype(o_ref.dtype)


def _band_prompt_kernel(*refs, n_tiles):
    q_ref = refs[0]
    k_refs = refs[1:1 + n_tiles]
    v_refs = refs[1 + n_tiles:1 + 2 * n_tiles]
    bias_ref = refs[1 + 2 * n_tiles]
    o_ref = refs[2 + 2 * n_tiles]
    j = pl.program_id(1)
    tile_ok = [(j - (n_tiles - 1) + i) >= 0 for i in range(n_tiles - 1)] + [None]
    _band_heads(q_ref[...], [r[...] for r in k_refs], [r[...] for r in v_refs], bias_ref, tile_ok, o_ref)


def _band_prompt(p, bias, cols, batch, seq):
    qb = BAND_QB
    n_tiles = BAND_CHUNKS * CHUNK // qb + 1
    width = bias.shape[0] * HEAD_DIM
    nq = seq // qb
    cq, ck, cv = cols["qb"] // width, cols["kb"] // width, cols["vb"] // width
    assert cols["qb"] % width == 0 and cols["kb"] % width == 0 and cols["vb"] % width == 0

    def kv_spec(c, i):
        return pl.BlockSpec((qb, width), lambda b, j: (b * nq + jnp.maximum(j - (n_tiles - 1) + i, 0), c))

    return pl.pallas_call(
        functools.partial(_band_prompt_kernel, n_tiles=n_tiles),
        grid=(batch, nq),
        in_specs=([pl.BlockSpec((qb, width), lambda b, j: (b * nq + j, cq))]
                  + [kv_spec(ck, i) for i in range(n_tiles)]
                  + [kv_spec(cv, i) for i in range(n_tiles)]
                  + [pl.BlockSpec(bias.shape, lambda b, j: (0, 0, 0))]),
        out_specs=pl.BlockSpec((qb, width), lambda b, j: (b * nq + j, 0)),
        out_shape=jax.ShapeDtypeStruct((batch * seq, width), BF16),
        compiler_params=_cparams(("arbitrary", "arbitrary")),
        name="band_prompt",
    )(*([p] * (1 + 2 * n_tiles)), bias)


def _band_sample_kernel(q_ref, k_ref, v_ref, ck_ref, cv_ref, bias_ref, o_ref):
    _band_heads(q_ref[...], [ck_ref[0], k_ref[...]], [cv_ref[0], v_ref[...]], bias_ref, [None, None], o_ref)


def _band_sample(p, cache_k, cache_v, bias, cols, batch, t):
    width = bias.shape[0] * HEAD_DIM
    nbc = cache_k.shape[1]
    cq, ck, cv = cols["qb"] // width, cols["kb"] // width, cols["vb"] // width
    return pl.pallas_call(
        _band_sample_kernel,
        grid=(batch,),
        in_specs=[pl.BlockSpec((t, width), lambda b: (b, cq)),
                  pl.BlockSpec((t, width), lambda b: (b, ck)),
                  pl.BlockSpec((t, width), lambda b: (b, cv)),
                  pl.BlockSpec((1, nbc, width), lambda b: (b, 0, 0)),
                  pl.BlockSpec((1, nbc, width), lambda b: (b, 0, 0)),
                  pl.BlockSpec(bias.shape, lambda b: (0, 0, 0))],
        out_specs=pl.BlockSpec((t, width), lambda b: (b, 0)),
        out_shape=jax.ShapeDtypeStruct((batch * t, width), BF16),
        compiler_params=_cparams(("arbitrary",)),
        name="band_sample",
    )(p, p, p, cache_k, cache_v, bias)


def _band_bias(table, q0, nq, k0, nk):
    n = nq + nk
    j = jnp.arange(n)
    g = table[:, jnp.clip(j + (q0 - k0 - nk + 1), -REL_CLIP, REL_CLIP) + REL_CLIP].astype(F32)
    hank = jnp.tile(g, (1, nq + 1))[:, :nq * (n + 1)].reshape(-1, nq, n + 1)[:, :, :nk]
    bias = hank[:, :, ::-1]
    qc = (q0 + jnp.arange(nq))[:, None] // CHUNK
    kc = (k0 + jnp.arange(nk))[None, :] // CHUNK
    vis = (kc <= qc) & (kc >= qc - BAND_CHUNKS)
    return jnp.where(vis[None], bias, NEG_BIG)


def _ssm_kernel(u_ref, t_ref, p_ref, q_ref, a_ref, h0_ref, y_ref, hre_o, him_o, sre_ref, sim_ref,
                *, seg, tiles_per_seq):
    L = SSM_L
    x = jnp.concatenate([u_ref[:, s, :] for s in range(L)], axis=1).astype(BF16)
    sre_ref[...] = jnp.dot(x, p_ref[0, 0], preferred_element_type=F32)
    sim_ref[...] = jnp.dot(x, p_ref[0, 1], preferred_element_type=F32)
    rows, w = sre_ref.shape
    kseg = lax.broadcasted_iota(I32, (SUBLANES, 1), 0) % seg
    apk_re, apk_im = a_ref[0, 0, 0:SUBLANES, :], a_ref[0, 1, 0:SUBLANES, :]

    def cmad(x_re, x_im, a_re, a_im, y_re, y_im):
        return x_re + a_re * y_re - a_im * y_im, x_im + a_re * y_im + a_im * y_re

    def tile(i, carry):
        r = pl.ds(pl.multiple_of(i * SUBLANES, SUBLANES), SUBLANES)
        x_re, x_im = sre_ref[r, :], sim_ref[r, :]
        for n, d in enumerate((1, 2, 4)):
            if d < seg:
                ad_re = a_ref[0, 0, SUBLANES + n:SUBLANES + n + 1, :]
                ad_im = a_ref[0, 1, SUBLANES + n:SUBLANES + n + 1, :]
                sh_re = jnp.where(kseg >= d, pltpu.roll(x_re, d, 0), 0.0)
                sh_im = jnp.where(kseg >= d, pltpu.roll(x_im, d, 0), 0.0)
                x_re, x_im = cmad(x_re, x_im, ad_re, ad_im, sh_re, sh_im)
        if tiles_per_seq > 1:
            first = (i % tiles_per_seq) == 0
            hin_re = jnp.where(first, h0_ref[0, 0], carry[0])
            hin_im = jnp.where(first, h0_ref[0, 1], carry[1])
        else:
            hin_re, hin_im = h0_ref[0, 0, r, :], h0_ref[0, 1, r, :]
        inc_re, inc_im = cmad(x_re, x_im, apk_re, apk_im, hin_re, hin_im)
        sre_ref[r, :] = jnp.where(kseg == 0, hin_re, pltpu.roll(inc_re, 1, 0))
        sim_ref[r, :] = jnp.where(kseg == 0, hin_im, pltpu.roll(inc_im, 1, 0))
        hre_o[0, r, :] = inc_re
        him_o[0, r, :] = inc_im
        return inc_re[SUBLANES - 1:SUBLANES, :], inc_im[SUBLANES - 1:SUBLANES, :]

    zero = jnp.zeros((1, w), F32)
    lax.fori_loop(0, rows // SUBLANES, tile, (zero, zero))
    y = (jnp.dot(x, t_ref[0], preferred_element_type=F32)
         + jnp.dot(sre_ref[...].astype(BF16), q_ref[0, 0], preferred_element_type=F32)
         + jnp.dot(sim_ref[...].astype(BF16), q_ref[0, 1], preferred_element_type=F32))
    for t in range(L):
        y_ref[:, t, :] = y[:, t * LANES:(t + 1) * LANES]


def _ssm(p3, uc_col, tmat, pmat, qmat, apow, h0, rows_blk, seg, tiles_per_seq):
    rows_total = p3.shape[0]
    nb, _, _, w = pmat.shape
    kl = SSM_L * LANES
    h0_rows = h0.shape[2] // (rows_total // rows_blk)
    hspec = pl.BlockSpec((1, rows_blk, w), lambda i, r: (i, r, 0))
    return pl.pallas_call(
        functools.partial(_ssm_kernel, seg=seg, tiles_per_seq=tiles_per_seq),
        grid=(nb, rows_total // rows_blk),
        in_specs=[pl.BlockSpec((rows_blk, SSM_L, LANES), lambda i, r: (r, 0, uc_col // LANES + i)),
                  pl.BlockSpec((1, kl, kl), lambda i, r: (i, 0, 0)),
                  pl.BlockSpec((1, 2, kl, w), lambda i, r: (i, 0, 0, 0)),
                  pl.BlockSpec((1, 2, w, kl), lambda i, r: (i, 0, 0, 0)),
                  pl.BlockSpec((1, 2, 2 * SUBLANES, w), lambda i, r: (i, 0, 0, 0)),
                  pl.BlockSpec((1, 2, h0_rows, w), lambda i, r: (i, 0, r, 0))],
        out_specs=[pl.BlockSpec((rows_blk, SSM_L, LANES), lambda i, r: (r, 0, i)), hspec, hspec],
        out_shape=[jax.ShapeDtypeStruct((rows_total, SSM_L, nb * LANES), F32),
                   jax.ShapeDtypeStruct((nb, rows_total, w), F32),
                   jax.ShapeDtypeStruct((nb, rows_total, w), F32)],
        scratch_shapes=[pltpu.VMEM((rows_blk, w), F32), pltpu.VMEM((rows_blk, w), F32)],
        compiler_params=_cparams(("arbitrary", "arbitrary")),
        name="ssm",
    )(p3, tmat, pmat, qmat, apow, h0)


def _ssm_weights(a_re, a_im, log_dt, b_re, b_im, c_re, c_im, d):
    hp = lax.Precision.HIGHEST
    g, p = a_re.shape
    gc = b_re.shape[2]
    L = SSM_L
    a_re, a_im = a_re.astype(F32), a_im.astype(F32)
    dt = jnp.exp(log_dt.astype(F32))[:, None]
    mag = jnp.exp(dt * a_re)
    ab_re, ab_im = mag * jnp.cos(dt * a_im), mag * jnp.sin(dt * a_im)
    n_re = ab_re - 1.0
    n_im = ab_im
    den = a_re * a_re + a_im * a_im
    cc_re = (n_re * a_re + n_im * a_im) / den
    cc_im = (n_im * a_re - n_re * a_im) / den
    b_re, b_im = b_re.astype(F32), b_im.astype(F32)
    bb_re = cc_re[..., None] * b_re - cc_im[..., None] * b_im
    bb_im = cc_re[..., None] * b_im + cc_im[..., None] * b_re
    c_re, c_im = c_re.astype(F32), c_im.astype(F32)

    def cmul(x, y):
        return x[0] * y[0] - x[1] * y[1], x[0] * y[1] + x[1] * y[0]

    rep = lambda a: jnp.broadcast_to(a[None], (L,) + a.shape)
    pw_re, pw_im = lax.associative_scan(cmul, (rep(ab_re), rep(ab_im)), axis=0)
    pw_re = jnp.concatenate([jnp.ones((1, g, p), F32), pw_re], axis=0)
    pw_im = jnp.concatenate([jnp.zeros((1, g, p), F32), pw_im], axis=0)
    ca_re = c_re[None] * pw_re[:, :, None, :] - c_im[None] * pw_im[:, :, None, :]
    ca_im = c_re[None] * pw_im[:, :, None, :] + c_im[None] * pw_re[:, :, None, :]
    taps = (jnp.einsum("tgop,gpi->tgoi", ca_re[:L], bb_re, precision=hp)
            - jnp.einsum("tgop,gpi->tgoi", ca_im[:L], bb_im, precision=hp))
    taps = taps.at[0].add(jax.vmap(jnp.diag)(d.astype(F32)))
    s_idx = jnp.arange(L)[:, None]
    t_idx = jnp.arange(L)[None, :]
    lag = t_idx - s_idx
    tt = jnp.where((lag >= 0)[:, :, None, None, None], taps[jnp.clip(lag, 0, L - 1)], 0.0)
    gb = SSM_GB
    nb = g // gb
    eye = jnp.eye(gb, dtype=F32)

    def block_diag(a, ax, perm):
        a = a.reshape(a.shape[:ax] + (nb, gb, 1) + a.shape[ax + 1:])
        e = eye.reshape((1,) * (ax + 1) + (gb, gb) + (1,) * (a.ndim - ax - 3))
        return (a * e).transpose(perm)

    tmat = block_diag(tt, 2, (2, 0, 3, 6, 1, 4, 5)).reshape(nb, L * gb * gc, L * gb * gc)
    rev = L - 1 - jnp.arange(L)
    pin_re = pw_re[rev][:, :, :, None] * bb_re[None] - pw_im[rev][:, :, :, None] * bb_im[None]
    pin_im = pw_re[rev][:, :, :, None] * bb_im[None] + pw_im[rev][:, :, :, None] * bb_re[None]
    to_p = lambda a: block_diag(a, 1, (1, 0, 2, 5, 3, 4)).reshape(nb, L * gb * gc, gb * p)
    to_q = lambda a: block_diag(a, 1, (1, 3, 5, 0, 2, 4)).reshape(nb, gb * p, L * gb * gc)
    pmat = jnp.stack([to_p(pin_re), to_p(pin_im)], axis=1)
    qmat = jnp.stack([to_q(ca_re[1:]), -to_q(ca_im[1:])], axis=1)
    rep8 = lambda a: jnp.broadcast_to(a[None], (SUBLANES,) + a.shape)
    apl_re, apl_im = lax.associative_scan(cmul, (rep8(pw_re[L]), rep8(pw_im[L])), axis=0)
    apl = jnp.stack([apl_re, apl_im]).reshape(2, SUBLANES, nb, gb * p)
    return tmat.astype(BF16), pmat.astype(BF16), qmat.astype(BF16), apl


def _ssm_apply(p, uc_col, b, t, h0_re, h0_im, weights, one_seq_per_step):
    tmat, pmat, qmat, apl = weights
    nb, w = pmat.shape[0], pmat.shape[3]
    L = SSM_L
    nc = t // L
    assert t % L == 0 and (nc % SUBLANES == 0 or SUBLANES % nc == 0)
    seg = min(nc, SUBLANES)
    if one_seq_per_step:
        assert nc % SUBLANES == 0
        rows_blk, tiles_per_seq, reps = nc, nc // SUBLANES, SUBLANES
    else:
        rows_blk, tiles_per_seq, reps = b * nc, 1, nc
        assert nc <= SUBLANES and rows_blk % SUBLANES == 0
    pw_rows = np.array([k % seg for k in range(SUBLANES)] + [0, 1, 3] + [0] * (SUBLANES - 3))
    apow = apl[:, pw_rows].transpose(2, 0, 1, 3)

    def pack(h):
        h = h.astype(F32).reshape(b, nb, w).transpose(1, 0, 2)
        return jnp.repeat(h, reps, axis=1)

    h0 = jnp.stack([pack(h0_re), pack(h0_im)], axis=1)
    p3 = p.reshape(b * nc, L, p.shape[1])
    y, hre, him = _ssm(p3, uc_col, tmat, pmat, qmat, apow, h0, rows_blk, seg, tiles_per_seq)

    def last(h):
        return h.reshape(nb, b, nc, w)[:, :, nc - 1].transpose(1, 0, 2).reshape(b, nb * SSM_GB, C_STATE)

    return y.reshape(b * t, nb * LANES), last(hre), last(him)


def _glu_kernel(y_ref, w_ref, b_ref, o_ref):
    z = _gelu(y_ref[...])
    a = jnp.dot(z.astype(BF16), w_ref[...], preferred_element_type=F32) + b_ref[...]
    o_ref[...] = (z * _sigmoid(a)).astype(o_ref.dtype)


def _glu(y, w, b, tm):
    t, n = y.shape
    return pl.pallas_call(
        _glu_kernel,
        grid=(t // tm,),
        in_specs=[pl.BlockSpec((tm, n), lambda i: (i, 0)),
                  pl.BlockSpec((n, n), lambda i: (0, 0)),
                  pl.BlockSpec((1, n), lambda i: (0, 0))],
        out_specs=pl.BlockSpec((tm, n), lambda i: (i, 0)),
        out_shape=jax.ShapeDtypeStruct((t, n), BF16),
        compiler_params=_cparams(("arbitrary",)),
        name="glu",
    )(y, w, b.reshape(1, n))


def _merge_kernel(oa_ref, ob_ref, oc_ref, ga_ref, gb_ref, gc_ref, w_ref, o_ref):
    acc = None
    for n, (o, g) in enumerate(((oa_ref, ga_ref), (ob_ref, gb_ref), (oc_ref, gc_ref))):
        br = jnp.dot(o[...], w_ref[n], preferred_element_type=F32)
        t = _sigmoid(g[...]) * br
        acc = t if acc is None else acc + t
    o_ref[...] = acc.astype(o_ref.dtype)


def _merge(oa, ob, oc, p, w, g_col, tm, tn):
    t, m = oa.shape
    d = w.shape[2]
    assert g_col % tn == 0 and d % tn == 0

    def gspec(n):
        return pl.BlockSpec((tm, tn), lambda i, j: (i, (g_col + n * d) // tn + j))

    ospec = pl.BlockSpec((tm, m), lambda i, j: (i, 0))
    return pl.pallas_call(
        _merge_kernel,
        grid=(t // tm, d // tn),
        in_specs=[ospec, ospec, ospec, gspec(0), gspec(1), gspec(2),
                  pl.BlockSpec((N_BRANCH, m, tn), lambda i, j: (0, 0, j))],
        out_specs=pl.BlockSpec((tm, tn), lambda i, j: (i, j)),
        out_shape=jax.ShapeDtypeStruct((t, d), BF16),
        compiler_params=_cparams(("arbitrary", "arbitrary")),
        name="merge",
    )(oa, ob, oc, p, p, p, w)


def _mmres_kernel(a_ref, w_ref, r_ref, o_ref):
    o_ref[...] = r_ref[...] + jnp.dot(a_ref[...], w_ref[...], preferred_element_type=F32)


def _mmres(a, w, res, tm, tn):
    t, k = a.shape
    n = w.shape[1]
    return pl.pallas_call(
        _mmres_kernel,
        grid=(t // tm, n // tn),
        in_specs=[pl.BlockSpec((tm, k), lambda i, j: (i, 0)),
                  pl.BlockSpec((k, tn), lambda i, j: (0, j)),
                  pl.BlockSpec((tm, tn), lambda i, j: (i, j))],
        out_specs=pl.BlockSpec((tm, tn), lambda i, j: (i, j)),
        out_shape=jax.ShapeDtypeStruct((t, n), F32),
        compiler_params=_cparams(("arbitrary", "arbitrary")),
        name="mmres",
    )(a, w, res)


def _ffn_in_kernel(x_ref, g_ref, wu_ref, wv_ref, wc_ref, bc_ref, st_ref, o_ref, tail_ref,
                   h_ref, up_ref, carry_ref, *, ns, ts, tiles_per_seq):
    i, j = pl.program_id(0), pl.program_id(1)
    pad = SUBLANES

    @pl.when(j == 0)
    def _():
        h_ref[...] = _rms(x_ref[...], g_ref[...]).astype(BF16)

    h = h_ref[...]
    u = jnp.dot(h, wu_ref[...], preferred_element_type=F32)
    v = jnp.dot(h, wv_ref[...], preferred_element_type=F32)
    tn = u.shape[1]
    up_ref[:, pad:, :] = u.reshape(ns, ts, tn)
    first = (i % tiles_per_seq) == 0

    @pl.when(first)
    def _():
        up_ref[:, pad - 2:pad, :] = st_ref[...]

    @pl.when(jnp.logical_not(first))
    def _():
        up_ref[:, pad - 2:pad, :] = carry_ref[j]

    last2 = up_ref[:, ts + pad - 2:ts + pad, :]
    carry_ref[j] = last2
    tail_ref[0] = last2
    c = bc_ref[...].reshape(1, 1, tn)
    for tap in range(CONV_W):
        c = c + up_ref[:, pad - 2 + tap:pad - 2 + tap + ts, :] * wc_ref[tap:tap + 1, :].reshape(1, 1, tn)
    o_ref[...] = (_gelu(c).reshape(ns * ts, tn) * v).astype(o_ref.dtype)


def _ffn_in(x, g, w, wc, bc, state, ns, ts, tn):
    t, d = x.shape
    f = w.shape[1] // 2
    tm = ns * ts
    nj = f // tn
    seq_len = t // state.shape[0]
    tiles_per_seq = max(seq_len // tm, 1)
    kern = functools.partial(_ffn_in_kernel, ns=ns, ts=ts, tiles_per_seq=tiles_per_seq)
    return pl.pallas_call(
        kern,
        grid=(t // tm, nj),
        in_specs=[pl.BlockSpec((tm, d), lambda i, j: (i, 0)),
                  pl.BlockSpec((1, d), lambda i, j: (0, 0)),
                  pl.BlockSpec((d, tn), lambda i, j: (0, j)),
                  pl.BlockSpec((d, tn), lambda i, j: (0, nj + j)),
                  pl.BlockSpec((CONV_W, tn), lambda i, j: (0, j)),
                  pl.BlockSpec((1, tn), lambda i, j: (0, j)),
                  pl.BlockSpec((ns, CONV_W - 1, tn), lambda i, j: (i // tiles_per_seq, 0, j))],
        out_specs=[pl.BlockSpec((tm, tn), lambda i, j: (i, j)),
                   pl.BlockSpec((1, ns, CONV_W - 1, tn), lambda i, j: (i, 0, 0, j))],
        out_shape=[jax.ShapeDtypeStruct((t, f), BF16),
                   jax.ShapeDtypeStruct((t // tm, ns, CONV_W - 1, f), F32)],
        scratch_shapes=[pltpu.VMEM((tm, d), BF16),
                        pltpu.VMEM((ns, ts + SUBLANES, tn), F32),
                        pltpu.VMEM((nj, ns, CONV_W - 1, tn), F32)],
        compiler_params=_cparams(("arbitrary", "arbitrary")),
        name="ffn_in",
    )(x, g.reshape(1, d), w, w, wc, bc.reshape(1, f), state)


def _norm_kernel(x_ref, g_ref, o_ref):
    o_ref[...] = _rms(x_ref[...], g_ref[...])


def _norm(x, g, tm):
    t, d = x.shape
    return pl.pallas_call(
        _norm_kernel,
        grid=(t // tm,),
        in_specs=[pl.BlockSpec((tm, d), lambda i: (i, 0)), pl.BlockSpec((1, d), lambda i: (0, 0))],
        out_specs=pl.BlockSpec((tm, d), lambda i: (i, 0)),
        out_shape=jax.ShapeDtypeStruct((t, d), F32),
        compiler_params=_cparams(("arbitrary",)),
        name="final_norm",
    )(x, g.reshape(1, d))


def _row_tile(t, cap=1024):
    tm = min(t, cap)
    while t % tm:
        tm //= 2
    return tm


def _pack_w_in(w_in, d_model):
    mix = d_model // 2
    sizes = dict(qa=mix, ka=A_KV * HEAD_DIM, va=A_KV * HEAD_DIM, qi=IDX_HEADS * IDX_DIM, ki=IDX_DIM,
                 wi=IDX_HEADS, qb=mix, kb=mix, vb=mix, uc=mix, g=N_BRANCH * d_model)
    src_order = ["qa", "ka", "va", "qi", "ki", "wi", "qb", "kb", "vb", "uc", "g"]
    src, off = {}, 0
    for name in src_order:
        src[name] = (off, sizes[name])
        off += sizes[name]
    assert off == w_in.shape[1]
    dst_order = ["qa", "qb", "kb", "vb", "uc", "g", "qi", "ka", "va", "ki", "wi"]
    cols, parts, off = {}, [], 0
    for name in dst_order:
        s, n = src[name]
        cols[name] = off
        parts.append(w_in[:, s:s + n])
        off += n
    cols["kw"] = cols["ki"]
    tn = 512
    total = -(-off // tn) * tn
    parts.append(jnp.zeros((w_in.shape[0], total - off), w_in.dtype))
    return jnp.concatenate(parts, axis=1).astype(BF16), cols, tn


def _to_heads_T(x, b, t):
    return x.reshape(b, t, x.shape[1]).transpose(0, 2, 1)


def _pad_axis(x, axis, size):
    pad = [(0, 0)] * x.ndim
    pad[axis] = (0, size - x.shape[axis])
    return jnp.pad(x, pad)


def _layer(x, pos, prm, cache, is_prompt):
    b, t, d = x.shape
    mix = d // 2
    xt = x.reshape(b * t, d)
    tm = _row_tile(b * t)
    cols = prm["cols"]
    p = _proj(xt, prm["norm1"], prm["w_in"], tm, prm["w_in_tn"])

    tabs = _rope_tables(pos)
    tm_r = _row_tile(t) if is_prompt else tm
    if not is_prompt:
        tabs = jnp.tile(tabs, (1, b, 1))
    qa, ka, kab, vab, qi, kw, kib = _rope(p, tabs, cols, tm_r)
    if is_prompt:
        sq, s_valid, q_pos0 = t, t, 0
        k_all, v_all, ki_all = kab.reshape(b, t, -1), vab.reshape(b, t, -1), kib.reshape(b, t, LANES)
    else:
        cak, cav, caki = cache[0], cache[1], cache[2]
        past = cak.shape[1]
        sq, s_valid, q_pos0 = DSA_QB, past + t, past
        k_all = jnp.concatenate([cak.reshape(b, past, -1).astype(BF16), kab.reshape(b, t, -1)], axis=1)
        v_all = jnp.concatenate([cav.reshape(b, past, -1).astype(BF16), vab.reshape(b, t, -1)], axis=1)
        ki_new = kib.reshape(b, t, LANES)
        ki_all = jnp.concatenate([_pad_axis(caki.astype(BF16), 2, LANES), ki_new], axis=1)
    s_pad = -(-s_valid // DSA_TK) * DSA_TK
    k_all = _pad_axis(k_all, 1, s_pad)
    vT = _pad_axis(v_all, 1, s_pad).transpose(0, 2, 1)
    ki_all = _pad_axis(ki_all, 1, s_pad)
    qaT = _pad_axis(_to_heads_T(qa, b, t), 2, sq)
    qiT = _pad_axis(_to_heads_T(qi, b, t), 2, sq)
    wiT = _pad_axis(_to_heads_T(kw[:, IDX_DIM:IDX_DIM + IDX_HEADS], b, t), 2, sq)
    topk = min(TOPK_MAX, s_valid // 4)
    oaT = _dsa(qaT, qiT, wiT, k_all, vT, ki_all, s_valid=s_valid, q_pos0=q_pos0, topk=topk)
    oa = oaT[:, :, :t].transpose(0, 2, 1).reshape(b * t, mix)

    nh = mix // HEAD_DIM
    if is_prompt:
        n_tiles = BAND_CHUNKS * CHUNK // BAND_QB + 1
        bias = _band_bias(prm["rel_bias"], BAND_CHUNKS * CHUNK, BAND_QB, 0, n_tiles * BAND_QB)
        ob = _band_prompt(p, bias, cols, b, t)
    else:
        cbk, cbv = cache[3], cache[4]
        nbc = cbk.shape[1]
        past = cache[0].shape[1]
        bias = _band_bias(prm["rel_bias"], past, t, past - nbc, nbc + t)
        ob = _band_sample(p, cbk.reshape(b, nbc, nh * HEAD_DIM), cbv.reshape(b, nbc, nh * HEAD_DIM),
                          bias, cols, b, t)

    g_ssm = mix // C_GROUP
    if is_prompt:
        h0_re = jnp.zeros((b, g_ssm, C_STATE), F32)
        h0_im = h0_re
    else:
        h0_re, h0_im = cache[5], cache[6]
    yc, hr, hi = _ssm_apply(p, cols["uc"], b, t, h0_re, h0_im, prm["ssm"], is_prompt)
    oc = _glu(yc, prm["w_glu"], prm["b_glu"], tm)

    merged = _merge(oa, ob, oc, p, prm["w_branch"], cols["g"], tm, 512)
    x1 = _mmres(merged, prm["w_out"], xt, tm, 1024)

    f = prm["w_conv"].shape[1]
    if is_prompt:
        state = jnp.zeros((b, CONV_W - 1, f), F32)
        ns, ts = 1, _row_tile(t)
    else:
        state = cache[7].astype(F32)
        ns, ts = b, t
    act, tails = _ffn_in(x1, prm["norm2"], prm["w_ffn_in"], prm["w_conv"], prm["b_conv"], state, ns, ts, 512)
    x2 = _mmres(act, prm["w_down"], x1, _row_tile(b * t, 512), 512)
    if is_prompt:
        buf = tails.reshape(b, t // ts, CONV_W - 1, f)[:, -1]
    else:
        buf = tails[0]

    ka4 = ka.reshape(b, t, A_KV, HEAD_DIM)
    va4 = p[:, cols["va"]:cols["va"] + A_KV * HEAD_DIM].reshape(b, t, A_KV, HEAD_DIM)
    ki3 = kw[:, :IDX_DIM].reshape(b, t, IDX_DIM)
    nb = min(BAND_CHUNKS * CHUNK, t) if is_prompt else t
    pb = p.reshape(b, t, p.shape[1])[:, t - nb:]
    kb4 = pb[:, :, cols["kb"]:cols["kb"] + mix].reshape(b, nb, nh, HEAD_DIM)
    vb4 = pb[:, :, cols["vb"]:cols["vb"] + mix].reshape(b, nb, nh, HEAD_DIM)
    return x2.reshape(b, t, d), (ka4, va4, ki3, kb4, vb4, hr, hi, buf)


def kernel(x_prompt, x_sample, cache_a_k, cache_a_v, cache_a_kidx, cache_b_k, cache_b_v, state_c_re, state_c_im,
           state_ffn_conv, norm1_g, w_in, rel_bias, ssm_a_re, ssm_a_im, ssm_log_dt, ssm_b_re, ssm_b_im, ssm_c_re,
           ssm_c_im, ssm_d, w_glu, b_glu, w_branch, w_out, norm2_g, w_ffn_in, w_ffn_conv, b_ffn_conv, w_ffn_down,
           normf_g):
    depth = w_in.shape[0]
    d = x_prompt.shape[2]
    pos_p = jnp.arange(x_prompt.shape[1])
    pos_s = cache_a_k.shape[2] + jnp.arange(x_sample.shape[1])
    xp, xs = x_prompt, x_sample
    st_p, st_s = [], []
    for l in range(depth):
        w_in_l, cols, tn = _pack_w_in(w_in[l], d)
        prm = dict(norm1=norm1_g[l], w_in=w_in_l, cols=cols, w_in_tn=tn, rel_bias=rel_bias[l],
                   ssm=_ssm_weights(ssm_a_re[l], ssm_a_im[l], ssm_log_dt[l], ssm_b_re[l], ssm_b_im[l],
                                    ssm_c_re[l], ssm_c_im[l], ssm_d[l]),
                   w_glu=w_glu[l].astype(BF16), b_glu=b_glu[l], w_branch=w_branch[l].astype(BF16),
                   w_out=w_out[l].astype(BF16), norm2=norm2_g[l], w_ffn_in=w_ffn_in[l].astype(BF16),
                   w_conv=w_ffn_conv[l], b_conv=b_ffn_conv[l], w_down=w_ffn_down[l].astype(BF16))
        xp, sp = _layer(xp, pos_p, prm, None, True)
        cache_l = (cache_a_k[l], cache_a_v[l], cache_a_kidx[l], cache_b_k[l], cache_b_v[l],
                   state_c_re[l], state_c_im[l], state_ffn_conv[l])
        xs, ss = _layer(xs, pos_s, prm, cache_l, False)
        st_p.append(sp)
        st_s.append(ss)
    bp, tp, _ = xp.shape
    bs, tsq, _ = xs.shape
    y_prompt = _norm(xp.reshape(bp * tp, d), normf_g, _row_tile(bp * tp)).reshape(bp, tp, d)
    y_sample = _norm(xs.reshape(bs * tsq, d), normf_g, _row_tile(bs * tsq)).reshape(bs, tsq, d)
    outs_p = [jnp.stack([s[i] for s in st_p]) for i in range(8)]
    outs_s = [jnp.stack([s[i] for s in st_s]) for i in range(8)]
    return (y_prompt, y_sample, *outs_p, *outs_s)
```

```python
import functools
import math

import jax
import jax.numpy as jnp
import numpy as np
from jax import lax
from jax.experimental import pallas as pl
from jax.experimental.pallas import tpu as pltpu

F32 = jnp.float32
BF16 = jnp.bfloat16
I32 = jnp.int32

CHUNK = 64
EPS = 1e-6
ROPE_THETA = 500000.0
HEAD_DIM = 128
A_KV = 2
A_GROUP = 4
A_ROT = HEAD_DIM // 4
IDX_HEADS = 8
IDX_DIM = 64
IDX_ROT = IDX_DIM // 4
TOPK_MAX = 256
BAND_CHUNKS = 8
REL_CLIP = 256
C_GROUP = 16
C_STATE = 64
CONV_W = 3
N_BRANCH = 3
ATTN_SCALE = HEAD_DIM ** -0.5
IDX_SCALE = (IDX_DIM * IDX_HEADS) ** -0.5

LANES = 128
SUBLANES = 8
VMEM_LIMIT = 56 * 1024 * 1024

NEG_BIG = -1e30
KEY_NEG_INF = -2139095041
KEY_POS_INF = 2139095040
INT32_MIN = -2147483648

SSM_L = SUBLANES
SSM_GB = LANES // C_GROUP
DSA_TK = 512
DSA_TA = 1024
DSA_QB = 128
BAND_QB = 128


def _cparams(sem):
    return pltpu.CompilerParams(dimension_semantics=sem, vmem_limit_bytes=VMEM_LIMIT)


def _gelu(x):
    return 0.5 * x * (1.0 + jnp.tanh(math.sqrt(2.0 / math.pi) * (x + 0.044715 * (x * x * x))))


def _sigmoid(x):
    return 1.0 / (1.0 + jnp.exp(-x))


def _rms(x, g):
    ms = jnp.mean(x * x, axis=-1, keepdims=True)
    return (x * lax.rsqrt(ms + EPS)) * g


def _proj_kernel(x_ref, g_ref, w_ref, o_ref, h_ref):
    @pl.when(pl.program_id(1) == 0)
    def _():
        h_ref[...] = _rms(x_ref[...], g_ref[...]).astype(BF16)

    o_ref[...] = jnp.dot(h_ref[...], w_ref[...], preferred_element_type=F32)


def _proj(x, g, w, tm, tn):
    t, d = x.shape
    n = w.shape[1]
    return pl.pallas_call(
        _proj_kernel,
        grid=(t // tm, n // tn),
        in_specs=[pl.BlockSpec((tm, d), lambda i, j: (i, 0)),
                  pl.BlockSpec((1, d), lambda i, j: (0, 0)),
                  pl.BlockSpec((d, tn), lambda i, j: (0, j))],
        out_specs=pl.BlockSpec((tm, tn), lambda i, j: (i, j)),
        out_shape=jax.ShapeDtypeStruct((t, n), F32),
        scratch_shapes=[pltpu.VMEM((tm, d), BF16)],
        compiler_params=_cparams(("arbitrary", "arbitrary")),
        name="proj",
    )(x, g.reshape(1, d), w)


def _rot(x, c, sa, sb, half):
    n = x.shape[-1]
    return x * c + pltpu.roll(x, n - half, 1) * sa + pltpu.roll(x, half, 1) * sb


def _rope_kernel(qa_ref, ka_ref, va_ref, qi_ref, kw_ref, tab_ref,
                 qa_o, ka_o, kab_o, vab_o, qi_o, kw_o, kib_o):
    ca, saa, sba = tab_ref[0], tab_ref[1], tab_ref[2]
    ci, sai, sbi = tab_ref[3], tab_ref[4], tab_ref[5]
    ck, sak, sbk = tab_ref[6], tab_ref[7], tab_ref[8]
    for h in range(qa_ref.shape[1] // LANES):
        sl = slice(h * LANES, (h + 1) * LANES)
        qa_o[:, sl] = _rot(qa_ref[:, sl], ca, saa, sba, A_ROT // 2).astype(BF16)
    for h in range(ka_ref.shape[1] // LANES):
        sl = slice(h * LANES, (h + 1) * LANES)
        k = _rot(ka_ref[:, sl], ca, saa, sba, A_ROT // 2)
        ka_o[:, sl] = k
        kab_o[:, sl] = k.astype(BF16)
    vab_o[...] = va_ref[...].astype(BF16)
    for h in range(qi_ref.shape[1] // LANES):
        sl = slice(h * LANES, (h + 1) * LANES)
        qi_o[:, sl] = _rot(qi_ref[:, sl], ci, sai, sbi, IDX_ROT // 2).astype(BF16)
    kw = _rot(kw_ref[...], ck, sak, sbk, IDX_ROT // 2)
    kw_o[...] = kw
    kib_o[...] = kw.astype(BF16)


def _rope(p, tabs, cols, tm):
    t = p.shape[0]
    n_tab_blocks = tabs.shape[1] // tm
    d_qa, d_kv, d_qi = A_KV * A_GROUP * HEAD_DIM, A_KV * HEAD_DIM, IDX_HEADS * IDX_DIM

    def col(width, off):
        assert off % width == 0
        return pl.BlockSpec((tm, width), lambda i: (i, off // width))

    def out(width):
        return pl.BlockSpec((tm, width), lambda i: (i, 0))

    return pl.pallas_call(
        _rope_kernel,
        grid=(t // tm,),
        in_specs=[col(d_qa, cols["qa"]), col(d_kv, cols["ka"]), col(d_kv, cols["va"]),
                  col(d_qi, cols["qi"]), col(LANES, cols["kw"]),
                  pl.BlockSpec((9, tm, LANES), lambda i: (0, i % n_tab_blocks, 0))],
        out_specs=[out(d_qa), out(d_kv), out(d_kv), out(d_kv), out(d_qi), out(LANES), out(LANES)],
        out_shape=[jax.ShapeDtypeStruct((t, d_qa), BF16),
                   jax.ShapeDtypeStruct((t, d_kv), F32),
                   jax.ShapeDtypeStruct((t, d_kv), BF16),
                   jax.ShapeDtypeStruct((t, d_kv), BF16),
                   jax.ShapeDtypeStruct((t, d_qi), BF16),
                   jax.ShapeDtypeStruct((t, LANES), F32),
                   jax.ShapeDtypeStruct((t, LANES), BF16)],
        compiler_params=_cparams(("arbitrary",)),
        name="rope",
    )(p, p, p, p, p, tabs)


def _rope_tables(pos):
    pos = pos.astype(F32)[:, None]

    def cs(rot):
        half = rot // 2
        inv = jnp.float32(ROPE_THETA) ** (-jnp.arange(half, dtype=F32) / half)
        ang = pos * inv[None, :]
        return jnp.cos(ang), jnp.sin(ang)

    def tabs(rot, width):
        half = rot // 2
        c, s = cs(rot)
        n = pos.shape[0]
        one = jnp.ones((n, width - rot), F32)
        zero = jnp.zeros((n, width - rot), F32)
        zh = jnp.zeros((n, half), F32)
        return (jnp.concatenate([c, c, one], 1), jnp.concatenate([-s, zh, zero], 1),
                jnp.concatenate([zh, s, zero], 1))

    ca, saa, sba = tabs(A_ROT, HEAD_DIM)
    c64, sa64, sb64 = tabs(IDX_ROT, IDX_DIM)
    n = pos.shape[0]
    one64, zero64 = jnp.ones((n, IDX_DIM), F32), jnp.zeros((n, IDX_DIM), F32)
    return jnp.stack([ca, saa, sba,
                      jnp.concatenate([c64, c64], 1), jnp.concatenate([sa64, sa64], 1),
                      jnp.concatenate([sb64, sb64], 1),
                      jnp.concatenate([c64, one64], 1), jnp.concatenate([sa64, zero64], 1),
                      jnp.concatenate([sb64, zero64], 1)])


def _dsa_kernel(qaT_ref, qiT_ref, wiT_ref, k_ref, vT_ref, ki_ref, o_ref, key_ref, acc_ref,
                *, s_valid, q_pos0, topk):
    qb, tk = DSA_QB, DSA_TK
    j = pl.program_id(1)
    q_first = q_pos0 + j * qb
    qpos = q_first + lax.broadcasted_iota(I32, (1, qb), 1)
    assert CHUNK == 64 and tk == 512
    n_adm = jnp.minimum(((qpos >> 6) + 1) * CHUNK, s_valid)
    n_max = jnp.minimum((((q_first + qb - 1) >> 6) + 1) * CHUNK, s_valid)
    nkt = (n_max + tk - 1) >> 9

    @pl.when(j == 0)
    def _():
        key_ref[...] = jnp.full(key_ref.shape, KEY_NEG_INF, I32)

    qi = qiT_ref[0]
    rhs = jnp.concatenate([qi[h * IDX_DIM:(h + 1) * IDX_DIM, :] for h in range(IDX_HEADS)], axis=1)
    rhs = jnp.concatenate([rhs, jnp.zeros((LANES - IDX_DIM, IDX_HEADS * qb), BF16)], axis=0)
    wi = wiT_ref[0]

    def p1(kt, carry):
        off = pl.multiple_of(kt * tk, tk)
        kid = ki_ref[0, pl.ds(off, tk), :]
        s_all = jnp.dot(kid, rhs, preferred_element_type=F32)
        acc = wi[0:1, :] * jnp.maximum(s_all[:, 0:qb], 0.0)
        for h in range(1, IDX_HEADS):
            acc = acc + wi[h:h + 1, :] * jnp.maximum(s_all[:, h * qb:(h + 1) * qb], 0.0)
        score = acc * IDX_SCALE
        kpos = off + lax.broadcasted_iota(I32, (tk, 1), 0)
        score = jnp.where(kpos < n_adm, score, -jnp.inf)
        u = pltpu.bitcast(score, I32)
        key_ref[pl.ds(off, tk), :] = u ^ (lax.shift_right_arithmetic(u, 31) & 0x7FFFFFFF)
        return carry

    lax.fori_loop(0, nkt, p1, 0)

    def count_ge(cand):
        def body(kt, cnt):
            blk = key_ref[pl.ds(pl.multiple_of(kt * tk, tk), tk), :]
            m = jnp.where(blk >= cand, 1, 0).astype(I32)
            return cnt + jnp.sum(m.reshape(tk // SUBLANES, SUBLANES, qb), axis=0)

        cnt8 = lax.fori_loop(0, nkt, body, jnp.zeros((SUBLANES, qb), I32))
        return jnp.sum(cnt8, axis=0, keepdims=True)

    def bit_body(i, prefix):
        cand_u = prefix | lax.shift_left(jnp.int32(1), 31 - i)
        cnt = count_ge(cand_u ^ INT32_MIN)
        return jnp.where(cnt >= topk, cand_u, prefix)

    prefix = lax.fori_loop(0, 32, bit_body, jnp.zeros((1, qb), I32))
    thr = prefix ^ INT32_MIN
    n_gt = count_ge(thr + 1)
    n_ge = count_ge(thr)
    take_eq = topk - n_gt
    cut_ties = jnp.max(jnp.where((n_ge > topk) & (thr > KEY_NEG_INF), 1, 0)) > 0

    @pl.when(cut_ties)
    def _():
        r = lax.broadcasted_iota(I32, (tk, tk), 0)
        c = lax.broadcasted_iota(I32, (tk, tk), 1)
        tri = jnp.where(c <= r, 1.0, 0.0).astype(BF16)
        take = take_eq.astype(F32)

        def body(kt, seen):
            off = pl.multiple_of(kt * tk, tk)
            blk = key_ref[pl.ds(off, tk), :]
            eq = blk == thr
            incl = jnp.dot(tri, jnp.where(eq, 1.0, 0.0).astype(BF16), preferred_element_type=F32)
            drop = eq & (seen + incl > take)
            key_ref[pl.ds(off, tk), :] = jnp.where(drop, thr - 1, blk)
            return seen + incl[tk - 1:tk, :]

        lax.fori_loop(0, nkt, body, jnp.zeros((1, qb), F32))

    thr_lo = jnp.maximum(thr, KEY_NEG_INF + 1)
    qa = qaT_ref[0]
    ng = A_GROUP * qb
    qg = [jnp.concatenate([qa[(g * A_GROUP + hh) * HEAD_DIM:(g * A_GROUP + hh + 1) * HEAD_DIM, :]
                           for hh in range(A_GROUP)], axis=1) for g in range(A_KV)]
    acc_ref[...] = jnp.zeros(acc_ref.shape, F32)

    c_exp = ATTN_SCALE * math.log2(math.e)

    ta = DSA_TA

    def p3(kt, carry):
        off = pl.multiple_of(kt * ta, ta)
        blk = key_ref[pl.ds(off, ta), :]
        scored = off + lax.broadcasted_iota(I32, (ta, 1), 0) < nkt * tk
        mask1 = jnp.where((blk >= thr_lo) & (blk < KEY_POS_INF) & scored, 0.0, NEG_BIG)
        mask = jnp.concatenate([mask1] * A_GROUP, axis=1)
        new = []
        for g in range(A_KV):
            m, l = carry[2 * g], carry[2 * g + 1]
            kg = k_ref[0, pl.ds(off, ta), g * HEAD_DIM:(g + 1) * HEAD_DIM]
            s = jnp.dot(kg, qg[g], preferred_element_type=F32) + mask
            m_new = jnp.maximum(m, jnp.max(s, axis=0, keepdims=True))
            alpha = jnp.exp2((m - m_new) * c_exp)
            p = jnp.exp2((s - m_new) * c_exp)
            l = alpha * l + jnp.sum(p, axis=0, keepdims=True)
            vg = vT_ref[0, g * HEAD_DIM:(g + 1) * HEAD_DIM, pl.ds(off, ta)]
            acc_ref[g] = alpha * acc_ref[g] + jnp.dot(vg, p.astype(BF16), preferred_element_type=F32)
            new += [m_new, l]
        return tuple(new)

    init = (jnp.full((1, ng), NEG_BIG, F32), jnp.zeros((1, ng), F32)) * A_KV
    fin = lax.fori_loop(0, (nkt * tk + ta - 1) // ta, p3, init)
    for g in range(A_KV):
        o = acc_ref[g] / fin[2 * g + 1]
        for hh in range(A_GROUP):
            h = g * A_GROUP + hh
            o_ref[0, h * HEAD_DIM:(h + 1) * HEAD_DIM, :] = o[:, hh * qb:(hh + 1) * qb].astype(o_ref.dtype)


def _dsa(qaT, qiT, wiT, k, vT, ki, *, s_valid, q_pos0, topk):
    b, dq, sq = qaT.shape
    s_pad = k.shape[1]
    assert s_pad % DSA_TK == 0 and s_pad % DSA_TA == 0 and sq % DSA_QB == 0 and topk <= DSA_TK
    kern = functools.partial(_dsa_kernel, s_valid=s_valid, q_pos0=q_pos0, topk=topk)
    return pl.pallas_call(
        kern,
        grid=(b, sq // DSA_QB),
        in_specs=[pl.BlockSpec((1, dq, DSA_QB), lambda i, j: (i, 0, j)),
                  pl.BlockSpec((1, qiT.shape[1], DSA_QB), lambda i, j: (i, 0, j)),
                  pl.BlockSpec((1, IDX_HEADS, DSA_QB), lambda i, j: (i, 0, j)),
                  pl.BlockSpec((1, s_pad, k.shape[2]), lambda i, j: (i, 0, 0)),
                  pl.BlockSpec((1, vT.shape[1], s_pad), lambda i, j: (i, 0, 0)),
                  pl.BlockSpec((1, s_pad, LANES), lambda i, j: (i, 0, 0))],
        out_specs=pl.BlockSpec((1, dq, DSA_QB), lambda i, j: (i, 0, j)),
        out_shape=jax.ShapeDtypeStruct((b, dq, sq), BF16),
        scratch_shapes=[pltpu.VMEM((s_pad, DSA_QB), I32),
                        pltpu.VMEM((A_KV, HEAD_DIM, A_GROUP * DSA_QB), F32)],
        compiler_params=_cparams(("arbitrary", "arbitrary")),
        name="dsa",
    )(qaT, qiT, wiT, k, vT, ki)


def _band_heads(q, k_all, v_all, bias_ref, n_hidden, o_ref):
    nh = q.shape[1] // HEAD_DIM
    tk = k_all.shape[0]
    hidden = None if n_hidden is None else lax.broadcasted_iota(I32, (1, tk), 1) < n_hidden
    for h in range(nh):
        sl = slice(h * HEAD_DIM, (h + 1) * HEAD_DIM)
        s = lax.dot_general(q[:, sl].astype(BF16), k_all[:, sl], (((1,), (1,)), ((), ())),
                            preferred_element_type=F32) * ATTN_SCALE + bias_ref[h]
        if hidden is not None:
            s = jnp.where(hidden, NEG_BIG, s)
        m = jnp.max(s, axis=1, keepdims=True)
        p = jnp.exp(s - m)
        w = (p / jnp.sum(p, axis=1, keepdims=True)).astype(BF16)
        o_ref[:, sl] = jnp.dot(w, v_all[:, sl], preferred_element_type=F32).astype(o_ref.dtype)


def _band_prompt_kernel(*refs, n_tiles):
    q_ref = refs[0]
    k_refs = refs[1:1 + n_tiles]
    v_refs = refs[1 + n_tiles:1 + 2 * n_tiles]
    bias_ref = refs[1 + 2 * n_tiles]
    o_ref = refs[2 + 2 * n_tiles]
    j = pl.program_id(1)
    n_hidden = jnp.maximum(n_tiles - 1 - j, 0) * q_ref.shape[0]
    k_all = jnp.concatenate([r[...] for r in k_refs], axis=0).astype(BF16)
    v_all = jnp.concatenate([r[...] for r in v_refs], axis=0).astype(BF16)
    _band_heads(q_ref[...], k_all, v_all, bias_ref, n_hidden, o_ref)


def _band_prompt(p, bias, cols, batch, seq):
    qb = BAND_QB
    n_tiles = BAND_CHUNKS * CHUNK // qb + 1
    width = bias.shape[0] * HEAD_DIM
    nq = seq // qb
    cq, ck, cv = cols["qb"] // width, cols["kb"] // width, cols["vb"] // width
    assert cols["qb"] % width == 0 and cols["kb"] % width == 0 and cols["vb"] % width == 0

    def kv_spec(c, i):
        return pl.BlockSpec((qb, width), lambda b, j: (b * nq + jnp.maximum(j - (n_tiles - 1) + i, 0), c))

    return pl.pallas_call(
        functools.partial(_band_prompt_kernel, n_tiles=n_tiles),
        grid=(batch, nq),
        in_specs=([pl.BlockSpec((qb, width), lambda b, j: (b * nq + j, cq))]
                  + [kv_spec(ck, i) for i in range(n_tiles)]
                  + [kv_spec(cv, i) for i in range(n_tiles)]
                  + [pl.BlockSpec(bias.shape, lambda b, j: (0, 0, 0))]),
        out_specs=pl.BlockSpec((qb, width), lambda b, j: (b * nq + j, 0)),
        out_shape=jax.ShapeDtypeStruct((batch * seq, width), BF16),
        compiler_params=_cparams(("arbitrary", "arbitrary")),
        name="band_prompt",
    )(*([p] * (1 + 2 * n_tiles)), bias)


def _band_sample_kernel(q_ref, k_ref, v_ref, ck_ref, cv_ref, bias_ref, o_ref):
    k_all = jnp.concatenate([ck_ref[0], k_ref[...]], axis=0).astype(BF16)
    v_all = jnp.concatenate([cv_ref[0], v_ref[...]], axis=0).astype(BF16)
    _band_heads(q_ref[...], k_all, v_all, bias_ref, None, o_ref)


def _band_sample(p, cache_k, cache_v, bias, cols, batch, t):
    width = bias.shape[0] * HEAD_DIM
    nbc = cache_k.shape[1]
    cq, ck, cv = cols["qb"] // width, cols["kb"] // width, cols["vb"] // width
    return pl.pallas_call(
        _band_sample_kernel,
        grid=(batch,),
        in_specs=[pl.BlockSpec((t, width), lambda b: (b, cq)),
                  pl.BlockSpec((t, width), lambda b: (b, ck)),
                  pl.BlockSpec((t, width), lambda b: (b, cv)),
                  pl.BlockSpec((1, nbc, width), lambda b: (b, 0, 0)),
                  pl.BlockSpec((1, nbc, width), lambda b: (b, 0, 0)),
                  pl.BlockSpec(bias.shape, lambda b: (0, 0, 0))],
        out_specs=pl.BlockSpec((t, width), lambda b: (b, 0)),
        out_shape=jax.ShapeDtypeStruct((batch * t, width), BF16),
        compiler_params=_cparams(("arbitrary",)),
        name="band_sample",
    )(p, p, p, cache_k, cache_v, bias)


def _band_bias(table, q0, nq, k0, nk):
    n = nq + nk
    j = jnp.arange(n)
    g = table[:, jnp.clip(j + (q0 - k0 - nk + 1), -REL_CLIP, REL_CLIP) + REL_CLIP].astype(F32)
    hank = jnp.tile(g, (1, nq + 1))[:, :nq * (n + 1)].reshape(-1, nq, n + 1)[:, :, :nk]
    bias = hank[:, :, ::-1]
    qc = (q0 + jnp.arange(nq))[:, None] // CHUNK
    kc = (k0 + jnp.arange(nk))[None, :] // CHUNK
    vis = (kc <= qc) & (kc >= qc - BAND_CHUNKS)
    return jnp.where(vis[None], bias, NEG_BIG)


def _ssm_kernel(u_ref, t_ref, p_ref, q_ref, a_ref, h0_ref, y_ref, hre_o, him_o, sre_ref, sim_ref,
                *, seg, tiles_per_seq):
    L = SSM_L
    x = jnp.concatenate([u_ref[:, s, :] for s in range(L)], axis=1).astype(BF16)
    sre_ref[...] = jnp.dot(x, p_ref[0, 0], preferred_element_type=F32)
    sim_ref[...] = jnp.dot(x, p_ref[0, 1], preferred_element_type=F32)
    rows, w = sre_ref.shape
    kseg = lax.broadcasted_iota(I32, (SUBLANES, 1), 0) % seg
    apk_re, apk_im = a_ref[0, 0, 0:SUBLANES, :], a_ref[0, 1, 0:SUBLANES, :]

    def cmad(x_re, x_im, a_re, a_im, y_re, y_im):
        return x_re + a_re * y_re - a_im * y_im, x_im + a_re * y_im + a_im * y_re

    def tile(i, carry):
        r = pl.ds(pl.multiple_of(i * SUBLANES, SUBLANES), SUBLANES)
        x_re, x_im = sre_ref[r, :], sim_ref[r, :]
        for n, d in enumerate((1, 2, 4)):
            if d < seg:
                ad_re = a_ref[0, 0, SUBLANES + n:SUBLANES + n + 1, :]
                ad_im = a_ref[0, 1, SUBLANES + n:SUBLANES + n + 1, :]
                sh_re = jnp.where(kseg >= d, pltpu.roll(x_re, d, 0), 0.0)
                sh_im = jnp.where(kseg >= d, pltpu.roll(x_im, d, 0), 0.0)
                x_re, x_im = cmad(x_re, x_im, ad_re, ad_im, sh_re, sh_im)
        if tiles_per_seq > 1:
            first = (i % tiles_per_seq) == 0
            hin_re = jnp.where(first, h0_ref[0, 0], carry[0])
            hin_im = jnp.where(first, h0_ref[0, 1], carry[1])
        else:
            hin_re, hin_im = h0_ref[0, 0, r, :], h0_ref[0, 1, r, :]
        inc_re, inc_im = cmad(x_re, x_im, apk_re, apk_im, hin_re, hin_im)
        sre_ref[r, :] = jnp.where(kseg == 0, hin_re, pltpu.roll(inc_re, 1, 0))
        sim_ref[r, :] = jnp.where(kseg == 0, hin_im, pltpu.roll(inc_im, 1, 0))
        hre_o[0, r, :] = inc_re
        him_o[0, r, :] = inc_im
        return inc_re[SUBLANES - 1:SUBLANES, :], inc_im[SUBLANES - 1:SUBLANES, :]

    zero = jnp.zeros((1, w), F32)
    lax.fori_loop(0, rows // SUBLANES, tile, (zero, zero))
    y = (jnp.dot(x, t_ref[0], preferred_element_type=F32)
         + jnp.dot(sre_ref[...].astype(BF16), q_ref[0, 0], preferred_element_type=F32)
         + jnp.dot(sim_ref[...].astype(BF16), q_ref[0, 1], preferred_element_type=F32))
    for t in range(L):
        y_ref[:, t, :] = y[:, t * LANES:(t + 1) * LANES]


def _ssm(p3, uc_col, tmat, pmat, qmat, apow, h0, rows_blk, seg, tiles_per_seq):
    rows_total = p3.shape[0]
    nb, _, _, w = pmat.shape
    kl = SSM_L * LANES
    h0_rows = h0.shape[2] // (rows_total // rows_blk)
    hspec = pl.BlockSpec((1, rows_blk, w), lambda i, r: (i, r, 0))
    return pl.pallas_call(
        functools.partial(_ssm_kernel, seg=seg, tiles_per_seq=tiles_per_seq),
        grid=(nb, rows_total // rows_blk),
        in_specs=[pl.BlockSpec((rows_blk, SSM_L, LANES), lambda i, r: (r, 0, uc_col // LANES + i)),
                  pl.BlockSpec((1, kl, kl), lambda i, r: (i, 0, 0)),
                  pl.BlockSpec((1, 2, kl, w), lambda i, r: (i, 0, 0, 0)),
                  pl.BlockSpec((1, 2, w, kl), lambda i, r: (i, 0, 0, 0)),
                  pl.BlockSpec((1, 2, 2 * SUBLANES, w), lambda i, r: (i, 0, 0, 0)),
                  pl.BlockSpec((1, 2, h0_rows, w), lambda i, r: (i, 0, r, 0))],
        out_specs=[pl.BlockSpec((rows_blk, SSM_L, LANES), lambda i, r: (r, 0, i)), hspec, hspec],
        out_shape=[jax.ShapeDtypeStruct((rows_total, SSM_L, nb * LANES), F32),
                   jax.ShapeDtypeStruct((nb, rows_total, w), F32),
                   jax.ShapeDtypeStruct((nb, rows_total, w), F32)],
        scratch_shapes=[pltpu.VMEM((rows_blk, w), F32), pltpu.VMEM((rows_blk, w), F32)],
        compiler_params=_cparams(("arbitrary", "arbitrary")),
        name="ssm",
    )(p3, tmat, pmat, qmat, apow, h0)


def _ssm_weights(a_re, a_im, log_dt, b_re, b_im, c_re, c_im, d):
    hp = lax.Precision.HIGHEST
    g, p = a_re.shape
    gc = b_re.shape[2]
    L = SSM_L
    a_re, a_im = a_re.astype(F32), a_im.astype(F32)
    dt = jnp.exp(log_dt.astype(F32))[:, None]
    mag = jnp.exp(dt * a_re)
    ab_re, ab_im = mag * jnp.cos(dt * a_im), mag * jnp.sin(dt * a_im)
    n_re = ab_re - 1.0
    n_im = ab_im
    den = a_re * a_re + a_im * a_im
    cc_re = (n_re * a_re + n_im * a_im) / den
    cc_im = (n_im * a_re - n_re * a_im) / den
    b_re, b_im = b_re.astype(F32), b_im.astype(F32)
    bb_re = cc_re[..., None] * b_re - cc_im[..., None] * b_im
    bb_im = cc_re[..., None] * b_im + cc_im[..., None] * b_re
    c_re, c_im = c_re.astype(F32), c_im.astype(F32)

    def cmul(x, y):
        return x[0] * y[0] - x[1] * y[1], x[0] * y[1] + x[1] * y[0]

    rep = lambda a: jnp.broadcast_to(a[None], (L,) + a.shape)
    pw_re, pw_im = lax.associative_scan(cmul, (rep(ab_re), rep(ab_im)), axis=0)
    pw_re = jnp.concatenate([jnp.ones((1, g, p), F32), pw_re], axis=0)
    pw_im = jnp.concatenate([jnp.zeros((1, g, p), F32), pw_im], axis=0)
    ca_re = c_re[None] * pw_re[:, :, None, :] - c_im[None] * pw_im[:, :, None, :]
    ca_im = c_re[None] * pw_im[:, :, None, :] + c_im[None] * pw_re[:, :, None, :]
    taps = (jnp.einsum("tgop,gpi->tgoi", ca_re[:L], bb_re, precision=hp)
            - jnp.einsum("tgop,gpi->tgoi", ca_im[:L], bb_im, precision=hp))
    taps = taps.at[0].add(jax.vmap(jnp.diag)(d.astype(F32)))
    s_idx = jnp.arange(L)[:, None]
    t_idx = jnp.arange(L)[None, :]
    lag = t_idx - s_idx
    tt = jnp.where((lag >= 0)[:, :, None, None, None], taps[jnp.clip(lag, 0, L - 1)], 0.0)
    gb = SSM_GB
    nb = g // gb
    n_x = L * gb * gc
    x = jnp.arange(n_x)
    x_grp, x_loc = (x // gc) % gb, (x // (gb * gc)) * gc + x % gc
    place_x = ((x_grp[None, :, None] == jnp.arange(gb)[:, None, None])
               & (x_loc[None, :, None] == jnp.arange(L * gc)[None, None, :])).astype(F32)
    q = jnp.arange(gb * p)
    place_q = ((q[None, :, None] // p == jnp.arange(gb)[:, None, None])
               & (q[None, :, None] % p == jnp.arange(p)[None, None, :])).astype(F32)
    blk = lambda a: a.reshape((nb, gb) + a.shape[1:])
    t_grp = blk(tt.transpose(2, 0, 4, 1, 3).reshape(g, L * gc, L * gc))
    tmat = jnp.einsum("gxr,bgrc,gyc->bxy", place_x, t_grp, place_x)
    rev = L - 1 - jnp.arange(L)
    pin_re = pw_re[rev][:, :, :, None] * bb_re[None] - pw_im[rev][:, :, :, None] * bb_im[None]
    pin_im = pw_re[rev][:, :, :, None] * bb_im[None] + pw_im[rev][:, :, :, None] * bb_re[None]
    to_p = lambda a: jnp.einsum("gxr,bgrp,gqp->bxq", place_x,
                                blk(a.transpose(1, 0, 3, 2).reshape(g, L * gc, p)), place_q)
    to_q = lambda a: jnp.einsum("gqp,bgpc,gyc->bqy", place_q,
                                blk(a.transpose(1, 3, 0, 2).reshape(g, p, L * gc)), place_x)
    pmat = jnp.stack([to_p(pin_re), to_p(pin_im)], axis=1)
    qmat = jnp.stack([to_q(ca_re[1:]), -to_q(ca_im[1:])], axis=1)
    rep8 = lambda a: jnp.broadcast_to(a[None], (SUBLANES,) + a.shape)
    apl_re, apl_im = lax.associative_scan(cmul, (rep8(pw_re[L]), rep8(pw_im[L])), axis=0)
    apl = jnp.stack([apl_re, apl_im]).reshape(2, SUBLANES, nb, gb * p)
    return tmat.astype(BF16), pmat.astype(BF16), qmat.astype(BF16), apl


def _ssm_apply(p, uc_col, b, t, h0_re, h0_im, weights, one_seq_per_step):
    tmat, pmat, qmat, apl = weights
    nb, w = pmat.shape[0], pmat.shape[3]
    L = SSM_L
    nc = t // L
    assert t % L == 0 and (nc % SUBLANES == 0 or SUBLANES % nc == 0)
    seg = min(nc, SUBLANES)
    if one_seq_per_step:
        assert nc % SUBLANES == 0
        rows_blk, tiles_per_seq, reps = nc, nc // SUBLANES, SUBLANES
    else:
        rows_blk, tiles_per_seq, reps = b * nc, 1, nc
        assert nc <= SUBLANES and rows_blk % SUBLANES == 0
    pw_rows = np.array([k % seg for k in range(SUBLANES)] + [0, 1, 3] + [0] * (SUBLANES - 3))
    apow = apl[:, pw_rows].transpose(2, 0, 1, 3)

    def pack(h):
        h = h.astype(F32).reshape(b, nb, w).transpose(1, 0, 2)
        return jnp.repeat(h, reps, axis=1)

    h0 = jnp.stack([pack(h0_re), pack(h0_im)], axis=1)
    p3 = p.reshape(b * nc, L, p.shape[1])
    y, hre, him = _ssm(p3, uc_col, tmat, pmat, qmat, apow, h0, rows_blk, seg, tiles_per_seq)

    def last(h):
        return h.reshape(nb, b, nc, w)[:, :, nc - 1].transpose(1, 0, 2).reshape(b, nb * SSM_GB, C_STATE)

    return y.reshape(b * t, nb * LANES), last(hre), last(him)


def _glu_kernel(y_ref, w_ref, b_ref, o_ref):
    z = _gelu(y_ref[...])
    a = jnp.dot(z.astype(BF16), w_ref[...], preferred_element_type=F32) + b_ref[...]
    o_ref[...] = (z * _sigmoid(a)).astype(o_ref.dtype)


def _glu(y, w, b, tm):
    t, n = y.shape
    return pl.pallas_call(
        _glu_kernel,
        grid=(t // tm,),
        in_specs=[pl.BlockSpec((tm, n), lambda i: (i, 0)),
                  pl.BlockSpec((n, n), lambda i: (0, 0)),
                  pl.BlockSpec((1, n), lambda i: (0, 0))],
        out_specs=pl.BlockSpec((tm, n), lambda i: (i, 0)),
        out_shape=jax.ShapeDtypeStruct((t, n), BF16),
        compiler_params=_cparams(("arbitrary",)),
        name="glu",
    )(y, w, b.reshape(1, n))


def _merge_kernel(oa_ref, ob_ref, oc_ref, ga_ref, gb_ref, gc_ref, w_ref, o_ref):
    acc = None
    for n, (o, g) in enumerate(((oa_ref, ga_ref), (ob_ref, gb_ref), (oc_ref, gc_ref))):
        br = jnp.dot(o[...], w_ref[n], preferred_element_type=F32)
        t = _sigmoid(g[...]) * br
        acc = t if acc is None else acc + t
    o_ref[...] = acc.astype(o_ref.dtype)


def _merge(oa, ob, oc, p, w, g_col, tm, tn):
    t, m = oa.shape
    d = w.shape[2]
    assert g_col % tn == 0 and d % tn == 0

    def gspec(n):
        return pl.BlockSpec((tm, tn), lambda i, j: (i, (g_col + n * d) // tn + j))

    ospec = pl.BlockSpec((tm, m), lambda i, j: (i, 0))
    return pl.pallas_call(
        _merge_kernel,
        grid=(t // tm, d // tn),
        in_specs=[ospec, ospec, ospec, gspec(0), gspec(1), gspec(2),
                  pl.BlockSpec((N_BRANCH, m, tn), lambda i, j: (0, 0, j))],
        out_specs=pl.BlockSpec((tm, tn), lambda i, j: (i, j)),
        out_shape=jax.ShapeDtypeStruct((t, d), BF16),
        compiler_params=_cparams(("arbitrary", "arbitrary")),
        name="merge",
    )(oa, ob, oc, p, p, p, w)


def _mmres_kernel(a_ref, w_ref, r_ref, o_ref):
    o_ref[...] = r_ref[...] + jnp.dot(a_ref[...], w_ref[...], preferred_element_type=F32)


def _mmres(a, w, res, tm, tn):
    t, k = a.shape
    n = w.shape[1]
    return pl.pallas_call(
        _mmres_kernel,
        grid=(t // tm, n // tn),
        in_specs=[pl.BlockSpec((tm, k), lambda i, j: (i, 0)),
                  pl.BlockSpec((k, tn), lambda i, j: (0, j)),
                  pl.BlockSpec((tm, tn), lambda i, j: (i, j))],
        out_specs=pl.BlockSpec((tm, tn), lambda i, j: (i, j)),
        out_shape=jax.ShapeDtypeStruct((t, n), F32),
        compiler_params=_cparams(("arbitrary", "arbitrary")),
        name="mmres",
    )(a, w, res)


def _ffn_in_kernel(x_ref, g_ref, wu_ref, wv_ref, wc_ref, bc_ref, st_ref, o_ref, tail_ref,
                   h_ref, up_ref, carry_ref, *, ns, ts, tiles_per_seq):
    i, j = pl.program_id(0), pl.program_id(1)
    pad = SUBLANES

    @pl.when(j == 0)
    def _():
        h_ref[...] = _rms(x_ref[...], g_ref[...]).astype(BF16)

    @pl.when((i == 0) & (j == 0))
    def _():
        carry_ref[...] = jnp.zeros(carry_ref.shape, F32)

    h = h_ref[...]
    u = jnp.dot(h, wu_ref[...], preferred_element_type=F32)
    tn = u.shape[1]
    up_ref[:, pad:, :] = u.reshape(ns, ts, tn)
    first = (i % tiles_per_seq) == 0
    up_ref[:, pad - 2:pad, :] = jnp.where(first, st_ref[...], carry_ref[j])
    last2 = up_ref[:, ts + pad - 2:ts + pad, :]
    carry_ref[j] = last2
    tail_ref[0] = last2
    c = bc_ref[...].reshape(1, 1, tn)
    for tap in range(CONV_W):
        c = c + up_ref[:, pad - 2 + tap:pad - 2 + tap + ts, :] * wc_ref[tap:tap + 1, :].reshape(1, 1, tn)
    gl = _gelu(c).reshape(ns * ts, tn)
    v = jnp.dot(h, wv_ref[...], preferred_element_type=F32)
    o_ref[...] = (gl * v).astype(o_ref.dtype)


def _ffn_in(x, g, w, wc, bc, state, ns, ts, tn):
    t, d = x.shape
    f = w.shape[1] // 2
    tm = ns * ts
    nj = f // tn
    seq_len = t // state.shape[0]
    tiles_per_seq = max(seq_len // tm, 1)
    kern = functools.partial(_ffn_in_kernel, ns=ns, ts=ts, tiles_per_seq=tiles_per_seq)
    return pl.pallas_call(
        kern,
        grid=(t // tm, nj),
        in_specs=[pl.BlockSpec((tm, d), lambda i, j: (i, 0)),
                  pl.BlockSpec((1, d), lambda i, j: (0, 0)),
                  pl.BlockSpec((d, tn), lambda i, j: (0, j)),
                  pl.BlockSpec((d, tn), lambda i, j: (0, nj + j)),
                  pl.BlockSpec((CONV_W, tn), lambda i, j: (0, j)),
                  pl.BlockSpec((1, tn), lambda i, j: (0, j)),
                  pl.BlockSpec((ns, CONV_W - 1, tn), lambda i, j: (i // tiles_per_seq, 0, j))],
        out_specs=[pl.BlockSpec((tm, tn), lambda i, j: (i, j)),
                   pl.BlockSpec((1, ns, CONV_W - 1, tn), lambda i, j: (i, 0, 0, j))],
        out_shape=[jax.ShapeDtypeStruct((t, f), BF16),
                   jax.ShapeDtypeStruct((t // tm, ns, CONV_W - 1, f), F32)],
        scratch_shapes=[pltpu.VMEM((tm, d), BF16),
                        pltpu.VMEM((ns, ts + SUBLANES, tn), F32),
                        pltpu.VMEM((nj, ns, CONV_W - 1, tn), F32)],
        compiler_params=_cparams(("arbitrary", "arbitrary")),
        name="ffn_in",
    )(x, g.reshape(1, d), w, w, wc, bc.reshape(1, f), state)


def _norm_kernel(x_ref, g_ref, o_ref):
    o_ref[...] = _rms(x_ref[...], g_ref[...])


def _norm(x, g, tm):
    t, d = x.shape
    return pl.pallas_call(
        _norm_kernel,
        grid=(t // tm,),
        in_specs=[pl.BlockSpec((tm, d), lambda i: (i, 0)), pl.BlockSpec((1, d), lambda i: (0, 0))],
        out_specs=pl.BlockSpec((tm, d), lambda i: (i, 0)),
        out_shape=jax.ShapeDtypeStruct((t, d), F32),
        compiler_params=_cparams(("arbitrary",)),
        name="final_norm",
    )(x, g.reshape(1, d))


def _row_tile(t, cap=1024):
    tm = min(t, cap)
    while t % tm:
        tm //= 2
    return tm


def _pack_w_in(w_in, d_model):
    mix = d_model // 2
    sizes = dict(qa=mix, ka=A_KV * HEAD_DIM, va=A_KV * HEAD_DIM, qi=IDX_HEADS * IDX_DIM, ki=IDX_DIM,
                 wi=IDX_HEADS, qb=mix, kb=mix, vb=mix, uc=mix, g=N_BRANCH * d_model)
    src_order = ["qa", "ka", "va", "qi", "ki", "wi", "qb", "kb", "vb", "uc", "g"]
    src, off = {}, 0
    for name in src_order:
        src[name] = (off, sizes[name])
        off += sizes[name]
    assert off == w_in.shape[1]
    dst_order = ["qa", "qb", "kb", "vb", "uc", "g", "qi", "ka", "va", "ki", "wi"]
    cols, parts, off = {}, [], 0
    for name in dst_order:
        s, n = src[name]
        cols[name] = off
        parts.append(w_in[:, s:s + n])
        off += n
    cols["kw"] = cols["ki"]
    tn = 512
    total = -(-off // tn) * tn
    parts.append(jnp.zeros((w_in.shape[0], total - off), w_in.dtype))
    return jnp.concatenate(parts, axis=1).astype(BF16), cols, tn


def _to_heads_T(x, b, t):
    return x.reshape(b, t, x.shape[1]).transpose(0, 2, 1)


def _pad_axis(x, axis, size):
    pad = [(0, 0)] * x.ndim
    pad[axis] = (0, size - x.shape[axis])
    return jnp.pad(x, pad)


def _layer(x, pos, prm, cache, is_prompt):
    b, t, d = x.shape
    mix = d // 2
    xt = x.reshape(b * t, d)
    tm = _row_tile(b * t)
    cols = prm["cols"]
    p = _proj(xt, prm["norm1"], prm["w_in"], tm, prm["w_in_tn"])

    tabs = _rope_tables(pos)
    tm_r = _row_tile(t) if is_prompt else tm
    if not is_prompt:
        tabs = jnp.tile(tabs, (1, b, 1))
    qa, ka, kab, vab, qi, kw, kib = _rope(p, tabs, cols, tm_r)
    if is_prompt:
        sq, s_valid, q_pos0 = t, t, 0
        k_all, v_all, ki_all = kab.reshape(b, t, -1), vab.reshape(b, t, -1), kib.reshape(b, t, LANES)
    else:
        cak, cav, caki = cache[0], cache[1], cache[2]
        past = cak.shape[1]
        sq, s_valid, q_pos0 = DSA_QB, past + t, past
        k_all = jnp.concatenate([cak.reshape(b, past, -1).astype(BF16), kab.reshape(b, t, -1)], axis=1)
        v_all = jnp.concatenate([cav.reshape(b, past, -1).astype(BF16), vab.reshape(b, t, -1)], axis=1)
        ki_new = kib.reshape(b, t, LANES)
        ki_all = jnp.concatenate([_pad_axis(caki.astype(BF16), 2, LANES), ki_new], axis=1)
    s_pad = -(-s_valid // DSA_TA) * DSA_TA
    k_all = _pad_axis(k_all, 1, s_pad)
    vT = _pad_axis(v_all, 1, s_pad).transpose(0, 2, 1)
    ki_all = _pad_axis(ki_all, 1, s_pad)
    qaT = _pad_axis(_to_heads_T(qa, b, t), 2, sq)
    qiT = _pad_axis(_to_heads_T(qi, b, t), 2, sq)
    wiT = _pad_axis(_to_heads_T(kw[:, IDX_DIM:IDX_DIM + IDX_HEADS], b, t), 2, sq)
    topk = min(TOPK_MAX, s_valid // 4)
    oaT = _dsa(qaT, qiT, wiT, k_all, vT, ki_all, s_valid=s_valid, q_pos0=q_pos0, topk=topk)
    oa = oaT[:, :, :t].transpose(0, 2, 1).reshape(b * t, mix)

    nh = mix // HEAD_DIM
    if is_prompt:
        n_tiles = BAND_CHUNKS * CHUNK // BAND_QB + 1
        bias = _band_bias(prm["rel_bias"], BAND_CHUNKS * CHUNK, BAND_QB, 0, n_tiles * BAND_QB)
        ob = _band_prompt(p, bias, cols, b, t)
    else:
        cbk, cbv = cache[3], cache[4]
        nbc = cbk.shape[1]
        past = cache[0].shape[1]
        bias = _band_bias(prm["rel_bias"], past, t, past - nbc, nbc + t)
        ob = _band_sample(p, cbk.reshape(b, nbc, nh * HEAD_DIM), cbv.reshape(b, nbc, nh * HEAD_DIM),
                          bias, cols, b, t)

    g_ssm = mix // C_GROUP
    if is_prompt:
        h0_re = jnp.zeros((b, g_ssm, C_STATE), F32)
        h0_im = h0_re
    else:
        h0_re, h0_im = cache[5], cache[6]
    yc, hr, hi = _ssm_apply(p, cols["uc"], b, t, h0_re, h0_im, prm["ssm"], is_prompt)
    oc = _glu(yc, prm["w_glu"], prm["b_glu"], tm)

    merged = _merge(oa, ob, oc, p, prm["w_branch"], cols["g"], tm, 512)
    x1 = _mmres(merged, prm["w_out"], xt, tm, 1024)

    f = prm["w_conv"].shape[1]
    if is_prompt:
        state = jnp.zeros((b, CONV_W - 1, f), F32)
        ns, ts = 1, _row_tile(t)
    else:
        state = cache[7].astype(F32)
        ns, ts = b, t
    act, tails = _ffn_in(x1, prm["norm2"], prm["w_ffn_in"], prm["w_conv"], prm["b_conv"], state, ns, ts, 512)
    x2 = _mmres(act, prm["w_down"], x1, _row_tile(b * t, 512), 512)
    if is_prompt:
        buf = tails.reshape(b, t // ts, CONV_W - 1, f)[:, -1]
    else:
        buf = tails[0]

    ka4 = ka.reshape(b, t, A_KV, HEAD_DIM)
    va4 = p[:, cols["va"]:cols["va"] + A_KV * HEAD_DIM].reshape(b, t, A_KV, HEAD_DIM)
    ki3 = kw[:, :IDX_DIM].reshape(b, t, IDX_DIM)
    nb = min(BAND_CHUNKS * CHUNK, t) if is_prompt else t
    pb = p.reshape(b, t, p.shape[1])[:, t - nb:]
    kb4 = pb[:, :, cols["kb"]:cols["kb"] + mix].reshape(b, nb, nh, HEAD_DIM)
    vb4 = pb[:, :, cols["vb"]:cols["vb"] + mix].reshape(b, nb, nh, HEAD_DIM)
    return x2.reshape(b, t, d), (ka4, va4, ki3, kb4, vb4, hr, hi, buf)


def kernel(x_prompt, x_sample, cache_a_k, cache_a_v, cache_a_kidx, cache_b_k, cache_b_v, state_c_re, state_c_im,
           state_ffn_conv, norm1_g, w_in, rel_bias, ssm_a_re, ssm_a_im, ssm_log_dt, ssm_b_re, ssm_b_im, ssm_c_re,
           ssm_c_im, ssm_d, w_glu, b_glu, w_branch, w_out, norm2_g, w_ffn_in, w_ffn_conv, b_ffn_conv, w_ffn_down,
           normf_g):
    depth = w_in.shape[0]
    d = x_prompt.shape[2]
    pos_p = jnp.arange(x_prompt.shape[1])
    pos_s = cache_a_k.shape[2] + jnp.arange(x_sample.shape[1])
    xp, xs = x_prompt, x_sample
    st_p, st_s = [], []
    for l in range(depth):
        w_in_l, cols, tn = _pack_w_in(w_in[l], d)
        prm = dict(norm1=norm1_g[l], w_in=w_in_l, cols=cols, w_in_tn=tn, rel_bias=rel_bias[l],
                   ssm=_ssm_weights(ssm_a_re[l], ssm_a_im[l], ssm_log_dt[l], ssm_b_re[l], ssm_b_im[l],
                                    ssm_c_re[l], ssm_c_im[l], ssm_d[l]),
                   w_glu=w_glu[l].astype(BF16), b_glu=b_glu[l], w_branch=w_branch[l].astype(BF16),
                   w_out=w_out[l].astype(BF16), norm2=norm2_g[l], w_ffn_in=w_ffn_in[l].astype(BF16),
                   w_conv=w_ffn_conv[l], b_conv=b_ffn_conv[l], w_down=w_ffn_down[l].astype(BF16))
        xp, sp = _layer(xp, pos_p, prm, None, True)
        cache_l = (cache_a_k[l], cache_a_v[l], cache_a_kidx[l], cache_b_k[l], cache_b_v[l],
                   state_c_re[l], state_c_im[l], state_ffn_conv[l])
        xs, ss = _layer(xs, pos_s, prm, cache_l, False)
        st_p.append(sp)
        st_s.append(ss)
    bp, tp, _ = xp.shape
    bs, tsq, _ = xs.shape
    y_prompt = _norm(xp.reshape(bp * tp, d), normf_g, _row_tile(bp * tp)).reshape(bp, tp, d)
    y_sample = _norm(xs.reshape(bs * tsq, d), normf_g, _row_tile(bs * tsq)).reshape(bs, tsq, d)
    outs_p = [jnp.stack([s[i] for s in st_p]) for i in range(8)]
    outs_s = [jnp.stack([s[i] for s in st_s]) for i in range(8)]
    return (y_prompt, y_sample, *outs_p, *outs_s)
```

```python
import functools
import math

import jax
import jax.numpy as jnp
import numpy as np
from jax import lax
from jax.experimental import pallas as pl
from jax.experimental.pallas import tpu as pltpu

F32 = jnp.float32
BF16 = jnp.bfloat16
I32 = jnp.int32

CHUNK = 64
EPS = 1e-6
ROPE_THETA = 500000.0
HEAD_DIM = 128
A_KV = 2
A_GROUP = 4
A_ROT = HEAD_DIM // 4
IDX_HEADS = 8
IDX_DIM = 64
IDX_ROT = IDX_DIM // 4
TOPK_MAX = 256
BAND_CHUNKS = 8
REL_CLIP = 256
C_GROUP = 16
C_STATE = 64
CONV_W = 3
N_BRANCH = 3
ATTN_SCALE = HEAD_DIM ** -0.5
IDX_SCALE = (IDX_DIM * IDX_HEADS) ** -0.5

LANES = 128
SUBLANES = 8
VMEM_LIMIT = 56 * 1024 * 1024

NEG_BIG = -1e30
KEY_NEG_INF = -2139095041
KEY_POS_INF = 2139095040
INT32_MIN = -2147483648

SSM_L = SUBLANES
SSM_GB = LANES // C_GROUP
DSA_TK = 512
DSA_TA = 512
DSA_VROWS = HEAD_DIM + 16
DSA_QB = 128
BAND_QB = 128


def _cparams(sem):
    return pltpu.CompilerParams(dimension_semantics=sem, vmem_limit_bytes=VMEM_LIMIT)


def _gelu(x):
    return 0.5 * x * (1.0 + jnp.tanh(math.sqrt(2.0 / math.pi) * (x + 0.044715 * (x * x * x))))


def _sigmoid(x):
    return 1.0 / (1.0 + jnp.exp(-x))


def _rms(x, g):
    ms = jnp.mean(x * x, axis=-1, keepdims=True)
    return (x * lax.rsqrt(ms + EPS)) * g


def _proj_kernel(x_ref, g_ref, w_ref, o_ref, h_ref):
    @pl.when(pl.program_id(1) == 0)
    def _():
        h_ref[...] = _rms(x_ref[...], g_ref[...]).astype(BF16)

    o_ref[...] = jnp.dot(h_ref[...], w_ref[...], preferred_element_type=F32)


def _proj(x, g, w, tm, tn):
    t, d = x.shape
    n = w.shape[1]
    return pl.pallas_call(
        _proj_kernel,
        grid=(t // tm, n // tn),
        in_specs=[pl.BlockSpec((tm, d), lambda i, j: (i, 0)),
                  pl.BlockSpec((1, d), lambda i, j: (0, 0)),
                  pl.BlockSpec((d, tn), lambda i, j: (0, j))],
        out_specs=pl.BlockSpec((tm, tn), lambda i, j: (i, j)),
        out_shape=jax.ShapeDtypeStruct((t, n), F32),
        scratch_shapes=[pltpu.VMEM((tm, d), BF16)],
        compiler_params=_cparams(("arbitrary", "arbitrary")),
        name="proj",
    )(x, g.reshape(1, d), w)


def _rot(x, c, sa, sb, half):
    n = x.shape[-1]
    return x * c + pltpu.roll(x, n - half, 1) * sa + pltpu.roll(x, half, 1) * sb


def _rope_kernel(qa_ref, ka_ref, va_ref, qi_ref, kw_ref, tab_ref,
                 qa_o, ka_o, kab_o, vab_o, qi_o, kw_o, kib_o):
    ca, saa, sba = tab_ref[0], tab_ref[1], tab_ref[2]
    ci, sai, sbi = tab_ref[3], tab_ref[4], tab_ref[5]
    ck, sak, sbk = tab_ref[6], tab_ref[7], tab_ref[8]
    for h in range(qa_ref.shape[1] // LANES):
        sl = slice(h * LANES, (h + 1) * LANES)
        qa_o[:, sl] = _rot(qa_ref[:, sl], ca, saa, sba, A_ROT // 2).astype(BF16)
    for h in range(ka_ref.shape[1] // LANES):
        sl = slice(h * LANES, (h + 1) * LANES)
        k = _rot(ka_ref[:, sl], ca, saa, sba, A_ROT // 2)
        ka_o[:, sl] = k
        kab_o[:, sl] = k.astype(BF16)
    vab_o[...] = va_ref[...].astype(BF16)
    for h in range(qi_ref.shape[1] // LANES):
        sl = slice(h * LANES, (h + 1) * LANES)
        qi_o[:, sl] = _rot(qi_ref[:, sl], ci, sai, sbi, IDX_ROT // 2).astype(BF16)
    kw = _rot(kw_ref[...], ck, sak, sbk, IDX_ROT // 2)
    kw_o[...] = kw
    kib_o[...] = kw.astype(BF16)


def _rope(p, tabs, cols, tm):
    t = p.shape[0]
    n_tab_blocks = tabs.shape[1] // tm
    d_qa, d_kv, d_qi = A_KV * A_GROUP * HEAD_DIM, A_KV * HEAD_DIM, IDX_HEADS * IDX_DIM

    def col(width, off):
        assert off % width == 0
        return pl.BlockSpec((tm, width), lambda i: (i, off // width))

    def out(width):
        return pl.BlockSpec((tm, width), lambda i: (i, 0))

    return pl.pallas_call(
        _rope_kernel,
        grid=(t // tm,),
        in_specs=[col(d_qa, cols["qa"]), col(d_kv, cols["ka"]), col(d_kv, cols["va"]),
                  col(d_qi, cols["qi"]), col(LANES, cols["kw"]),
                  pl.BlockSpec((9, tm, LANES), lambda i: (0, i % n_tab_blocks, 0))],
        out_specs=[out(d_qa), out(d_kv), out(d_kv), out(d_kv), out(d_qi), out(LANES), out(LANES)],
        out_shape=[jax.ShapeDtypeStruct((t, d_qa), BF16),
                   jax.ShapeDtypeStruct((t, d_kv), F32),
                   jax.ShapeDtypeStruct((t, d_kv), BF16),
                   jax.ShapeDtypeStruct((t, d_kv), BF16),
                   jax.ShapeDtypeStruct((t, d_qi), BF16),
                   jax.ShapeDtypeStruct((t, LANES), F32),
                   jax.ShapeDtypeStruct((t, LANES), BF16)],
        compiler_params=_cparams(("arbitrary",)),
        name="rope",
    )(p, p, p, p, p, tabs)


def _rope_tables(pos):
    pos = pos.astype(F32)[:, None]

    def cs(rot):
        half = rot // 2
        inv = jnp.float32(ROPE_THETA) ** (-jnp.arange(half, dtype=F32) / half)
        ang = pos * inv[None, :]
        return jnp.cos(ang), jnp.sin(ang)

    def tabs(rot, width):
        half = rot // 2
        c, s = cs(rot)
        n = pos.shape[0]
        one = jnp.ones((n, width - rot), F32)
        zero = jnp.zeros((n, width - rot), F32)
        zh = jnp.zeros((n, half), F32)
        return (jnp.concatenate([c, c, one], 1), jnp.concatenate([-s, zh, zero], 1),
                jnp.concatenate([zh, s, zero], 1))

    ca, saa, sba = tabs(A_ROT, HEAD_DIM)
    c64, sa64, sb64 = tabs(IDX_ROT, IDX_DIM)
    n = pos.shape[0]
    one64, zero64 = jnp.ones((n, IDX_DIM), F32), jnp.zeros((n, IDX_DIM), F32)
    return jnp.stack([ca, saa, sba,
                      jnp.concatenate([c64, c64], 1), jnp.concatenate([sa64, sa64], 1),
                      jnp.concatenate([sb64, sb64], 1),
                      jnp.concatenate([c64, one64], 1), jnp.concatenate([sa64, zero64], 1),
                      jnp.concatenate([sb64, zero64], 1)])


def _dsa_kernel(qaT_ref, qiT_ref, wiT_ref, k_ref, vT_ref, ki_ref, o_ref, key_ref, acc_ref,
                s0_ref, s1_ref, p0_ref, p1_ref, a0_ref, a1_ref,
                *, s_valid, q_pos0, topk):
    qb, tk = DSA_QB, DSA_TK
    j = pl.program_id(1)
    q_first = q_pos0 + j * qb
    qpos = q_first + lax.broadcasted_iota(I32, (1, qb), 1)
    assert CHUNK == 64 and tk == 512
    n_adm = jnp.minimum(((qpos >> 6) + 1) * CHUNK, s_valid)
    n_max = jnp.minimum((((q_first + qb - 1) >> 6) + 1) * CHUNK, s_valid)
    nkt = (n_max + tk - 1) >> 9

    qi = qiT_ref[0]
    rhs = jnp.concatenate([qi[h * IDX_DIM:(h + 1) * IDX_DIM, :] for h in range(IDX_HEADS)], axis=1)
    rhs = jnp.concatenate([rhs, jnp.zeros((LANES - IDX_DIM, IDX_HEADS * qb), BF16)], axis=0)
    wi = wiT_ref[0]

    def p1(kt, carry):
        off = pl.multiple_of(kt * tk, tk)
        kid = ki_ref[0, pl.ds(off, tk), :]
        s_all = jnp.dot(kid, rhs, preferred_element_type=F32)
        acc = wi[0:1, :] * jnp.maximum(s_all[:, 0:qb], 0.0)
        for h in range(1, IDX_HEADS):
            acc = acc + wi[h:h + 1, :] * jnp.maximum(s_all[:, h * qb:(h + 1) * qb], 0.0)
        score = acc * IDX_SCALE
        kpos = off + lax.broadcasted_iota(I32, (tk, 1), 0)
        score = jnp.where(kpos < n_adm, score, -jnp.inf)
        u = pltpu.bitcast(score, I32)
        key_ref[pl.ds(off, tk), :] = u ^ (lax.shift_right_arithmetic(u, 31) & 0x7FFFFFFF)
        return carry

    lax.fori_loop(0, nkt, p1, 0)

    def count_ge(cand):
        def body(kt, cnt):
            blk = key_ref[pl.ds(pl.multiple_of(kt * tk, tk), tk), :]
            m = jnp.where(blk >= cand, 1, 0).astype(I32)
            return cnt + jnp.sum(m.reshape(tk // SUBLANES, SUBLANES, qb), axis=0)

        cnt8 = lax.fori_loop(0, nkt, body, jnp.zeros((SUBLANES, qb), I32))
        return jnp.sum(cnt8, axis=0, keepdims=True)

    def bit_body(i, carry):
        prefix, n_ge = carry
        cand_u = prefix | lax.shift_left(jnp.int32(1), 31 - i)
        cnt = count_ge(cand_u ^ INT32_MIN)
        ok = cnt >= topk
        return jnp.where(ok, cand_u, prefix), jnp.where(ok, cnt, n_ge)

    prefix, n_ge = lax.fori_loop(0, 32, bit_body, (jnp.zeros((1, qb), I32), jnp.broadcast_to(nkt * tk, (1, qb))))
    thr = prefix ^ INT32_MIN
    cut_ties = jnp.max(jnp.where((n_ge > topk) & (thr > KEY_NEG_INF), 1, 0)) > 0

    @pl.when(cut_ties)
    def _():
        r = lax.broadcasted_iota(I32, (tk, tk), 0)
        c = lax.broadcasted_iota(I32, (tk, tk), 1)
        tri = jnp.where(c <= r, 1.0, 0.0).astype(BF16)
        take = (topk - count_ge(thr + 1)).astype(F32)

        def body(kt, seen):
            off = pl.multiple_of(kt * tk, tk)
            blk = key_ref[pl.ds(off, tk), :]
            eq = blk == thr
            incl = jnp.dot(tri, jnp.where(eq, 1.0, 0.0).astype(BF16), preferred_element_type=F32)
            drop = eq & (seen + incl > take)
            key_ref[pl.ds(off, tk), :] = jnp.where(drop, thr - 1, blk)
            return seen + incl[tk - 1:tk, :]

        lax.fori_loop(0, nkt, body, jnp.zeros((1, qb), F32))

    thr_lo = jnp.maximum(thr, KEY_NEG_INF + 1)
    qa = qaT_ref[0]
    ng = A_GROUP * qb
    qg = [jnp.concatenate([qa[(g * A_GROUP + hh) * HEAD_DIM:(g * A_GROUP + hh + 1) * HEAD_DIM, :]
                           for hh in range(A_GROUP)], axis=1) for g in range(A_KV)]
    acc_ref[...] = jnp.zeros(acc_ref.shape, F32)

    c_exp = ATTN_SCALE * math.log2(math.e)

    ta = DSA_TA
    n_steps = nkt * (tk // ta)
    last = n_steps - 1
    bufs = ((s0_ref, p0_ref, a0_ref), (s1_ref, p1_ref, a1_ref))

    def logits(step, s_ref):
        off = pl.multiple_of(jnp.minimum(step, last) * ta, ta)
        for g in range(A_KV):
            kg = k_ref[0, pl.ds(off, ta), g * HEAD_DIM:(g + 1) * HEAD_DIM]
            s_ref[g] = jnp.dot(kg, qg[g], preferred_element_type=F32)

    def numerators(step, s_ref, p_ref, a_ref, ms):
        off = pl.multiple_of(jnp.minimum(step, last) * ta, ta)
        blk = key_ref[pl.ds(off, ta), :]
        sel = (blk >= thr_lo) & (blk < KEY_POS_INF) & (step <= last)
        mask = jnp.concatenate([jnp.where(sel, 0.0, NEG_BIG)] * A_GROUP, axis=1)
        new = []
        for g in range(A_KV):
            s = s_ref[g] + mask
            m_new = jnp.maximum(ms[g], jnp.max(s, axis=0, keepdims=True))
            a_ref[g] = jnp.broadcast_to(jnp.exp2((ms[g] - m_new) * c_exp), (SUBLANES, ng))
            p_ref[g] = jnp.exp2((s - m_new).astype(BF16) * c_exp)
            new.append(m_new)
        return tuple(new)

    def weighted_values(step, p_ref, a_ref):
        off = pl.multiple_of(jnp.clip(step, 0, last) * ta, ta)
        for g in range(A_KV):
            vg = vT_ref[0, g * DSA_VROWS:(g + 1) * DSA_VROWS, pl.ds(off, ta)]
            acc_ref[g] = a_ref[g, 0:1, :] * acc_ref[g] + jnp.dot(vg, p_ref[g], preferred_element_type=F32)

    p1_ref[...] = jnp.zeros(p1_ref.shape, BF16)
    a1_ref[...] = jnp.ones(a1_ref.shape, F32)
    logits(0, s0_ref)

    def two_steps(i2, ms):
        i = 2 * i2
        for par in range(2):
            cur, nxt = bufs[par], bufs[1 - par]
            logits(i + par + 1, nxt[0])
            ms = numerators(i + par, cur[0], cur[1], cur[2], ms)
            weighted_values(i + par - 1, nxt[1], nxt[2])
        return ms

    init = (jnp.full((1, ng), NEG_BIG, F32),) * A_KV
    n_pairs = (n_steps + 1) // 2
    lax.fori_loop(0, n_pairs, two_steps, init)
    weighted_values(2 * n_pairs - 1, p1_ref, a1_ref)
    for g in range(A_KV):
        o = acc_ref[g, 0:HEAD_DIM, :] / acc_ref[g, HEAD_DIM:HEAD_DIM + 1, :]
        for hh in range(A_GROUP):
            h = g * A_GROUP + hh
            o_ref[0, h * HEAD_DIM:(h + 1) * HEAD_DIM, :] = o[:, hh * qb:(hh + 1) * qb].astype(o_ref.dtype)


def _dsa(qaT, qiT, wiT, k, vT, ki, *, s_valid, q_pos0, topk):
    b, dq, sq = qaT.shape
    s_pad = k.shape[1]
    assert s_pad % DSA_TK == 0 and s_pad % DSA_TA == 0 and sq % DSA_QB == 0 and topk <= DSA_TK
    kern = functools.partial(_dsa_kernel, s_valid=s_valid, q_pos0=q_pos0, topk=topk)
    return pl.pallas_call(
        kern,
        grid=(b, sq // DSA_QB),
        in_specs=[pl.BlockSpec((1, dq, DSA_QB), lambda i, j: (i, 0, j)),
                  pl.BlockSpec((1, qiT.shape[1], DSA_QB), lambda i, j: (i, 0, j)),
                  pl.BlockSpec((1, IDX_HEADS, DSA_QB), lambda i, j: (i, 0, j)),
                  pl.BlockSpec((1, s_pad, k.shape[2]), lambda i, j: (i, 0, 0)),
                  pl.BlockSpec((1, vT.shape[1], s_pad), lambda i, j: (i, 0, 0)),
                  pl.BlockSpec((1, s_pad, LANES), lambda i, j: (i, 0, 0))],
        out_specs=pl.BlockSpec((1, dq, DSA_QB), lambda i, j: (i, 0, j)),
        out_shape=jax.ShapeDtypeStruct((b, dq, sq), BF16),
        scratch_shapes=[pltpu.VMEM((s_pad, DSA_QB), I32),
                        pltpu.VMEM((A_KV, DSA_VROWS, A_GROUP * DSA_QB), F32)]
                       + [pltpu.VMEM((A_KV, DSA_TA, A_GROUP * DSA_QB), F32)] * 2
                       + [pltpu.VMEM((A_KV, DSA_TA, A_GROUP * DSA_QB), BF16)] * 2
                       + [pltpu.VMEM((A_KV, SUBLANES, A_GROUP * DSA_QB), F32)] * 2,
        compiler_params=_cparams(("arbitrary", "arbitrary")),
        name="dsa",
    )(qaT, qiT, wiT, k, vT, ki)


def _band_heads(q, k_head, v_head, bias_ref, n_hidden, o_ref, s_ref):
    nh = q.shape[1] // HEAD_DIM
    tk = bias_ref.shape[2]
    hidden = None if n_hidden is None else lax.broadcasted_iota(I32, (1, tk), 1) < n_hidden
    for h in range(nh):
        sl = slice(h * HEAD_DIM, (h + 1) * HEAD_DIM)
        s = lax.dot_general(q[:, sl].astype(BF16), k_head(h), (((1,), (1,)), ((), ())),
                            preferred_element_type=F32) * ATTN_SCALE + bias_ref[h]
        if hidden is not None:
            s = jnp.where(hidden, NEG_BIG, s)
        s_ref[h] = s
    ms = [jnp.max(s_ref[h], axis=1, keepdims=True) for h in range(nh)]
    ls = []
    for h in range(nh):
        p = jnp.exp(s_ref[h] - ms[h])
        s_ref[h] = p
        ls.append(jnp.sum(p, axis=1, keepdims=True))
    for h in range(nh):
        sl = slice(h * HEAD_DIM, (h + 1) * HEAD_DIM)
        w = (s_ref[h] / ls[h]).astype(BF16)
        o_ref[:, sl] = jnp.dot(w, v_head(h), preferred_element_type=F32).astype(o_ref.dtype)


def _band_prompt_kernel(*refs, n_tiles):
    q_ref = refs[0]
    k_refs = refs[1:1 + n_tiles]
    v_refs = refs[1 + n_tiles:1 + 2 * n_tiles]
    bias_ref = refs[1 + 2 * n_tiles]
    o_ref = refs[2 + 2 * n_tiles]
    s_ref = refs[3 + 2 * n_tiles]
    j = pl.program_id(1)
    n_hidden = jnp.maximum(n_tiles - 1 - j, 0) * q_ref.shape[0]
    k_all = jnp.concatenate([r[...] for r in k_refs], axis=0).astype(BF16)
    v_all = jnp.concatenate([r[...] for r in v_refs], axis=0).astype(BF16)
    head = lambda a: lambda h: a[:, h * HEAD_DIM:(h + 1) * HEAD_DIM]
    _band_heads(q_ref[...], head(k_all), head(v_all), bias_ref, n_hidden, o_ref, s_ref)


def _band_prompt(p, bias, cols, batch, seq):
    qb = BAND_QB
    n_tiles = BAND_CHUNKS * CHUNK // qb + 1
    width = bias.shape[0] * HEAD_DIM
    nq = seq // qb
    cq, ck, cv = cols["qb"] // width, cols["kb"] // width, cols["vb"] // width
    assert cols["qb"] % width == 0 and cols["kb"] % width == 0 and cols["vb"] % width == 0

    def kv_spec(c, i):
        return pl.BlockSpec((qb, width), lambda b, j: (b * nq + jnp.maximum(j - (n_tiles - 1) + i, 0), c))

    return pl.pallas_call(
        functools.partial(_band_prompt_kernel, n_tiles=n_tiles),
        grid=(batch, nq),
        in_specs=([pl.BlockSpec((qb, width), lambda b, j: (b * nq + j, cq))]
                  + [kv_spec(ck, i) for i in range(n_tiles)]
                  + [kv_spec(cv, i) for i in range(n_tiles)]
                  + [pl.BlockSpec(bias.shape, lambda b, j: (0, 0, 0))]),
        out_specs=pl.BlockSpec((qb, width), lambda b, j: (b * nq + j, 0)),
        out_shape=jax.ShapeDtypeStruct((batch * seq, width), BF16),
        compiler_params=_cparams(("arbitrary", "arbitrary")),
        scratch_shapes=[pltpu.VMEM(bias.shape, F32)],
        name="band_prompt",
    )(*([p] * (1 + 2 * n_tiles)), bias)


def _band_sample_kernel(q_ref, k_ref, v_ref, ck_ref, cv_ref, bias_ref, o_ref, s_ref):
    def head(cache_ref, new_ref):
        return lambda h: jnp.concatenate(
            [cache_ref[0, :, h, :], new_ref[:, h * HEAD_DIM:(h + 1) * HEAD_DIM]], axis=0).astype(BF16)

    _band_heads(q_ref[...], head(ck_ref, k_ref), head(cv_ref, v_ref), bias_ref, None, o_ref, s_ref)


def _band_sample(p, cache_k, cache_v, bias, cols, batch, t):
    nh = bias.shape[0]
    width = nh * HEAD_DIM
    nbc = cache_k.shape[1]
    cq, ck, cv = cols["qb"] // width, cols["kb"] // width, cols["vb"] // width
    return pl.pallas_call(
        _band_sample_kernel,
        grid=(batch,),
        in_specs=[pl.BlockSpec((t, width), lambda b: (b, cq)),
                  pl.BlockSpec((t, width), lambda b: (b, ck)),
                  pl.BlockSpec((t, width), lambda b: (b, cv)),
                  pl.BlockSpec((1, nbc, nh, HEAD_DIM), lambda b: (b, 0, 0, 0)),
                  pl.BlockSpec((1, nbc, nh, HEAD_DIM), lambda b: (b, 0, 0, 0)),
                  pl.BlockSpec(bias.shape, lambda b: (0, 0, 0))],
        out_specs=pl.BlockSpec((t, width), lambda b: (b, 0)),
        out_shape=jax.ShapeDtypeStruct((batch * t, width), BF16),
        compiler_params=_cparams(("arbitrary",)),
        scratch_shapes=[pltpu.VMEM(bias.shape, F32)],
        name="band_sample",
    )(p, p, p, cache_k, cache_v, bias)


def _band_bias(table, q0, nq, k0, nk):
    n = nq + nk
    j = jnp.arange(n)
    g = table[:, jnp.clip(j + (q0 - k0 - nk + 1), -REL_CLIP, REL_CLIP) + REL_CLIP].astype(F32)
    hank = jnp.tile(g, (1, nq + 1))[:, :nq * (n + 1)].reshape(-1, nq, n + 1)[:, :, :nk]
    bias = hank[:, :, ::-1]
    qc = (q0 + jnp.arange(nq))[:, None] // CHUNK
    kc = (k0 + jnp.arange(nk))[None, :] // CHUNK
    vis = (kc <= qc) & (kc >= qc - BAND_CHUNKS)
    return jnp.where(vis[None], bias, NEG_BIG)


def _ssm_kernel(u_ref, t_ref, p_ref, q_ref, a_ref, h0_ref, y_ref, hre_o, him_o, sre_ref, sim_ref,
                *, seg, tiles_per_seq):
    L = SSM_L
    x = jnp.concatenate([u_ref[:, s, :] for s in range(L)], axis=1).astype(BF16)
    sre_ref[...] = jnp.dot(x, p_ref[0, 0], preferred_element_type=F32)
    sim_ref[...] = jnp.dot(x, p_ref[0, 1], preferred_element_type=F32)
    rows, w = sre_ref.shape
    kseg = lax.broadcasted_iota(I32, (SUBLANES, 1), 0) % seg
    apk_re, apk_im = a_ref[0, 0, 0:SUBLANES, :], a_ref[0, 1, 0:SUBLANES, :]

    def cmad(x_re, x_im, a_re, a_im, y_re, y_im):
        return x_re + a_re * y_re - a_im * y_im, x_im + a_re * y_im + a_im * y_re

    def tile(i, carry):
        r = pl.ds(pl.multiple_of(i * SUBLANES, SUBLANES), SUBLANES)
        x_re, x_im = sre_ref[r, :], sim_ref[r, :]
        for n, d in enumerate((1, 2, 4)):
            if d < seg:
                ad_re = a_ref[0, 0, SUBLANES + n:SUBLANES + n + 1, :]
                ad_im = a_ref[0, 1, SUBLANES + n:SUBLANES + n + 1, :]
                sh_re = jnp.where(kseg >= d, pltpu.roll(x_re, d, 0), 0.0)
                sh_im = jnp.where(kseg >= d, pltpu.roll(x_im, d, 0), 0.0)
                x_re, x_im = cmad(x_re, x_im, ad_re, ad_im, sh_re, sh_im)
        if tiles_per_seq > 1:
            first = (i % tiles_per_seq) == 0
            hin_re = jnp.where(first, h0_ref[0, 0], carry[0])
            hin_im = jnp.where(first, h0_ref[0, 1], carry[1])
        else:
            hin_re, hin_im = h0_ref[0, 0, r, :], h0_ref[0, 1, r, :]
        inc_re, inc_im = cmad(x_re, x_im, apk_re, apk_im, hin_re, hin_im)
        sre_ref[r, :] = jnp.where(kseg == 0, hin_re, pltpu.roll(inc_re, 1, 0))
        sim_ref[r, :] = jnp.where(kseg == 0, hin_im, pltpu.roll(inc_im, 1, 0))
        hre_o[0, r, :] = inc_re
        him_o[0, r, :] = inc_im
        return inc_re[SUBLANES - 1:SUBLANES, :], inc_im[SUBLANES - 1:SUBLANES, :]

    zero = jnp.zeros((1, w), F32)
    lax.fori_loop(0, rows // SUBLANES, tile, (zero, zero))
    y = (jnp.dot(x, t_ref[0], preferred_element_type=F32)
         + jnp.dot(sre_ref[...].astype(BF16), q_ref[0, 0], preferred_element_type=F32)
         + jnp.dot(sim_ref[...].astype(BF16), q_ref[0, 1], preferred_element_type=F32))
    for t in range(L):
        y_ref[:, t, :] = y[:, t * LANES:(t + 1) * LANES]


def _ssm(p3, uc_col, tmat, pmat, qmat, apow, h0, rows_blk, seg, tiles_per_seq):
    rows_total = p3.shape[0]
    nb, _, _, w = pmat.shape
    kl = SSM_L * LANES
    h0_rows = h0.shape[2] // (rows_total // rows_blk)
    hspec = pl.BlockSpec((1, rows_blk, w), lambda i, r: (i, r, 0))
    return pl.pallas_call(
        functools.partial(_ssm_kernel, seg=seg, tiles_per_seq=tiles_per_seq),
        grid=(nb, rows_total // rows_blk),
        in_specs=[pl.BlockSpec((rows_blk, SSM_L, LANES), lambda i, r: (r, 0, uc_col // LANES + i)),
                  pl.BlockSpec((1, kl, kl), lambda i, r: (i, 0, 0)),
                  pl.BlockSpec((1, 2, kl, w), lambda i, r: (i, 0, 0, 0)),
                  pl.BlockSpec((1, 2, w, kl), lambda i, r: (i, 0, 0, 0)),
                  pl.BlockSpec((1, 2, 2 * SUBLANES, w), lambda i, r: (i, 0, 0, 0)),
                  pl.BlockSpec((1, 2, h0_rows, w), lambda i, r: (i, 0, r, 0))],
        out_specs=[pl.BlockSpec((rows_blk, SSM_L, LANES), lambda i, r: (r, 0, i)), hspec, hspec],
        out_shape=[jax.ShapeDtypeStruct((rows_total, SSM_L, nb * LANES), F32),
                   jax.ShapeDtypeStruct((nb, rows_total, w), F32),
                   jax.ShapeDtypeStruct((nb, rows_total, w), F32)],
        scratch_shapes=[pltpu.VMEM((rows_blk, w), F32), pltpu.VMEM((rows_blk, w), F32)],
        compiler_params=_cparams(("arbitrary", "arbitrary")),
        name="ssm",
    )(p3, tmat, pmat, qmat, apow, h0)


def _ssm_weights(a_re, a_im, log_dt, b_re, b_im, c_re, c_im, d):
    hp = lax.Precision.HIGHEST
    g, p = a_re.shape
    gc = b_re.shape[2]
    L = SSM_L
    a_re, a_im = a_re.astype(F32), a_im.astype(F32)
    dt = jnp.exp(log_dt.astype(F32))[:, None]
    mag = jnp.exp(dt * a_re)
    ab_re, ab_im = mag * jnp.cos(dt * a_im), mag * jnp.sin(dt * a_im)
    n_re = ab_re - 1.0
    n_im = ab_im
    den = a_re * a_re + a_im * a_im
    cc_re = (n_re * a_re + n_im * a_im) / den
    cc_im = (n_im * a_re - n_re * a_im) / den
    b_re, b_im = b_re.astype(F32), b_im.astype(F32)
    bb_re = cc_re[..., None] * b_re - cc_im[..., None] * b_im
    bb_im = cc_re[..., None] * b_im + cc_im[..., None] * b_re
    c_re, c_im = c_re.astype(F32), c_im.astype(F32)

    def cmul(x, y):
        return x[0] * y[0] - x[1] * y[1], x[0] * y[1] + x[1] * y[0]

    rep = lambda a: jnp.broadcast_to(a[None], (L,) + a.shape)
    pw_re, pw_im = lax.associative_scan(cmul, (rep(ab_re), rep(ab_im)), axis=0)
    pw_re = jnp.concatenate([jnp.ones((1, g, p), F32), pw_re], axis=0)
    pw_im = jnp.concatenate([jnp.zeros((1, g, p), F32), pw_im], axis=0)
    ca_re = c_re[None] * pw_re[:, :, None, :] - c_im[None] * pw_im[:, :, None, :]
    ca_im = c_re[None] * pw_im[:, :, None, :] + c_im[None] * pw_re[:, :, None, :]
    taps = (jnp.einsum("tgop,gpi->tgoi", ca_re[:L], bb_re, precision=hp)
            - jnp.einsum("tgop,gpi->tgoi", ca_im[:L], bb_im, precision=hp))
    taps = taps.at[0].add(jax.vmap(jnp.diag)(d.astype(F32)))
    s_idx = jnp.arange(L)[:, None]
    t_idx = jnp.arange(L)[None, :]
    lag = t_idx - s_idx
    tt = jnp.where((lag >= 0)[:, :, None, None, None], taps[jnp.clip(lag, 0, L - 1)], 0.0)
    gb = SSM_GB
    nb = g // gb
    n_x = L * gb * gc
    x = jnp.arange(n_x)
    x_grp, x_loc = (x // gc) % gb, (x // (gb * gc)) * gc + x % gc
    place_x = ((x_grp[None, :, None] == jnp.arange(gb)[:, None, None])
               & (x_loc[None, :, None] == jnp.arange(L * gc)[None, None, :])).astype(F32)
    q = jnp.arange(gb * p)
    place_q = ((q[None, :, None] // p == jnp.arange(gb)[:, None, None])
               & (q[None, :, None] % p == jnp.arange(p)[None, None, :])).astype(F32)
    blk = lambda a: a.reshape((nb, gb) + a.shape[1:])
    t_grp = blk(tt.transpose(2, 0, 4, 1, 3).reshape(g, L * gc, L * gc))
    tmat = jnp.einsum("gxr,bgrc,gyc->bxy", place_x, t_grp, place_x)
    rev = L - 1 - jnp.arange(L)
    pin_re = pw_re[rev][:, :, :, None] * bb_re[None] - pw_im[rev][:, :, :, None] * bb_im[None]
    pin_im = pw_re[rev][:, :, :, None] * bb_im[None] + pw_im[rev][:, :, :, None] * bb_re[None]
    to_p = lambda a: jnp.einsum("gxr,bgrp,gqp->bxq", place_x,
                                blk(a.transpose(1, 0, 3, 2).reshape(g, L * gc, p)), place_q)
    to_q = lambda a: jnp.einsum("gqp,bgpc,gyc->bqy", place_q,
                                blk(a.transpose(1, 3, 0, 2).reshape(g, p, L * gc)), place_x)
    pmat = jnp.stack([to_p(pin_re), to_p(pin_im)], axis=1)
    qmat = jnp.stack([to_q(ca_re[1:]), -to_q(ca_im[1:])], axis=1)
    rep8 = lambda a: jnp.broadcast_to(a[None], (SUBLANES,) + a.shape)
    apl_re, apl_im = lax.associative_scan(cmul, (rep8(pw_re[L]), rep8(pw_im[L])), axis=0)
    apl = jnp.stack([apl_re, apl_im]).reshape(2, SUBLANES, nb, gb * p)
    return tmat.astype(BF16), pmat.astype(BF16), qmat.astype(BF16), apl


def _ssm_apply(p, uc_col, b, t, h0_re, h0_im, weights, one_seq_per_step):
    tmat, pmat, qmat, apl = weights
    nb, w = pmat.shape[0], pmat.shape[3]
    L = SSM_L
    nc = t // L
    assert t % L == 0 and (nc % SUBLANES == 0 or SUBLANES % nc == 0)
    seg = min(nc, SUBLANES)
    if one_seq_per_step:
        assert nc % SUBLANES == 0
        rows_blk, tiles_per_seq, reps = nc, nc // SUBLANES, SUBLANES
    else:
        rows_blk, tiles_per_seq, reps = b * nc, 1, nc
        assert nc <= SUBLANES and rows_blk % SUBLANES == 0
    pw_rows = np.array([k % seg for k in range(SUBLANES)] + [0, 1, 3] + [0] * (SUBLANES - 3))
    apow = apl[:, pw_rows].transpose(2, 0, 1, 3)

    def pack(h):
        h = h.astype(F32).reshape(b, nb, w).transpose(1, 0, 2)
        return jnp.repeat(h, reps, axis=1)

    h0 = jnp.stack([pack(h0_re), pack(h0_im)], axis=1)
    p3 = p.reshape(b * nc, L, p.shape[1])
    y, hre, him = _ssm(p3, uc_col, tmat, pmat, qmat, apow, h0, rows_blk, seg, tiles_per_seq)

    def last(h):
        return h.reshape(nb, b, nc, w)[:, :, nc - 1].transpose(1, 0, 2).reshape(b, nb * SSM_GB, C_STATE)

    return y.reshape(b * t, nb * LANES), last(hre), last(him)


def _glu_kernel(y_ref, w_ref, b_ref, o_ref):
    z = _gelu(y_ref[...])
    a = jnp.dot(z.astype(BF16), w_ref[...], preferred_element_type=F32) + b_ref[...]
    o_ref[...] = (z * _sigmoid(a)).astype(o_ref.dtype)


def _glu(y, w, b, tm):
    t, n = y.shape
    return pl.pallas_call(
        _glu_kernel,
        grid=(t // tm,),
        in_specs=[pl.BlockSpec((tm, n), lambda i: (i, 0)),
                  pl.BlockSpec((n, n), lambda i: (0, 0)),
                  pl.BlockSpec((1, n), lambda i: (0, 0))],
        out_specs=pl.BlockSpec((tm, n), lambda i: (i, 0)),
        out_shape=jax.ShapeDtypeStruct((t, n), BF16),
        compiler_params=_cparams(("arbitrary",)),
        name="glu",
    )(y, w, b.reshape(1, n))


def _merge_kernel(oa_ref, ob_ref, oc_ref, ga_ref, gb_ref, gc_ref, w_ref, o_ref):
    acc = None
    for n, (o, g) in enumerate(((oa_ref, ga_ref), (ob_ref, gb_ref), (oc_ref, gc_ref))):
        br = jnp.dot(o[...], w_ref[n], preferred_element_type=F32)
        t = _sigmoid(g[...]) * br
        acc = t if acc is None else acc + t
    o_ref[...] = acc.astype(o_ref.dtype)


def _merge(oa, ob, oc, p, w, g_col, tm, tn):
    t, m = oa.shape
    d = w.shape[2]
    assert g_col % tn == 0 and d % tn == 0

    def gspec(n):
        return pl.BlockSpec((tm, tn), lambda i, j: (i, (g_col + n * d) // tn + j))

    ospec = pl.BlockSpec((tm, m), lambda i, j: (i, 0))
    return pl.pallas_call(
        _merge_kernel,
        grid=(t // tm, d // tn),
        in_specs=[ospec, ospec, ospec, gspec(0), gspec(1), gspec(2),
                  pl.BlockSpec((N_BRANCH, m, tn), lambda i, j: (0, 0, j))],
        out_specs=pl.BlockSpec((tm, tn), lambda i, j: (i, j)),
        out_shape=jax.ShapeDtypeStruct((t, d), BF16),
        compiler_params=_cparams(("arbitrary", "arbitrary")),
        name="merge",
    )(oa, ob, oc, p, p, p, w)


def _mmres_kernel(a_ref, w_ref, r_ref, o_ref):
    o_ref[...] = r_ref[...] + jnp.dot(a_ref[...], w_ref[...], preferred_element_type=F32)


def _mmres(a, w, res, tm, tn):
    t, k = a.shape
    n = w.shape[1]
    return pl.pallas_call(
        _mmres_kernel,
        grid=(t // tm, n // tn),
        in_specs=[pl.BlockSpec((tm, k), lambda i, j: (i, 0)),
                  pl.BlockSpec((k, tn), lambda i, j: (0, j)),
                  pl.BlockSpec((tm, tn), lambda i, j: (i, j))],
        out_specs=pl.BlockSpec((tm, tn), lambda i, j: (i, j)),
        out_shape=jax.ShapeDtypeStruct((t, n), F32),
        compiler_params=_cparams(("arbitrary", "arbitrary")),
        name="mmres",
    )(a, w, res)


def _ffn_in_kernel(x_ref, g_ref, wu_ref, wv_ref, wc_ref, bc_ref, st_ref, o_ref, tail_ref,
                   h_ref, up_ref, carry_ref, *, ns, ts, tiles_per_seq):
    i, j = pl.program_id(0), pl.program_id(1)
    pad = SUBLANES

    @pl.when(j == 0)
    def _():
        h_ref[...] = _rms(x_ref[...], g_ref[...]).astype(BF16)

    @pl.when((i == 0) & (j == 0))
    def _():
        carry_ref[...] = jnp.zeros(carry_ref.shape, F32)

    h = h_ref[...]
    u = jnp.dot(h, wu_ref[...], preferred_element_type=F32)
    tn = u.shape[1]
    up_ref[:, pad:, :] = u.reshape(ns, ts, tn)
    first = (i % tiles_per_seq) == 0
    up_ref[:, pad - 2:pad, :] = jnp.where(first, st_ref[...], carry_ref[j])
    last2 = up_ref[:, ts + pad - 2:ts + pad, :]
    carry_ref[j] = last2
    tail_ref[0] = last2
    c = bc_ref[...].reshape(1, 1, tn)
    for tap in range(CONV_W):
        c = c + up_ref[:, pad - 2 + tap:pad - 2 + tap + ts, :] * wc_ref[tap:tap + 1, :].reshape(1, 1, tn)
    gl = _gelu(c).reshape(ns * ts, tn)
    v = jnp.dot(h, wv_ref[...], preferred_element_type=F32)
    o_ref[...] = (gl * v).astype(o_ref.dtype)


def _ffn_in(x, g, w, wc, bc, state, ns, ts, tn):
    t, d = x.shape
    f = w.shape[1] // 2
    tm = ns * ts
    nj = f // tn
    seq_len = t // state.shape[0]
    tiles_per_seq = max(seq_len // tm, 1)
    kern = functools.partial(_ffn_in_kernel, ns=ns, ts=ts, tiles_per_seq=tiles_per_seq)
    return pl.pallas_call(
        kern,
        grid=(t // tm, nj),
        in_specs=[pl.BlockSpec((tm, d), lambda i, j: (i, 0)),
                  pl.BlockSpec((1, d), lambda i, j: (0, 0)),
                  pl.BlockSpec((d, tn), lambda i, j: (0, j)),
                  pl.BlockSpec((d, tn), lambda i, j: (0, nj + j)),
                  pl.BlockSpec((CONV_W, tn), lambda i, j: (0, j)),
                  pl.BlockSpec((1, tn), lambda i, j: (0, j)),
                  pl.BlockSpec((ns, CONV_W - 1, tn), lambda i, j: (i // tiles_per_seq, 0, j))],
        out_specs=[pl.BlockSpec((tm, tn), lambda i, j: (i, j)),
                   pl.BlockSpec((1, ns, CONV_W - 1, tn), lambda i, j: (i, 0, 0, j))],
        out_shape=[jax.ShapeDtypeStruct((t, f), BF16),
                   jax.ShapeDtypeStruct((t // tm, ns, CONV_W - 1, f), F32)],
        scratch_shapes=[pltpu.VMEM((tm, d), BF16),
                        pltpu.VMEM((ns, ts + SUBLANES, tn), F32),
                        pltpu.VMEM((nj, ns, CONV_W - 1, tn), F32)],
        compiler_params=_cparams(("arbitrary", "arbitrary")),
        name="ffn_in",
    )(x, g.reshape(1, d), w, w, wc, bc.reshape(1, f), state)


def _norm_kernel(x_ref, g_ref, o_ref):
    o_ref[...] = _rms(x_ref[...], g_ref[...])


def _norm(x, g, tm):
    t, d = x.shape
    return pl.pallas_call(
        _norm_kernel,
        grid=(t // tm,),
        in_specs=[pl.BlockSpec((tm, d), lambda i: (i, 0)), pl.BlockSpec((1, d), lambda i: (0, 0))],
        out_specs=pl.BlockSpec((tm, d), lambda i: (i, 0)),
        out_shape=jax.ShapeDtypeStruct((t, d), F32),
        compiler_params=_cparams(("arbitrary",)),
        name="final_norm",
    )(x, g.reshape(1, d))


def _row_tile(t, cap=1024):
    tm = min(t, cap)
    while t % tm:
        tm //= 2
    return tm


def _pack_w_in(w_in, d_model):
    mix = d_model // 2
    sizes = dict(qa=mix, ka=A_KV * HEAD_DIM, va=A_KV * HEAD_DIM, qi=IDX_HEADS * IDX_DIM, ki=IDX_DIM,
                 wi=IDX_HEADS, qb=mix, kb=mix, vb=mix, uc=mix, g=N_BRANCH * d_model)
    src_order = ["qa", "ka", "va", "qi", "ki", "wi", "qb", "kb", "vb", "uc", "g"]
    src, off = {}, 0
    for name in src_order:
        src[name] = (off, sizes[name])
        off += sizes[name]
    assert off == w_in.shape[1]
    dst_order = ["qa", "qb", "kb", "vb", "uc", "g", "qi", "ka", "va", "ki", "wi"]
    cols, parts, off = {}, [], 0
    for name in dst_order:
        s, n = src[name]
        cols[name] = off
        parts.append(w_in[:, s:s + n])
        off += n
    cols["kw"] = cols["ki"]
    tn = 512
    total = -(-off // tn) * tn
    parts.append(jnp.zeros((w_in.shape[0], total - off), w_in.dtype))
    return jnp.concatenate(parts, axis=1).astype(BF16), cols, tn


def _to_heads_T(x, b, t):
    return x.reshape(b, t, x.shape[1]).transpose(0, 2, 1)


def _pad_axis(x, axis, size):
    pad = [(0, 0)] * x.ndim
    pad[axis] = (0, size - x.shape[axis])
    return jnp.pad(x, pad)


def _layer(x, pos, prm, cache, is_prompt):
    b, t, d = x.shape
    mix = d // 2
    xt = x.reshape(b * t, d)
    tm = _row_tile(b * t)
    cols = prm["cols"]
    p = _proj(xt, prm["norm1"], prm["w_in"], tm, prm["w_in_tn"])

    tabs = _rope_tables(pos)
    tm_r = _row_tile(t) if is_prompt else tm
    if not is_prompt:
        tabs = jnp.tile(tabs, (1, b, 1))
    qa, ka, kab, vab, qi, kw, kib = _rope(p, tabs, cols, tm_r)
    if is_prompt:
        sq, s_valid, q_pos0 = t, t, 0
        k_all, v_all, ki_all = kab.reshape(b, t, -1), vab.reshape(b, t, -1), kib.reshape(b, t, LANES)
    else:
        cak, cav, caki = cache[0], cache[1], cache[2]
        past = cak.shape[1]
        sq, s_valid, q_pos0 = DSA_QB, past + t, past
        k_all = jnp.concatenate([cak.reshape(b, past, -1).astype(BF16), kab.reshape(b, t, -1)], axis=1)
        v_all = jnp.concatenate([cav.reshape(b, past, -1).astype(BF16), vab.reshape(b, t, -1)], axis=1)
        ki_new = kib.reshape(b, t, LANES)
        ki_all = jnp.concatenate([_pad_axis(caki.astype(BF16), 2, LANES), ki_new], axis=1)
    s_pad = -(-s_valid // DSA_TK) * DSA_TK
    k_all = _pad_axis(k_all, 1, s_pad)
    vT = _pad_axis(v_all, 1, s_pad).transpose(0, 2, 1).reshape(b, A_KV, HEAD_DIM, s_pad)
    ones = jnp.zeros((b, A_KV, DSA_VROWS - HEAD_DIM, s_pad), BF16).at[:, :, 0].set(1.0)
    vT = jnp.concatenate([vT, ones], axis=2).reshape(b, A_KV * DSA_VROWS, s_pad)
    ki_all = _pad_axis(ki_all, 1, s_pad)
    qaT = _pad_axis(_to_heads_T(qa, b, t), 2, sq)
    qiT = _pad_axis(_to_heads_T(qi, b, t), 2, sq)
    wiT = _pad_axis(_to_heads_T(kw[:, IDX_DIM:IDX_DIM + IDX_HEADS], b, t), 2, sq)
    topk = min(TOPK_MAX, s_valid // 4)
    oaT = _dsa(qaT, qiT, wiT, k_all, vT, ki_all, s_valid=s_valid, q_pos0=q_pos0, topk=topk)
    oa = oaT[:, :, :t].transpose(0, 2, 1).reshape(b * t, mix)

    nh = mix // HEAD_DIM
    if is_prompt:
        n_tiles = BAND_CHUNKS * CHUNK // BAND_QB + 1
        bias = _band_bias(prm["rel_bias"], BAND_CHUNKS * CHUNK, BAND_QB, 0, n_tiles * BAND_QB)
        ob = _band_prompt(p, bias, cols, b, t)
    else:
        cbk, cbv = cache[3], cache[4]
        nbc = cbk.shape[1]
        past = cache[0].shape[1]
        bias = _band_bias(prm["rel_bias"], past, t, past - nbc, nbc + t)
        ob = _band_sample(p, cbk.astype(F32), cbv.astype(F32), bias, cols, b, t)

    g_ssm = mix // C_GROUP
    if is_prompt:
        h0_re = jnp.zeros((b, g_ssm, C_STATE), F32)
        h0_im = h0_re
    else:
        h0_re, h0_im = cache[5], cache[6]
    yc, hr, hi = _ssm_apply(p, cols["uc"], b, t, h0_re, h0_im, prm["ssm"], is_prompt)
    oc = _glu(yc, prm["w_glu"], prm["b_glu"], tm)

    merged = _merge(oa, ob, oc, p, prm["w_branch"], cols["g"], tm, 512)
    x1 = _mmres(merged, prm["w_out"], xt, tm, 1024)

    f = prm["w_conv"].shape[1]
    if is_prompt:
        state = jnp.zeros((b, CONV_W - 1, f), F32)
        ns, ts = 1, _row_tile(t)
    else:
        state = cache[7].astype(F32)
        ns, ts = b, t
    act, tails = _ffn_in(x1, prm["norm2"], prm["w_ffn_in"], prm["w_conv"], prm["b_conv"], state, ns, ts, 512)
    x2 = _mmres(act, prm["w_down"], x1, _row_tile(b * t, 512), 512)
    if is_prompt:
        buf = tails.reshape(b, t // ts, CONV_W - 1, f)[:, -1]
    else:
        buf = tails[0]

    ka4 = ka.reshape(b, t, A_KV, HEAD_DIM)
    va4 = p[:, cols["va"]:cols["va"] + A_KV * HEAD_DIM].reshape(b, t, A_KV, HEAD_DIM)
    ki3 = kw[:, :IDX_DIM].reshape(b, t, IDX_DIM)
    nb = min(BAND_CHUNKS * CHUNK, t) if is_prompt else t
    pb = p.reshape(b, t, p.shape[1])[:, t - nb:]
    kb4 = pb[:, :, cols["kb"]:cols["kb"] + mix].reshape(b, nb, nh, HEAD_DIM)
    vb4 = pb[:, :, cols["vb"]:cols["vb"] + mix].reshape(b, nb, nh, HEAD_DIM)
    return x2.reshape(b, t, d), (ka4, va4, ki3, kb4, vb4, hr, hi, buf)


def kernel(x_prompt, x_sample, cache_a_k, cache_a_v, cache_a_kidx, cache_b_k, cache_b_v, state_c_re, state_c_im,
           state_ffn_conv, norm1_g, w_in, rel_bias, ssm_a_re, ssm_a_im, ssm_log_dt, ssm_b_re, ssm_b_im, ssm_c_re,
           ssm_c_im, ssm_d, w_glu, b_glu, w_branch, w_out, norm2_g, w_ffn_in, w_ffn_conv, b_ffn_conv, w_ffn_down,
           normf_g):
    depth = w_in.shape[0]
    d = x_prompt.shape[2]
    pos_p = jnp.arange(x_prompt.shape[1])
    pos_s = cache_a_k.shape[2] + jnp.arange(x_sample.shape[1])
    xp, xs = x_prompt, x_sample
    st_p, st_s = [], []
    ssm_all = jax.vmap(_ssm_weights)(ssm_a_re, ssm_a_im, ssm_log_dt, ssm_b_re, ssm_b_im, ssm_c_re, ssm_c_im, ssm_d)
    for l in range(depth):
        w_in_l, cols, tn = _pack_w_in(w_in[l], d)
        prm = dict(norm1=norm1_g[l], w_in=w_in_l, cols=cols, w_in_tn=tn, rel_bias=rel_bias[l],
                   ssm=tuple(a[l] for a in ssm_all),
                   w_glu=w_glu[l].astype(BF16), b_glu=b_glu[l], w_branch=w_branch[l].astype(BF16),
                   w_out=w_out[l].astype(BF16), norm2=norm2_g[l], w_ffn_in=w_ffn_in[l].astype(BF16),
                   w_conv=w_ffn_conv[l], b_conv=b_ffn_conv[l], w_down=w_ffn_down[l].astype(BF16))
        xp, sp = _layer(xp, pos_p, prm, None, True)
        cache_l = (cache_a_k[l], cache_a_v[l], cache_a_kidx[l], cache_b_k[l], cache_b_v[l],
                   state_c_re[l], state_c_im[l], state_ffn_conv[l])
        xs, ss = _layer(xs, pos_s, prm, cache_l, False)
        st_p.append(sp)
        st_s.append(ss)
    bp, tp, _ = xp.shape
    bs, tsq, _ = xs.shape
    y_prompt = _norm(xp.reshape(bp * tp, d), normf_g, _row_tile(bp * tp)).reshape(bp, tp, d)
    y_sample = _norm(xs.reshape(bs * tsq, d), normf_g, _row_tile(bs * tsq)).reshape(bs, tsq, d)
    outs_p = [jnp.stack([s[i] for s in st_p]) for i in range(8)]
    outs_s = [jnp.stack([s[i] for s in st_s]) for i in range(8)]
    return (y_prompt, y_sample, *outs_p, *outs_s)
```

```python
import functools
import math

import jax
import jax.numpy as jnp
import numpy as np
from jax import lax
from jax.experimental import pallas as pl
from jax.experimental.pallas import tpu as pltpu

F32 = jnp.float32
BF16 = jnp.bfloat16
I32 = jnp.int32

CHUNK = 64
EPS = 1e-6
ROPE_THETA = 500000.0
HEAD_DIM = 128
A_KV = 2
A_GROUP = 4
A_ROT = HEAD_DIM // 4
IDX_HEADS = 8
IDX_DIM = 64
IDX_ROT = IDX_DIM // 4
TOPK_MAX = 256
BAND_CHUNKS = 8
REL_CLIP = 256
C_GROUP = 16
C_STATE = 64
CONV_W = 3
N_BRANCH = 3
ATTN_SCALE = HEAD_DIM ** -0.5
IDX_SCALE = (IDX_DIM * IDX_HEADS) ** -0.5

LANES = 128
SUBLANES = 8
VMEM_LIMIT = 56 * 1024 * 1024

NEG_BIG = -1e30
KEY_NEG_INF = -2139095041
KEY_POS_INF = 2139095040
INT32_MIN = -2147483648

SSM_L = SUBLANES
SSM_GB = LANES // C_GROUP
DSA_TK = 512
DSA_TA = 512
DSA_VROWS = HEAD_DIM + 16
DSA_QB = 128
BAND_QB = 128


def _cparams(sem):
    return pltpu.CompilerParams(dimension_semantics=sem, vmem_limit_bytes=VMEM_LIMIT)


def _gelu(x):
    return 0.5 * x * (1.0 + jnp.tanh(math.sqrt(2.0 / math.pi) * (x + 0.044715 * (x * x * x))))


def _sigmoid(x):
    return 1.0 / (1.0 + jnp.exp(-x))


def _rms(x, g):
    ms = jnp.mean(x * x, axis=-1, keepdims=True)
    return (x * lax.rsqrt(ms + EPS)) * g


def _proj_kernel(x_ref, g_ref, w_ref, o_ref, h_ref):
    @pl.when(pl.program_id(1) == 0)
    def _():
        h_ref[...] = _rms(x_ref[...], g_ref[...]).astype(BF16)

    o_ref[...] = jnp.dot(h_ref[...], w_ref[...], preferred_element_type=F32)


def _proj(x, g, w, tm, tn):
    t, d = x.shape
    n = w.shape[1]
    return pl.pallas_call(
        _proj_kernel,
        grid=(t // tm, n // tn),
        in_specs=[pl.BlockSpec((tm, d), lambda i, j: (i, 0)),
                  pl.BlockSpec((1, d), lambda i, j: (0, 0)),
                  pl.BlockSpec((d, tn), lambda i, j: (0, j))],
        out_specs=pl.BlockSpec((tm, tn), lambda i, j: (i, j)),
        out_shape=jax.ShapeDtypeStruct((t, n), F32),
        scratch_shapes=[pltpu.VMEM((tm, d), BF16)],
        compiler_params=_cparams(("arbitrary", "arbitrary")),
        name="proj",
    )(x, g.reshape(1, d), w)


def _rot(x, c, sa, sb, half):
    n = x.shape[-1]
    return x * c + pltpu.roll(x, n - half, 1) * sa + pltpu.roll(x, half, 1) * sb


def _rope_kernel(qa_ref, ka_ref, va_ref, qi_ref, kw_ref, tab_ref,
                 qa_o, ka_o, kab_o, vab_o, qi_o, kw_o, kib_o):
    ca, saa, sba = tab_ref[0], tab_ref[1], tab_ref[2]
    ci, sai, sbi = tab_ref[3], tab_ref[4], tab_ref[5]
    ck, sak, sbk = tab_ref[6], tab_ref[7], tab_ref[8]
    for h in range(qa_ref.shape[1] // LANES):
        sl = slice(h * LANES, (h + 1) * LANES)
        qa_o[:, sl] = _rot(qa_ref[:, sl], ca, saa, sba, A_ROT // 2).astype(BF16)
    for h in range(ka_ref.shape[1] // LANES):
        sl = slice(h * LANES, (h + 1) * LANES)
        k = _rot(ka_ref[:, sl], ca, saa, sba, A_ROT // 2)
        ka_o[:, sl] = k
        kab_o[:, sl] = k.astype(BF16)
    vab_o[...] = va_ref[...].astype(BF16)
    for h in range(qi_ref.shape[1] // LANES):
        sl = slice(h * LANES, (h + 1) * LANES)
        qi_o[:, sl] = _rot(qi_ref[:, sl], ci, sai, sbi, IDX_ROT // 2).astype(BF16)
    kw = _rot(kw_ref[...], ck, sak, sbk, IDX_ROT // 2)
    kw_o[...] = kw
    kib_o[...] = kw.astype(BF16)


def _rope(p, tabs, cols, tm):
    t = p.shape[0]
    n_tab_blocks = tabs.shape[1] // tm
    d_qa, d_kv, d_qi = A_KV * A_GROUP * HEAD_DIM, A_KV * HEAD_DIM, IDX_HEADS * IDX_DIM

    def col(width, off):
        assert off % width == 0
        return pl.BlockSpec((tm, width), lambda i: (i, off // width))

    def out(width):
        return pl.BlockSpec((tm, width), lambda i: (i, 0))

    return pl.pallas_call(
        _rope_kernel,
        grid=(t // tm,),
        in_specs=[col(d_qa, cols["qa"]), col(d_kv, cols["ka"]), col(d_kv, cols["va"]),
                  col(d_qi, cols["qi"]), col(LANES, cols["kw"]),
                  pl.BlockSpec((9, tm, LANES), lambda i: (0, i % n_tab_blocks, 0))],
        out_specs=[out(d_qa), out(d_kv), out(d_kv), out(d_kv), out(d_qi), out(LANES), out(LANES)],
        out_shape=[jax.ShapeDtypeStruct((t, d_qa), BF16),
                   jax.ShapeDtypeStruct((t, d_kv), F32),
                   jax.ShapeDtypeStruct((t, d_kv), BF16),
                   jax.ShapeDtypeStruct((t, d_kv), BF16),
                   jax.ShapeDtypeStruct((t, d_qi), BF16),
                   jax.ShapeDtypeStruct((t, LANES), F32),
                   jax.ShapeDtypeStruct((t, LANES), BF16)],
        compiler_params=_cparams(("arbitrary",)),
        name="rope",
    )(p, p, p, p, p, tabs)


def _rope_tables(pos):
    pos = pos.astype(F32)[:, None]

    def cs(rot):
        half = rot // 2
        inv = jnp.float32(ROPE_THETA) ** (-jnp.arange(half, dtype=F32) / half)
        ang = pos * inv[None, :]
        return jnp.cos(ang), jnp.sin(ang)

    def tabs(rot, width):
        half = rot // 2
        c, s = cs(rot)
        n = pos.shape[0]
        one = jnp.ones((n, width - rot), F32)
        zero = jnp.zeros((n, width - rot), F32)
        zh = jnp.zeros((n, half), F32)
        return (jnp.concatenate([c, c, one], 1), jnp.concatenate([-s, zh, zero], 1),
                jnp.concatenate([zh, s, zero], 1))

    ca, saa, sba = tabs(A_ROT, HEAD_DIM)
    c64, sa64, sb64 = tabs(IDX_ROT, IDX_DIM)
    n = pos.shape[0]
    one64, zero64 = jnp.ones((n, IDX_DIM), F32), jnp.zeros((n, IDX_DIM), F32)
    return jnp.stack([ca, saa, sba,
                      jnp.concatenate([c64, c64], 1), jnp.concatenate([sa64, sa64], 1),
                      jnp.concatenate([sb64, sb64], 1),
                      jnp.concatenate([c64, one64], 1), jnp.concatenate([sa64, zero64], 1),
                      jnp.concatenate([sb64, zero64], 1)])


def _dsa_kernel(qaT_ref, qiT_ref, wiT_ref, k_ref, vT_ref, ki_ref, o_ref, key_ref, acc_ref,
                s0_ref, s1_ref, p0_ref, p1_ref, a0_ref, a1_ref,
                *, s_valid, q_pos0, topk):
    qb, tk = DSA_QB, DSA_TK
    j = pl.program_id(1)
    q_first = q_pos0 + j * qb
    qpos = q_first + lax.broadcasted_iota(I32, (1, qb), 1)
    assert CHUNK == 64 and tk == 512
    n_adm = jnp.minimum(((qpos >> 6) + 1) * CHUNK, s_valid)
    n_max = jnp.minimum((((q_first + qb - 1) >> 6) + 1) * CHUNK, s_valid)
    nkt = (n_max + tk - 1) >> 9

    qi = qiT_ref[0]
    rhs = jnp.concatenate([qi[h * IDX_DIM:(h + 1) * IDX_DIM, :] for h in range(IDX_HEADS)], axis=1)
    rhs = jnp.concatenate([rhs, jnp.zeros((LANES - IDX_DIM, IDX_HEADS * qb), BF16)], axis=0)
    wi = wiT_ref[0]

    def p1(kt, carry):
        off = pl.multiple_of(kt * tk, tk)
        kid = ki_ref[0, pl.ds(off, tk), :]
        s_all = jnp.dot(kid, rhs, preferred_element_type=F32)
        acc = wi[0:1, :] * jnp.maximum(s_all[:, 0:qb], 0.0)
        for h in range(1, IDX_HEADS):
            acc = acc + wi[h:h + 1, :] * jnp.maximum(s_all[:, h * qb:(h + 1) * qb], 0.0)
        score = acc * IDX_SCALE
        kpos = off + lax.broadcasted_iota(I32, (tk, 1), 0)
        score = jnp.where(kpos < n_adm, score, -jnp.inf)
        u = pltpu.bitcast(score, I32)
        key_ref[pl.ds(off, tk), :] = u ^ (lax.shift_right_arithmetic(u, 31) & 0x7FFFFFFF)
        return carry

    lax.fori_loop(0, nkt, p1, 0)

    def count_ge(cand):
        def body(kt, cnt):
            blk = key_ref[pl.ds(pl.multiple_of(kt * tk, tk), tk), :]
            m = jnp.where(blk >= cand, 1, 0).astype(I32)
            return cnt + jnp.sum(m.reshape(tk // SUBLANES, SUBLANES, qb), axis=0)

        cnt8 = lax.fori_loop(0, nkt, body, jnp.zeros((SUBLANES, qb), I32))
        return jnp.sum(cnt8, axis=0, keepdims=True)

    def bit_body(i, carry):
        prefix, n_ge = carry
        cand_u = prefix | lax.shift_left(jnp.int32(1), 31 - i)
        cnt = count_ge(cand_u ^ INT32_MIN)
        ok = cnt >= topk
        return jnp.where(ok, cand_u, prefix), jnp.where(ok, cnt, n_ge)

    prefix, n_ge = lax.fori_loop(0, 32, bit_body, (jnp.zeros((1, qb), I32), jnp.broadcast_to(nkt * tk, (1, qb))))
    thr = prefix ^ INT32_MIN
    cut_ties = jnp.max(jnp.where((n_ge > topk) & (thr > KEY_NEG_INF), 1, 0)) > 0

    @pl.when(cut_ties)
    def _():
        r = lax.broadcasted_iota(I32, (tk, tk), 0)
        c = lax.broadcasted_iota(I32, (tk, tk), 1)
        tri = jnp.where(c <= r, 1.0, 0.0).astype(BF16)
        take = (topk - count_ge(thr + 1)).astype(F32)

        def body(kt, seen):
            off = pl.multiple_of(kt * tk, tk)
            blk = key_ref[pl.ds(off, tk), :]
            eq = blk == thr
            incl = jnp.dot(tri, jnp.where(eq, 1.0, 0.0).astype(BF16), preferred_element_type=F32)
            drop = eq & (seen + incl > take)
            key_ref[pl.ds(off, tk), :] = jnp.where(drop, thr - 1, blk)
            return seen + incl[tk - 1:tk, :]

        lax.fori_loop(0, nkt, body, jnp.zeros((1, qb), F32))

    thr_lo = jnp.maximum(thr, KEY_NEG_INF + 1)
    qa = qaT_ref[0]
    ng = A_GROUP * qb
    qg = [jnp.concatenate([qa[(g * A_GROUP + hh) * HEAD_DIM:(g * A_GROUP + hh + 1) * HEAD_DIM, :]
                           for hh in range(A_GROUP)], axis=1) for g in range(A_KV)]
    acc_ref[...] = jnp.zeros(acc_ref.shape, F32)

    c_exp = ATTN_SCALE * math.log2(math.e)

    ta = DSA_TA
    n_steps = nkt * (tk // ta)
    last = n_steps - 1
    bufs = ((s0_ref, p0_ref, a0_ref), (s1_ref, p1_ref, a1_ref))

    def logits(step, s_ref):
        off = pl.multiple_of(jnp.minimum(step, last) * ta, ta)
        for g in range(A_KV):
            kg = k_ref[0, pl.ds(off, ta), g * HEAD_DIM:(g + 1) * HEAD_DIM]
            s_ref[g] = jnp.dot(kg, qg[g], preferred_element_type=F32)

    def numerators(step, s_ref, p_ref, a_ref, ms):
        off = pl.multiple_of(jnp.minimum(step, last) * ta, ta)
        blk = key_ref[pl.ds(off, ta), :]
        sel = (blk >= thr_lo) & (blk < KEY_POS_INF) & (step <= last)
        mask = jnp.concatenate([jnp.where(sel, 0.0, NEG_BIG)] * A_GROUP, axis=1)
        new = []
        for g in range(A_KV):
            s = s_ref[g] + mask
            m_new = jnp.maximum(ms[g], jnp.max(s, axis=0, keepdims=True))
            a_ref[g] = jnp.broadcast_to(jnp.exp2((ms[g] - m_new) * c_exp), (SUBLANES, ng))
            p_ref[g] = jnp.exp2((s - m_new).astype(BF16) * c_exp)
            new.append(m_new)
        return tuple(new)

    def weighted_values(step, p_ref, a_ref):
        off = pl.multiple_of(jnp.clip(step, 0, last) * ta, ta)
        for g in range(A_KV):
            vg = vT_ref[0, g * DSA_VROWS:(g + 1) * DSA_VROWS, pl.ds(off, ta)]
            acc_ref[g] = a_ref[g, 0:1, :] * acc_ref[g] + jnp.dot(vg, p_ref[g], preferred_element_type=F32)

    p1_ref[...] = jnp.zeros(p1_ref.shape, BF16)
    a1_ref[...] = jnp.ones(a1_ref.shape, F32)
    logits(0, s0_ref)

    def two_steps(i2, ms):
        i = 2 * i2
        for par in range(2):
            cur, nxt = bufs[par], bufs[1 - par]
            logits(i + par + 1, nxt[0])
            ms = numerators(i + par, cur[0], cur[1], cur[2], ms)
            weighted_values(i + par - 1, nxt[1], nxt[2])
        return ms

    init = (jnp.full((1, ng), NEG_BIG, F32),) * A_KV
    n_pairs = (n_steps + 1) // 2
    lax.fori_loop(0, n_pairs, two_steps, init)
    weighted_values(2 * n_pairs - 1, p1_ref, a1_ref)
    for g in range(A_KV):
        o = acc_ref[g, 0:HEAD_DIM, :] / acc_ref[g, HEAD_DIM:HEAD_DIM + 1, :]
        for hh in range(A_GROUP):
            h = g * A_GROUP + hh
            o_ref[0, h * HEAD_DIM:(h + 1) * HEAD_DIM, :] = o[:, hh * qb:(hh + 1) * qb].astype(o_ref.dtype)


def _dsa(qaT, qiT, wiT, k, vT, ki, *, s_valid, q_pos0, topk):
    b, dq, sq = qaT.shape
    s_pad = k.shape[1]
    assert s_pad % DSA_TK == 0 and s_pad % DSA_TA == 0 and sq % DSA_QB == 0 and topk <= DSA_TK
    kern = functools.partial(_dsa_kernel, s_valid=s_valid, q_pos0=q_pos0, topk=topk)
    return pl.pallas_call(
        kern,
        grid=(b, sq // DSA_QB),
        in_specs=[pl.BlockSpec((1, dq, DSA_QB), lambda i, j: (i, 0, j)),
                  pl.BlockSpec((1, qiT.shape[1], DSA_QB), lambda i, j: (i, 0, j)),
                  pl.BlockSpec((1, IDX_HEADS, DSA_QB), lambda i, j: (i, 0, j)),
                  pl.BlockSpec((1, s_pad, k.shape[2]), lambda i, j: (i, 0, 0)),
                  pl.BlockSpec((1, vT.shape[1], s_pad), lambda i, j: (i, 0, 0)),
                  pl.BlockSpec((1, s_pad, LANES), lambda i, j: (i, 0, 0))],
        out_specs=pl.BlockSpec((1, dq, DSA_QB), lambda i, j: (i, 0, j)),
        out_shape=jax.ShapeDtypeStruct((b, dq, sq), BF16),
        scratch_shapes=[pltpu.VMEM((s_pad, DSA_QB), I32),
                        pltpu.VMEM((A_KV, DSA_VROWS, A_GROUP * DSA_QB), F32)]
                       + [pltpu.VMEM((A_KV, DSA_TA, A_GROUP * DSA_QB), F32)] * 2
                       + [pltpu.VMEM((A_KV, DSA_TA, A_GROUP * DSA_QB), BF16)] * 2
                       + [pltpu.VMEM((A_KV, SUBLANES, A_GROUP * DSA_QB), F32)] * 2,
        compiler_params=_cparams(("arbitrary", "arbitrary")),
        name="dsa",
    )(qaT, qiT, wiT, k, vT, ki)


def _cache_prep_kernel(ck_ref, cv_ref, cki_ref, kn_ref, vn_ref, kin_ref, k_o, vT_o, ki_o, *, n_cache_tiles):
    kt = pl.program_id(1)
    tk = k_o.shape[1]
    extra = DSA_VROWS - HEAD_DIM
    ones_rows = jnp.where(lax.broadcasted_iota(I32, (extra, tk), 0) == 0, 1.0, 0.0).astype(BF16)

    def emit(k_head, v_head, ki):
        for g in range(A_KV):
            k_o[0, :, g * HEAD_DIM:(g + 1) * HEAD_DIM] = k_head(g).astype(BF16)
            vT_o[0, g * DSA_VROWS:g * DSA_VROWS + HEAD_DIM, :] = v_head(g).astype(F32).T.astype(BF16)
            vT_o[0, g * DSA_VROWS + HEAD_DIM:(g + 1) * DSA_VROWS, :] = ones_rows
        ki_o[0] = ki

    @pl.when(kt < n_cache_tiles)
    def _():
        ki = jnp.concatenate([cki_ref[0, 0], jnp.zeros((tk, LANES - IDX_DIM), F32)], axis=1)
        emit(lambda g: ck_ref[0, 0, :, g, :], lambda g: cv_ref[0, 0, :, g, :], ki.astype(BF16))

    @pl.when(kt >= n_cache_tiles)
    def _():
        t = kn_ref.shape[0]
        rows = lambda a: jnp.concatenate([a, jnp.zeros((tk - t, a.shape[1]), a.dtype)], axis=0)
        kn, vn = rows(kn_ref[...]), rows(vn_ref[...])
        head = lambda a: lambda g: a[:, g * HEAD_DIM:(g + 1) * HEAD_DIM]
        emit(head(kn), head(vn), rows(kin_ref[...]))


def _cache_prep(cache_k, cache_v, cache_ki, layer, kn, vn, kin, b, t):
    tk = DSA_TK
    past = cache_k.shape[2]
    assert past % tk == 0 and t <= tk
    nct = past // tk
    s_pad = past + tk
    cidx = lambda i, j: jnp.minimum(j, nct - 1)
    return pl.pallas_call(
        functools.partial(_cache_prep_kernel, n_cache_tiles=nct),
        grid=(b, nct + 1),
        in_specs=[pl.BlockSpec((1, 1, tk, A_KV, HEAD_DIM), lambda i, j: (layer, i, cidx(i, j), 0, 0)),
                  pl.BlockSpec((1, 1, tk, A_KV, HEAD_DIM), lambda i, j: (layer, i, cidx(i, j), 0, 0)),
                  pl.BlockSpec((1, 1, tk, IDX_DIM), lambda i, j: (layer, i, cidx(i, j), 0)),
                  pl.BlockSpec((t, A_KV * HEAD_DIM), lambda i, j: (i, 0)),
                  pl.BlockSpec((t, A_KV * HEAD_DIM), lambda i, j: (i, 0)),
                  pl.BlockSpec((t, LANES), lambda i, j: (i, 0))],
        out_specs=[pl.BlockSpec((1, tk, A_KV * HEAD_DIM), lambda i, j: (i, j, 0)),
                   pl.BlockSpec((1, A_KV * DSA_VROWS, tk), lambda i, j: (i, 0, j)),
                   pl.BlockSpec((1, tk, LANES), lambda i, j: (i, j, 0))],
        out_shape=[jax.ShapeDtypeStruct((b, s_pad, A_KV * HEAD_DIM), BF16),
                   jax.ShapeDtypeStruct((b, A_KV * DSA_VROWS, s_pad), BF16),
                   jax.ShapeDtypeStruct((b, s_pad, LANES), BF16)],
        compiler_params=_cparams(("arbitrary", "arbitrary")),
        name="cache_prep",
    )(cache_k, cache_v, cache_ki, kn, vn, kin)


def _band_heads(q, k_head, v_head, bias_ref, n_hidden, o_ref, s_ref):
    nh = q.shape[1] // HEAD_DIM
    tk = bias_ref.shape[2]
    hidden = None if n_hidden is None else lax.broadcasted_iota(I32, (1, tk), 1) < n_hidden
    for h in range(nh):
        sl = slice(h * HEAD_DIM, (h + 1) * HEAD_DIM)
        s = lax.dot_general(q[:, sl].astype(BF16), k_head(h), (((1,), (1,)), ((), ())),
                            preferred_element_type=F32) * ATTN_SCALE + bias_ref[h]
        if hidden is not None:
            s = jnp.where(hidden, NEG_BIG, s)
        s_ref[h] = s
    ms = [jnp.max(s_ref[h], axis=1, keepdims=True) for h in range(nh)]
    ls = []
    for h in range(nh):
        p = jnp.exp(s_ref[h] - ms[h])
        s_ref[h] = p
        ls.append(jnp.sum(p, axis=1, keepdims=True))
    for h in range(nh):
        sl = slice(h * HEAD_DIM, (h + 1) * HEAD_DIM)
        w = (s_ref[h] / ls[h]).astype(BF16)
        o_ref[:, sl] = jnp.dot(w, v_head(h), preferred_element_type=F32).astype(o_ref.dtype)


def _band_prompt_kernel(*refs, n_tiles):
    q_ref = refs[0]
    k_refs = refs[1:1 + n_tiles]
    v_refs = refs[1 + n_tiles:1 + 2 * n_tiles]
    bias_ref = refs[1 + 2 * n_tiles]
    o_ref = refs[2 + 2 * n_tiles]
    s_ref = refs[3 + 2 * n_tiles]
    j = pl.program_id(1)
    n_hidden = jnp.maximum(n_tiles - 1 - j, 0) * q_ref.shape[0]
    k_all = jnp.concatenate([r[...] for r in k_refs], axis=0).astype(BF16)
    v_all = jnp.concatenate([r[...] for r in v_refs], axis=0).astype(BF16)
    head = lambda a: lambda h: a[:, h * HEAD_DIM:(h + 1) * HEAD_DIM]
    _band_heads(q_ref[...], head(k_all), head(v_all), bias_ref, n_hidden, o_ref, s_ref)


def _band_prompt(p, bias, cols, batch, seq):
    qb = BAND_QB
    n_tiles = BAND_CHUNKS * CHUNK // qb + 1
    width = bias.shape[0] * HEAD_DIM
    nq = seq // qb
    cq, ck, cv = cols["qb"] // width, cols["kb"] // width, cols["vb"] // width
    assert cols["qb"] % width == 0 and cols["kb"] % width == 0 and cols["vb"] % width == 0

    def kv_spec(c, i):
        return pl.BlockSpec((qb, width), lambda b, j: (b * nq + jnp.maximum(j - (n_tiles - 1) + i, 0), c))

    return pl.pallas_call(
        functools.partial(_band_prompt_kernel, n_tiles=n_tiles),
        grid=(batch, nq),
        in_specs=([pl.BlockSpec((qb, width), lambda b, j: (b * nq + j, cq))]
                  + [kv_spec(ck, i) for i in range(n_tiles)]
                  + [kv_spec(cv, i) for i in range(n_tiles)]
                  + [pl.BlockSpec(bias.shape, lambda b, j: (0, 0, 0))]),
        out_specs=pl.BlockSpec((qb, width), lambda b, j: (b * nq + j, 0)),
        out_shape=jax.ShapeDtypeStruct((batch * seq, width), BF16),
        compiler_params=_cparams(("arbitrary", "arbitrary")),
        scratch_shapes=[pltpu.VMEM(bias.shape, F32)],
        name="band_prompt",
    )(*([p] * (1 + 2 * n_tiles)), bias)


def _band_sample_kernel(q_ref, k_ref, v_ref, ck_ref, cv_ref, bias_ref, o_ref, s_ref):
    def head(cache_ref, new_ref):
        return lambda h: jnp.concatenate(
            [cache_ref[0, 0, :, h, :], new_ref[:, h * HEAD_DIM:(h + 1) * HEAD_DIM]], axis=0).astype(BF16)

    _band_heads(q_ref[...], head(ck_ref, k_ref), head(cv_ref, v_ref), bias_ref, None, o_ref, s_ref)


def _band_sample(p, cache_k, cache_v, layer, bias, cols, batch, t):
    nh = bias.shape[0]
    width = nh * HEAD_DIM
    nbc = cache_k.shape[2]
    cq, ck, cv = cols["qb"] // width, cols["kb"] // width, cols["vb"] // width
    return pl.pallas_call(
        _band_sample_kernel,
        grid=(batch,),
        in_specs=[pl.BlockSpec((t, width), lambda b: (b, cq)),
                  pl.BlockSpec((t, width), lambda b: (b, ck)),
                  pl.BlockSpec((t, width), lambda b: (b, cv)),
                  pl.BlockSpec((1, 1, nbc, nh, HEAD_DIM), lambda b: (layer, b, 0, 0, 0)),
                  pl.BlockSpec((1, 1, nbc, nh, HEAD_DIM), lambda b: (layer, b, 0, 0, 0)),
                  pl.BlockSpec(bias.shape, lambda b: (0, 0, 0))],
        out_specs=pl.BlockSpec((t, width), lambda b: (b, 0)),
        out_shape=jax.ShapeDtypeStruct((batch * t, width), BF16),
        compiler_params=_cparams(("arbitrary",)),
        scratch_shapes=[pltpu.VMEM(bias.shape, F32)],
        name="band_sample",
    )(p, p, p, cache_k, cache_v, bias)


def _band_bias(table, q0, nq, k0, nk):
    n = nq + nk
    j = jnp.arange(n)
    g = table[:, jnp.clip(j + (q0 - k0 - nk + 1), -REL_CLIP, REL_CLIP) + REL_CLIP].astype(F32)
    hank = jnp.tile(g, (1, nq + 1))[:, :nq * (n + 1)].reshape(-1, nq, n + 1)[:, :, :nk]
    bias = hank[:, :, ::-1]
    qc = (q0 + jnp.arange(nq))[:, None] // CHUNK
    kc = (k0 + jnp.arange(nk))[None, :] // CHUNK
    vis = (kc <= qc) & (kc >= qc - BAND_CHUNKS)
    return jnp.where(vis[None], bias, NEG_BIG)


def _ssm_kernel(u_ref, t_ref, p_ref, q_ref, a_ref, h0_ref, y_ref, hre_o, him_o, sre_ref, sim_ref,
                *, seg, tiles_per_seq):
    L = SSM_L
    x = jnp.concatenate([u_ref[:, s, :] for s in range(L)], axis=1).astype(BF16)
    sre_ref[...] = jnp.dot(x, p_ref[0, 0], preferred_element_type=F32)
    sim_ref[...] = jnp.dot(x, p_ref[0, 1], preferred_element_type=F32)
    rows, w = sre_ref.shape
    kseg = lax.broadcasted_iota(I32, (SUBLANES, 1), 0) % seg
    apk_re, apk_im = a_ref[0, 0, 0:SUBLANES, :], a_ref[0, 1, 0:SUBLANES, :]

    def cmad(x_re, x_im, a_re, a_im, y_re, y_im):
        return x_re + a_re * y_re - a_im * y_im, x_im + a_re * y_im + a_im * y_re

    def tile(i, carry):
        r = pl.ds(pl.multiple_of(i * SUBLANES, SUBLANES), SUBLANES)
        x_re, x_im = sre_ref[r, :], sim_ref[r, :]
        for n, d in enumerate((1, 2, 4)):
            if d < seg:
                ad_re = a_ref[0, 0, SUBLANES + n:SUBLANES + n + 1, :]
                ad_im = a_ref[0, 1, SUBLANES + n:SUBLANES + n + 1, :]
                sh_re = jnp.where(kseg >= d, pltpu.roll(x_re, d, 0), 0.0)
                sh_im = jnp.where(kseg >= d, pltpu.roll(x_im, d, 0), 0.0)
                x_re, x_im = cmad(x_re, x_im, ad_re, ad_im, sh_re, sh_im)
        if tiles_per_seq > 1:
            first = (i % tiles_per_seq) == 0
            hin_re = jnp.where(first, h0_ref[0, 0], carry[0])
            hin_im = jnp.where(first, h0_ref[0, 1], carry[1])
        else:
            hin_re, hin_im = h0_ref[0, 0, r, :], h0_ref[0, 1, r, :]
        inc_re, inc_im = cmad(x_re, x_im, apk_re, apk_im, hin_re, hin_im)
        sre_ref[r, :] = jnp.where(kseg == 0, hin_re, pltpu.roll(inc_re, 1, 0))
        sim_ref[r, :] = jnp.where(kseg == 0, hin_im, pltpu.roll(inc_im, 1, 0))
        hre_o[0, r, :] = inc_re
        him_o[0, r, :] = inc_im
        return inc_re[SUBLANES - 1:SUBLANES, :], inc_im[SUBLANES - 1:SUBLANES, :]

    zero = jnp.zeros((1, w), F32)
    lax.fori_loop(0, rows // SUBLANES, tile, (zero, zero))
    y = (jnp.dot(x, t_ref[0], preferred_element_type=F32)
         + jnp.dot(sre_ref[...].astype(BF16), q_ref[0, 0], preferred_element_type=F32)
         + jnp.dot(sim_ref[...].astype(BF16), q_ref[0, 1], preferred_element_type=F32))
    for t in range(L):
        y_ref[:, t, :] = y[:, t * LANES:(t + 1) * LANES]


def _ssm(p3, uc_col, tmat, pmat, qmat, apow, h0, rows_blk, seg, tiles_per_seq):
    rows_total = p3.shape[0]
    nb, _, _, w = pmat.shape
    kl = SSM_L * LANES
    h0_rows = h0.shape[2] // (rows_total // rows_blk)
    hspec = pl.BlockSpec((1, rows_blk, w), lambda i, r: (i, r, 0))
    return pl.pallas_call(
        functools.partial(_ssm_kernel, seg=seg, tiles_per_seq=tiles_per_seq),
        grid=(nb, rows_total // rows_blk),
        in_specs=[pl.BlockSpec((rows_blk, SSM_L, LANES), lambda i, r: (r, 0, uc_col // LANES + i)),
                  pl.BlockSpec((1, kl, kl), lambda i, r: (i, 0, 0)),
                  pl.BlockSpec((1, 2, kl, w), lambda i, r: (i, 0, 0, 0)),
                  pl.BlockSpec((1, 2, w, kl), lambda i, r: (i, 0, 0, 0)),
                  pl.BlockSpec((1, 2, 2 * SUBLANES, w), lambda i, r: (i, 0, 0, 0)),
                  pl.BlockSpec((1, 2, h0_rows, w), lambda i, r: (i, 0, r, 0))],
        out_specs=[pl.BlockSpec((rows_blk, SSM_L, LANES), lambda i, r: (r, 0, i)), hspec, hspec],
        out_shape=[jax.ShapeDtypeStruct((rows_total, SSM_L, nb * LANES), F32),
                   jax.ShapeDtypeStruct((nb, rows_total, w), F32),
                   jax.ShapeDtypeStruct((nb, rows_total, w), F32)],
        scratch_shapes=[pltpu.VMEM((rows_blk, w), F32), pltpu.VMEM((rows_blk, w), F32)],
        compiler_params=_cparams(("arbitrary", "arbitrary")),
        name="ssm",
    )(p3, tmat, pmat, qmat, apow, h0)


def _ssm_weights(a_re, a_im, log_dt, b_re, b_im, c_re, c_im, d):
    hp = lax.Precision.HIGHEST
    g, p = a_re.shape
    gc = b_re.shape[2]
    L = SSM_L
    a_re, a_im = a_re.astype(F32), a_im.astype(F32)
    dt = jnp.exp(log_dt.astype(F32))[:, None]
    mag = jnp.exp(dt * a_re)
    ab_re, ab_im = mag * jnp.cos(dt * a_im), mag * jnp.sin(dt * a_im)
    n_re = ab_re - 1.0
    n_im = ab_im
    den = a_re * a_re + a_im * a_im
    cc_re = (n_re * a_re + n_im * a_im) / den
    cc_im = (n_im * a_re - n_re * a_im) / den
    b_re, b_im = b_re.astype(F32), b_im.astype(F32)
    bb_re = cc_re[..., None] * b_re - cc_im[..., None] * b_im
    bb_im = cc_re[..., None] * b_im + cc_im[..., None] * b_re
    c_re, c_im = c_re.astype(F32), c_im.astype(F32)

    def cmul(x, y):
        return x[0] * y[0] - x[1] * y[1], x[0] * y[1] + x[1] * y[0]

    rep = lambda a: jnp.broadcast_to(a[None], (L,) + a.shape)
    pw_re, pw_im = lax.associative_scan(cmul, (rep(ab_re), rep(ab_im)), axis=0)
    pw_re = jnp.concatenate([jnp.ones((1, g, p), F32), pw_re], axis=0)
    pw_im = jnp.concatenate([jnp.zeros((1, g, p), F32), pw_im], axis=0)
    ca_re = c_re[None] * pw_re[:, :, None, :] - c_im[None] * pw_im[:, :, None, :]
    ca_im = c_re[None] * pw_im[:, :, None, :] + c_im[None] * pw_re[:, :, None, :]
    taps = (jnp.einsum("tgop,gpi->tgoi", ca_re[:L], bb_re, precision=hp)
            - jnp.einsum("tgop,gpi->tgoi", ca_im[:L], bb_im, precision=hp))
    taps = taps.at[0].add(jax.vmap(jnp.diag)(d.astype(F32)))
    s_idx = jnp.arange(L)[:, None]
    t_idx = jnp.arange(L)[None, :]
    lag = t_idx - s_idx
    tt = jnp.where((lag >= 0)[:, :, None, None, None], taps[jnp.clip(lag, 0, L - 1)], 0.0)
    gb = SSM_GB
    nb = g // gb
    n_x = L * gb * gc
    x = jnp.arange(n_x)
    x_grp, x_loc = (x // gc) % gb, (x // (gb * gc)) * gc + x % gc
    place_x = ((x_grp[None, :, None] == jnp.arange(gb)[:, None, None])
               & (x_loc[None, :, None] == jnp.arange(L * gc)[None, None, :])).astype(F32)
    q = jnp.arange(gb * p)
    place_q = ((q[None, :, None] // p == jnp.arange(gb)[:, None, None])
               & (q[None, :, None] % p == jnp.arange(p)[None, None, :])).astype(F32)
    blk = lambda a: a.reshape((nb, gb) + a.shape[1:])
    t_grp = blk(tt.transpose(2, 0, 4, 1, 3).reshape(g, L * gc, L * gc))
    tmat = jnp.einsum("gxr,bgrc,gyc->bxy", place_x, t_grp, place_x)
    rev = L - 1 - jnp.arange(L)
    pin_re = pw_re[rev][:, :, :, None] * bb_re[None] - pw_im[rev][:, :, :, None] * bb_im[None]
    pin_im = pw_re[rev][:, :, :, None] * bb_im[None] + pw_im[rev][:, :, :, None] * bb_re[None]
    to_p = lambda a: jnp.einsum("gxr,bgrp,gqp->bxq", place_x,
                                blk(a.transpose(1, 0, 3, 2).reshape(g, L * gc, p)), place_q)
    to_q = lambda a: jnp.einsum("gqp,bgpc,gyc->bqy", place_q,
                                blk(a.transpose(1, 3, 0, 2).reshape(g, p, L * gc)), place_x)
    pmat = jnp.stack([to_p(pin_re), to_p(pin_im)], axis=1)
    qmat = jnp.stack([to_q(ca_re[1:]), -to_q(ca_im[1:])], axis=1)
    rep8 = lambda a: jnp.broadcast_to(a[None], (SUBLANES,) + a.shape)
    apl_re, apl_im = lax.associative_scan(cmul, (rep8(pw_re[L]), rep8(pw_im[L])), axis=0)
    apl = jnp.stack([apl_re, apl_im]).reshape(2, SUBLANES, nb, gb * p)
    return tmat.astype(BF16), pmat.astype(BF16), qmat.astype(BF16), apl


def _ssm_apply(p, uc_col, b, t, h0_re, h0_im, weights, one_seq_per_step):
    tmat, pmat, qmat, apl = weights
    nb, w = pmat.shape[0], pmat.shape[3]
    L = SSM_L
    nc = t // L
    assert t % L == 0 and (nc % SUBLANES == 0 or SUBLANES % nc == 0)
    seg = min(nc, SUBLANES)
    if one_seq_per_step:
        assert nc % SUBLANES == 0
        rows_blk, tiles_per_seq, reps = nc, nc // SUBLANES, SUBLANES
    else:
        rows_blk, tiles_per_seq, reps = b * nc, 1, nc
        assert nc <= SUBLANES and rows_blk % SUBLANES == 0
    pw_rows = [k % seg for k in range(SUBLANES)] + [0, 1, 3] + [0] * (SUBLANES - 3)
    apow = jnp.stack([apl[:, k] for k in pw_rows], axis=1).transpose(2, 0, 1, 3)

    def pack(h):
        h = h.astype(F32).reshape(b, nb, w).transpose(1, 0, 2)
        return jnp.repeat(h, reps, axis=1)

    h0 = jnp.stack([pack(h0_re), pack(h0_im)], axis=1)
    p3 = p.reshape(b * nc, L, p.shape[1])
    y, hre, him = _ssm(p3, uc_col, tmat, pmat, qmat, apow, h0, rows_blk, seg, tiles_per_seq)

    def last(h):
        return h.reshape(nb, b, nc, w)[:, :, nc - 1].transpose(1, 0, 2).reshape(b, nb * SSM_GB, C_STATE)

    return y.reshape(b * t, nb * LANES), last(hre), last(him)


def _glu_kernel(y_ref, w_ref, b_ref, o_ref):
    z = _gelu(y_ref[...])
    a = jnp.dot(z.astype(BF16), w_ref[...], preferred_element_type=F32) + b_ref[...]
    o_ref[...] = (z * _sigmoid(a)).astype(o_ref.dtype)


def _glu(y, w, b, tm):
    t, n = y.shape
    return pl.pallas_call(
        _glu_kernel,
        grid=(t // tm,),
        in_specs=[pl.BlockSpec((tm, n), lambda i: (i, 0)),
                  pl.BlockSpec((n, n), lambda i: (0, 0)),
                  pl.BlockSpec((1, n), lambda i: (0, 0))],
        out_specs=pl.BlockSpec((tm, n), lambda i: (i, 0)),
        out_shape=jax.ShapeDtypeStruct((t, n), BF16),
        compiler_params=_cparams(("arbitrary",)),
        name="glu",
    )(y, w, b.reshape(1, n))


def _merge_kernel(oa_ref, ob_ref, oc_ref, ga_ref, gb_ref, gc_ref, w_ref, o_ref):
    acc = None
    for n, (o, g) in enumerate(((oa_ref, ga_ref), (ob_ref, gb_ref), (oc_ref, gc_ref))):
        br = jnp.dot(o[...], w_ref[n], preferred_element_type=F32)
        t = _sigmoid(g[...]) * br
        acc = t if acc is None else acc + t
    o_ref[...] = acc.astype(o_ref.dtype)


def _merge(oa, ob, oc, p, w, g_col, tm, tn):
    t, m = oa.shape
    d = w.shape[2]
    assert g_col % tn == 0 and d % tn == 0

    def gspec(n):
        return pl.BlockSpec((tm, tn), lambda i, j: (i, (g_col + n * d) // tn + j))

    ospec = pl.BlockSpec((tm, m), lambda i, j: (i, 0))
    return pl.pallas_call(
        _merge_kernel,
        grid=(t // tm, d // tn),
        in_specs=[ospec, ospec, ospec, gspec(0), gspec(1), gspec(2),
                  pl.BlockSpec((N_BRANCH, m, tn), lambda i, j: (0, 0, j))],
        out_specs=pl.BlockSpec((tm, tn), lambda i, j: (i, j)),
        out_shape=jax.ShapeDtypeStruct((t, d), BF16),
        compiler_params=_cparams(("arbitrary", "arbitrary")),
        name="merge",
    )(oa, ob, oc, p, p, p, w)


def _mmres_kernel(a_ref, w_ref, r_ref, o_ref):
    o_ref[...] = r_ref[...] + jnp.dot(a_ref[...], w_ref[...], preferred_element_type=F32)


def _mmres(a, w, res, tm, tn):
    t, k = a.shape
    n = w.shape[1]
    return pl.pallas_call(
        _mmres_kernel,
        grid=(t // tm, n // tn),
        in_specs=[pl.BlockSpec((tm, k), lambda i, j: (i, 0)),
                  pl.BlockSpec((k, tn), lambda i, j: (0, j)),
                  pl.BlockSpec((tm, tn), lambda i, j: (i, j))],
        out_specs=pl.BlockSpec((tm, tn), lambda i, j: (i, j)),
        out_shape=jax.ShapeDtypeStruct((t, n), F32),
        compiler_params=_cparams(("arbitrary", "arbitrary")),
        name="mmres",
    )(a, w, res)


def _ffn_in_kernel(x_ref, g_ref, wu_ref, wv_ref, wc_ref, bc_ref, st_ref, o_ref, tail_ref,
                   h_ref, up_ref, carry_ref, *, ns, ts, tiles_per_seq):
    i, j = pl.program_id(0), pl.program_id(1)
    pad = SUBLANES

    @pl.when(j == 0)
    def _():
        h_ref[...] = _rms(x_ref[...], g_ref[...]).astype(BF16)

    @pl.when((i == 0) & (j == 0))
    def _():
        carry_ref[...] = jnp.zeros(carry_ref.shape, F32)

    h = h_ref[...]
    u = jnp.dot(h, wu_ref[...], preferred_element_type=F32)
    tn = u.shape[1]
    up_ref[:, pad:, :] = u.reshape(ns, ts, tn)
    first = (i % tiles_per_seq) == 0
    up_ref[:, pad - 2:pad, :] = jnp.where(first, st_ref[...], carry_ref[j])
    last2 = up_ref[:, ts + pad - 2:ts + pad, :]
    carry_ref[j] = last2
    tail_ref[0] = last2
    c = bc_ref[...].reshape(1, 1, tn)
    for tap in range(CONV_W):
        c = c + up_ref[:, pad - 2 + tap:pad - 2 + tap + ts, :] * wc_ref[tap:tap + 1, :].reshape(1, 1, tn)
    gl = _gelu(c).reshape(ns * ts, tn)
    v = jnp.dot(h, wv_ref[...], preferred_element_type=F32)
    o_ref[...] = (gl * v).astype(o_ref.dtype)


def _ffn_in(x, g, w, wc, bc, state, ns, ts, tn):
    t, d = x.shape
    f = w.shape[1] // 2
    tm = ns * ts
    nj = f // tn
    seq_len = t // state.shape[0]
    tiles_per_seq = max(seq_len // tm, 1)
    kern = functools.partial(_ffn_in_kernel, ns=ns, ts=ts, tiles_per_seq=tiles_per_seq)
    return pl.pallas_call(
        kern,
        grid=(t // tm, nj),
        in_specs=[pl.BlockSpec((tm, d), lambda i, j: (i, 0)),
                  pl.BlockSpec((1, d), lambda i, j: (0, 0)),
                  pl.BlockSpec((d, tn), lambda i, j: (0, j)),
                  pl.BlockSpec((d, tn), lambda i, j: (0, nj + j)),
                  pl.BlockSpec((CONV_W, tn), lambda i, j: (0, j)),
                  pl.BlockSpec((1, tn), lambda i, j: (0, j)),
                  pl.BlockSpec((ns, CONV_W - 1, tn), lambda i, j: (i // tiles_per_seq, 0, j))],
        out_specs=[pl.BlockSpec((tm, tn), lambda i, j: (i, j)),
                   pl.BlockSpec((1, ns, CONV_W - 1, tn), lambda i, j: (i, 0, 0, j))],
        out_shape=[jax.ShapeDtypeStruct((t, f), BF16),
                   jax.ShapeDtypeStruct((t // tm, ns, CONV_W - 1, f), F32)],
        scratch_shapes=[pltpu.VMEM((tm, d), BF16),
                        pltpu.VMEM((ns, ts + SUBLANES, tn), F32),
                        pltpu.VMEM((nj, ns, CONV_W - 1, tn), F32)],
        compiler_params=_cparams(("arbitrary", "arbitrary")),
        name="ffn_in",
    )(x, g.reshape(1, d), w, w, wc, bc.reshape(1, f), state)


def _norm_kernel(x_ref, g_ref, o_ref):
    o_ref[...] = _rms(x_ref[...], g_ref[...])


def _norm(x, g, tm):
    t, d = x.shape
    return pl.pallas_call(
        _norm_kernel,
        grid=(t // tm,),
        in_specs=[pl.BlockSpec((tm, d), lambda i: (i, 0)), pl.BlockSpec((1, d), lambda i: (0, 0))],
        out_specs=pl.BlockSpec((tm, d), lambda i: (i, 0)),
        out_shape=jax.ShapeDtypeStruct((t, d), F32),
        compiler_params=_cparams(("arbitrary",)),
        name="final_norm",
    )(x, g.reshape(1, d))


def _row_tile(t, cap=1024):
    tm = min(t, cap)
    while t % tm:
        tm //= 2
    return tm


def _pack_w_in(w_in, d_model):
    mix = d_model // 2
    sizes = dict(qa=mix, ka=A_KV * HEAD_DIM, va=A_KV * HEAD_DIM, qi=IDX_HEADS * IDX_DIM, ki=IDX_DIM,
                 wi=IDX_HEADS, qb=mix, kb=mix, vb=mix, uc=mix, g=N_BRANCH * d_model)
    src_order = ["qa", "ka", "va", "qi", "ki", "wi", "qb", "kb", "vb", "uc", "g"]
    src, off = {}, 0
    for name in src_order:
        src[name] = (off, sizes[name])
        off += sizes[name]
    assert off == w_in.shape[1]
    dst_order = ["qa", "qb", "kb", "vb", "uc", "g", "qi", "ka", "va", "ki", "wi"]
    cols, parts, off = {}, [], 0
    for name in dst_order:
        s, n = src[name]
        cols[name] = off
        parts.append(w_in[:, s:s + n])
        off += n
    cols["kw"] = cols["ki"]
    tn = 1280
    total = -(-off // tn) * tn
    parts.append(jnp.zeros((w_in.shape[0], total - off), w_in.dtype))
    return jnp.concatenate(parts, axis=1).astype(BF16), cols, tn


def _to_heads_T(x, b, t):
    return x.reshape(b, t, x.shape[1]).transpose(0, 2, 1)


def _pad_axis(x, axis, size):
    pad = [(0, 0)] * x.ndim
    pad[axis] = (0, size - x.shape[axis])
    return jnp.pad(x, pad)


def _layer(x, pos, prm, cache, layer, is_prompt):
    b, t, d = x.shape
    mix = d // 2
    xt = x.reshape(b * t, d)
    tm = _row_tile(b * t)
    cols = prm["cols"]
    p = _proj(xt, prm["norm1"], prm["w_in"], tm, prm["w_in_tn"])

    tabs = _rope_tables(pos)
    tm_r = _row_tile(t) if is_prompt else tm
    if not is_prompt:
        tabs = jnp.tile(tabs, (1, b, 1))
    qa, ka, kab, vab, qi, kw, kib = _rope(p, tabs, cols, tm_r)
    if is_prompt:
        sq, s_valid, q_pos0 = t, t, 0
        s_pad = -(-s_valid // DSA_TK) * DSA_TK
        k_all = _pad_axis(kab.reshape(b, t, -1), 1, s_pad)
        ki_all = _pad_axis(kib.reshape(b, t, LANES), 1, s_pad)
        vT = _pad_axis(vab.reshape(b, t, -1), 1, s_pad).transpose(0, 2, 1).reshape(b, A_KV, HEAD_DIM, s_pad)
        ones = jnp.zeros((b, A_KV, DSA_VROWS - HEAD_DIM, s_pad), BF16).at[:, :, 0].set(1.0)
        vT = jnp.concatenate([vT, ones], axis=2).reshape(b, A_KV * DSA_VROWS, s_pad)
    else:
        past = cache["a_k"].shape[2]
        sq, s_valid, q_pos0 = DSA_QB, past + t, past
        k_all, vT, ki_all = _cache_prep(cache["a_k"], cache["a_v"], cache["a_kidx"], layer, kab, vab, kib, b, t)
    qaT = _pad_axis(_to_heads_T(qa, b, t), 2, sq)
    qiT = _pad_axis(_to_heads_T(qi, b, t), 2, sq)
    wiT = _pad_axis(_to_heads_T(kw[:, IDX_DIM:IDX_DIM + IDX_HEADS], b, t), 2, sq)
    topk = min(TOPK_MAX, s_valid // 4)
    oaT = _dsa(qaT, qiT, wiT, k_all, vT, ki_all, s_valid=s_valid, q_pos0=q_pos0, topk=topk)
    oa = oaT[:, :, :t].transpose(0, 2, 1).reshape(b * t, mix)

    nh = mix // HEAD_DIM
    if is_prompt:
        n_tiles = BAND_CHUNKS * CHUNK // BAND_QB + 1
        bias = _band_bias(prm["rel_bias"], BAND_CHUNKS * CHUNK, BAND_QB, 0, n_tiles * BAND_QB)
        ob = _band_prompt(p, bias, cols, b, t)
    else:
        nbc = cache["b_k"].shape[2]
        past = cache["a_k"].shape[2]
        bias = _band_bias(prm["rel_bias"], past, t, past - nbc, nbc + t)
        ob = _band_sample(p, cache["b_k"], cache["b_v"], layer, bias, cols, b, t)

    g_ssm = mix // C_GROUP
    if is_prompt:
        h0_re = jnp.zeros((b, g_ssm, C_STATE), F32)
        h0_im = h0_re
    else:
        h0_re, h0_im = cache["c_re"][layer], cache["c_im"][layer]
    yc, hr, hi = _ssm_apply(p, cols["uc"], b, t, h0_re, h0_im, prm["ssm"], is_prompt)
    oc = _glu(yc, prm["w_glu"], prm["b_glu"], tm)

    merged = _merge(oa, ob, oc, p, prm["w_branch"], cols["g"], tm, 512)
    x1 = _mmres(merged, prm["w_out"], xt, tm, 1024)

    f = prm["w_conv"].shape[1]
    if is_prompt:
        state = jnp.zeros((b, CONV_W - 1, f), F32)
        ns, ts = 1, _row_tile(t)
    else:
        state = cache["ffn_conv"][layer].astype(F32)
        ns, ts = b, t
    act, tails = _ffn_in(x1, prm["norm2"], prm["w_ffn_in"], prm["w_conv"], prm["b_conv"], state, ns, ts, 512)
    x2 = _mmres(act, prm["w_down"], x1, _row_tile(b * t, 512), 512)
    if is_prompt:
        buf = tails.reshape(b, t // ts, CONV_W - 1, f)[:, -1]
    else:
        buf = tails[0]

    ka4 = ka.reshape(b, t, A_KV, HEAD_DIM)
    va4 = p[:, cols["va"]:cols["va"] + A_KV * HEAD_DIM].reshape(b, t, A_KV, HEAD_DIM)
    ki3 = kw[:, :IDX_DIM].reshape(b, t, IDX_DIM)
    nb = min(BAND_CHUNKS * CHUNK, t) if is_prompt else t
    pb = p.reshape(b, t, p.shape[1])[:, t - nb:]
    kb4 = pb[:, :, cols["kb"]:cols["kb"] + mix].reshape(b, nb, nh, HEAD_DIM)
    vb4 = pb[:, :, cols["vb"]:cols["vb"] + mix].reshape(b, nb, nh, HEAD_DIM)
    return x2.reshape(b, t, d), (ka4, va4, ki3, kb4, vb4, hr, hi, buf)


def kernel(x_prompt, x_sample, cache_a_k, cache_a_v, cache_a_kidx, cache_b_k, cache_b_v, state_c_re, state_c_im,
           state_ffn_conv, norm1_g, w_in, rel_bias, ssm_a_re, ssm_a_im, ssm_log_dt, ssm_b_re, ssm_b_im, ssm_c_re,
           ssm_c_im, ssm_d, w_glu, b_glu, w_branch, w_out, norm2_g, w_ffn_in, w_ffn_conv, b_ffn_conv, w_ffn_down,
           normf_g):
    depth = w_in.shape[0]
    d = x_prompt.shape[2]
    pos_p = jnp.arange(x_prompt.shape[1])
    pos_s = cache_a_k.shape[2] + jnp.arange(x_sample.shape[1])
    xp, xs = x_prompt, x_sample
    st_p, st_s = [], []
    caches = dict(a_k=cache_a_k, a_v=cache_a_v, a_kidx=cache_a_kidx, b_k=cache_b_k, b_v=cache_b_v,
                  c_re=state_c_re, c_im=state_c_im, ffn_conv=state_ffn_conv)
    ssm_all = jax.vmap(_ssm_weights)(ssm_a_re, ssm_a_im, ssm_log_dt, ssm_b_re, ssm_b_im, ssm_c_re, ssm_c_im, ssm_d)
    for l in range(depth):
        w_in_l, cols, tn = _pack_w_in(w_in[l], d)
        prm = dict(norm1=norm1_g[l], w_in=w_in_l, cols=cols, w_in_tn=tn, rel_bias=rel_bias[l],
                   ssm=tuple(a[l] for a in ssm_all),
                   w_glu=w_glu[l].astype(BF16), b_glu=b_glu[l], w_branch=w_branch[l].astype(BF16),
                   w_out=w_out[l].astype(BF16), norm2=norm2_g[l], w_ffn_in=w_ffn_in[l].astype(BF16),
                   w_conv=w_ffn_conv[l], b_conv=b_ffn_conv[l], w_down=w_ffn_down[l].astype(BF16))
        xp, sp = _layer(xp, pos_p, prm, None, l, True)
        xs, ss = _layer(xs, pos_s, prm, caches, l, False)
        st_p.append(sp)
        st_s.append(ss)
    bp, tp, _ = xp.shape
    bs, tsq, _ = xs.shape
    y_prompt = _norm(xp.reshape(bp * tp, d), normf_g, _row_tile(bp * tp)).reshape(bp, tp, d)
    y_sample = _norm(xs.reshape(bs * tsq, d), normf_g, _row_tile(bs * tsq)).reshape(bs, tsq, d)
    outs_p = [jnp.stack([s[i] for s in st_p]) for i in range(8)]
    outs_s = [jnp.stack([s[i] for s in st_s]) for i in range(8)]
    return (y_prompt, y_sample, *outs_p, *outs_s)
```

```python
import functools
import math

import jax
import jax.numpy as jnp
import numpy as np
from jax import lax
from jax.experimental import pallas as pl
from jax.experimental.pallas import tpu as pltpu

F32 = jnp.float32
BF16 = jnp.bfloat16
I32 = jnp.int32

CHUNK = 64
EPS = 1e-6
ROPE_THETA = 500000.0
HEAD_DIM = 128
A_KV = 2
A_GROUP = 4
A_ROT = HEAD_DIM // 4
IDX_HEADS = 8
IDX_DIM = 64
IDX_ROT = IDX_DIM // 4
TOPK_MAX = 256
BAND_CHUNKS = 8
REL_CLIP = 256
C_GROUP = 16
C_STATE = 64
CONV_W = 3
N_BRANCH = 3
ATTN_SCALE = HEAD_DIM ** -0.5
IDX_SCALE = (IDX_DIM * IDX_HEADS) ** -0.5

LANES = 128
SUBLANES = 8
VMEM_LIMIT = 56 * 1024 * 1024

NEG_BIG = -1e30
KEY_NEG_INF = -2139095041
KEY_POS_INF = 2139095040
INT32_MIN = -2147483648

SSM_L = SUBLANES
SSM_GB = LANES // C_GROUP
DSA_TK = 512
DSA_TA = 512
DSA_VROWS = HEAD_DIM + 16
DSA_QB = 128
BAND_QB = 128


def _cparams(sem):
    return pltpu.CompilerParams(dimension_semantics=sem, vmem_limit_bytes=VMEM_LIMIT)


def _gelu(x):
    return 0.5 * x * (1.0 + jnp.tanh(math.sqrt(2.0 / math.pi) * (x + 0.044715 * (x * x * x))))


def _sigmoid(x):
    return 1.0 / (1.0 + jnp.exp(-x))


def _rms(x, g):
    ms = jnp.mean(x * x, axis=-1, keepdims=True)
    return (x * lax.rsqrt(ms + EPS)) * g


def _proj_kernel(x_ref, g_ref, w_ref, o_ref, h_ref):
    @pl.when(pl.program_id(1) == 0)
    def _():
        h_ref[...] = _rms(x_ref[...], g_ref[...]).astype(BF16)

    o_ref[...] = jnp.dot(h_ref[...], w_ref[...], preferred_element_type=F32)


def _proj(x, g, w, tm, tn):
    t, d = x.shape
    n = w.shape[1]
    return pl.pallas_call(
        _proj_kernel,
        grid=(t // tm, n // tn),
        in_specs=[pl.BlockSpec((tm, d), lambda i, j: (i, 0)),
                  pl.BlockSpec((1, d), lambda i, j: (0, 0)),
                  pl.BlockSpec((d, tn), lambda i, j: (0, j))],
        out_specs=pl.BlockSpec((tm, tn), lambda i, j: (i, j)),
        out_shape=jax.ShapeDtypeStruct((t, n), F32),
        scratch_shapes=[pltpu.VMEM((tm, d), BF16)],
        compiler_params=_cparams(("arbitrary", "arbitrary")),
        name="proj",
    )(x, g.reshape(1, d), w)


def _rot(x, c, sa, sb, half):
    n = x.shape[-1]
    return x * c + pltpu.roll(x, n - half, 1) * sa + pltpu.roll(x, half, 1) * sb


def _rope_kernel(qa_ref, ka_ref, va_ref, qi_ref, kw_ref, tab_ref,
                 qa_o, ka_o, kab_o, vab_o, qi_o, kw_o, kib_o):
    ca, saa, sba = tab_ref[0], tab_ref[1], tab_ref[2]
    ci, sai, sbi = tab_ref[3], tab_ref[4], tab_ref[5]
    ck, sak, sbk = tab_ref[6], tab_ref[7], tab_ref[8]
    for h in range(qa_ref.shape[1] // LANES):
        sl = slice(h * LANES, (h + 1) * LANES)
        qa_o[:, sl] = _rot(qa_ref[:, sl], ca, saa, sba, A_ROT // 2).astype(BF16)
    for h in range(ka_ref.shape[1] // LANES):
        sl = slice(h * LANES, (h + 1) * LANES)
        k = _rot(ka_ref[:, sl], ca, saa, sba, A_ROT // 2)
        ka_o[:, sl] = k
        kab_o[:, sl] = k.astype(BF16)
    vab_o[...] = va_ref[...].astype(BF16)
    for h in range(qi_ref.shape[1] // LANES):
        sl = slice(h * LANES, (h + 1) * LANES)
        qi_o[:, sl] = _rot(qi_ref[:, sl], ci, sai, sbi, IDX_ROT // 2).astype(BF16)
    kw = _rot(kw_ref[...], ck, sak, sbk, IDX_ROT // 2)
    kw_o[...] = kw
    kib_o[...] = kw.astype(BF16)


def _rope(p, tabs, cols, tm):
    t = p.shape[0]
    n_tab_blocks = tabs.shape[1] // tm
    d_qa, d_kv, d_qi = A_KV * A_GROUP * HEAD_DIM, A_KV * HEAD_DIM, IDX_HEADS * IDX_DIM

    def col(width, off):
        assert off % width == 0
        return pl.BlockSpec((tm, width), lambda i: (i, off // width))

    def out(width):
        return pl.BlockSpec((tm, width), lambda i: (i, 0))

    return pl.pallas_call(
        _rope_kernel,
        grid=(t // tm,),
        in_specs=[col(d_qa, cols["qa"]), col(d_kv, cols["ka"]), col(d_kv, cols["va"]),
                  col(d_qi, cols["qi"]), col(LANES, cols["kw"]),
                  pl.BlockSpec((9, tm, LANES), lambda i: (0, i % n_tab_blocks, 0))],
        out_specs=[out(d_qa), out(d_kv), out(d_kv), out(d_kv), out(d_qi), out(LANES), out(LANES)],
        out_shape=[jax.ShapeDtypeStruct((t, d_qa), BF16),
                   jax.ShapeDtypeStruct((t, d_kv), F32),
                   jax.ShapeDtypeStruct((t, d_kv), BF16),
                   jax.ShapeDtypeStruct((t, d_kv), BF16),
                   jax.ShapeDtypeStruct((t, d_qi), BF16),
                   jax.ShapeDtypeStruct((t, LANES), F32),
                   jax.ShapeDtypeStruct((t, LANES), BF16)],
        compiler_params=_cparams(("arbitrary",)),
        name="rope",
    )(p, p, p, p, p, tabs)


def _rope_tables(pos):
    pos = pos.astype(F32)[:, None]

    def cs(rot):
        half = rot // 2
        inv = jnp.float32(ROPE_THETA) ** (-jnp.arange(half, dtype=F32) / half)
        ang = pos * inv[None, :]
        return jnp.cos(ang), jnp.sin(ang)

    def tabs(rot, width):
        half = rot // 2
        c, s = cs(rot)
        n = pos.shape[0]
        one = jnp.ones((n, width - rot), F32)
        zero = jnp.zeros((n, width - rot), F32)
        zh = jnp.zeros((n, half), F32)
        return (jnp.concatenate([c, c, one], 1), jnp.concatenate([-s, zh, zero], 1),
                jnp.concatenate([zh, s, zero], 1))

    ca, saa, sba = tabs(A_ROT, HEAD_DIM)
    c64, sa64, sb64 = tabs(IDX_ROT, IDX_DIM)
    n = pos.shape[0]
    one64, zero64 = jnp.ones((n, IDX_DIM), F32), jnp.zeros((n, IDX_DIM), F32)
    return jnp.stack([ca, saa, sba,
                      jnp.concatenate([c64, c64], 1), jnp.concatenate([sa64, sa64], 1),
                      jnp.concatenate([sb64, sb64], 1),
                      jnp.concatenate([c64, one64], 1), jnp.concatenate([sa64, zero64], 1),
                      jnp.concatenate([sb64, zero64], 1)])


def _dsa_kernel(qaT_ref, qiT_ref, wiT_ref, k_ref, vT_ref, ki_ref, o_ref, key_ref, acc_ref,
                s0_ref, s1_ref, p0_ref, p1_ref, a0_ref, a1_ref,
                *, s_valid, q_pos0, topk):
    qb, tk = DSA_QB, DSA_TK
    j = pl.program_id(1)
    q_first = q_pos0 + j * qb
    qpos = q_first + lax.broadcasted_iota(I32, (1, qb), 1)
    assert CHUNK == 64 and tk == 512
    n_adm = jnp.minimum(((qpos >> 6) + 1) * CHUNK, s_valid)
    n_max = jnp.minimum((((q_first + qb - 1) >> 6) + 1) * CHUNK, s_valid)
    nkt = (n_max + tk - 1) >> 9

    qi = qiT_ref[0]
    rhs = jnp.concatenate([qi[h * IDX_DIM:(h + 1) * IDX_DIM, :] for h in range(IDX_HEADS)], axis=1)
    rhs = jnp.concatenate([rhs, jnp.zeros((LANES - IDX_DIM, IDX_HEADS * qb), BF16)], axis=0)
    wi = wiT_ref[0]

    def p1(kt, carry):
        off = pl.multiple_of(kt * tk, tk)
        kid = ki_ref[0, pl.ds(off, tk), :]
        s_all = jnp.dot(kid, rhs, preferred_element_type=F32)
        acc = wi[0:1, :] * jnp.maximum(s_all[:, 0:qb], 0.0)
        for h in range(1, IDX_HEADS):
            acc = acc + wi[h:h + 1, :] * jnp.maximum(s_all[:, h * qb:(h + 1) * qb], 0.0)
        score = acc * IDX_SCALE
        kpos = off + lax.broadcasted_iota(I32, (tk, 1), 0)
        score = jnp.where(kpos < n_adm, score, -jnp.inf)
        u = pltpu.bitcast(score, I32)
        key_ref[pl.ds(off, tk), :] = u ^ (lax.shift_right_arithmetic(u, 31) & 0x7FFFFFFF)
        return carry

    lax.fori_loop(0, nkt, p1, 0)

    pairs = key_ref.shape[0] % (2 * tk) == 0
    if pairs:
        @pl.when(nkt % 2 == 1)
        def _():
            key_ref[pl.ds(pl.multiple_of(nkt * tk, tk), tk), :] = jnp.full((tk, qb), INT32_MIN, I32)

    n_trips, per_trip = ((nkt + 1) // 2, 2) if pairs else (nkt, 1)

    def count_ge(cand):
        def body(kt, cnt):
            for half in range(per_trip):
                blk = key_ref[pl.ds(pl.multiple_of((kt * per_trip + half) * tk, tk), tk), :]
                m = jnp.where(blk >= cand, 1, 0).astype(I32)
                cnt = cnt + jnp.sum(m.reshape(tk // SUBLANES, SUBLANES, qb), axis=0)
            return cnt

        cnt8 = lax.fori_loop(0, n_trips, body, jnp.zeros((SUBLANES, qb), I32))
        return jnp.sum(cnt8, axis=0, keepdims=True)

    def bit_body(i, carry):
        prefix, n_ge = carry
        cand_u = prefix | lax.shift_left(jnp.int32(1), 31 - i)
        cnt = count_ge(cand_u ^ INT32_MIN)
        ok = cnt >= topk
        return jnp.where(ok, cand_u, prefix), jnp.where(ok, cnt, n_ge)

    prefix, n_ge = lax.fori_loop(0, 32, bit_body, (jnp.zeros((1, qb), I32), jnp.broadcast_to(nkt * tk, (1, qb))))
    thr = prefix ^ INT32_MIN
    cut_ties = jnp.max(jnp.where((n_ge > topk) & (thr > KEY_NEG_INF), 1, 0)) > 0

    @pl.when(cut_ties)
    def _():
        r = lax.broadcasted_iota(I32, (tk, tk), 0)
        c = lax.broadcasted_iota(I32, (tk, tk), 1)
        tri = jnp.where(c <= r, 1.0, 0.0).astype(BF16)
        take = (topk - count_ge(thr + 1)).astype(F32)

        def body(kt, seen):
            off = pl.multiple_of(kt * tk, tk)
            blk = key_ref[pl.ds(off, tk), :]
            eq = blk == thr
            incl = jnp.dot(tri, jnp.where(eq, 1.0, 0.0).astype(BF16), preferred_element_type=F32)
            drop = eq & (seen + incl > take)
            key_ref[pl.ds(off, tk), :] = jnp.where(drop, thr - 1, blk)
            return seen + incl[tk - 1:tk, :]

        lax.fori_loop(0, nkt, body, jnp.zeros((1, qb), F32))

    thr_lo = jnp.maximum(thr, KEY_NEG_INF + 1)
    qa = qaT_ref[0]
    ng = A_GROUP * qb
    qg = [jnp.concatenate([qa[(g * A_GROUP + hh) * HEAD_DIM:(g * A_GROUP + hh + 1) * HEAD_DIM, :]
                           for hh in range(A_GROUP)], axis=1) for g in range(A_KV)]
    acc_ref[...] = jnp.zeros(acc_ref.shape, F32)

    c_exp = ATTN_SCALE * math.log2(math.e)

    ta = DSA_TA
    n_steps = nkt * (tk // ta)
    last = n_steps - 1
    bufs = ((s0_ref, p0_ref, a0_ref), (s1_ref, p1_ref, a1_ref))

    def logits(step, s_ref):
        off = pl.multiple_of(jnp.minimum(step, last) * ta, ta)
        for g in range(A_KV):
            kg = k_ref[0, pl.ds(off, ta), g * HEAD_DIM:(g + 1) * HEAD_DIM]
            s_ref[g] = jnp.dot(kg, qg[g], preferred_element_type=F32)

    def numerators(step, s_ref, p_ref, a_ref, ms):
        off = pl.multiple_of(jnp.minimum(step, last) * ta, ta)
        blk = key_ref[pl.ds(off, ta), :]
        sel = (blk >= thr_lo) & (blk < KEY_POS_INF) & (step <= last)
        mask = jnp.concatenate([jnp.where(sel, 0.0, NEG_BIG)] * A_GROUP, axis=1)
        new = []
        for g in range(A_KV):
            s = s_ref[g] + mask
            m_new = jnp.maximum(ms[g], jnp.max(s, axis=0, keepdims=True))
            a_ref[g] = jnp.broadcast_to(jnp.exp2((ms[g] - m_new) * c_exp), (SUBLANES, ng))
            p_ref[g] = jnp.exp2((s - m_new).astype(BF16) * c_exp)
            new.append(m_new)
        return tuple(new)

    def weighted_values(step, p_ref, a_ref):
        off = pl.multiple_of(jnp.clip(step, 0, last) * ta, ta)
        for g in range(A_KV):
            vg = vT_ref[0, g * DSA_VROWS:(g + 1) * DSA_VROWS, pl.ds(off, ta)]
            acc_ref[g] = a_ref[g, 0:1, :] * acc_ref[g] + jnp.dot(vg, p_ref[g], preferred_element_type=F32)

    p1_ref[...] = jnp.zeros(p1_ref.shape, BF16)
    a1_ref[...] = jnp.ones(a1_ref.shape, F32)
    logits(0, s0_ref)

    def two_steps(i2, ms):
        i = 2 * i2
        for par in range(2):
            cur, nxt = bufs[par], bufs[1 - par]
            logits(i + par + 1, nxt[0])
            ms = numerators(i + par, cur[0], cur[1], cur[2], ms)
            weighted_values(i + par - 1, nxt[1], nxt[2])
        return ms

    init = (jnp.full((1, ng), NEG_BIG, F32),) * A_KV
    n_pairs = (n_steps + 1) // 2
    lax.fori_loop(0, n_pairs, two_steps, init)
    weighted_values(2 * n_pairs - 1, p1_ref, a1_ref)
    for g in range(A_KV):
        o = acc_ref[g, 0:HEAD_DIM, :] / acc_ref[g, HEAD_DIM:HEAD_DIM + 1, :]
        for hh in range(A_GROUP):
            h = g * A_GROUP + hh
            o_ref[0, h * HEAD_DIM:(h + 1) * HEAD_DIM, :] = o[:, hh * qb:(hh + 1) * qb].astype(o_ref.dtype)


def _dsa(qaT, qiT, wiT, k, vT, ki, *, s_valid, q_pos0, topk):
    b, dq, sq = qaT.shape
    s_pad = k.shape[1]
    assert s_pad % DSA_TK == 0 and s_pad % DSA_TA == 0 and sq % DSA_QB == 0 and topk <= DSA_TK
    kern = functools.partial(_dsa_kernel, s_valid=s_valid, q_pos0=q_pos0, topk=topk)
    return pl.pallas_call(
        kern,
        grid=(b, sq // DSA_QB),
        in_specs=[pl.BlockSpec((1, dq, DSA_QB), lambda i, j: (i, 0, j)),
                  pl.BlockSpec((1, qiT.shape[1], DSA_QB), lambda i, j: (i, 0, j)),
                  pl.BlockSpec((1, IDX_HEADS, DSA_QB), lambda i, j: (i, 0, j)),
                  pl.BlockSpec((1, s_pad, k.shape[2]), lambda i, j: (i, 0, 0)),
                  pl.BlockSpec((1, vT.shape[1], s_pad), lambda i, j: (i, 0, 0)),
                  pl.BlockSpec((1, s_pad, LANES), lambda i, j: (i, 0, 0))],
        out_specs=pl.BlockSpec((1, dq, DSA_QB), lambda i, j: (i, 0, j)),
        out_shape=jax.ShapeDtypeStruct((b, dq, sq), BF16),
        scratch_shapes=[pltpu.VMEM((s_pad, DSA_QB), I32),
                        pltpu.VMEM((A_KV, DSA_VROWS, A_GROUP * DSA_QB), F32)]
                       + [pltpu.VMEM((A_KV, DSA_TA, A_GROUP * DSA_QB), F32)] * 2
                       + [pltpu.VMEM((A_KV, DSA_TA, A_GROUP * DSA_QB), BF16)] * 2
                       + [pltpu.VMEM((A_KV, SUBLANES, A_GROUP * DSA_QB), F32)] * 2,
        compiler_params=_cparams(("arbitrary", "arbitrary")),
        name="dsa",
    )(qaT, qiT, wiT, k, vT, ki)


def _cache_prep_kernel(ck_ref, cv_ref, cki_ref, kn_ref, vn_ref, kin_ref, k_o, vT_o, ki_o, *, n_cache_tiles):
    kt = pl.program_id(1)
    tk = k_o.shape[1]
    extra = DSA_VROWS - HEAD_DIM
    ones_rows = jnp.where(lax.broadcasted_iota(I32, (extra, tk), 0) == 0, 1.0, 0.0).astype(BF16)

    def emit(k_head, v_head, ki):
        for g in range(A_KV):
            k_o[0, :, g * HEAD_DIM:(g + 1) * HEAD_DIM] = k_head(g).astype(BF16)
            vT_o[0, g * DSA_VROWS:g * DSA_VROWS + HEAD_DIM, :] = v_head(g).astype(F32).T.astype(BF16)
            vT_o[0, g * DSA_VROWS + HEAD_DIM:(g + 1) * DSA_VROWS, :] = ones_rows
        ki_o[0] = ki

    @pl.when(kt < n_cache_tiles)
    def _():
        ki = jnp.concatenate([cki_ref[0, 0], jnp.zeros((tk, LANES - IDX_DIM), F32)], axis=1)
        emit(lambda g: ck_ref[0, 0, :, g, :], lambda g: cv_ref[0, 0, :, g, :], ki.astype(BF16))

    @pl.when(kt >= n_cache_tiles)
    def _():
        t = kn_ref.shape[0]
        rows = lambda a: jnp.concatenate([a, jnp.zeros((tk - t, a.shape[1]), a.dtype)], axis=0)
        kn, vn = rows(kn_ref[...]), rows(vn_ref[...])
        head = lambda a: lambda g: a[:, g * HEAD_DIM:(g + 1) * HEAD_DIM]
        emit(head(kn), head(vn), rows(kin_ref[...]))


def _cache_prep(cache_k, cache_v, cache_ki, layer, kn, vn, kin, b, t):
    tk = DSA_TK
    past = cache_k.shape[2]
    assert past % tk == 0 and t <= tk
    nct = past // tk
    s_pad = past + tk
    cidx = lambda i, j: jnp.minimum(j, nct - 1)
    return pl.pallas_call(
        functools.partial(_cache_prep_kernel, n_cache_tiles=nct),
        grid=(b, nct + 1),
        in_specs=[pl.BlockSpec((1, 1, tk, A_KV, HEAD_DIM), lambda i, j: (layer, i, cidx(i, j), 0, 0)),
                  pl.BlockSpec((1, 1, tk, A_KV, HEAD_DIM), lambda i, j: (layer, i, cidx(i, j), 0, 0)),
                  pl.BlockSpec((1, 1, tk, IDX_DIM), lambda i, j: (layer, i, cidx(i, j), 0)),
                  pl.BlockSpec((t, A_KV * HEAD_DIM), lambda i, j: (i, 0)),
                  pl.BlockSpec((t, A_KV * HEAD_DIM), lambda i, j: (i, 0)),
                  pl.BlockSpec((t, LANES), lambda i, j: (i, 0))],
        out_specs=[pl.BlockSpec((1, tk, A_KV * HEAD_DIM), lambda i, j: (i, j, 0)),
                   pl.BlockSpec((1, A_KV * DSA_VROWS, tk), lambda i, j: (i, 0, j)),
                   pl.BlockSpec((1, tk, LANES), lambda i, j: (i, j, 0))],
        out_shape=[jax.ShapeDtypeStruct((b, s_pad, A_KV * HEAD_DIM), BF16),
                   jax.ShapeDtypeStruct((b, A_KV * DSA_VROWS, s_pad), BF16),
                   jax.ShapeDtypeStruct((b, s_pad, LANES), BF16)],
        compiler_params=_cparams(("arbitrary", "arbitrary")),
        name="cache_prep",
    )(cache_k, cache_v, cache_ki, kn, vn, kin)


def _band_heads(q, k_head, v_head, bias_ref, n_hidden, o_ref, s_ref):
    nh = q.shape[1] // HEAD_DIM
    tk = bias_ref.shape[2]
    hidden = None if n_hidden is None else lax.broadcasted_iota(I32, (1, tk), 1) < n_hidden
    for h in range(nh):
        sl = slice(h * HEAD_DIM, (h + 1) * HEAD_DIM)
        s = lax.dot_general(q[:, sl].astype(BF16), k_head(h), (((1,), (1,)), ((), ())),
                            preferred_element_type=F32) * ATTN_SCALE + bias_ref[h]
        if hidden is not None:
            s = jnp.where(hidden, NEG_BIG, s)
        s_ref[h] = s
    ms = [jnp.max(s_ref[h], axis=1, keepdims=True) for h in range(nh)]
    ls = []
    for h in range(nh):
        p = jnp.exp(s_ref[h] - ms[h])
        s_ref[h] = p
        ls.append(jnp.sum(p, axis=1, keepdims=True))
    for h in range(nh):
        sl = slice(h * HEAD_DIM, (h + 1) * HEAD_DIM)
        w = (s_ref[h] / ls[h]).astype(BF16)
        o_ref[:, sl] = jnp.dot(w, v_head(h), preferred_element_type=F32).astype(o_ref.dtype)


def _band_prompt_kernel(*refs, n_tiles):
    q_ref = refs[0]
    k_refs = refs[1:1 + n_tiles]
    v_refs = refs[1 + n_tiles:1 + 2 * n_tiles]
    bias_ref = refs[1 + 2 * n_tiles]
    o_ref = refs[2 + 2 * n_tiles]
    s_ref = refs[3 + 2 * n_tiles]
    j = pl.program_id(1)
    n_hidden = jnp.maximum(n_tiles - 1 - j, 0) * q_ref.shape[0]
    k_all = jnp.concatenate([r[...] for r in k_refs], axis=0).astype(BF16)
    v_all = jnp.concatenate([r[...] for r in v_refs], axis=0).astype(BF16)
    head = lambda a: lambda h: a[:, h * HEAD_DIM:(h + 1) * HEAD_DIM]
    _band_heads(q_ref[...], head(k_all), head(v_all), bias_ref, n_hidden, o_ref, s_ref)


def _band_prompt(p, bias, cols, batch, seq):
    qb = BAND_QB
    n_tiles = BAND_CHUNKS * CHUNK // qb + 1
    width = bias.shape[0] * HEAD_DIM
    nq = seq // qb
    cq, ck, cv = cols["qb"] // width, cols["kb"] // width, cols["vb"] // width
    assert cols["qb"] % width == 0 and cols["kb"] % width == 0 and cols["vb"] % width == 0

    def kv_spec(c, i):
        return pl.BlockSpec((qb, width), lambda b, j: (b * nq + jnp.maximum(j - (n_tiles - 1) + i, 0), c))

    return pl.pallas_call(
        functools.partial(_band_prompt_kernel, n_tiles=n_tiles),
        grid=(batch, nq),
        in_specs=([pl.BlockSpec((qb, width), lambda b, j: (b * nq + j, cq))]
                  + [kv_spec(ck, i) for i in range(n_tiles)]
                  + [kv_spec(cv, i) for i in range(n_tiles)]
                  + [pl.BlockSpec(bias.shape, lambda b, j: (0, 0, 0))]),
        out_specs=pl.BlockSpec((qb, width), lambda b, j: (b * nq + j, 0)),
        out_shape=jax.ShapeDtypeStruct((batch * seq, width), BF16),
        compiler_params=_cparams(("arbitrary", "arbitrary")),
        scratch_shapes=[pltpu.VMEM(bias.shape, F32)],
        name="band_prompt",
    )(*([p] * (1 + 2 * n_tiles)), bias)


def _band_sample_kernel(q_ref, k_ref, v_ref, ck_ref, cv_ref, bias_ref, o_ref, s_ref):
    def head(cache_ref, new_ref):
        return lambda h: jnp.concatenate(
            [cache_ref[0, 0, :, h, :], new_ref[:, h * HEAD_DIM:(h + 1) * HEAD_DIM]], axis=0).astype(BF16)

    _band_heads(q_ref[...], head(ck_ref, k_ref), head(cv_ref, v_ref), bias_ref, None, o_ref, s_ref)


def _band_sample(p, cache_k, cache_v, layer, bias, cols, batch, t):
    nh = bias.shape[0]
    width = nh * HEAD_DIM
    nbc = cache_k.shape[2]
    cq, ck, cv = cols["qb"] // width, cols["kb"] // width, cols["vb"] // width
    return pl.pallas_call(
        _band_sample_kernel,
        grid=(batch,),
        in_specs=[pl.BlockSpec((t, width), lambda b: (b, cq)),
                  pl.BlockSpec((t, width), lambda b: (b, ck)),
                  pl.BlockSpec((t, width), lambda b: (b, cv)),
                  pl.BlockSpec((1, 1, nbc, nh, HEAD_DIM), lambda b: (layer, b, 0, 0, 0)),
                  pl.BlockSpec((1, 1, nbc, nh, HEAD_DIM), lambda b: (layer, b, 0, 0, 0)),
                  pl.BlockSpec(bias.shape, lambda b: (0, 0, 0))],
        out_specs=pl.BlockSpec((t, width), lambda b: (b, 0)),
        out_shape=jax.ShapeDtypeStruct((batch * t, width), BF16),
        compiler_params=_cparams(("arbitrary",)),
        scratch_shapes=[pltpu.VMEM(bias.shape, F32)],
        name="band_sample",
    )(p, p, p, cache_k, cache_v, bias)


def _band_bias(table, q0, nq, k0, nk):
    n = nq + nk
    j = jnp.arange(n)
    g = table[:, jnp.clip(j + (q0 - k0 - nk + 1), -REL_CLIP, REL_CLIP) + REL_CLIP].astype(F32)
    hank = jnp.tile(g, (1, nq + 1))[:, :nq * (n + 1)].reshape(-1, nq, n + 1)[:, :, :nk]
    bias = hank[:, :, ::-1]
    qc = (q0 + jnp.arange(nq))[:, None] // CHUNK
    kc = (k0 + jnp.arange(nk))[None, :] // CHUNK
    vis = (kc <= qc) & (kc >= qc - BAND_CHUNKS)
    return jnp.where(vis[None], bias, NEG_BIG)


def _ssm_kernel(u_ref, t_ref, p_ref, q_ref, a_ref, h0_ref, y_ref, hre_o, him_o, sre_ref, sim_ref,
                *, seg, tiles_per_seq):
    L = SSM_L
    x = jnp.concatenate([u_ref[:, s, :] for s in range(L)], axis=1).astype(BF16)
    sre_ref[...] = jnp.dot(x, p_ref[0, 0], preferred_element_type=F32)
    sim_ref[...] = jnp.dot(x, p_ref[0, 1], preferred_element_type=F32)
    rows, w = sre_ref.shape
    kseg = lax.broadcasted_iota(I32, (SUBLANES, 1), 0) % seg
    apk_re, apk_im = a_ref[0, 0, 0:SUBLANES, :], a_ref[0, 1, 0:SUBLANES, :]

    def cmad(x_re, x_im, a_re, a_im, y_re, y_im):
        return x_re + a_re * y_re - a_im * y_im, x_im + a_re * y_im + a_im * y_re

    def tile(i, carry):
        r = pl.ds(pl.multiple_of(i * SUBLANES, SUBLANES), SUBLANES)
        x_re, x_im = sre_ref[r, :], sim_ref[r, :]
        for n, d in enumerate((1, 2, 4)):
            if d < seg:
                ad_re = a_ref[0, 0, SUBLANES + n:SUBLANES + n + 1, :]
                ad_im = a_ref[0, 1, SUBLANES + n:SUBLANES + n + 1, :]
                sh_re = jnp.where(kseg >= d, pltpu.roll(x_re, d, 0), 0.0)
                sh_im = jnp.where(kseg >= d, pltpu.roll(x_im, d, 0), 0.0)
                x_re, x_im = cmad(x_re, x_im, ad_re, ad_im, sh_re, sh_im)
        if tiles_per_seq > 1:
            first = (i % tiles_per_seq) == 0
            hin_re = jnp.where(first, h0_ref[0, 0], carry[0])
            hin_im = jnp.where(first, h0_ref[0, 1], carry[1])
        else:
            hin_re, hin_im = h0_ref[0, 0, r, :], h0_ref[0, 1, r, :]
        inc_re, inc_im = cmad(x_re, x_im, apk_re, apk_im, hin_re, hin_im)
        sre_ref[r, :] = jnp.where(kseg == 0, hin_re, pltpu.roll(inc_re, 1, 0))
        sim_ref[r, :] = jnp.where(kseg == 0, hin_im, pltpu.roll(inc_im, 1, 0))
        hre_o[0, r, :] = inc_re
        him_o[0, r, :] = inc_im
        return inc_re[SUBLANES - 1:SUBLANES, :], inc_im[SUBLANES - 1:SUBLANES, :]

    zero = jnp.zeros((1, w), F32)
    lax.fori_loop(0, rows // SUBLANES, tile, (zero, zero))
    y = (jnp.dot(x, t_ref[0], preferred_element_type=F32)
         + jnp.dot(sre_ref[...].astype(BF16), q_ref[0, 0], preferred_element_type=F32)
         + jnp.dot(sim_ref[...].astype(BF16), q_ref[0, 1], preferred_element_type=F32))
    for t in range(L):
        y_ref[:, t, :] = y[:, t * LANES:(t + 1) * LANES]


def _ssm(p3, uc_col, tmat, pmat, qmat, apow, h0, rows_blk, seg, tiles_per_seq):
    rows_total = p3.shape[0]
    nb, _, _, w = pmat.shape
    kl = SSM_L * LANES
    h0_rows = h0.shape[2] // (rows_total // rows_blk)
    hspec = pl.BlockSpec((1, rows_blk, w), lambda i, r: (i, r, 0))
    return pl.pallas_call(
        functools.partial(_ssm_kernel, seg=seg, tiles_per_seq=tiles_per_seq),
        grid=(nb, rows_total // rows_blk),
        in_specs=[pl.BlockSpec((rows_blk, SSM_L, LANES), lambda i, r: (r, 0, uc_col // LANES + i)),
                  pl.BlockSpec((1, kl, kl), lambda i, r: (i, 0, 0)),
                  pl.BlockSpec((1, 2, kl, w), lambda i, r: (i, 0, 0, 0)),
                  pl.BlockSpec((1, 2, w, kl), lambda i, r: (i, 0, 0, 0)),
                  pl.BlockSpec((1, 2, 2 * SUBLANES, w), lambda i, r: (i, 0, 0, 0)),
                  pl.BlockSpec((1, 2, h0_rows, w), lambda i, r: (i, 0, r, 0))],
        out_specs=[pl.BlockSpec((rows_blk, SSM_L, LANES), lambda i, r: (r, 0, i)), hspec, hspec],
        out_shape=[jax.ShapeDtypeStruct((rows_total, SSM_L, nb * LANES), F32),
                   jax.ShapeDtypeStruct((nb, rows_total, w), F32),
                   jax.ShapeDtypeStruct((nb, rows_total, w), F32)],
        scratch_shapes=[pltpu.VMEM((rows_blk, w), F32), pltpu.VMEM((rows_blk, w), F32)],
        compiler_params=_cparams(("arbitrary", "arbitrary")),
        name="ssm",
    )(p3, tmat, pmat, qmat, apow, h0)


def _ssm_weights(a_re, a_im, log_dt, b_re, b_im, c_re, c_im, d):
    hp = lax.Precision.HIGHEST
    g, p = a_re.shape
    gc = b_re.shape[2]
    L = SSM_L
    a_re, a_im = a_re.astype(F32), a_im.astype(F32)
    dt = jnp.exp(log_dt.astype(F32))[:, None]
    mag = jnp.exp(dt * a_re)
    ab_re, ab_im = mag * jnp.cos(dt * a_im), mag * jnp.sin(dt * a_im)
    n_re = ab_re - 1.0
    n_im = ab_im
    den = a_re * a_re + a_im * a_im
    cc_re = (n_re * a_re + n_im * a_im) / den
    cc_im = (n_im * a_re - n_re * a_im) / den
    b_re, b_im = b_re.astype(F32), b_im.astype(F32)
    bb_re = cc_re[..., None] * b_re - cc_im[..., None] * b_im
    bb_im = cc_re[..., None] * b_im + cc_im[..., None] * b_re
    c_re, c_im = c_re.astype(F32), c_im.astype(F32)

    def cmul(x, y):
        return x[0] * y[0] - x[1] * y[1], x[0] * y[1] + x[1] * y[0]

    rep = lambda a: jnp.broadcast_to(a[None], (L,) + a.shape)
    pw_re, pw_im = lax.associative_scan(cmul, (rep(ab_re), rep(ab_im)), axis=0)
    pw_re = jnp.concatenate([jnp.ones((1, g, p), F32), pw_re], axis=0)
    pw_im = jnp.concatenate([jnp.zeros((1, g, p), F32), pw_im], axis=0)
    ca_re = c_re[None] * pw_re[:, :, None, :] - c_im[None] * pw_im[:, :, None, :]
    ca_im = c_re[None] * pw_im[:, :, None, :] + c_im[None] * pw_re[:, :, None, :]
    taps = (jnp.einsum("tgop,gpi->tgoi", ca_re[:L], bb_re, precision=hp)
            - jnp.einsum("tgop,gpi->tgoi", ca_im[:L], bb_im, precision=hp))
    taps = taps.at[0].add(jax.vmap(jnp.diag)(d.astype(F32)))
    s_idx = jnp.arange(L)[:, None]
    t_idx = jnp.arange(L)[None, :]
    lag = t_idx - s_idx
    tt = jnp.where((lag >= 0)[:, :, None, None, None], taps[jnp.clip(lag, 0, L - 1)], 0.0)
    gb = SSM_GB
    nb = g // gb
    n_x = L * gb * gc
    x = jnp.arange(n_x)
    x_grp, x_loc = (x // gc) % gb, (x // (gb * gc)) * gc + x % gc
    place_x = ((x_grp[None, :, None] == jnp.arange(gb)[:, None, None])
               & (x_loc[None, :, None] == jnp.arange(L * gc)[None, None, :])).astype(F32)
    q = jnp.arange(gb * p)
    place_q = ((q[None, :, None] // p == jnp.arange(gb)[:, None, None])
               & (q[None, :, None] % p == jnp.arange(p)[None, None, :])).astype(F32)
    blk = lambda a: a.reshape((nb, gb) + a.shape[1:])
    t_grp = blk(tt.transpose(2, 0, 4, 1, 3).reshape(g, L * gc, L * gc))
    tmat = jnp.einsum("gxr,bgrc,gyc->bxy", place_x, t_grp, place_x)
    rev = L - 1 - jnp.arange(L)
    pin_re = pw_re[rev][:, :, :, None] * bb_re[None] - pw_im[rev][:, :, :, None] * bb_im[None]
    pin_im = pw_re[rev][:, :, :, None] * bb_im[None] + pw_im[rev][:, :, :, None] * bb_re[None]
    to_p = lambda a: jnp.einsum("gxr,bgrp,gqp->bxq", place_x,
                                blk(a.transpose(1, 0, 3, 2).reshape(g, L * gc, p)), place_q)
    to_q = lambda a: jnp.einsum("gqp,bgpc,gyc->bqy", place_q,
                                blk(a.transpose(1, 3, 0, 2).reshape(g, p, L * gc)), place_x)
    pmat = jnp.stack([to_p(pin_re), to_p(pin_im)], axis=1)
    qmat = jnp.stack([to_q(ca_re[1:]), -to_q(ca_im[1:])], axis=1)
    rep8 = lambda a: jnp.broadcast_to(a[None], (SUBLANES,) + a.shape)
    apl_re, apl_im = lax.associative_scan(cmul, (rep8(pw_re[L]), rep8(pw_im[L])), axis=0)
    apl = jnp.stack([apl_re, apl_im]).reshape(2, SUBLANES, nb, gb * p)
    return tmat.astype(BF16), pmat.astype(BF16), qmat.astype(BF16), apl


def _ssm_apply(p, uc_col, b, t, h0_re, h0_im, weights, one_seq_per_step):
    tmat, pmat, qmat, apl = weights
    nb, w = pmat.shape[0], pmat.shape[3]
    L = SSM_L
    nc = t // L
    assert t % L == 0 and (nc % SUBLANES == 0 or SUBLANES % nc == 0)
    seg = min(nc, SUBLANES)
    if one_seq_per_step:
        assert nc % SUBLANES == 0
        rows_blk, tiles_per_seq, reps = nc, nc // SUBLANES, SUBLANES
    else:
        rows_blk, tiles_per_seq, reps = b * nc, 1, nc
        assert nc <= SUBLANES and rows_blk % SUBLANES == 0
    pw_rows = [k % seg for k in range(SUBLANES)] + [0, 1, 3] + [0] * (SUBLANES - 3)
    apow = jnp.stack([apl[:, k] for k in pw_rows], axis=1).transpose(2, 0, 1, 3)

    def pack(h):
        h = h.astype(F32).reshape(b, nb, w).transpose(1, 0, 2)
        return jnp.repeat(h, reps, axis=1)

    h0 = jnp.stack([pack(h0_re), pack(h0_im)], axis=1)
    p3 = p.reshape(b * nc, L, p.shape[1])
    y, hre, him = _ssm(p3, uc_col, tmat, pmat, qmat, apow, h0, rows_blk, seg, tiles_per_seq)

    def last(h):
        return h.reshape(nb, b, nc, w)[:, :, nc - 1].transpose(1, 0, 2).reshape(b, nb * SSM_GB, C_STATE)

    return y.reshape(b * t, nb * LANES), last(hre), last(him)


def _glu_kernel(y_ref, w_ref, b_ref, o_ref):
    z = _gelu(y_ref[...])
    a = jnp.dot(z.astype(BF16), w_ref[...], preferred_element_type=F32) + b_ref[...]
    o_ref[...] = (z * _sigmoid(a)).astype(o_ref.dtype)


def _glu(y, w, b, tm):
    t, n = y.shape
    return pl.pallas_call(
        _glu_kernel,
        grid=(t // tm,),
        in_specs=[pl.BlockSpec((tm, n), lambda i: (i, 0)),
                  pl.BlockSpec((n, n), lambda i: (0, 0)),
                  pl.BlockSpec((1, n), lambda i: (0, 0))],
        out_specs=pl.BlockSpec((tm, n), lambda i: (i, 0)),
        out_shape=jax.ShapeDtypeStruct((t, n), BF16),
        compiler_params=_cparams(("arbitrary",)),
        name="glu",
    )(y, w, b.reshape(1, n))


def _merge_kernel(oa_ref, ob_ref, oc_ref, ga_ref, gb_ref, gc_ref, w_ref, o_ref):
    acc = None
    for n, (o, g) in enumerate(((oa_ref, ga_ref), (ob_ref, gb_ref), (oc_ref, gc_ref))):
        br = jnp.dot(o[...], w_ref[n], preferred_element_type=F32)
        t = _sigmoid(g[...]) * br
        acc = t if acc is None else acc + t
    o_ref[...] = acc.astype(o_ref.dtype)


def _merge(oa, ob, oc, p, w, g_col, tm, tn):
    t, m = oa.shape
    d = w.shape[2]
    assert g_col % tn == 0 and d % tn == 0

    def gspec(n):
        return pl.BlockSpec((tm, tn), lambda i, j: (i, (g_col + n * d) // tn + j))

    ospec = pl.BlockSpec((tm, m), lambda i, j: (i, 0))
    return pl.pallas_call(
        _merge_kernel,
        grid=(t // tm, d // tn),
        in_specs=[ospec, ospec, ospec, gspec(0), gspec(1), gspec(2),
                  pl.BlockSpec((N_BRANCH, m, tn), lambda i, j: (0, 0, j))],
        out_specs=pl.BlockSpec((tm, tn), lambda i, j: (i, j)),
        out_shape=jax.ShapeDtypeStruct((t, d), BF16),
        compiler_params=_cparams(("arbitrary", "arbitrary")),
        name="merge",
    )(oa, ob, oc, p, p, p, w)


def _mmres_kernel(a_ref, w_ref, r_ref, o_ref):
    o_ref[...] = r_ref[...] + jnp.dot(a_ref[...], w_ref[...], preferred_element_type=F32)


def _mmres(a, w, res, tm, tn):
    t, k = a.shape
    n = w.shape[1]
    return pl.pallas_call(
        _mmres_kernel,
        grid=(t // tm, n // tn),
        in_specs=[pl.BlockSpec((tm, k), lambda i, j: (i, 0)),
                  pl.BlockSpec((k, tn), lambda i, j: (0, j)),
                  pl.BlockSpec((tm, tn), lambda i, j: (i, j))],
        out_specs=pl.BlockSpec((tm, tn), lambda i, j: (i, j)),
        out_shape=jax.ShapeDtypeStruct((t, n), F32),
        compiler_params=_cparams(("arbitrary", "arbitrary")),
        name="mmres",
    )(a, w, res)


def _ffn_in_kernel(x_ref, g_ref, wu_ref, wv_ref, wc_ref, bc_ref, st_ref, o_ref, tail_ref,
                   h_ref, up_ref, carry_ref, *, ns, ts, tiles_per_seq):
    i, j = pl.program_id(0), pl.program_id(1)
    pad = SUBLANES

    @pl.when(j == 0)
    def _():
        h_ref[...] = _rms(x_ref[...], g_ref[...]).astype(BF16)

    @pl.when((i == 0) & (j == 0))
    def _():
        carry_ref[...] = jnp.zeros(carry_ref.shape, F32)

    h = h_ref[...]
    u = jnp.dot(h, wu_ref[...], preferred_element_type=F32)
    tn = u.shape[1]
    up_ref[:, pad:, :] = u.reshape(ns, ts, tn)
    first = (i % tiles_per_seq) == 0
    up_ref[:, pad - 2:pad, :] = jnp.where(first, st_ref[...], carry_ref[j])
    last2 = up_ref[:, ts + pad - 2:ts + pad, :]
    carry_ref[j] = last2
    tail_ref[0] = last2
    c = bc_ref[...].reshape(1, 1, tn)
    for tap in range(CONV_W):
        c = c + up_ref[:, pad - 2 + tap:pad - 2 + tap + ts, :] * wc_ref[tap:tap + 1, :].reshape(1, 1, tn)
    gl = _gelu(c).reshape(ns * ts, tn)
    v = jnp.dot(h, wv_ref[...], preferred_element_type=F32)
    o_ref[...] = (gl * v).astype(o_ref.dtype)


def _ffn_in(x, g, w, wc, bc, state, ns, ts, tn):
    t, d = x.shape
    f = w.shape[1] // 2
    tm = ns * ts
    nj = f // tn
    seq_len = t // state.shape[0]
    tiles_per_seq = max(seq_len // tm, 1)
    kern = functools.partial(_ffn_in_kernel, ns=ns, ts=ts, tiles_per_seq=tiles_per_seq)
    return pl.pallas_call(
        kern,
        grid=(t // tm, nj),
        in_specs=[pl.BlockSpec((tm, d), lambda i, j: (i, 0)),
                  pl.BlockSpec((1, d), lambda i, j: (0, 0)),
                  pl.BlockSpec((d, tn), lambda i, j: (0, j)),
                  pl.BlockSpec((d, tn), lambda i, j: (0, nj + j)),
                  pl.BlockSpec((CONV_W, tn), lambda i, j: (0, j)),
                  pl.BlockSpec((1, tn), lambda i, j: (0, j)),
                  pl.BlockSpec((ns, CONV_W - 1, tn), lambda i, j: (i // tiles_per_seq, 0, j))],
        out_specs=[pl.BlockSpec((tm, tn), lambda i, j: (i, j)),
                   pl.BlockSpec((1, ns, CONV_W - 1, tn), lambda i, j: (i, 0, 0, j))],
        out_shape=[jax.ShapeDtypeStruct((t, f), BF16),
                   jax.ShapeDtypeStruct((t // tm, ns, CONV_W - 1, f), F32)],
        scratch_shapes=[pltpu.VMEM((tm, d), BF16),
                        pltpu.VMEM((ns, ts + SUBLANES, tn), F32),
                        pltpu.VMEM((nj, ns, CONV_W - 1, tn), F32)],
        compiler_params=_cparams(("arbitrary", "arbitrary")),
        name="ffn_in",
    )(x, g.reshape(1, d), w, w, wc, bc.reshape(1, f), state)


def _norm_kernel(x_ref, g_ref, o_ref):
    o_ref[...] = _rms(x_ref[...], g_ref[...])


def _norm(x, g, tm):
    t, d = x.shape
    return pl.pallas_call(
        _norm_kernel,
        grid=(t // tm,),
        in_specs=[pl.BlockSpec((tm, d), lambda i: (i, 0)), pl.BlockSpec((1, d), lambda i: (0, 0))],
        out_specs=pl.BlockSpec((tm, d), lambda i: (i, 0)),
        out_shape=jax.ShapeDtypeStruct((t, d), F32),
        compiler_params=_cparams(("arbitrary",)),
        name="final_norm",
    )(x, g.reshape(1, d))


def _row_tile(t, cap=1024):
    tm = min(t, cap)
    while t % tm:
        tm //= 2
    return tm


def _pack_w_in(w_in, d_model):
    mix = d_model // 2
    sizes = dict(qa=mix, ka=A_KV * HEAD_DIM, va=A_KV * HEAD_DIM, qi=IDX_HEADS * IDX_DIM, ki=IDX_DIM,
                 wi=IDX_HEADS, qb=mix, kb=mix, vb=mix, uc=mix, g=N_BRANCH * d_model)
    src_order = ["qa", "ka", "va", "qi", "ki", "wi", "qb", "kb", "vb", "uc", "g"]
    src, off = {}, 0
    for name in src_order:
        src[name] = (off, sizes[name])
        off += sizes[name]
    assert off == w_in.shape[1]
    dst_order = ["qa", "qb", "kb", "vb", "uc", "g", "qi", "ka", "va", "ki", "wi"]
    cols, parts, off = {}, [], 0
    for name in dst_order:
        s, n = src[name]
        cols[name] = off
        parts.append(w_in[:, s:s + n])
        off += n
    cols["kw"] = cols["ki"]
    tn = 1280
    total = -(-off // tn) * tn
    parts.append(jnp.zeros((w_in.shape[0], total - off), w_in.dtype))
    return jnp.concatenate(parts, axis=1).astype(BF16), cols, tn


def _to_heads_T(x, b, t):
    return x.reshape(b, t, x.shape[1]).transpose(0, 2, 1)


def _pad_axis(x, axis, size):
    pad = [(0, 0)] * x.ndim
    pad[axis] = (0, size - x.shape[axis])
    return jnp.pad(x, pad)


def _layer(x, pos, prm, cache, layer, is_prompt):
    b, t, d = x.shape
    mix = d // 2
    xt = x.reshape(b * t, d)
    tm = _row_tile(b * t)
    cols = prm["cols"]
    p = _proj(xt, prm["norm1"], prm["w_in"], tm, prm["w_in_tn"])

    tabs = _rope_tables(pos)
    tm_r = _row_tile(t) if is_prompt else tm
    if not is_prompt:
        tabs = jnp.tile(tabs, (1, b, 1))
    qa, ka, kab, vab, qi, kw, kib = _rope(p, tabs, cols, tm_r)
    if is_prompt:
        sq, s_valid, q_pos0 = t, t, 0
        s_pad = -(-s_valid // DSA_TK) * DSA_TK
        k_all = _pad_axis(kab.reshape(b, t, -1), 1, s_pad)
        ki_all = _pad_axis(kib.reshape(b, t, LANES), 1, s_pad)
        vT = _pad_axis(vab.reshape(b, t, -1), 1, s_pad).transpose(0, 2, 1).reshape(b, A_KV, HEAD_DIM, s_pad)
        ones = jnp.broadcast_to((jnp.arange(DSA_VROWS - HEAD_DIM) == 0).astype(BF16)[None, None, :, None],
                                (b, A_KV, DSA_VROWS - HEAD_DIM, s_pad))
        vT = jnp.concatenate([vT, ones], axis=2).reshape(b, A_KV * DSA_VROWS, s_pad)
    else:
        past = cache["a_k"].shape[2]
        sq, s_valid, q_pos0 = DSA_QB, past + t, past
        k_all, vT, ki_all = _cache_prep(cache["a_k"], cache["a_v"], cache["a_kidx"], layer, kab, vab, kib, b, t)
    qaT = _pad_axis(_to_heads_T(qa, b, t), 2, sq)
    qiT = _pad_axis(_to_heads_T(qi, b, t), 2, sq)
    wiT = _pad_axis(_to_heads_T(kw[:, IDX_DIM:IDX_DIM + IDX_HEADS], b, t), 2, sq)
    topk = min(TOPK_MAX, s_valid // 4)
    oaT = _dsa(qaT, qiT, wiT, k_all, vT, ki_all, s_valid=s_valid, q_pos0=q_pos0, topk=topk)
    oa = oaT[:, :, :t].transpose(0, 2, 1).reshape(b * t, mix)

    nh = mix // HEAD_DIM
    if is_prompt:
        n_tiles = BAND_CHUNKS * CHUNK // BAND_QB + 1
        bias = _band_bias(prm["rel_bias"], BAND_CHUNKS * CHUNK, BAND_QB, 0, n_tiles * BAND_QB)
        ob = _band_prompt(p, bias, cols, b, t)
    else:
        nbc = cache["b_k"].shape[2]
        past = cache["a_k"].shape[2]
        bias = _band_bias(prm["rel_bias"], past, t, past - nbc, nbc + t)
        ob = _band_sample(p, cache["b_k"], cache["b_v"], layer, bias, cols, b, t)

    g_ssm = mix // C_GROUP
    if is_prompt:
        h0_re = jnp.zeros((b, g_ssm, C_STATE), F32)
        h0_im = h0_re
    else:
        h0_re, h0_im = cache["c_re"][layer], cache["c_im"][layer]
    yc, hr, hi = _ssm_apply(p, cols["uc"], b, t, h0_re, h0_im, prm["ssm"], is_prompt)
    oc = _glu(yc, prm["w_glu"], prm["b_glu"], tm)

    merged = _merge(oa, ob, oc, p, prm["w_branch"], cols["g"], tm, 512)
    x1 = _mmres(merged, prm["w_out"], xt, tm, 1024)

    f = prm["w_conv"].shape[1]
    if is_prompt:
        state = jnp.zeros((b, CONV_W - 1, f), F32)
        ns, ts = 1, _row_tile(t)
    else:
        state = cache["ffn_conv"][layer].astype(F32)
        ns, ts = b, t
    act, tails = _ffn_in(x1, prm["norm2"], prm["w_ffn_in"], prm["w_conv"], prm["b_conv"], state, ns, ts, 512)
    x2 = _mmres(act, prm["w_down"], x1, _row_tile(b * t, 512), 512)
    if is_prompt:
        buf = tails.reshape(b, t // ts, CONV_W - 1, f)[:, -1]
    else:
        buf = tails[0]

    ka4 = ka.reshape(b, t, A_KV, HEAD_DIM)
    va4 = p[:, cols["va"]:cols["va"] + A_KV * HEAD_DIM].reshape(b, t, A_KV, HEAD_DIM)
    ki3 = kw[:, :IDX_DIM].reshape(b, t, IDX_DIM)
    nb = min(BAND_CHUNKS * CHUNK, t) if is_prompt else t
    pb = p.reshape(b, t, p.shape[1])[:, t - nb:]
    kb4 = pb[:, :, cols["kb"]:cols["kb"] + mix].reshape(b, nb, nh, HEAD_DIM)
    vb4 = pb[:, :, cols["vb"]:cols["vb"] + mix].reshape(b, nb, nh, HEAD_DIM)
    return x2.reshape(b, t, d), (ka4, va4, ki3, kb4, vb4, hr, hi, buf)


def kernel(x_prompt, x_sample, cache_a_k, cache_a_v, cache_a_kidx, cache_b_k, cache_b_v, state_c_re, state_c_im,
           state_ffn_conv, norm1_g, w_in, rel_bias, ssm_a_re, ssm_a_im, ssm_log_dt, ssm_b_re, ssm_b_im, ssm_c_re,
           ssm_c_im, ssm_d, w_glu, b_glu, w_branch, w_out, norm2_g, w_ffn_in, w_ffn_conv, b_ffn_conv, w_ffn_down,
           normf_g):
    depth = w_in.shape[0]
    d = x_prompt.shape[2]
    pos_p = jnp.arange(x_prompt.shape[1])
    pos_s = cache_a_k.shape[2] + jnp.arange(x_sample.shape[1])
    xp, xs = x_prompt, x_sample
    st_p, st_s = [], []
    caches = dict(a_k=cache_a_k, a_v=cache_a_v, a_kidx=cache_a_kidx, b_k=cache_b_k, b_v=cache_b_v,
                  c_re=state_c_re, c_im=state_c_im, ffn_conv=state_ffn_conv)
    for l in range(depth):
        w_in_l, cols, tn = _pack_w_in(w_in[l], d)
        prm = dict(norm1=norm1_g[l], w_in=w_in_l, cols=cols, w_in_tn=tn, rel_bias=rel_bias[l],
                   ssm=_ssm_weights(ssm_a_re[l], ssm_a_im[l], ssm_log_dt[l], ssm_b_re[l], ssm_b_im[l],
                                    ssm_c_re[l], ssm_c_im[l], ssm_d[l]),
                   w_glu=w_glu[l].astype(BF16), b_glu=b_glu[l], w_branch=w_branch[l].astype(BF16),
                   w_out=w_out[l].astype(BF16), norm2=norm2_g[l], w_ffn_in=w_ffn_in[l].astype(BF16),
                   w_conv=w_ffn_conv[l], b_conv=b_ffn_conv[l], w_down=w_ffn_down[l].astype(BF16))
        xp, sp = _layer(xp, pos_p, prm, None, l, True)
        xs, ss = _layer(xs, pos_s, prm, caches, l, False)
        st_p.append(sp)
        st_s.append(ss)
    bp, tp, _ = xp.shape
    bs, tsq, _ = xs.shape
    y_prompt = _norm(xp.reshape(bp * tp, d), normf_g, _row_tile(bp * tp)).reshape(bp, tp, d)
    y_sample = _norm(xs.reshape(bs * tsq, d), normf_g, _row_tile(bs * tsq)).reshape(bs, tsq, d)
    outs_p = [jnp.stack([s[i] for s in st_p]) for i in range(8)]
    outs_s = [jnp.stack([s[i] for s in st_s]) for i in range(8)]
    return (y_prompt, y_sample, *outs_p, *outs_s)
```

```python
import functools
import math

import jax
import jax.numpy as jnp
import numpy as np
from jax import lax
from jax.experimental import pallas as pl
from jax.experimental.pallas import tpu as pltpu

F32 = jnp.float32
BF16 = jnp.bfloat16
I32 = jnp.int32

CHUNK = 64
EPS = 1e-6
ROPE_THETA = 500000.0
HEAD_DIM = 128
A_KV = 2
A_GROUP = 4
A_ROT = HEAD_DIM // 4
IDX_HEADS = 8
IDX_DIM = 64
IDX_ROT = IDX_DIM // 4
TOPK_MAX = 256
BAND_CHUNKS = 8
REL_CLIP = 256
C_GROUP = 16
C_STATE = 64
CONV_W = 3
N_BRANCH = 3
ATTN_SCALE = HEAD_DIM ** -0.5
IDX_SCALE = (IDX_DIM * IDX_HEADS) ** -0.5

LANES = 128
SUBLANES = 8
VMEM_LIMIT = 56 * 1024 * 1024

NEG_BIG = -1e30
KEY_NEG_INF = -2139095041
KEY_POS_INF = 2139095040
INT32_MIN = -2147483648

SSM_L = SUBLANES
SSM_GB = LANES // C_GROUP
DSA_TK = 512
DSA_TA = 512
DSA_VROWS = HEAD_DIM + 16
DSA_QB = 128
BAND_QB = 128


def _cparams(sem):
    return pltpu.CompilerParams(dimension_semantics=sem, vmem_limit_bytes=VMEM_LIMIT)


def _gelu(x):
    return 0.5 * x * (1.0 + jnp.tanh(math.sqrt(2.0 / math.pi) * (x + 0.044715 * (x * x * x))))


def _sigmoid(x):
    return 1.0 / (1.0 + jnp.exp(-x))


def _rms(x, g):
    ms = jnp.mean(x * x, axis=-1, keepdims=True)
    return (x * lax.rsqrt(ms + EPS)) * g


def _proj_kernel(x_ref, g_ref, w_ref, o_ref, h_ref):
    @pl.when(pl.program_id(1) == 0)
    def _():
        h_ref[...] = _rms(x_ref[...], g_ref[...]).astype(BF16)

    o_ref[...] = jnp.dot(h_ref[...], w_ref[...], preferred_element_type=F32)


def _proj(x, g, w, tm, tn):
    t, d = x.shape
    n = w.shape[1]
    return pl.pallas_call(
        _proj_kernel,
        grid=(t // tm, n // tn),
        in_specs=[pl.BlockSpec((tm, d), lambda i, j: (i, 0)),
                  pl.BlockSpec((1, d), lambda i, j: (0, 0)),
                  pl.BlockSpec((d, tn), lambda i, j: (0, j))],
        out_specs=pl.BlockSpec((tm, tn), lambda i, j: (i, j)),
        out_shape=jax.ShapeDtypeStruct((t, n), F32),
        scratch_shapes=[pltpu.VMEM((tm, d), BF16)],
        compiler_params=_cparams(("arbitrary", "arbitrary")),
        name="proj",
    )(x, g.reshape(1, d), w)


def _rot(x, c, sa, sb, half):
    n = x.shape[-1]
    return x * c + pltpu.roll(x, n - half, 1) * sa + pltpu.roll(x, half, 1) * sb


def _rope_kernel(qa_ref, ka_ref, va_ref, qi_ref, kw_ref, tab_ref,
                 qa_o, ka_o, kab_o, vab_o, qi_o, kw_o, kib_o):
    ca, saa, sba = tab_ref[0], tab_ref[1], tab_ref[2]
    ci, sai, sbi = tab_ref[3], tab_ref[4], tab_ref[5]
    ck, sak, sbk = tab_ref[6], tab_ref[7], tab_ref[8]
    for h in range(qa_ref.shape[1] // LANES):
        sl = slice(h * LANES, (h + 1) * LANES)
        qa_o[:, sl] = _rot(qa_ref[:, sl], ca, saa, sba, A_ROT // 2).astype(BF16)
    for h in range(ka_ref.shape[1] // LANES):
        sl = slice(h * LANES, (h + 1) * LANES)
        k = _rot(ka_ref[:, sl], ca, saa, sba, A_ROT // 2)
        ka_o[:, sl] = k
        kab_o[:, sl] = k.astype(BF16)
    vab_o[...] = va_ref[...].astype(BF16)
    for h in range(qi_ref.shape[1] // LANES):
        sl = slice(h * LANES, (h + 1) * LANES)
        qi_o[:, sl] = _rot(qi_ref[:, sl], ci, sai, sbi, IDX_ROT // 2).astype(BF16)
    kw = _rot(kw_ref[...], ck, sak, sbk, IDX_ROT // 2)
    kw_o[...] = kw
    kib_o[...] = kw.astype(BF16)


def _rope(p, tabs, cols, tm):
    t = p.shape[0]
    n_tab_blocks = tabs.shape[1] // tm
    d_qa, d_kv, d_qi = A_KV * A_GROUP * HEAD_DIM, A_KV * HEAD_DIM, IDX_HEADS * IDX_DIM

    def col(width, off):
        assert off % width == 0
        return pl.BlockSpec((tm, width), lambda i: (i, off // width))

    def out(width):
        return pl.BlockSpec((tm, width), lambda i: (i, 0))

    return pl.pallas_call(
        _rope_kernel,
        grid=(t // tm,),
        in_specs=[col(d_qa, cols["qa"]), col(d_kv, cols["ka"]), col(d_kv, cols["va"]),
                  col(d_qi, cols["qi"]), col(LANES, cols["kw"]),
                  pl.BlockSpec((9, tm, LANES), lambda i: (0, i % n_tab_blocks, 0))],
        out_specs=[out(d_qa), out(d_kv), out(d_kv), out(d_kv), out(d_qi), out(LANES), out(LANES)],
        out_shape=[jax.ShapeDtypeStruct((t, d_qa), BF16),
                   jax.ShapeDtypeStruct((t, d_kv), F32),
                   jax.ShapeDtypeStruct((t, d_kv), BF16),
                   jax.ShapeDtypeStruct((t, d_kv), BF16),
                   jax.ShapeDtypeStruct((t, d_qi), BF16),
                   jax.ShapeDtypeStruct((t, LANES), F32),
                   jax.ShapeDtypeStruct((t, LANES), BF16)],
        compiler_params=_cparams(("arbitrary",)),
        name="rope",
    )(p, p, p, p, p, tabs)


def _rope_tables(pos):
    pos = pos.astype(F32)[:, None]

    def cs(rot):
        half = rot // 2
        inv = jnp.float32(ROPE_THETA) ** (-jnp.arange(half, dtype=F32) / half)
        ang = pos * inv[None, :]
        return jnp.cos(ang), jnp.sin(ang)

    def tabs(rot, width):
        half = rot // 2
        c, s = cs(rot)
        n = pos.shape[0]
        one = jnp.ones((n, width - rot), F32)
        zero = jnp.zeros((n, width - rot), F32)
        zh = jnp.zeros((n, half), F32)
        return (jnp.concatenate([c, c, one], 1), jnp.concatenate([-s, zh, zero], 1),
                jnp.concatenate([zh, s, zero], 1))

    ca, saa, sba = tabs(A_ROT, HEAD_DIM)
    c64, sa64, sb64 = tabs(IDX_ROT, IDX_DIM)
    n = pos.shape[0]
    one64, zero64 = jnp.ones((n, IDX_DIM), F32), jnp.zeros((n, IDX_DIM), F32)
    return jnp.stack([ca, saa, sba,
                      jnp.concatenate([c64, c64], 1), jnp.concatenate([sa64, sa64], 1),
                      jnp.concatenate([sb64, sb64], 1),
                      jnp.concatenate([c64, one64], 1), jnp.concatenate([sa64, zero64], 1),
                      jnp.concatenate([sb64, zero64], 1)])


def _dsa_kernel(qaT_ref, qiT_ref, wiT_ref, k_ref, vT_ref, ki_ref, o_ref, key_ref, acc_ref,
                s0_ref, s1_ref, p0_ref, p1_ref, a0_ref, a1_ref,
                *, s_valid, q_pos0, topk):
    qb, tk = DSA_QB, DSA_TK
    j = pl.program_id(1)
    q_first = q_pos0 + j * qb
    qpos = q_first + lax.broadcasted_iota(I32, (1, qb), 1)
    assert CHUNK == 64 and tk == 512
    n_adm = jnp.minimum(((qpos >> 6) + 1) * CHUNK, s_valid)
    n_max = jnp.minimum((((q_first + qb - 1) >> 6) + 1) * CHUNK, s_valid)
    nkt = (n_max + tk - 1) >> 9

    qi = qiT_ref[0]
    rhs = jnp.concatenate([qi[h * IDX_DIM:(h + 1) * IDX_DIM, :] for h in range(IDX_HEADS)], axis=1)
    rhs = jnp.concatenate([rhs, jnp.zeros((LANES - IDX_DIM, IDX_HEADS * qb), BF16)], axis=0)
    wi = wiT_ref[0]

    def p1(kt, carry):
        off = pl.multiple_of(kt * tk, tk)
        kid = ki_ref[0, pl.ds(off, tk), :]
        s_all = jnp.dot(kid, rhs, preferred_element_type=F32)
        acc = wi[0:1, :] * jnp.maximum(s_all[:, 0:qb], 0.0)
        for h in range(1, IDX_HEADS):
            acc = acc + wi[h:h + 1, :] * jnp.maximum(s_all[:, h * qb:(h + 1) * qb], 0.0)
        score = acc * IDX_SCALE
        kpos = off + lax.broadcasted_iota(I32, (tk, 1), 0)
        score = jnp.where(kpos < n_adm, score, -jnp.inf)
        u = pltpu.bitcast(score, I32)
        key_ref[pl.ds(off, tk), :] = u ^ (lax.shift_right_arithmetic(u, 31) & 0x7FFFFFFF)
        return carry

    lax.fori_loop(0, nkt, p1, 0)

    pairs = key_ref.shape[0] % (2 * tk) == 0
    if pairs:
        @pl.when(nkt % 2 == 1)
        def _():
            key_ref[pl.ds(pl.multiple_of(nkt * tk, tk), tk), :] = jnp.full((tk, qb), INT32_MIN, I32)

    n_trips, per_trip = ((nkt + 1) // 2, 2) if pairs else (nkt, 1)

    def count_ge(cand):
        def body(kt, cnt):
            for half in range(per_trip):
                blk = key_ref[pl.ds(pl.multiple_of((kt * per_trip + half) * tk, tk), tk), :]
                m = jnp.where(blk >= cand, 1, 0).astype(I32)
                cnt = cnt + jnp.sum(m.reshape(tk // SUBLANES, SUBLANES, qb), axis=0)
            return cnt

        cnt8 = lax.fori_loop(0, n_trips, body, jnp.zeros((SUBLANES, qb), I32))
        return jnp.sum(cnt8, axis=0, keepdims=True)

    def bit_body(i, carry):
        prefix, n_ge = carry
        cand_u = prefix | lax.shift_left(jnp.int32(1), 31 - i)
        cnt = count_ge(cand_u ^ INT32_MIN)
        ok = cnt >= topk
        return jnp.where(ok, cand_u, prefix), jnp.where(ok, cnt, n_ge)

    prefix, n_ge = lax.fori_loop(0, 32, bit_body, (jnp.zeros((1, qb), I32), jnp.broadcast_to(nkt * tk, (1, qb))))
    thr = prefix ^ INT32_MIN
    cut_ties = jnp.max(jnp.where((n_ge > topk) & (thr > KEY_NEG_INF), 1, 0)) > 0

    @pl.when(cut_ties)
    def _():
        r = lax.broadcasted_iota(I32, (tk, tk), 0)
        c = lax.broadcasted_iota(I32, (tk, tk), 1)
        tri = jnp.where(c <= r, 1.0, 0.0).astype(BF16)
        take = (topk - count_ge(thr + 1)).astype(F32)

        def body(kt, seen):
            off = pl.multiple_of(kt * tk, tk)
            blk = key_ref[pl.ds(off, tk), :]
            eq = blk == thr
            incl = jnp.dot(tri, jnp.where(eq, 1.0, 0.0).astype(BF16), preferred_element_type=F32)
            drop = eq & (seen + incl > take)
            key_ref[pl.ds(off, tk), :] = jnp.where(drop, thr - 1, blk)
            return seen + incl[tk - 1:tk, :]

        lax.fori_loop(0, nkt, body, jnp.zeros((1, qb), F32))

    thr_lo = jnp.maximum(thr, KEY_NEG_INF + 1)
    qa = qaT_ref[0]
    ng = A_GROUP * qb
    qg = [jnp.concatenate([qa[(g * A_GROUP + hh) * HEAD_DIM:(g * A_GROUP + hh + 1) * HEAD_DIM, :]
                           for hh in range(A_GROUP)], axis=1) for g in range(A_KV)]
    acc_ref[...] = jnp.zeros(acc_ref.shape, F32)

    c_exp = ATTN_SCALE * math.log2(math.e)

    ta = DSA_TA
    n_steps = nkt * (tk // ta)
    last = n_steps - 1
    bufs = ((s0_ref, p0_ref, a0_ref), (s1_ref, p1_ref, a1_ref))

    def logits(step, s_ref):
        off = pl.multiple_of(jnp.minimum(step, last) * ta, ta)
        for g in range(A_KV):
            kg = k_ref[0, pl.ds(off, ta), g * HEAD_DIM:(g + 1) * HEAD_DIM]
            s_ref[g] = jnp.dot(kg, qg[g], preferred_element_type=F32)

    def numerators(step, s_ref, p_ref, a_ref, ms):
        off = pl.multiple_of(jnp.minimum(step, last) * ta, ta)
        blk = key_ref[pl.ds(off, ta), :]
        sel = (blk >= thr_lo) & (blk < KEY_POS_INF) & (step <= last)
        mask = jnp.concatenate([jnp.where(sel, 0.0, NEG_BIG)] * A_GROUP, axis=1)
        new = []
        for g in range(A_KV):
            s = s_ref[g] + mask
            m_new = jnp.maximum(ms[g], jnp.max(s, axis=0, keepdims=True))
            a_ref[g] = jnp.broadcast_to(jnp.exp2((ms[g] - m_new) * c_exp), (SUBLANES, ng))
            p_ref[g] = jnp.exp2((s - m_new).astype(BF16) * c_exp)
            new.append(m_new)
        return tuple(new)

    def weighted_values(step, p_ref, a_ref):
        off = pl.multiple_of(jnp.clip(step, 0, last) * ta, ta)
        for g in range(A_KV):
            vg = vT_ref[0, g * DSA_VROWS:(g + 1) * DSA_VROWS, pl.ds(off, ta)]
            acc_ref[g] = a_ref[g, 0:1, :] * acc_ref[g] + jnp.dot(vg, p_ref[g], preferred_element_type=F32)

    p1_ref[...] = jnp.zeros(p1_ref.shape, BF16)
    a1_ref[...] = jnp.ones(a1_ref.shape, F32)
    logits(0, s0_ref)

    def two_steps(i2, ms):
        i = 2 * i2
        for par in range(2):
            cur, nxt = bufs[par], bufs[1 - par]
            logits(i + par + 1, nxt[0])
            ms = numerators(i + par, cur[0], cur[1], cur[2], ms)
            weighted_values(i + par - 1, nxt[1], nxt[2])
        return ms

    init = (jnp.full((1, ng), NEG_BIG, F32),) * A_KV
    n_pairs = (n_steps + 1) // 2
    lax.fori_loop(0, n_pairs, two_steps, init)
    weighted_values(2 * n_pairs - 1, p1_ref, a1_ref)
    for g in range(A_KV):
        o = acc_ref[g, 0:HEAD_DIM, :] / acc_ref[g, HEAD_DIM:HEAD_DIM + 1, :]
        for hh in range(A_GROUP):
            h = g * A_GROUP + hh
            o_ref[0, h * HEAD_DIM:(h + 1) * HEAD_DIM, :] = o[:, hh * qb:(hh + 1) * qb].astype(o_ref.dtype)


def _dsa(qaT, qiT, wiT, k, vT, ki, *, s_valid, q_pos0, topk):
    b, dq, sq = qaT.shape
    s_pad = k.shape[1]
    assert s_pad % DSA_TK == 0 and s_pad % DSA_TA == 0 and sq % DSA_QB == 0 and topk <= DSA_TK
    kern = functools.partial(_dsa_kernel, s_valid=s_valid, q_pos0=q_pos0, topk=topk)
    return pl.pallas_call(
        kern,
        grid=(b, sq // DSA_QB),
        in_specs=[pl.BlockSpec((1, dq, DSA_QB), lambda i, j: (i, 0, j)),
                  pl.BlockSpec((1, qiT.shape[1], DSA_QB), lambda i, j: (i, 0, j)),
                  pl.BlockSpec((1, IDX_HEADS, DSA_QB), lambda i, j: (i, 0, j)),
                  pl.BlockSpec((1, s_pad, k.shape[2]), lambda i, j: (i, 0, 0)),
                  pl.BlockSpec((1, vT.shape[1], s_pad), lambda i, j: (i, 0, 0)),
                  pl.BlockSpec((1, s_pad, LANES), lambda i, j: (i, 0, 0))],
        out_specs=pl.BlockSpec((1, dq, DSA_QB), lambda i, j: (i, 0, j)),
        out_shape=jax.ShapeDtypeStruct((b, dq, sq), BF16),
        scratch_shapes=[pltpu.VMEM((s_pad, DSA_QB), I32),
                        pltpu.VMEM((A_KV, DSA_VROWS, A_GROUP * DSA_QB), F32)]
                       + [pltpu.VMEM((A_KV, DSA_TA, A_GROUP * DSA_QB), F32)] * 2
                       + [pltpu.VMEM((A_KV, DSA_TA, A_GROUP * DSA_QB), BF16)] * 2
                       + [pltpu.VMEM((A_KV, SUBLANES, A_GROUP * DSA_QB), F32)] * 2,
        compiler_params=_cparams(("arbitrary", "arbitrary")),
        name="dsa",
    )(qaT, qiT, wiT, k, vT, ki)


def _cache_prep_kernel(ck_ref, cv_ref, cki_ref, kn_ref, vn_ref, kin_ref, k_o, vT_o, ki_o, *, n_cache_tiles):
    kt = pl.program_id(1)
    tk = k_o.shape[1]
    extra = DSA_VROWS - HEAD_DIM
    ones_rows = jnp.where(lax.broadcasted_iota(I32, (extra, tk), 0) == 0, 1.0, 0.0).astype(BF16)

    def emit(k_head, v_head, ki):
        for g in range(A_KV):
            k_o[0, :, g * HEAD_DIM:(g + 1) * HEAD_DIM] = k_head(g).astype(BF16)
            vT_o[0, g * DSA_VROWS:g * DSA_VROWS + HEAD_DIM, :] = v_head(g).astype(F32).T.astype(BF16)
            vT_o[0, g * DSA_VROWS + HEAD_DIM:(g + 1) * DSA_VROWS, :] = ones_rows
        ki_o[0] = ki

    @pl.when(kt < n_cache_tiles)
    def _():
        ki = jnp.concatenate([cki_ref[0, 0], jnp.zeros((tk, LANES - IDX_DIM), F32)], axis=1)
        emit(lambda g: ck_ref[0, 0, :, g, :], lambda g: cv_ref[0, 0, :, g, :], ki.astype(BF16))

    @pl.when(kt >= n_cache_tiles)
    def _():
        t = kn_ref.shape[0]
        rows = lambda a: jnp.concatenate([a, jnp.zeros((tk - t, a.shape[1]), a.dtype)], axis=0)
        kn, vn = rows(kn_ref[...]), rows(vn_ref[...])
        head = lambda a: lambda g: a[:, g * HEAD_DIM:(g + 1) * HEAD_DIM]
        emit(head(kn), head(vn), rows(kin_ref[...]))


def _cache_prep(cache_k, cache_v, cache_ki, layer, kn, vn, kin, b, t):
    tk = DSA_TK
    past = cache_k.shape[2]
    assert past % tk == 0 and t <= tk
    nct = past // tk
    s_pad = past + tk
    cidx = lambda i, j: jnp.minimum(j, nct - 1)
    return pl.pallas_call(
        functools.partial(_cache_prep_kernel, n_cache_tiles=nct),
        grid=(b, nct + 1),
        in_specs=[pl.BlockSpec((1, 1, tk, A_KV, HEAD_DIM), lambda i, j: (layer, i, cidx(i, j), 0, 0)),
                  pl.BlockSpec((1, 1, tk, A_KV, HEAD_DIM), lambda i, j: (layer, i, cidx(i, j), 0, 0)),
                  pl.BlockSpec((1, 1, tk, IDX_DIM), lambda i, j: (layer, i, cidx(i, j), 0)),
                  pl.BlockSpec((t, A_KV * HEAD_DIM), lambda i, j: (i, 0)),
                  pl.BlockSpec((t, A_KV * HEAD_DIM), lambda i, j: (i, 0)),
                  pl.BlockSpec((t, LANES), lambda i, j: (i, 0))],
        out_specs=[pl.BlockSpec((1, tk, A_KV * HEAD_DIM), lambda i, j: (i, j, 0)),
                   pl.BlockSpec((1, A_KV * DSA_VROWS, tk), lambda i, j: (i, 0, j)),
                   pl.BlockSpec((1, tk, LANES), lambda i, j: (i, j, 0))],
        out_shape=[jax.ShapeDtypeStruct((b, s_pad, A_KV * HEAD_DIM), BF16),
                   jax.ShapeDtypeStruct((b, A_KV * DSA_VROWS, s_pad), BF16),
                   jax.ShapeDtypeStruct((b, s_pad, LANES), BF16)],
        compiler_params=_cparams(("arbitrary", "arbitrary")),
        name="cache_prep",
    )(cache_k, cache_v, cache_ki, kn, vn, kin)


def _band_heads(q, k_head, v_head, bias_ref, n_hidden, o_ref, s_ref):
    nh = q.shape[1] // HEAD_DIM
    tk = bias_ref.shape[2]
    hidden = None if n_hidden is None else lax.broadcasted_iota(I32, (1, tk), 1) < n_hidden
    for h in range(nh):
        sl = slice(h * HEAD_DIM, (h + 1) * HEAD_DIM)
        s = lax.dot_general(q[:, sl].astype(BF16), k_head(h), (((1,), (1,)), ((), ())),
                            preferred_element_type=F32) * ATTN_SCALE + bias_ref[h]
        if hidden is not None:
            s = jnp.where(hidden, NEG_BIG, s)
        s_ref[h] = s
    ms = [jnp.max(s_ref[h], axis=1, keepdims=True) for h in range(nh)]
    ls = []
    for h in range(nh):
        p = jnp.exp(s_ref[h] - ms[h])
        s_ref[h] = p
        ls.append(jnp.sum(p, axis=1, keepdims=True))
    for h in range(nh):
        sl = slice(h * HEAD_DIM, (h + 1) * HEAD_DIM)
        w = (s_ref[h] / ls[h]).astype(BF16)
        o_ref[:, sl] = jnp.dot(w, v_head(h), preferred_element_type=F32).astype(o_ref.dtype)


def _band_prompt_kernel(*refs, n_tiles):
    q_ref = refs[0]
    k_refs = refs[1:1 + n_tiles]
    v_refs = refs[1 + n_tiles:1 + 2 * n_tiles]
    bias_ref = refs[1 + 2 * n_tiles]
    o_ref = refs[2 + 2 * n_tiles]
    s_ref = refs[3 + 2 * n_tiles]
    j = pl.program_id(1)
    n_hidden = jnp.maximum(n_tiles - 1 - j, 0) * q_ref.shape[0]
    k_all = jnp.concatenate([r[...] for r in k_refs], axis=0).astype(BF16)
    v_all = jnp.concatenate([r[...] for r in v_refs], axis=0).astype(BF16)
    head = lambda a: lambda h: a[:, h * HEAD_DIM:(h + 1) * HEAD_DIM]
    _band_heads(q_ref[...], head(k_all), head(v_all), bias_ref, n_hidden, o_ref, s_ref)


def _band_prompt(p, bias, cols, batch, seq):
    qb = BAND_QB
    n_tiles = BAND_CHUNKS * CHUNK // qb + 1
    width = bias.shape[0] * HEAD_DIM
    nq = seq // qb
    cq, ck, cv = cols["qb"] // width, cols["kb"] // width, cols["vb"] // width
    assert cols["qb"] % width == 0 and cols["kb"] % width == 0 and cols["vb"] % width == 0

    def kv_spec(c, i):
        return pl.BlockSpec((qb, width), lambda b, j: (b * nq + jnp.maximum(j - (n_tiles - 1) + i, 0), c))

    return pl.pallas_call(
        functools.partial(_band_prompt_kernel, n_tiles=n_tiles),
        grid=(batch, nq),
        in_specs=([pl.BlockSpec((qb, width), lambda b, j: (b * nq + j, cq))]
                  + [kv_spec(ck, i) for i in range(n_tiles)]
                  + [kv_spec(cv, i) for i in range(n_tiles)]
                  + [pl.BlockSpec(bias.shape, lambda b, j: (0, 0, 0))]),
        out_specs=pl.BlockSpec((qb, width), lambda b, j: (b * nq + j, 0)),
        out_shape=jax.ShapeDtypeStruct((batch * seq, width), BF16),
        compiler_params=_cparams(("arbitrary", "arbitrary")),
        scratch_shapes=[pltpu.VMEM(bias.shape, F32)],
        name="band_prompt",
    )(*([p] * (1 + 2 * n_tiles)), bias)


def _band_sample_kernel(q_ref, k_ref, v_ref, ck_ref, cv_ref, bias_ref, o_ref, s_ref):
    def head(cache_ref, new_ref):
        return lambda h: jnp.concatenate(
            [cache_ref[0, 0, :, h, :], new_ref[:, h * HEAD_DIM:(h + 1) * HEAD_DIM]], axis=0).astype(BF16)

    _band_heads(q_ref[...], head(ck_ref, k_ref), head(cv_ref, v_ref), bias_ref, None, o_ref, s_ref)


def _band_sample(p, cache_k, cache_v, layer, bias, cols, batch, t):
    nh = bias.shape[0]
    width = nh * HEAD_DIM
    nbc = cache_k.shape[2]
    cq, ck, cv = cols["qb"] // width, cols["kb"] // width, cols["vb"] // width
    return pl.pallas_call(
        _band_sample_kernel,
        grid=(batch,),
        in_specs=[pl.BlockSpec((t, width), lambda b: (b, cq)),
                  pl.BlockSpec((t, width), lambda b: (b, ck)),
                  pl.BlockSpec((t, width), lambda b: (b, cv)),
                  pl.BlockSpec((1, 1, nbc, nh, HEAD_DIM), lambda b: (layer, b, 0, 0, 0)),
                  pl.BlockSpec((1, 1, nbc, nh, HEAD_DIM), lambda b: (layer, b, 0, 0, 0)),
                  pl.BlockSpec(bias.shape, lambda b: (0, 0, 0))],
        out_specs=pl.BlockSpec((t, width), lambda b: (b, 0)),
        out_shape=jax.ShapeDtypeStruct((batch * t, width), BF16),
        compiler_params=_cparams(("arbitrary",)),
        scratch_shapes=[pltpu.VMEM(bias.shape, F32)],
        name="band_sample",
    )(p, p, p, cache_k, cache_v, bias)


def _band_bias(table, q0, nq, k0, nk):
    n = nq + nk
    j = jnp.arange(n)
    g = table[:, jnp.clip(j + (q0 - k0 - nk + 1), -REL_CLIP, REL_CLIP) + REL_CLIP].astype(F32)
    hank = jnp.tile(g, (1, nq + 1))[:, :nq * (n + 1)].reshape(-1, nq, n + 1)[:, :, :nk]
    bias = hank[:, :, ::-1]
    qc = (q0 + jnp.arange(nq))[:, None] // CHUNK
    kc = (k0 + jnp.arange(nk))[None, :] // CHUNK
    vis = (kc <= qc) & (kc >= qc - BAND_CHUNKS)
    return jnp.where(vis[None], bias, NEG_BIG)


def _ssm_kernel(u_ref, t_ref, p_ref, q_ref, a_ref, h0_ref, y_ref, hre_o, him_o, sre_ref, sim_ref,
                *, seg, tiles_per_seq):
    L = SSM_L
    x = jnp.concatenate([u_ref[:, s, :] for s in range(L)], axis=1).astype(BF16)
    sre_ref[...] = jnp.dot(x, p_ref[0, 0], preferred_element_type=F32)
    sim_ref[...] = jnp.dot(x, p_ref[0, 1], preferred_element_type=F32)
    rows, w = sre_ref.shape
    kseg = lax.broadcasted_iota(I32, (SUBLANES, 1), 0) % seg
    apk_re, apk_im = a_ref[0, 0, 0:SUBLANES, :], a_ref[0, 1, 0:SUBLANES, :]

    def cmad(x_re, x_im, a_re, a_im, y_re, y_im):
        return x_re + a_re * y_re - a_im * y_im, x_im + a_re * y_im + a_im * y_re

    def tile(i, carry):
        r = pl.ds(pl.multiple_of(i * SUBLANES, SUBLANES), SUBLANES)
        x_re, x_im = sre_ref[r, :], sim_ref[r, :]
        for n, d in enumerate((1, 2, 4)):
            if d < seg:
                ad_re = a_ref[0, 0, SUBLANES + n:SUBLANES + n + 1, :]
                ad_im = a_ref[0, 1, SUBLANES + n:SUBLANES + n + 1, :]
                sh_re = jnp.where(kseg >= d, pltpu.roll(x_re, d, 0), 0.0)
                sh_im = jnp.where(kseg >= d, pltpu.roll(x_im, d, 0), 0.0)
                x_re, x_im = cmad(x_re, x_im, ad_re, ad_im, sh_re, sh_im)
        if tiles_per_seq > 1:
            first = (i % tiles_per_seq) == 0
            hin_re = jnp.where(first, h0_ref[0, 0], carry[0])
            hin_im = jnp.where(first, h0_ref[0, 1], carry[1])
        else:
            hin_re, hin_im = h0_ref[0, 0, r, :], h0_ref[0, 1, r, :]
        inc_re, inc_im = cmad(x_re, x_im, apk_re, apk_im, hin_re, hin_im)
        sre_ref[r, :] = jnp.where(kseg == 0, hin_re, pltpu.roll(inc_re, 1, 0))
        sim_ref[r, :] = jnp.where(kseg == 0, hin_im, pltpu.roll(inc_im, 1, 0))
        hre_o[0, r, :] = inc_re
        him_o[0, r, :] = inc_im
        return inc_re[SUBLANES - 1:SUBLANES, :], inc_im[SUBLANES - 1:SUBLANES, :]

    zero = jnp.zeros((1, w), F32)
    lax.fori_loop(0, rows // SUBLANES, tile, (zero, zero))
    y = (jnp.dot(x, t_ref[0], preferred_element_type=F32)
         + jnp.dot(sre_ref[...].astype(BF16), q_ref[0, 0], preferred_element_type=F32)
         + jnp.dot(sim_ref[...].astype(BF16), q_ref[0, 1], preferred_element_type=F32))
    for t in range(L):
        y_ref[:, t, :] = y[:, t * LANES:(t + 1) * LANES]


def _ssm(p3, uc_col, tmat, pmat, qmat, apow, h0, rows_blk, seg, tiles_per_seq):
    rows_total = p3.shape[0]
    nb, _, _, w = pmat.shape
    kl = SSM_L * LANES
    h0_rows = h0.shape[2] // (rows_total // rows_blk)
    hspec = pl.BlockSpec((1, rows_blk, w), lambda i, r: (i, r, 0))
    return pl.pallas_call(
        functools.partial(_ssm_kernel, seg=seg, tiles_per_seq=tiles_per_seq),
        grid=(nb, rows_total // rows_blk),
        in_specs=[pl.BlockSpec((rows_blk, SSM_L, LANES), lambda i, r: (r, 0, uc_col // LANES + i)),
                  pl.BlockSpec((1, kl, kl), lambda i, r: (i, 0, 0)),
                  pl.BlockSpec((1, 2, kl, w), lambda i, r: (i, 0, 0, 0)),
                  pl.BlockSpec((1, 2, w, kl), lambda i, r: (i, 0, 0, 0)),
                  pl.BlockSpec((1, 2, 2 * SUBLANES, w), lambda i, r: (i, 0, 0, 0)),
                  pl.BlockSpec((1, 2, h0_rows, w), lambda i, r: (i, 0, r, 0))],
        out_specs=[pl.BlockSpec((rows_blk, SSM_L, LANES), lambda i, r: (r, 0, i)), hspec, hspec],
        out_shape=[jax.ShapeDtypeStruct((rows_total, SSM_L, nb * LANES), F32),
                   jax.ShapeDtypeStruct((nb, rows_total, w), F32),
                   jax.ShapeDtypeStruct((nb, rows_total, w), F32)],
        scratch_shapes=[pltpu.VMEM((rows_blk, w), F32), pltpu.VMEM((rows_blk, w), F32)],
        compiler_params=_cparams(("arbitrary", "arbitrary")),
        name="ssm",
    )(p3, tmat, pmat, qmat, apow, h0)


def _ssm_weights(a_re, a_im, log_dt, b_re, b_im, c_re, c_im, d):
    hp = lax.Precision.HIGHEST
    g, p = a_re.shape
    gc = b_re.shape[2]
    L = SSM_L
    a_re, a_im = a_re.astype(F32), a_im.astype(F32)
    dt = jnp.exp(log_dt.astype(F32))[:, None]
    mag = jnp.exp(dt * a_re)
    ab_re, ab_im = mag * jnp.cos(dt * a_im), mag * jnp.sin(dt * a_im)
    n_re = ab_re - 1.0
    n_im = ab_im
    den = a_re * a_re + a_im * a_im
    cc_re = (n_re * a_re + n_im * a_im) / den
    cc_im = (n_im * a_re - n_re * a_im) / den
    b_re, b_im = b_re.astype(F32), b_im.astype(F32)
    bb_re = cc_re[..., None] * b_re - cc_im[..., None] * b_im
    bb_im = cc_re[..., None] * b_im + cc_im[..., None] * b_re
    c_re, c_im = c_re.astype(F32), c_im.astype(F32)

    def cmul(x, y):
        return x[0] * y[0] - x[1] * y[1], x[0] * y[1] + x[1] * y[0]

    rep = lambda a: jnp.broadcast_to(a[None], (L,) + a.shape)
    pw_re, pw_im = lax.associative_scan(cmul, (rep(ab_re), rep(ab_im)), axis=0)
    pw_re = jnp.concatenate([jnp.ones((1, g, p), F32), pw_re], axis=0)
    pw_im = jnp.concatenate([jnp.zeros((1, g, p), F32), pw_im], axis=0)
    ca_re = c_re[None] * pw_re[:, :, None, :] - c_im[None] * pw_im[:, :, None, :]
    ca_im = c_re[None] * pw_im[:, :, None, :] + c_im[None] * pw_re[:, :, None, :]
    taps = (jnp.einsum("tgop,gpi->tgoi", ca_re[:L], bb_re, precision=hp)
            - jnp.einsum("tgop,gpi->tgoi", ca_im[:L], bb_im, precision=hp))
    taps = taps.at[0].add(jax.vmap(jnp.diag)(d.astype(F32)))
    s_idx = jnp.arange(L)[:, None]
    t_idx = jnp.arange(L)[None, :]
    lag = t_idx - s_idx
    tt = jnp.where((lag >= 0)[:, :, None, None, None], taps[jnp.clip(lag, 0, L - 1)], 0.0)
    gb = SSM_GB
    nb = g // gb
    n_x = L * gb * gc
    x = jnp.arange(n_x)
    x_grp, x_loc = (x // gc) % gb, (x // (gb * gc)) * gc + x % gc
    place_x = ((x_grp[None, :, None] == jnp.arange(gb)[:, None, None])
               & (x_loc[None, :, None] == jnp.arange(L * gc)[None, None, :])).astype(F32)
    q = jnp.arange(gb * p)
    place_q = ((q[None, :, None] // p == jnp.arange(gb)[:, None, None])
               & (q[None, :, None] % p == jnp.arange(p)[None, None, :])).astype(F32)
    blk = lambda a: a.reshape((nb, gb) + a.shape[1:])
    t_grp = blk(tt.transpose(2, 0, 4, 1, 3).reshape(g, L * gc, L * gc))
    tmat = jnp.einsum("gxr,bgrc,gyc->bxy", place_x, t_grp, place_x)
    rev = L - 1 - jnp.arange(L)
    pin_re = pw_re[rev][:, :, :, None] * bb_re[None] - pw_im[rev][:, :, :, None] * bb_im[None]
    pin_im = pw_re[rev][:, :, :, None] * bb_im[None] + pw_im[rev][:, :, :, None] * bb_re[None]
    to_p = lambda a: jnp.einsum("gxr,bgrp,gqp->bxq", place_x,
                                blk(a.transpose(1, 0, 3, 2).reshape(g, L * gc, p)), place_q)
    to_q = lambda a: jnp.einsum("gqp,bgpc,gyc->bqy", place_q,
                                blk(a.transpose(1, 3, 0, 2).reshape(g, p, L * gc)), place_x)
    pmat = jnp.stack([to_p(pin_re), to_p(pin_im)], axis=1)
    qmat = jnp.stack([to_q(ca_re[1:]), -to_q(ca_im[1:])], axis=1)
    rep8 = lambda a: jnp.broadcast_to(a[None], (SUBLANES,) + a.shape)
    apl_re, apl_im = lax.associative_scan(cmul, (rep8(pw_re[L]), rep8(pw_im[L])), axis=0)
    apl = jnp.stack([apl_re, apl_im]).reshape(2, SUBLANES, nb, gb * p)
    return tmat.astype(BF16), pmat.astype(BF16), qmat.astype(BF16), apl


def _ssm_apply(p, uc_col, b, t, h0_re, h0_im, weights, one_seq_per_step):
    tmat, pmat, qmat, apl = weights
    nb, w = pmat.shape[0], pmat.shape[3]
    L = SSM_L
    nc = t // L
    assert t % L == 0 and (nc % SUBLANES == 0 or SUBLANES % nc == 0)
    seg = min(nc, SUBLANES)
    if one_seq_per_step:
        assert nc % SUBLANES == 0
        rows_blk, tiles_per_seq, reps = nc, nc // SUBLANES, SUBLANES
    else:
        rows_blk, tiles_per_seq, reps = b * nc, 1, nc
        assert nc <= SUBLANES and rows_blk % SUBLANES == 0
    pw_rows = [k % seg for k in range(SUBLANES)] + [0, 1, 3] + [0] * (SUBLANES - 3)
    apow = jnp.stack([apl[:, k] for k in pw_rows], axis=1).transpose(2, 0, 1, 3)

    def pack(h):
        h = h.astype(F32).reshape(b, nb, w).transpose(1, 0, 2)
        return jnp.repeat(h, reps, axis=1)

    h0 = jnp.stack([pack(h0_re), pack(h0_im)], axis=1)
    p3 = p.reshape(b * nc, L, p.shape[1])
    y, hre, him = _ssm(p3, uc_col, tmat, pmat, qmat, apow, h0, rows_blk, seg, tiles_per_seq)

    def last(h):
        return h.reshape(nb, b, nc, w)[:, :, nc - 1].transpose(1, 0, 2).reshape(b, nb * SSM_GB, C_STATE)

    return y.reshape(b * t, nb * LANES), last(hre), last(him)


def _glu_kernel(y_ref, w_ref, b_ref, o_ref):
    z = _gelu(y_ref[...])
    a = jnp.dot(z.astype(BF16), w_ref[...], preferred_element_type=F32) + b_ref[...]
    o_ref[...] = (z * _sigmoid(a)).astype(o_ref.dtype)


def _glu(y, w, b, tm):
    t, n = y.shape
    return pl.pallas_call(
        _glu_kernel,
        grid=(t // tm,),
        in_specs=[pl.BlockSpec((tm, n), lambda i: (i, 0)),
                  pl.BlockSpec((n, n), lambda i: (0, 0)),
                  pl.BlockSpec((1, n), lambda i: (0, 0))],
        out_specs=pl.BlockSpec((tm, n), lambda i: (i, 0)),
        out_shape=jax.ShapeDtypeStruct((t, n), BF16),
        compiler_params=_cparams(("arbitrary",)),
        name="glu",
    )(y, w, b.reshape(1, n))


def _merge_kernel(oa_ref, ob_ref, oc_ref, ga_ref, gb_ref, gc_ref, w_ref, o_ref):
    acc = None
    for n, (o, g) in enumerate(((oa_ref, ga_ref), (ob_ref, gb_ref), (oc_ref, gc_ref))):
        br = jnp.dot(o[...], w_ref[n], preferred_element_type=F32)
        t = _sigmoid(g[...]) * br
        acc = t if acc is None else acc + t
    o_ref[...] = acc.astype(o_ref.dtype)


def _merge(oa, ob, oc, p, w, g_col, tm, tn):
    t, m = oa.shape
    d = w.shape[2]
    assert g_col % tn == 0 and d % tn == 0

    def gspec(n):
        return pl.BlockSpec((tm, tn), lambda i, j: (i, (g_col + n * d) // tn + j))

    ospec = pl.BlockSpec((tm, m), lambda i, j: (i, 0))
    return pl.pallas_call(
        _merge_kernel,
        grid=(t // tm, d // tn),
        in_specs=[ospec, ospec, ospec, gspec(0), gspec(1), gspec(2),
                  pl.BlockSpec((N_BRANCH, m, tn), lambda i, j: (0, 0, j))],
        out_specs=pl.BlockSpec((tm, tn), lambda i, j: (i, j)),
        out_shape=jax.ShapeDtypeStruct((t, d), BF16),
        compiler_params=_cparams(("arbitrary", "arbitrary")),
        name="merge",
    )(oa, ob, oc, p, p, p, w)


def _mmres_kernel(a_ref, w_ref, r_ref, o_ref):
    o_ref[...] = r_ref[...] + jnp.dot(a_ref[...], w_ref[...], preferred_element_type=F32)


def _mmres(a, w, res, tm, tn):
    t, k = a.shape
    n = w.shape[1]
    return pl.pallas_call(
        _mmres_kernel,
        grid=(t // tm, n // tn),
        in_specs=[pl.BlockSpec((tm, k), lambda i, j: (i, 0)),
                  pl.BlockSpec((k, tn), lambda i, j: (0, j)),
                  pl.BlockSpec((tm, tn), lambda i, j: (i, j))],
        out_specs=pl.BlockSpec((tm, tn), lambda i, j: (i, j)),
        out_shape=jax.ShapeDtypeStruct((t, n), F32),
        compiler_params=_cparams(("arbitrary", "arbitrary")),
        name="mmres",
    )(a, w, res)


def _ffn_in_kernel(x_ref, g_ref, wu_ref, wv_ref, wc_ref, bc_ref, st_ref, o_ref, tail_ref,
                   h_ref, up_ref, carry_ref, *, ns, ts, tiles_per_seq):
    i, j = pl.program_id(0), pl.program_id(1)
    pad = SUBLANES

    @pl.when(j == 0)
    def _():
        h_ref[...] = _rms(x_ref[...], g_ref[...]).astype(BF16)

    @pl.when((i == 0) & (j == 0))
    def _():
        carry_ref[...] = jnp.zeros(carry_ref.shape, F32)

    h = h_ref[...]
    u = jnp.dot(h, wu_ref[...], preferred_element_type=F32)
    tn = u.shape[1]
    up_ref[:, pad:, :] = u.reshape(ns, ts, tn)
    first = (i % tiles_per_seq) == 0
    up_ref[:, pad - 2:pad, :] = jnp.where(first, st_ref[...], carry_ref[j])
    last2 = up_ref[:, ts + pad - 2:ts + pad, :]
    carry_ref[j] = last2
    tail_ref[0] = last2
    c = bc_ref[...].reshape(1, 1, tn)
    for tap in range(CONV_W):
        c = c + up_ref[:, pad - 2 + tap:pad - 2 + tap + ts, :] * wc_ref[tap:tap + 1, :].reshape(1, 1, tn)
    gl = _gelu(c).reshape(ns * ts, tn)
    v = jnp.dot(h, wv_ref[...], preferred_element_type=F32)
    o_ref[...] = (gl * v).astype(o_ref.dtype)


def _ffn_in(x, g, w, wc, bc, state, ns, ts, tn):
    t, d = x.shape
    f = w.shape[1] // 2
    tm = ns * ts
    nj = f // tn
    seq_len = t // state.shape[0]
    tiles_per_seq = max(seq_len // tm, 1)
    kern = functools.partial(_ffn_in_kernel, ns=ns, ts=ts, tiles_per_seq=tiles_per_seq)
    return pl.pallas_call(
        kern,
        grid=(t // tm, nj),
        in_specs=[pl.BlockSpec((tm, d), lambda i, j: (i, 0)),
                  pl.BlockSpec((1, d), lambda i, j: (0, 0)),
                  pl.BlockSpec((d, tn), lambda i, j: (0, j)),
                  pl.BlockSpec((d, tn), lambda i, j: (0, nj + j)),
                  pl.BlockSpec((CONV_W, tn), lambda i, j: (0, j)),
                  pl.BlockSpec((1, tn), lambda i, j: (0, j)),
                  pl.BlockSpec((ns, CONV_W - 1, tn), lambda i, j: (i // tiles_per_seq, 0, j))],
        out_specs=[pl.BlockSpec((tm, tn), lambda i, j: (i, j)),
                   pl.BlockSpec((1, ns, CONV_W - 1, tn), lambda i, j: (i, 0, 0, j))],
        out_shape=[jax.ShapeDtypeStruct((t, f), BF16),
                   jax.ShapeDtypeStruct((t // tm, ns, CONV_W - 1, f), F32)],
        scratch_shapes=[pltpu.VMEM((tm, d), BF16),
                        pltpu.VMEM((ns, ts + SUBLANES, tn), F32),
                        pltpu.VMEM((nj, ns, CONV_W - 1, tn), F32)],
        compiler_params=_cparams(("arbitrary", "arbitrary")),
        name="ffn_in",
    )(x, g.reshape(1, d), w, w, wc, bc.reshape(1, f), state)


def _norm_kernel(x_ref, g_ref, o_ref):
    o_ref[...] = _rms(x_ref[...], g_ref[...])


def _norm(x, g, tm):
    t, d = x.shape
    return pl.pallas_call(
        _norm_kernel,
        grid=(t // tm,),
        in_specs=[pl.BlockSpec((tm, d), lambda i: (i, 0)), pl.BlockSpec((1, d), lambda i: (0, 0))],
        out_specs=pl.BlockSpec((tm, d), lambda i: (i, 0)),
        out_shape=jax.ShapeDtypeStruct((t, d), F32),
        compiler_params=_cparams(("arbitrary",)),
        name="final_norm",
    )(x, g.reshape(1, d))


def _row_tile(t, cap=1024):
    tm = min(t, cap)
    while t % tm:
        tm //= 2
    return tm


def _pack_w_in(w_in, d_model):
    mix = d_model // 2
    sizes = dict(qa=mix, ka=A_KV * HEAD_DIM, va=A_KV * HEAD_DIM, qi=IDX_HEADS * IDX_DIM, ki=IDX_DIM,
                 wi=IDX_HEADS, qb=mix, kb=mix, vb=mix, uc=mix, g=N_BRANCH * d_model)
    src_order = ["qa", "ka", "va", "qi", "ki", "wi", "qb", "kb", "vb", "uc", "g"]
    src, off = {}, 0
    for name in src_order:
        src[name] = (off, sizes[name])
        off += sizes[name]
    assert off == w_in.shape[1]
    dst_order = ["qa", "qb", "kb", "vb", "uc", "g", "qi", "ka", "va", "ki", "wi"]
    cols, parts, off = {}, [], 0
    for name in dst_order:
        s, n = src[name]
        cols[name] = off
        parts.append(w_in[:, s:s + n])
        off += n
    cols["kw"] = cols["ki"]
    tn = 1280
    total = -(-off // tn) * tn
    parts.append(jnp.zeros((w_in.shape[0], total - off), w_in.dtype))
    return jnp.concatenate(parts, axis=1).astype(BF16), cols, tn


def _to_heads_T(x, b, t):
    return x.reshape(b, t, x.shape[1]).transpose(0, 2, 1)


def _pad_axis(x, axis, size):
    pad = [(0, 0)] * x.ndim
    pad[axis] = (0, size - x.shape[axis])
    return jnp.pad(x, pad)


def _layer(x, pos, prm, cache, layer, is_prompt):
    b, t, d = x.shape
    mix = d // 2
    xt = x.reshape(b * t, d)
    tm = _row_tile(b * t)
    cols = prm["cols"]
    p = _proj(xt, prm["norm1"], prm["w_in"], tm, prm["w_in_tn"])

    tabs = _rope_tables(pos)
    tm_r = _row_tile(t) if is_prompt else tm
    if not is_prompt:
        tabs = jnp.tile(tabs, (1, b, 1))
    qa, ka, kab, vab, qi, kw, kib = _rope(p, tabs, cols, tm_r)
    if is_prompt:
        sq, s_valid, q_pos0 = t, t, 0
        s_pad = -(-s_valid // DSA_TK) * DSA_TK
        k_all = _pad_axis(kab.reshape(b, t, -1), 1, s_pad)
        ki_all = _pad_axis(kib.reshape(b, t, LANES), 1, s_pad)
        vT = _pad_axis(vab.reshape(b, t, -1), 1, s_pad).transpose(0, 2, 1).reshape(b, A_KV, HEAD_DIM, s_pad)
        ones = jnp.broadcast_to((jnp.arange(DSA_VROWS - HEAD_DIM) == 0).astype(BF16)[None, None, :, None],
                                (b, A_KV, DSA_VROWS - HEAD_DIM, s_pad))
        vT = jnp.concatenate([vT, ones], axis=2).reshape(b, A_KV * DSA_VROWS, s_pad)
    else:
        past = cache["a_k"].shape[2]
        sq, s_valid, q_pos0 = DSA_QB, past + t, past
        k_all, vT, ki_all = _cache_prep(cache["a_k"], cache["a_v"], cache["a_kidx"], layer, kab, vab, kib, b, t)
    qaT = _pad_axis(_to_heads_T(qa, b, t), 2, sq)
    qiT = _pad_axis(_to_heads_T(qi, b, t), 2, sq)
    wiT = _pad_axis(_to_heads_T(kw[:, IDX_DIM:IDX_DIM + IDX_HEADS], b, t), 2, sq)
    topk = min(TOPK_MAX, s_valid // 4)
    oaT = _dsa(qaT, qiT, wiT, k_all, vT, ki_all, s_valid=s_valid, q_pos0=q_pos0, topk=topk)
    oa = oaT[:, :, :t].transpose(0, 2, 1).reshape(b * t, mix)

    nh = mix // HEAD_DIM
    if is_prompt:
        n_tiles = BAND_CHUNKS * CHUNK // BAND_QB + 1
        bias = _band_bias(prm["rel_bias"], BAND_CHUNKS * CHUNK, BAND_QB, 0, n_tiles * BAND_QB)
        ob = _band_prompt(p, bias, cols, b, t)
    else:
        nbc = cache["b_k"].shape[2]
        past = cache["a_k"].shape[2]
        bias = _band_bias(prm["rel_bias"], past, t, past - nbc, nbc + t)
        ob = _band_sample(p, cache["b_k"], cache["b_v"], layer, bias, cols, b, t)

    g_ssm = mix // C_GROUP
    if is_prompt:
        h0_re = jnp.zeros((b, g_ssm, C_STATE), F32)
        h0_im = h0_re
    else:
        h0_re, h0_im = cache["c_re"][layer], cache["c_im"][layer]
    yc, hr, hi = _ssm_apply(p, cols["uc"], b, t, h0_re, h0_im, prm["ssm"], is_prompt)
    oc = _glu(yc, prm["w_glu"], prm["b_glu"], tm)

    merged = _merge(oa, ob, oc, p, prm["w_branch"], cols["g"], tm, 512)
    x1 = _mmres(merged, prm["w_out"], xt, tm, 1024)

    f = prm["w_conv"].shape[1]
    if is_prompt:
        state = jnp.zeros((b, CONV_W - 1, f), F32)
        ns, ts = 1, _row_tile(t)
    else:
        state = cache["ffn_conv"][layer].astype(F32)
        ns, ts = b, t
    act, tails = _ffn_in(x1, prm["norm2"], prm["w_ffn_in"], prm["w_conv"], prm["b_conv"], state, ns, ts, 512)
    x2 = _mmres(act, prm["w_down"], x1, tm, 512)
    if is_prompt:
        buf = tails.reshape(b, t // ts, CONV_W - 1, f)[:, -1]
    else:
        buf = tails[0]

    ka4 = ka.reshape(b, t, A_KV, HEAD_DIM)
    va4 = p[:, cols["va"]:cols["va"] + A_KV * HEAD_DIM].reshape(b, t, A_KV, HEAD_DIM)
    ki3 = kw[:, :IDX_DIM].reshape(b, t, IDX_DIM)
    nb = min(BAND_CHUNKS * CHUNK, t) if is_prompt else t
    pb = p.reshape(b, t, p.shape[1])[:, t - nb:]
    kb4 = pb[:, :, cols["kb"]:cols["kb"] + mix].reshape(b, nb, nh, HEAD_DIM)
    vb4 = pb[:, :, cols["vb"]:cols["vb"] + mix].reshape(b, nb, nh, HEAD_DIM)
    return x2.reshape(b, t, d), (ka4, va4, ki3, kb4, vb4, hr, hi, buf)


def kernel(x_prompt, x_sample, cache_a_k, cache_a_v, cache_a_kidx, cache_b_k, cache_b_v, state_c_re, state_c_im,
           state_ffn_conv, norm1_g, w_in, rel_bias, ssm_a_re, ssm_a_im, ssm_log_dt, ssm_b_re, ssm_b_im, ssm_c_re,
           ssm_c_im, ssm_d, w_glu, b_glu, w_branch, w_out, norm2_g, w_ffn_in, w_ffn_conv, b_ffn_conv, w_ffn_down,
           normf_g):
    depth = w_in.shape[0]
    d = x_prompt.shape[2]
    pos_p = jnp.arange(x_prompt.shape[1])
    pos_s = cache_a_k.shape[2] + jnp.arange(x_sample.shape[1])
    xp, xs = x_prompt, x_sample
    st_p, st_s = [], []
    caches = dict(a_k=cache_a_k, a_v=cache_a_v, a_kidx=cache_a_kidx, b_k=cache_b_k, b_v=cache_b_v,
                  c_re=state_c_re, c_im=state_c_im, ffn_conv=state_ffn_conv)
    for l in range(depth):
        w_in_l, cols, tn = _pack_w_in(w_in[l], d)
        prm = dict(norm1=norm1_g[l], w_in=w_in_l, cols=cols, w_in_tn=tn, rel_bias=rel_bias[l],
                   ssm=_ssm_weights(ssm_a_re[l], ssm_a_im[l], ssm_log_dt[l], ssm_b_re[l], ssm_b_im[l],
                                    ssm_c_re[l], ssm_c_im[l], ssm_d[l]),
                   w_glu=w_glu[l].astype(BF16), b_glu=b_glu[l], w_branch=w_branch[l].astype(BF16),
                   w_out=w_out[l].astype(BF16), norm2=norm2_g[l], w_ffn_in=w_ffn_in[l].astype(BF16),
                   w_conv=w_ffn_conv[l], b_conv=b_ffn_conv[l], w_down=w_ffn_down[l].astype(BF16))
        xp, sp = _layer(xp, pos_p, prm, None, l, True)
        xs, ss = _layer(xs, pos_s, prm, caches, l, False)
        st_p.append(sp)
        st_s.append(ss)
    bp, tp, _ = xp.shape
    bs, tsq, _ = xs.shape
    y_prompt = _norm(xp.reshape(bp * tp, d), normf_g, _row_tile(bp * tp)).reshape(bp, tp, d)
    y_sample = _norm(xs.reshape(bs * tsq, d), normf_g, _row_tile(bs * tsq)).reshape(bs, tsq, d)
    outs_p = [jnp.stack([s[i] for s in st_p]) for i in range(8)]
    outs_s = [jnp.stack([s[i] for s in st_s]) for i in range(8)]
    return (y_prompt, y_sample, *outs_p, *outs_s)
```

```python
import functools
import math

import jax
import jax.numpy as jnp
import numpy as np
from jax import lax
from jax.experimental import pallas as pl
from jax.experimental.pallas import tpu as pltpu

F32 = jnp.float32
BF16 = jnp.bfloat16
I32 = jnp.int32

CHUNK = 64
EPS = 1e-6
ROPE_THETA = 500000.0
HEAD_DIM = 128
A_KV = 2
A_GROUP = 4
A_ROT = HEAD_DIM // 4
IDX_HEADS = 8
IDX_DIM = 64
IDX_ROT = IDX_DIM // 4
TOPK_MAX = 256
BAND_CHUNKS = 8
REL_CLIP = 256
C_GROUP = 16
C_STATE = 64
CONV_W = 3
N_BRANCH = 3
ATTN_SCALE = HEAD_DIM ** -0.5
IDX_SCALE = (IDX_DIM * IDX_HEADS) ** -0.5

LANES = 128
SUBLANES = 8
VMEM_LIMIT = 56 * 1024 * 1024

NEG_BIG = -1e30
KEY_NEG_INF = -2139095041
KEY_POS_INF = 2139095040
INT32_MIN = -2147483648

SSM_L = SUBLANES
SSM_GB = LANES // C_GROUP
DSA_TK = 512
DSA_TA = 512
DSA_VROWS = HEAD_DIM + 16
DSA_QB = 128
BAND_QB = 128


def _cparams(sem):
    return pltpu.CompilerParams(dimension_semantics=sem, vmem_limit_bytes=VMEM_LIMIT)


def _gelu(x):
    return 0.5 * x * (1.0 + jnp.tanh(math.sqrt(2.0 / math.pi) * (x + 0.044715 * (x * x * x))))


def _sigmoid(x):
    return 1.0 / (1.0 + jnp.exp(-x))


def _rms(x, g):
    ms = jnp.mean(x * x, axis=-1, keepdims=True)
    return (x * lax.rsqrt(ms + EPS)) * g


def _proj_kernel(x_ref, g_ref, w_ref, o_ref, og_ref, h_ref, *, n_main):
    j = pl.program_id(1)

    @pl.when(j == 0)
    def _():
        h_ref[...] = _rms(x_ref[...], g_ref[...]).astype(BF16)

    acc = jnp.dot(h_ref[...], w_ref[...], preferred_element_type=F32)

    @pl.when(j < n_main)
    def _():
        o_ref[...] = acc

    @pl.when(j >= n_main)
    def _():
        og_ref[...] = acc.astype(BF16)


def _proj(x, g, w, tm, tn, n_main):
    t, d = x.shape
    n = w.shape[1]
    nj = n // tn
    return pl.pallas_call(
        functools.partial(_proj_kernel, n_main=n_main),
        grid=(t // tm, nj),
        in_specs=[pl.BlockSpec((tm, d), lambda i, j: (i, 0)),
                  pl.BlockSpec((1, d), lambda i, j: (0, 0)),
                  pl.BlockSpec((d, tn), lambda i, j: (0, j))],
        out_specs=[pl.BlockSpec((tm, tn), lambda i, j: (i, jnp.minimum(j, n_main - 1))),
                   pl.BlockSpec((tm, tn), lambda i, j: (i, jnp.maximum(j - n_main, 0)))],
        out_shape=[jax.ShapeDtypeStruct((t, n_main * tn), F32),
                   jax.ShapeDtypeStruct((t, (nj - n_main) * tn), BF16)],
        scratch_shapes=[pltpu.VMEM((tm, d), BF16)],
        compiler_params=_cparams(("arbitrary", "arbitrary")),
        name="proj",
    )(x, g.reshape(1, d), w)


def _rot(x, c, sa, sb, half):
    n = x.shape[-1]
    return x * c + pltpu.roll(x, n - half, 1) * sa + pltpu.roll(x, half, 1) * sb


def _rope_kernel(qa_ref, ka_ref, va_ref, qi_ref, kw_ref, tab_ref,
                 qa_o, ka_o, kab_o, vab_o, qi_o, kw_o, kib_o):
    ca, saa, sba = tab_ref[0], tab_ref[1], tab_ref[2]
    ci, sai, sbi = tab_ref[3], tab_ref[4], tab_ref[5]
    ck, sak, sbk = tab_ref[6], tab_ref[7], tab_ref[8]
    for h in range(qa_ref.shape[1] // LANES):
        sl = slice(h * LANES, (h + 1) * LANES)
        qa_o[:, sl] = _rot(qa_ref[:, sl], ca, saa, sba, A_ROT // 2).astype(BF16)
    for h in range(ka_ref.shape[1] // LANES):
        sl = slice(h * LANES, (h + 1) * LANES)
        k = _rot(ka_ref[:, sl], ca, saa, sba, A_ROT // 2)
        ka_o[:, sl] = k
        kab_o[:, sl] = k.astype(BF16)
    vab_o[...] = va_ref[...].astype(BF16)
    for h in range(qi_ref.shape[1] // LANES):
        sl = slice(h * LANES, (h + 1) * LANES)
        qi_o[:, sl] = _rot(qi_ref[:, sl], ci, sai, sbi, IDX_ROT // 2).astype(BF16)
    kw = _rot(kw_ref[...], ck, sak, sbk, IDX_ROT // 2)
    kw_o[...] = kw
    kib_o[...] = kw.astype(BF16)


def _rope(p, tabs, cols, tm):
    t = p.shape[0]
    n_tab_blocks = tabs.shape[1] // tm
    d_qa, d_kv, d_qi = A_KV * A_GROUP * HEAD_DIM, A_KV * HEAD_DIM, IDX_HEADS * IDX_DIM

    def col(width, off):
        assert off % width == 0
        return pl.BlockSpec((tm, width), lambda i: (i, off // width))

    def out(width):
        return pl.BlockSpec((tm, width), lambda i: (i, 0))

    return pl.pallas_call(
        _rope_kernel,
        grid=(t // tm,),
        in_specs=[col(d_qa, cols["qa"]), col(d_kv, cols["ka"]), col(d_kv, cols["va"]),
                  col(d_qi, cols["qi"]), col(LANES, cols["kw"]),
                  pl.BlockSpec((9, tm, LANES), lambda i: (0, i % n_tab_blocks, 0))],
        out_specs=[out(d_qa), out(d_kv), out(d_kv), out(d_kv), out(d_qi), out(LANES), out(LANES)],
        out_shape=[jax.ShapeDtypeStruct((t, d_qa), BF16),
                   jax.ShapeDtypeStruct((t, d_kv), F32),
                   jax.ShapeDtypeStruct((t, d_kv), BF16),
                   jax.ShapeDtypeStruct((t, d_kv), BF16),
                   jax.ShapeDtypeStruct((t, d_qi), BF16),
                   jax.ShapeDtypeStruct((t, LANES), F32),
                   jax.ShapeDtypeStruct((t, LANES), BF16)],
        compiler_params=_cparams(("arbitrary",)),
        name="rope",
    )(p, p, p, p, p, tabs)


def _rope_tables(pos):
    pos = pos.astype(F32)[:, None]

    def cs(rot):
        half = rot // 2
        inv = jnp.float32(ROPE_THETA) ** (-jnp.arange(half, dtype=F32) / half)
        ang = pos * inv[None, :]
        return jnp.cos(ang), jnp.sin(ang)

    def tabs(rot, width):
        half = rot // 2
        c, s = cs(rot)
        n = pos.shape[0]
        one = jnp.ones((n, width - rot), F32)
        zero = jnp.zeros((n, width - rot), F32)
        zh = jnp.zeros((n, half), F32)
        return (jnp.concatenate([c, c, one], 1), jnp.concatenate([-s, zh, zero], 1),
                jnp.concatenate([zh, s, zero], 1))

    ca, saa, sba = tabs(A_ROT, HEAD_DIM)
    c64, sa64, sb64 = tabs(IDX_ROT, IDX_DIM)
    n = pos.shape[0]
    one64, zero64 = jnp.ones((n, IDX_DIM), F32), jnp.zeros((n, IDX_DIM), F32)
    return jnp.stack([ca, saa, sba,
                      jnp.concatenate([c64, c64], 1), jnp.concatenate([sa64, sa64], 1),
                      jnp.concatenate([sb64, sb64], 1),
                      jnp.concatenate([c64, one64], 1), jnp.concatenate([sa64, zero64], 1),
                      jnp.concatenate([sb64, zero64], 1)])


def _dsa_kernel(qaT_ref, qiT_ref, wiT_ref, k_ref, vT_ref, ki_ref, o_ref, key_ref, acc_ref,
                s0_ref, s1_ref, p0_ref, p1_ref, a0_ref, a1_ref,
                *, s_valid, q_pos0, topk):
    qb, tk = DSA_QB, DSA_TK
    j = pl.program_id(1)
    q_first = q_pos0 + j * qb
    qpos = q_first + lax.broadcasted_iota(I32, (1, qb), 1)
    assert CHUNK == 64 and tk == 512
    n_adm = jnp.minimum(((qpos >> 6) + 1) * CHUNK, s_valid)
    n_max = jnp.minimum((((q_first + qb - 1) >> 6) + 1) * CHUNK, s_valid)
    nkt = (n_max + tk - 1) >> 9

    qi = qiT_ref[0]
    rhs = jnp.concatenate([qi[h * IDX_DIM:(h + 1) * IDX_DIM, :] for h in range(IDX_HEADS)], axis=1)
    rhs = jnp.concatenate([rhs, jnp.zeros((LANES - IDX_DIM, IDX_HEADS * qb), BF16)], axis=0)
    wi = wiT_ref[0]

    def p1(kt, carry):
        off = pl.multiple_of(kt * tk, tk)
        kid = ki_ref[0, pl.ds(off, tk), :]
        s_all = jnp.dot(kid, rhs, preferred_element_type=F32)
        acc = wi[0:1, :] * jnp.maximum(s_all[:, 0:qb], 0.0)
        for h in range(1, IDX_HEADS):
            acc = acc + wi[h:h + 1, :] * jnp.maximum(s_all[:, h * qb:(h + 1) * qb], 0.0)
        score = acc * IDX_SCALE
        kpos = off + lax.broadcasted_iota(I32, (tk, 1), 0)
        score = jnp.where(kpos < n_adm, score, -jnp.inf)
        u = pltpu.bitcast(score, I32)
        key_ref[pl.ds(off, tk), :] = u ^ (lax.shift_right_arithmetic(u, 31) & 0x7FFFFFFF)
        return carry

    lax.fori_loop(0, nkt, p1, 0)

    pairs = key_ref.shape[0] % (2 * tk) == 0
    if pairs:
        @pl.when(nkt % 2 == 1)
        def _():
            key_ref[pl.ds(pl.multiple_of(nkt * tk, tk), tk), :] = jnp.full((tk, qb), INT32_MIN, I32)

    n_trips, per_trip = ((nkt + 1) // 2, 2) if pairs else (nkt, 1)

    def count_ge(cand):
        def body(kt, cnt):
            for half in range(per_trip):
                blk = key_ref[pl.ds(pl.multiple_of((kt * per_trip + half) * tk, tk), tk), :]
                m = jnp.where(blk >= cand, 1, 0).astype(I32)
                cnt = cnt + jnp.sum(m.reshape(tk // SUBLANES, SUBLANES, qb), axis=0)
            return cnt

        cnt8 = lax.fori_loop(0, n_trips, body, jnp.zeros((SUBLANES, qb), I32))
        return jnp.sum(cnt8, axis=0, keepdims=True)

    def bit_body(i, carry):
        prefix, n_ge = carry
        cand_u = prefix | lax.shift_left(jnp.int32(1), 31 - i)
        cnt = count_ge(cand_u ^ INT32_MIN)
        ok = cnt >= topk
        return jnp.where(ok, cand_u, prefix), jnp.where(ok, cnt, n_ge)

    prefix, n_ge = lax.fori_loop(0, 32, bit_body, (jnp.zeros((1, qb), I32), jnp.broadcast_to(nkt * tk, (1, qb))))
    thr = prefix ^ INT32_MIN
    cut_ties = jnp.max(jnp.where((n_ge > topk) & (thr > KEY_NEG_INF), 1, 0)) > 0

    @pl.when(cut_ties)
    def _():
        r = lax.broadcasted_iota(I32, (tk, tk), 0)
        c = lax.broadcasted_iota(I32, (tk, tk), 1)
        tri = jnp.where(c <= r, 1.0, 0.0).astype(BF16)
        take = (topk - count_ge(thr + 1)).astype(F32)

        def body(kt, seen):
            off = pl.multiple_of(kt * tk, tk)
            blk = key_ref[pl.ds(off, tk), :]
            eq = blk == thr
            incl = jnp.dot(tri, jnp.where(eq, 1.0, 0.0).astype(BF16), preferred_element_type=F32)
            drop = eq & (seen + incl > take)
            key_ref[pl.ds(off, tk), :] = jnp.where(drop, thr - 1, blk)
            return seen + incl[tk - 1:tk, :]

        lax.fori_loop(0, nkt, body, jnp.zeros((1, qb), F32))

    thr_lo = jnp.maximum(thr, KEY_NEG_INF + 1)
    qa = qaT_ref[0]
    ng = A_GROUP * qb
    qg = [jnp.concatenate([qa[(g * A_GROUP + hh) * HEAD_DIM:(g * A_GROUP + hh + 1) * HEAD_DIM, :]
                           for hh in range(A_GROUP)], axis=1) for g in range(A_KV)]
    acc_ref[...] = jnp.zeros(acc_ref.shape, F32)

    c_exp = ATTN_SCALE * math.log2(math.e)

    ta = DSA_TA
    n_steps = nkt * (tk // ta)
    last = n_steps - 1
    bufs = ((s0_ref, p0_ref, a0_ref), (s1_ref, p1_ref, a1_ref))

    def logits(step, s_ref):
        off = pl.multiple_of(jnp.minimum(step, last) * ta, ta)
        for g in range(A_KV):
            kg = k_ref[0, pl.ds(off, ta), g * HEAD_DIM:(g + 1) * HEAD_DIM]
            s_ref[g] = jnp.dot(kg, qg[g], preferred_element_type=F32)

    def numerators(step, s_ref, p_ref, a_ref, ms):
        off = pl.multiple_of(jnp.minimum(step, last) * ta, ta)
        blk = key_ref[pl.ds(off, ta), :]
        sel = (blk >= thr_lo) & (blk < KEY_POS_INF) & (step <= last)
        mask = jnp.concatenate([jnp.where(sel, 0.0, NEG_BIG)] * A_GROUP, axis=1)
        new = []
        for g in range(A_KV):
            s = s_ref[g] + mask
            m_new = jnp.maximum(ms[g], jnp.max(s, axis=0, keepdims=True))
            a_ref[g] = jnp.broadcast_to(jnp.exp2((ms[g] - m_new) * c_exp), (SUBLANES, ng))
            p_ref[g] = jnp.exp2((s - m_new).astype(BF16) * c_exp)
            new.append(m_new)
        return tuple(new)

    def weighted_values(step, p_ref, a_ref):
        off = pl.multiple_of(jnp.clip(step, 0, last) * ta, ta)
        for g in range(A_KV):
            vg = vT_ref[0, g * DSA_VROWS:(g + 1) * DSA_VROWS, pl.ds(off, ta)]
            acc_ref[g] = a_ref[g, 0:1, :] * acc_ref[g] + jnp.dot(vg, p_ref[g], preferred_element_type=F32)

    p1_ref[...] = jnp.zeros(p1_ref.shape, BF16)
    a1_ref[...] = jnp.ones(a1_ref.shape, F32)
    logits(0, s0_ref)

    def two_steps(i2, ms):
        i = 2 * i2
        for par in range(2):
            cur, nxt = bufs[par], bufs[1 - par]
            logits(i + par + 1, nxt[0])
            ms = numerators(i + par, cur[0], cur[1], cur[2], ms)
            weighted_values(i + par - 1, nxt[1], nxt[2])
        return ms

    init = (jnp.full((1, ng), NEG_BIG, F32),) * A_KV
    n_pairs = (n_steps + 1) // 2
    lax.fori_loop(0, n_pairs, two_steps, init)
    weighted_values(2 * n_pairs - 1, p1_ref, a1_ref)
    for g in range(A_KV):
        o = acc_ref[g, 0:HEAD_DIM, :] / acc_ref[g, HEAD_DIM:HEAD_DIM + 1, :]
        for hh in range(A_GROUP):
            h = g * A_GROUP + hh
            o_ref[0, h * HEAD_DIM:(h + 1) * HEAD_DIM, :] = o[:, hh * qb:(hh + 1) * qb].astype(o_ref.dtype)


def _dsa(qaT, qiT, wiT, k, vT, ki, *, s_valid, q_pos0, topk):
    b, dq, sq = qaT.shape
    s_pad = k.shape[1]
    assert s_pad % DSA_TK == 0 and s_pad % DSA_TA == 0 and sq % DSA_QB == 0 and topk <= DSA_TK
    kern = functools.partial(_dsa_kernel, s_valid=s_valid, q_pos0=q_pos0, topk=topk)
    return pl.pallas_call(
        kern,
        grid=(b, sq // DSA_QB),
        in_specs=[pl.BlockSpec((1, dq, DSA_QB), lambda i, j: (i, 0, j)),
                  pl.BlockSpec((1, qiT.shape[1], DSA_QB), lambda i, j: (i, 0, j)),
                  pl.BlockSpec((1, IDX_HEADS, DSA_QB), lambda i, j: (i, 0, j)),
                  pl.BlockSpec((1, s_pad, k.shape[2]), lambda i, j: (i, 0, 0)),
                  pl.BlockSpec((1, vT.shape[1], s_pad), lambda i, j: (i, 0, 0)),
                  pl.BlockSpec((1, s_pad, LANES), lambda i, j: (i, 0, 0))],
        out_specs=pl.BlockSpec((1, dq, DSA_QB), lambda i, j: (i, 0, j)),
        out_shape=jax.ShapeDtypeStruct((b, dq, sq), BF16),
        scratch_shapes=[pltpu.VMEM((s_pad, DSA_QB), I32),
                        pltpu.VMEM((A_KV, DSA_VROWS, A_GROUP * DSA_QB), F32)]
                       + [pltpu.VMEM((A_KV, DSA_TA, A_GROUP * DSA_QB), F32)] * 2
                       + [pltpu.VMEM((A_KV, DSA_TA, A_GROUP * DSA_QB), BF16)] * 2
                       + [pltpu.VMEM((A_KV, SUBLANES, A_GROUP * DSA_QB), F32)] * 2,
        compiler_params=_cparams(("arbitrary", "arbitrary")),
        name="dsa",
    )(qaT, qiT, wiT, k, vT, ki)


def _cache_prep_kernel(ck_ref, cv_ref, cki_ref, kn_ref, vn_ref, kin_ref, k_o, vT_o, ki_o, *, n_cache_tiles):
    kt = pl.program_id(1)
    tk = k_o.shape[1]
    extra = DSA_VROWS - HEAD_DIM
    ones_rows = jnp.where(lax.broadcasted_iota(I32, (extra, tk), 0) == 0, 1.0, 0.0).astype(BF16)

    def emit(k_head, v_head, ki):
        for g in range(A_KV):
            k_o[0, :, g * HEAD_DIM:(g + 1) * HEAD_DIM] = k_head(g).astype(BF16)
            vT_o[0, g * DSA_VROWS:g * DSA_VROWS + HEAD_DIM, :] = v_head(g).astype(F32).T.astype(BF16)
            vT_o[0, g * DSA_VROWS + HEAD_DIM:(g + 1) * DSA_VROWS, :] = ones_rows
        ki_o[0] = ki

    @pl.when(kt < n_cache_tiles)
    def _():
        ki = jnp.concatenate([cki_ref[0, 0], jnp.zeros((tk, LANES - IDX_DIM), F32)], axis=1)
        emit(lambda g: ck_ref[0, 0, :, g, :], lambda g: cv_ref[0, 0, :, g, :], ki.astype(BF16))

    @pl.when(kt >= n_cache_tiles)
    def _():
        t = kn_ref.shape[0]
        rows = lambda a: jnp.concatenate([a, jnp.zeros((tk - t, a.shape[1]), a.dtype)], axis=0)
        kn, vn = rows(kn_ref[...]), rows(vn_ref[...])
        head = lambda a: lambda g: a[:, g * HEAD_DIM:(g + 1) * HEAD_DIM]
        emit(head(kn), head(vn), rows(kin_ref[...]))


def _cache_prep(cache_k, cache_v, cache_ki, layer, kn, vn, kin, b, t):
    tk = DSA_TK
    past = cache_k.shape[2]
    assert past % tk == 0 and t <= tk
    nct = past // tk
    s_pad = past + tk
    cidx = lambda i, j: jnp.minimum(j, nct - 1)
    return pl.pallas_call(
        functools.partial(_cache_prep_kernel, n_cache_tiles=nct),
        grid=(b, nct + 1),
        in_specs=[pl.BlockSpec((1, 1, tk, A_KV, HEAD_DIM), lambda i, j: (layer, i, cidx(i, j), 0, 0)),
                  pl.BlockSpec((1, 1, tk, A_KV, HEAD_DIM), lambda i, j: (layer, i, cidx(i, j), 0, 0)),
                  pl.BlockSpec((1, 1, tk, IDX_DIM), lambda i, j: (layer, i, cidx(i, j), 0)),
                  pl.BlockSpec((t, A_KV * HEAD_DIM), lambda i, j: (i, 0)),
                  pl.BlockSpec((t, A_KV * HEAD_DIM), lambda i, j: (i, 0)),
                  pl.BlockSpec((t, LANES), lambda i, j: (i, 0))],
        out_specs=[pl.BlockSpec((1, tk, A_KV * HEAD_DIM), lambda i, j: (i, j, 0)),
                   pl.BlockSpec((1, A_KV * DSA_VROWS, tk), lambda i, j: (i, 0, j)),
                   pl.BlockSpec((1, tk, LANES), lambda i, j: (i, j, 0))],
        out_shape=[jax.ShapeDtypeStruct((b, s_pad, A_KV * HEAD_DIM), BF16),
                   jax.ShapeDtypeStruct((b, A_KV * DSA_VROWS, s_pad), BF16),
                   jax.ShapeDtypeStruct((b, s_pad, LANES), BF16)],
        compiler_params=_cparams(("arbitrary", "arbitrary")),
        name="cache_prep",
    )(cache_k, cache_v, cache_ki, kn, vn, kin)


def _band_heads(q, k_head, v_head, bias_ref, n_hidden, o_ref, s_ref):
    nh = q.shape[1] // HEAD_DIM
    tk = bias_ref.shape[2]
    hidden = None if n_hidden is None else lax.broadcasted_iota(I32, (1, tk), 1) < n_hidden
    for h in range(nh):
        sl = slice(h * HEAD_DIM, (h + 1) * HEAD_DIM)
        s = lax.dot_general(q[:, sl].astype(BF16), k_head(h), (((1,), (1,)), ((), ())),
                            preferred_element_type=F32) * ATTN_SCALE + bias_ref[h]
        if hidden is not None:
            s = jnp.where(hidden, NEG_BIG, s)
        s_ref[h] = s
    ms = [jnp.max(s_ref[h], axis=1, keepdims=True) for h in range(nh)]
    ls = []
    for h in range(nh):
        p = jnp.exp(s_ref[h] - ms[h])
        s_ref[h] = p
        ls.append(jnp.sum(p, axis=1, keepdims=True))
    for h in range(nh):
        sl = slice(h * HEAD_DIM, (h + 1) * HEAD_DIM)
        w = (s_ref[h] / ls[h]).astype(BF16)
        o_ref[:, sl] = jnp.dot(w, v_head(h), preferred_element_type=F32).astype(o_ref.dtype)


def _band_prompt_kernel(*refs, n_tiles):
    q_ref = refs[0]
    k_refs = refs[1:1 + n_tiles]
    v_refs = refs[1 + n_tiles:1 + 2 * n_tiles]
    bias_ref = refs[1 + 2 * n_tiles]
    o_ref = refs[2 + 2 * n_tiles]
    s_ref = refs[3 + 2 * n_tiles]
    j = pl.program_id(1)
    n_hidden = jnp.maximum(n_tiles - 1 - j, 0) * q_ref.shape[0]
    k_all = jnp.concatenate([r[...] for r in k_refs], axis=0).astype(BF16)
    v_all = jnp.concatenate([r[...] for r in v_refs], axis=0).astype(BF16)
    head = lambda a: lambda h: a[:, h * HEAD_DIM:(h + 1) * HEAD_DIM]
    _band_heads(q_ref[...], head(k_all), head(v_all), bias_ref, n_hidden, o_ref, s_ref)


def _band_prompt(p, bias, cols, batch, seq):
    qb = BAND_QB
    n_tiles = BAND_CHUNKS * CHUNK // qb + 1
    width = bias.shape[0] * HEAD_DIM
    nq = seq // qb
    cq, ck, cv = cols["qb"] // width, cols["kb"] // width, cols["vb"] // width
    assert cols["qb"] % width == 0 and cols["kb"] % width == 0 and cols["vb"] % width == 0

    def kv_spec(c, i):
        return pl.BlockSpec((qb, width), lambda b, j: (b * nq + jnp.maximum(j - (n_tiles - 1) + i, 0), c))

    return pl.pallas_call(
        functools.partial(_band_prompt_kernel, n_tiles=n_tiles),
        grid=(batch, nq),
        in_specs=([pl.BlockSpec((qb, width), lambda b, j: (b * nq + j, cq))]
                  + [kv_spec(ck, i) for i in range(n_tiles)]
                  + [kv_spec(cv, i) for i in range(n_tiles)]
                  + [pl.BlockSpec(bias.shape, lambda b, j: (0, 0, 0))]),
        out_specs=pl.BlockSpec((qb, width), lambda b, j: (b * nq + j, 0)),
        out_shape=jax.ShapeDtypeStruct((batch * seq, width), BF16),
        compiler_params=_cparams(("arbitrary", "arbitrary")),
        scratch_shapes=[pltpu.VMEM(bias.shape, F32)],
        name="band_prompt",
    )(*([p] * (1 + 2 * n_tiles)), bias)


def _band_sample_kernel(q_ref, k_ref, v_ref, ck_ref, cv_ref, bias_ref, o_ref, s_ref):
    def head(cache_ref, new_ref):
        return lambda h: jnp.concatenate(
            [cache_ref[0, 0, :, h, :], new_ref[:, h * HEAD_DIM:(h + 1) * HEAD_DIM]], axis=0).astype(BF16)

    _band_heads(q_ref[...], head(ck_ref, k_ref), head(cv_ref, v_ref), bias_ref, None, o_ref, s_ref)


def _band_sample(p, cache_k, cache_v, layer, bias, cols, batch, t):
    nh = bias.shape[0]
    width = nh * HEAD_DIM
    nbc = cache_k.shape[2]
    cq, ck, cv = cols["qb"] // width, cols["kb"] // width, cols["vb"] // width
    return pl.pallas_call(
        _band_sample_kernel,
        grid=(batch,),
        in_specs=[pl.BlockSpec((t, width), lambda b: (b, cq)),
                  pl.BlockSpec((t, width), lambda b: (b, ck)),
                  pl.BlockSpec((t, width), lambda b: (b, cv)),
                  pl.BlockSpec((1, 1, nbc, nh, HEAD_DIM), lambda b: (layer, b, 0, 0, 0)),
                  pl.BlockSpec((1, 1, nbc, nh, HEAD_DIM), lambda b: (layer, b, 0, 0, 0)),
                  pl.BlockSpec(bias.shape, lambda b: (0, 0, 0))],
        out_specs=pl.BlockSpec((t, width), lambda b: (b, 0)),
        out_shape=jax.ShapeDtypeStruct((batch * t, width), BF16),
        compiler_params=_cparams(("arbitrary",)),
        scratch_shapes=[pltpu.VMEM(bias.shape, F32)],
        name="band_sample",
    )(p, p, p, cache_k, cache_v, bias)


def _band_bias(table, q0, nq, k0, nk):
    n = nq + nk
    j = jnp.arange(n)
    g = table[:, jnp.clip(j + (q0 - k0 - nk + 1), -REL_CLIP, REL_CLIP) + REL_CLIP].astype(F32)
    hank = jnp.tile(g, (1, nq + 1))[:, :nq * (n + 1)].reshape(-1, nq, n + 1)[:, :, :nk]
    bias = hank[:, :, ::-1]
    qc = (q0 + jnp.arange(nq))[:, None] // CHUNK
    kc = (k0 + jnp.arange(nk))[None, :] // CHUNK
    vis = (kc <= qc) & (kc >= qc - BAND_CHUNKS)
    return jnp.where(vis[None], bias, NEG_BIG)


def _ssm_kernel(u_ref, t_ref, p_ref, q_ref, a_ref, h0_ref, y_ref, hre_o, him_o, sre_ref, sim_ref,
                *, seg, tiles_per_seq):
    L = SSM_L
    x = jnp.concatenate([u_ref[:, s, :] for s in range(L)], axis=1).astype(BF16)
    sre_ref[...] = jnp.dot(x, p_ref[0, 0], preferred_element_type=F32)
    sim_ref[...] = jnp.dot(x, p_ref[0, 1], preferred_element_type=F32)
    rows, w = sre_ref.shape
    kseg = lax.broadcasted_iota(I32, (SUBLANES, 1), 0) % seg
    apk_re, apk_im = a_ref[0, 0, 0:SUBLANES, :], a_ref[0, 1, 0:SUBLANES, :]

    def cmad(x_re, x_im, a_re, a_im, y_re, y_im):
        return x_re + a_re * y_re - a_im * y_im, x_im + a_re * y_im + a_im * y_re

    def tile(i, carry):
        r = pl.ds(pl.multiple_of(i * SUBLANES, SUBLANES), SUBLANES)
        x_re, x_im = sre_ref[r, :], sim_ref[r, :]
        for n, d in enumerate((1, 2, 4)):
            if d < seg:
                ad_re = a_ref[0, 0, SUBLANES + n:SUBLANES + n + 1, :]
                ad_im = a_ref[0, 1, SUBLANES + n:SUBLANES + n + 1, :]
                sh_re = jnp.where(kseg >= d, pltpu.roll(x_re, d, 0), 0.0)
                sh_im = jnp.where(kseg >= d, pltpu.roll(x_im, d, 0), 0.0)
                x_re, x_im = cmad(x_re, x_im, ad_re, ad_im, sh_re, sh_im)
        if tiles_per_seq > 1:
            first = (i % tiles_per_seq) == 0
            hin_re = jnp.where(first, h0_ref[0, 0], carry[0])
            hin_im = jnp.where(first, h0_ref[0, 1], carry[1])
        else:
            hin_re, hin_im = h0_ref[0, 0, r, :], h0_ref[0, 1, r, :]
        inc_re, inc_im = cmad(x_re, x_im, apk_re, apk_im, hin_re, hin_im)
        sre_ref[r, :] = jnp.where(kseg == 0, hin_re, pltpu.roll(inc_re, 1, 0))
        sim_ref[r, :] = jnp.where(kseg == 0, hin_im, pltpu.roll(inc_im, 1, 0))
        hre_o[0, r, :] = inc_re
        him_o[0, r, :] = inc_im
        return inc_re[SUBLANES - 1:SUBLANES, :], inc_im[SUBLANES - 1:SUBLANES, :]

    zero = jnp.zeros((1, w), F32)
    lax.fori_loop(0, rows // SUBLANES, tile, (zero, zero))
    y = (jnp.dot(x, t_ref[0], preferred_element_type=F32)
         + jnp.dot(sre_ref[...].astype(BF16), q_ref[0, 0], preferred_element_type=F32)
         + jnp.dot(sim_ref[...].astype(BF16), q_ref[0, 1], preferred_element_type=F32))
    for t in range(L):
        y_ref[:, t, :] = y[:, t * LANES:(t + 1) * LANES]


def _ssm(p3, uc_col, tmat, pmat, qmat, apow, h0, rows_blk, seg, tiles_per_seq):
    rows_total = p3.shape[0]
    nb, _, _, w = pmat.shape
    kl = SSM_L * LANES
    h0_rows = h0.shape[2] // (rows_total // rows_blk)
    hspec = pl.BlockSpec((1, rows_blk, w), lambda i, r: (i, r, 0))
    return pl.pallas_call(
        functools.partial(_ssm_kernel, seg=seg, tiles_per_seq=tiles_per_seq),
        grid=(nb, rows_total // rows_blk),
        in_specs=[pl.BlockSpec((rows_blk, SSM_L, LANES), lambda i, r: (r, 0, uc_col // LANES + i)),
                  pl.BlockSpec((1, kl, kl), lambda i, r: (i, 0, 0)),
                  pl.BlockSpec((1, 2, kl, w), lambda i, r: (i, 0, 0, 0)),
                  pl.BlockSpec((1, 2, w, kl), lambda i, r: (i, 0, 0, 0)),
                  pl.BlockSpec((1, 2, 2 * SUBLANES, w), lambda i, r: (i, 0, 0, 0)),
                  pl.BlockSpec((1, 2, h0_rows, w), lambda i, r: (i, 0, r, 0))],
        out_specs=[pl.BlockSpec((rows_blk, SSM_L, LANES), lambda i, r: (r, 0, i)), hspec, hspec],
        out_shape=[jax.ShapeDtypeStruct((rows_total, SSM_L, nb * LANES), F32),
                   jax.ShapeDtypeStruct((nb, rows_total, w), F32),
                   jax.ShapeDtypeStruct((nb, rows_total, w), F32)],
        scratch_shapes=[pltpu.VMEM((rows_blk, w), F32), pltpu.VMEM((rows_blk, w), F32)],
        compiler_params=_cparams(("arbitrary", "arbitrary")),
        name="ssm",
    )(p3, tmat, pmat, qmat, apow, h0)


def _ssm_weights(a_re, a_im, log_dt, b_re, b_im, c_re, c_im, d):
    hp = lax.Precision.HIGHEST
    g, p = a_re.shape
    gc = b_re.shape[2]
    L = SSM_L
    a_re, a_im = a_re.astype(F32), a_im.astype(F32)
    dt = jnp.exp(log_dt.astype(F32))[:, None]
    mag = jnp.exp(dt * a_re)
    ab_re, ab_im = mag * jnp.cos(dt * a_im), mag * jnp.sin(dt * a_im)
    n_re = ab_re - 1.0
    n_im = ab_im
    den = a_re * a_re + a_im * a_im
    cc_re = (n_re * a_re + n_im * a_im) / den
    cc_im = (n_im * a_re - n_re * a_im) / den
    b_re, b_im = b_re.astype(F32), b_im.astype(F32)
    bb_re = cc_re[..., None] * b_re - cc_im[..., None] * b_im
    bb_im = cc_re[..., None] * b_im + cc_im[..., None] * b_re
    c_re, c_im = c_re.astype(F32), c_im.astype(F32)

    def cmul(x, y):
        return x[0] * y[0] - x[1] * y[1], x[0] * y[1] + x[1] * y[0]

    rep = lambda a: jnp.broadcast_to(a[None], (L,) + a.shape)
    pw_re, pw_im = lax.associative_scan(cmul, (rep(ab_re), rep(ab_im)), axis=0)
    pw_re = jnp.concatenate([jnp.ones((1, g, p), F32), pw_re], axis=0)
    pw_im = jnp.concatenate([jnp.zeros((1, g, p), F32), pw_im], axis=0)
    ca_re = c_re[None] * pw_re[:, :, None, :] - c_im[None] * pw_im[:, :, None, :]
    ca_im = c_re[None] * pw_im[:, :, None, :] + c_im[None] * pw_re[:, :, None, :]
    taps = (jnp.einsum("tgop,gpi->tgoi", ca_re[:L], bb_re, precision=hp)
            - jnp.einsum("tgop,gpi->tgoi", ca_im[:L], bb_im, precision=hp))
    taps = taps.at[0].add(jax.vmap(jnp.diag)(d.astype(F32)))
    s_idx = jnp.arange(L)[:, None]
    t_idx = jnp.arange(L)[None, :]
    lag = t_idx - s_idx
    tt = jnp.where((lag >= 0)[:, :, None, None, None], taps[jnp.clip(lag, 0, L - 1)], 0.0)
    gb = SSM_GB
    nb = g // gb
    n_x = L * gb * gc
    x = jnp.arange(n_x)
    x_grp, x_loc = (x // gc) % gb, (x // (gb * gc)) * gc + x % gc
    place_x = ((x_grp[None, :, None] == jnp.arange(gb)[:, None, None])
               & (x_loc[None, :, None] == jnp.arange(L * gc)[None, None, :])).astype(F32)
    q = jnp.arange(gb * p)
    place_q = ((q[None, :, None] // p == jnp.arange(gb)[:, None, None])
               & (q[None, :, None] % p == jnp.arange(p)[None, None, :])).astype(F32)
    blk = lambda a: a.reshape((nb, gb) + a.shape[1:])
    t_grp = blk(tt.transpose(2, 0, 4, 1, 3).reshape(g, L * gc, L * gc))
    tmat = jnp.einsum("gxr,bgrc,gyc->bxy", place_x, t_grp, place_x)
    rev = L - 1 - jnp.arange(L)
    pin_re = pw_re[rev][:, :, :, None] * bb_re[None] - pw_im[rev][:, :, :, None] * bb_im[None]
    pin_im = pw_re[rev][:, :, :, None] * bb_im[None] + pw_im[rev][:, :, :, None] * bb_re[None]
    to_p = lambda a: jnp.einsum("gxr,bgrp,gqp->bxq", place_x,
                                blk(a.transpose(1, 0, 3, 2).reshape(g, L * gc, p)), place_q)
    to_q = lambda a: jnp.einsum("gqp,bgpc,gyc->bqy", place_q,
                                blk(a.transpose(1, 3, 0, 2).reshape(g, p, L * gc)), place_x)
    pmat = jnp.stack([to_p(pin_re), to_p(pin_im)], axis=1)
    qmat = jnp.stack([to_q(ca_re[1:]), -to_q(ca_im[1:])], axis=1)
    rep8 = lambda a: jnp.broadcast_to(a[None], (SUBLANES,) + a.shape)
    apl_re, apl_im = lax.associative_scan(cmul, (rep8(pw_re[L]), rep8(pw_im[L])), axis=0)
    apl = jnp.stack([apl_re, apl_im]).reshape(2, SUBLANES, nb, gb * p)
    return tmat.astype(BF16), pmat.astype(BF16), qmat.astype(BF16), apl


def _ssm_apply(p, uc_col, b, t, h0_re, h0_im, weights, one_seq_per_step):
    tmat, pmat, qmat, apl = weights
    nb, w = pmat.shape[0], pmat.shape[3]
    L = SSM_L
    nc = t // L
    assert t % L == 0 and (nc % SUBLANES == 0 or SUBLANES % nc == 0)
    seg = min(nc, SUBLANES)
    if one_seq_per_step:
        assert nc % SUBLANES == 0
        rows_blk, tiles_per_seq, reps = nc, nc // SUBLANES, SUBLANES
    else:
        rows_blk, tiles_per_seq, reps = b * nc, 1, nc
        assert nc <= SUBLANES and rows_blk % SUBLANES == 0
    pw_rows = [k % seg for k in range(SUBLANES)] + [0, 1, 3] + [0] * (SUBLANES - 3)
    apow = jnp.stack([apl[:, k] for k in pw_rows], axis=1).transpose(2, 0, 1, 3)

    def pack(h):
        h = h.astype(F32).reshape(b, nb, w).transpose(1, 0, 2)
        return jnp.repeat(h, reps, axis=1)

    h0 = jnp.stack([pack(h0_re), pack(h0_im)], axis=1)
    p3 = p.reshape(b * nc, L, p.shape[1])
    y, hre, him = _ssm(p3, uc_col, tmat, pmat, qmat, apow, h0, rows_blk, seg, tiles_per_seq)

    def last(h):
        return h.reshape(nb, b, nc, w)[:, :, nc - 1].transpose(1, 0, 2).reshape(b, nb * SSM_GB, C_STATE)

    return y.reshape(b * t, nb * LANES), last(hre), last(him)


def _glu_kernel(y_ref, w_ref, b_ref, o_ref):
    z = _gelu(y_ref[...])
    a = jnp.dot(z.astype(BF16), w_ref[...], preferred_element_type=F32) + b_ref[...]
    o_ref[...] = (z * _sigmoid(a)).astype(o_ref.dtype)


def _glu(y, w, b, tm):
    t, n = y.shape
    return pl.pallas_call(
        _glu_kernel,
        grid=(t // tm,),
        in_specs=[pl.BlockSpec((tm, n), lambda i: (i, 0)),
                  pl.BlockSpec((n, n), lambda i: (0, 0)),
                  pl.BlockSpec((1, n), lambda i: (0, 0))],
        out_specs=pl.BlockSpec((tm, n), lambda i: (i, 0)),
        out_shape=jax.ShapeDtypeStruct((t, n), BF16),
        compiler_params=_cparams(("arbitrary",)),
        name="glu",
    )(y, w, b.reshape(1, n))


def _merge_kernel(oa_ref, ob_ref, oc_ref, ga_ref, gb_ref, gc_ref, w_ref, o_ref):
    acc = None
    for n, (o, g) in enumerate(((oa_ref, ga_ref), (ob_ref, gb_ref), (oc_ref, gc_ref))):
        br = jnp.dot(o[...], w_ref[n], preferred_element_type=F32)
        t = _sigmoid(g[...].astype(F32)) * br
        acc = t if acc is None else acc + t
    o_ref[...] = acc.astype(o_ref.dtype)


def _merge(oa, ob, oc, gates, w, tm, tn):
    t, m = oa.shape
    d = w.shape[2]
    assert d % tn == 0

    def gspec(n):
        return pl.BlockSpec((tm, tn), lambda i, j: (i, (n * d) // tn + j))

    ospec = pl.BlockSpec((tm, m), lambda i, j: (i, 0))
    return pl.pallas_call(
        _merge_kernel,
        grid=(t // tm, d // tn),
        in_specs=[ospec, ospec, ospec, gspec(0), gspec(1), gspec(2),
                  pl.BlockSpec((N_BRANCH, m, tn), lambda i, j: (0, 0, j))],
        out_specs=pl.BlockSpec((tm, tn), lambda i, j: (i, j)),
        out_shape=jax.ShapeDtypeStruct((t, d), BF16),
        compiler_params=_cparams(("arbitrary", "arbitrary")),
        name="merge",
    )(oa, ob, oc, gates, gates, gates, w)


def _mmres_kernel(a_ref, w_ref, r_ref, o_ref):
    o_ref[...] = r_ref[...] + jnp.dot(a_ref[...], w_ref[...], preferred_element_type=F32)


def _mmres(a, w, res, tm, tn):
    t, k = a.shape
    n = w.shape[1]
    return pl.pallas_call(
        _mmres_kernel,
        grid=(t // tm, n // tn),
        in_specs=[pl.BlockSpec((tm, k), lambda i, j: (i, 0)),
                  pl.BlockSpec((k, tn), lambda i, j: (0, j)),
                  pl.BlockSpec((tm, tn), lambda i, j: (i, j))],
        out_specs=pl.BlockSpec((tm, tn), lambda i, j: (i, j)),
        out_shape=jax.ShapeDtypeStruct((t, n), F32),
        compiler_params=_cparams(("arbitrary", "arbitrary")),
        name="mmres",
    )(a, w, res)


def _ffn_in_kernel(x_ref, g_ref, wu_ref, wv_ref, wc_ref, bc_ref, st_ref, o_ref, tail_ref,
                   h_ref, up_ref, carry_ref, *, ns, ts, tiles_per_seq):
    i, j = pl.program_id(0), pl.program_id(1)
    pad = SUBLANES

    @pl.when(j == 0)
    def _():
        h_ref[...] = _rms(x_ref[...], g_ref[...]).astype(BF16)

    @pl.when((i == 0) & (j == 0))
    def _():
        carry_ref[...] = jnp.zeros(carry_ref.shape, F32)

    h = h_ref[...]
    u = jnp.dot(h, wu_ref[...], preferred_element_type=F32)
    tn = u.shape[1]
    up_ref[:, pad:, :] = u.reshape(ns, ts, tn)
    first = (i % tiles_per_seq) == 0
    up_ref[:, pad - 2:pad, :] = jnp.where(first, st_ref[...], carry_ref[j])
    last2 = up_ref[:, ts + pad - 2:ts + pad, :]
    carry_ref[j] = last2
    tail_ref[0] = last2
    c = bc_ref[...].reshape(1, 1, tn)
    for tap in range(CONV_W):
        c = c + up_ref[:, pad - 2 + tap:pad - 2 + tap + ts, :] * wc_ref[tap:tap + 1, :].reshape(1, 1, tn)
    gl = _gelu(c).reshape(ns * ts, tn)
    v = jnp.dot(h, wv_ref[...], preferred_element_type=F32)
    o_ref[...] = (gl * v).astype(o_ref.dtype)


def _ffn_in(x, g, w, wc, bc, state, ns, ts, tn):
    t, d = x.shape
    f = w.shape[1] // 2
    tm = ns * ts
    nj = f // tn
    seq_len = t // state.shape[0]
    tiles_per_seq = max(seq_len // tm, 1)
    kern = functools.partial(_ffn_in_kernel, ns=ns, ts=ts, tiles_per_seq=tiles_per_seq)
    return pl.pallas_call(
        kern,
        grid=(t // tm, nj),
        in_specs=[pl.BlockSpec((tm, d), lambda i, j: (i, 0)),
                  pl.BlockSpec((1, d), lambda i, j: (0, 0)),
                  pl.BlockSpec((d, tn), lambda i, j: (0, j)),
                  pl.BlockSpec((d, tn), lambda i, j: (0, nj + j)),
                  pl.BlockSpec((CONV_W, tn), lambda i, j: (0, j)),
                  pl.BlockSpec((1, tn), lambda i, j: (0, j)),
                  pl.BlockSpec((ns, CONV_W - 1, tn), lambda i, j: (i // tiles_per_seq, 0, j))],
        out_specs=[pl.BlockSpec((tm, tn), lambda i, j: (i, j)),
                   pl.BlockSpec((1, ns, CONV_W - 1, tn), lambda i, j: (i, 0, 0, j))],
        out_shape=[jax.ShapeDtypeStruct((t, f), BF16),
                   jax.ShapeDtypeStruct((t // tm, ns, CONV_W - 1, f), F32)],
        scratch_shapes=[pltpu.VMEM((tm, d), BF16),
                        pltpu.VMEM((ns, ts + SUBLANES, tn), F32),
                        pltpu.VMEM((nj, ns, CONV_W - 1, tn), F32)],
        compiler_params=_cparams(("arbitrary", "arbitrary")),
        name="ffn_in",
    )(x, g.reshape(1, d), w, w, wc, bc.reshape(1, f), state)


def _norm_kernel(x_ref, g_ref, o_ref):
    o_ref[...] = _rms(x_ref[...], g_ref[...])


def _norm(x, g, tm):
    t, d = x.shape
    return pl.pallas_call(
        _norm_kernel,
        grid=(t // tm,),
        in_specs=[pl.BlockSpec((tm, d), lambda i: (i, 0)), pl.BlockSpec((1, d), lambda i: (0, 0))],
        out_specs=pl.BlockSpec((tm, d), lambda i: (i, 0)),
        out_shape=jax.ShapeDtypeStruct((t, d), F32),
        compiler_params=_cparams(("arbitrary",)),
        name="final_norm",
    )(x, g.reshape(1, d))


def _row_tile(t, cap=1024):
    tm = min(t, cap)
    while t % tm:
        tm //= 2
    return tm


def _pack_w_in(w_in, d_model):
    mix = d_model // 2
    sizes = dict(qa=mix, ka=A_KV * HEAD_DIM, va=A_KV * HEAD_DIM, qi=IDX_HEADS * IDX_DIM, ki=IDX_DIM,
                 wi=IDX_HEADS, qb=mix, kb=mix, vb=mix, uc=mix, g=N_BRANCH * d_model)
    src_order = ["qa", "ka", "va", "qi", "ki", "wi", "qb", "kb", "vb", "uc", "g"]
    src, off = {}, 0
    for name in src_order:
        src[name] = (off, sizes[name])
        off += sizes[name]
    assert off == w_in.shape[1]
    tn = 1280
    dst_order = ["qa", "qb", "kb", "vb", "uc", "qi", "ka", "va", "ki", "wi"]
    cols, parts, off = {}, [], 0

    def pad_to_tile():
        nonlocal off
        total = -(-off // tn) * tn
        parts.append(jnp.zeros((w_in.shape[0], total - off), w_in.dtype))
        off = total

    for name in dst_order:
        s, n = src[name]
        cols[name] = off
        parts.append(w_in[:, s:s + n])
        off += n
    cols["kw"] = cols["ki"]
    pad_to_tile()
    n_main = off // tn
    s, n = src["g"]
    parts.append(w_in[:, s:s + n])
    off += n
    pad_to_tile()
    return jnp.concatenate(parts, axis=1).astype(BF16), cols, tn, n_main


def _to_heads_T(x, b, t):
    return x.reshape(b, t, x.shape[1]).transpose(0, 2, 1)


def _pad_axis(x, axis, size):
    pad = [(0, 0)] * x.ndim
    pad[axis] = (0, size - x.shape[axis])
    return jnp.pad(x, pad)


def _layer(x, pos, prm, cache, layer, is_prompt):
    b, t, d = x.shape
    mix = d // 2
    xt = x.reshape(b * t, d)
    tm = _row_tile(b * t)
    cols = prm["cols"]
    p, gates = _proj(xt, prm["norm1"], prm["w_in"], tm, prm["w_in_tn"], prm["w_in_main"])

    tabs = _rope_tables(pos)
    tm_r = _row_tile(t) if is_prompt else tm
    if not is_prompt:
        tabs = jnp.tile(tabs, (1, b, 1))
    qa, ka, kab, vab, qi, kw, kib = _rope(p, tabs, cols, tm_r)
    if is_prompt:
        sq, s_valid, q_pos0 = t, t, 0
        s_pad = -(-s_valid // DSA_TK) * DSA_TK
        k_all = _pad_axis(kab.reshape(b, t, -1), 1, s_pad)
        ki_all = _pad_axis(kib.reshape(b, t, LANES), 1, s_pad)
        vT = _pad_axis(vab.reshape(b, t, -1), 1, s_pad).transpose(0, 2, 1).reshape(b, A_KV, HEAD_DIM, s_pad)
        ones = jnp.broadcast_to((jnp.arange(DSA_VROWS - HEAD_DIM) == 0).astype(BF16)[None, None, :, None],
                                (b, A_KV, DSA_VROWS - HEAD_DIM, s_pad))
        vT = jnp.concatenate([vT, ones], axis=2).reshape(b, A_KV * DSA_VROWS, s_pad)
    else:
        past = cache["a_k"].shape[2]
        sq, s_valid, q_pos0 = DSA_QB, past + t, past
        k_all, vT, ki_all = _cache_prep(cache["a_k"], cache["a_v"], cache["a_kidx"], layer, kab, vab, kib, b, t)
    qaT = _pad_axis(_to_heads_T(qa, b, t), 2, sq)
    qiT = _pad_axis(_to_heads_T(qi, b, t), 2, sq)
    wiT = _pad_axis(_to_heads_T(kw[:, IDX_DIM:IDX_DIM + IDX_HEADS], b, t), 2, sq)
    topk = min(TOPK_MAX, s_valid // 4)
    oaT = _dsa(qaT, qiT, wiT, k_all, vT, ki_all, s_valid=s_valid, q_pos0=q_pos0, topk=topk)
    oa = oaT[:, :, :t].transpose(0, 2, 1).reshape(b * t, mix)

    nh = mix // HEAD_DIM
    if is_prompt:
        n_tiles = BAND_CHUNKS * CHUNK // BAND_QB + 1
        bias = _band_bias(prm["rel_bias"], BAND_CHUNKS * CHUNK, BAND_QB, 0, n_tiles * BAND_QB)
        ob = _band_prompt(p, bias, cols, b, t)
    else:
        nbc = cache["b_k"].shape[2]
        past = cache["a_k"].shape[2]
        bias = _band_bias(prm["rel_bias"], past, t, past - nbc, nbc + t)
        ob = _band_sample(p, cache["b_k"], cache["b_v"], layer, bias, cols, b, t)

    g_ssm = mix // C_GROUP
    if is_prompt:
        h0_re = jnp.zeros((b, g_ssm, C_STATE), F32)
        h0_im = h0_re
    else:
        h0_re, h0_im = cache["c_re"][layer], cache["c_im"][layer]
    yc, hr, hi = _ssm_apply(p, cols["uc"], b, t, h0_re, h0_im, prm["ssm"], is_prompt)
    oc = _glu(yc, prm["w_glu"], prm["b_glu"], tm)

    merged = _merge(oa, ob, oc, gates, prm["w_branch"], tm, 512)
    x1 = _mmres(merged, prm["w_out"], xt, _row_tile(b * t, 512), 2048)

    f = prm["w_conv"].shape[1]
    if is_prompt:
        state = jnp.zeros((b, CONV_W - 1, f), F32)
        ns, ts = 1, _row_tile(t)
    else:
        state = cache["ffn_conv"][layer].astype(F32)
        ns, ts = b, t
    act, tails = _ffn_in(x1, prm["norm2"], prm["w_ffn_in"], prm["w_conv"], prm["b_conv"], state, ns, ts, 512)
    x2 = _mmres(act, prm["w_down"], x1, tm, 512)
    if is_prompt:
        buf = tails.reshape(b, t // ts, CONV_W - 1, f)[:, -1]
    else:
        buf = tails[0]

    ka4 = ka.reshape(b, t, A_KV, HEAD_DIM)
    va4 = p[:, cols["va"]:cols["va"] + A_KV * HEAD_DIM].reshape(b, t, A_KV, HEAD_DIM)
    ki3 = kw[:, :IDX_DIM].reshape(b, t, IDX_DIM)
    nb = min(BAND_CHUNKS * CHUNK, t) if is_prompt else t
    pb = p.reshape(b, t, p.shape[1])[:, t - nb:]
    kb4 = pb[:, :, cols["kb"]:cols["kb"] + mix].reshape(b, nb, nh, HEAD_DIM)
    vb4 = pb[:, :, cols["vb"]:cols["vb"] + mix].reshape(b, nb, nh, HEAD_DIM)
    return x2.reshape(b, t, d), (ka4, va4, ki3, kb4, vb4, hr, hi, buf)


def kernel(x_prompt, x_sample, cache_a_k, cache_a_v, cache_a_kidx, cache_b_k, cache_b_v, state_c_re, state_c_im,
           state_ffn_conv, norm1_g, w_in, rel_bias, ssm_a_re, ssm_a_im, ssm_log_dt, ssm_b_re, ssm_b_im, ssm_c_re,
           ssm_c_im, ssm_d, w_glu, b_glu, w_branch, w_out, norm2_g, w_ffn_in, w_ffn_conv, b_ffn_conv, w_ffn_down,
           normf_g):
    depth = w_in.shape[0]
    d = x_prompt.shape[2]
    pos_p = jnp.arange(x_prompt.shape[1])
    pos_s = cache_a_k.shape[2] + jnp.arange(x_sample.shape[1])
    xp, xs = x_prompt, x_sample
    st_p, st_s = [], []
    caches = dict(a_k=cache_a_k, a_v=cache_a_v, a_kidx=cache_a_kidx, b_k=cache_b_k, b_v=cache_b_v,
                  c_re=state_c_re, c_im=state_c_im, ffn_conv=state_ffn_conv)
    for l in range(depth):
        w_in_l, cols, tn, n_main = _pack_w_in(w_in[l], d)
        prm = dict(norm1=norm1_g[l], w_in=w_in_l, cols=cols, w_in_tn=tn, w_in_main=n_main, rel_bias=rel_bias[l],
                   ssm=_ssm_weights(ssm_a_re[l], ssm_a_im[l], ssm_log_dt[l], ssm_b_re[l], ssm_b_im[l],
                                    ssm_c_re[l], ssm_c_im[l], ssm_d[l]),
                   w_glu=w_glu[l].astype(BF16), b_glu=b_glu[l], w_branch=w_branch[l].astype(BF16),
                   w_out=w_out[l].astype(BF16), norm2=norm2_g[l], w_ffn_in=w_ffn_in[l].astype(BF16),
                   w_conv=w_ffn_conv[l], b_conv=b_ffn_conv[l], w_down=w_ffn_down[l].astype(BF16))
        xp, sp = _layer(xp, pos_p, prm, None, l, True)
        xs, ss = _layer(xs, pos_s, prm, caches, l, False)
        st_p.append(sp)
        st_s.append(ss)
    bp, tp, _ = xp.shape
    bs, tsq, _ = xs.shape
    y_prompt = _norm(xp.reshape(bp * tp, d), normf_g, _row_tile(bp * tp)).reshape(bp, tp, d)
    y_sample = _norm(xs.reshape(bs * tsq, d), normf_g, _row_tile(bs * tsq)).reshape(bs, tsq, d)
    outs_p = [jnp.stack([s[i] for s in st_p]) for i in range(8)]
    outs_s = [jnp.stack([s[i] for s in st_s]) for i in range(8)]
    return (y_prompt, y_sample, *outs_p, *outs_s)
```

```python
import functools
import math

import jax
import jax.numpy as jnp
import numpy as np
from jax import lax
from jax.experimental import pallas as pl
from jax.experimental.pallas import tpu as pltpu

F32 = jnp.float32
BF16 = jnp.bfloat16
I32 = jnp.int32

CHUNK = 64
EPS = 1e-6
ROPE_THETA = 500000.0
HEAD_DIM = 128
A_KV = 2
A_GROUP = 4
A_ROT = HEAD_DIM // 4
IDX_HEADS = 8
IDX_DIM = 64
IDX_ROT = IDX_DIM // 4
TOPK_MAX = 256
BAND_CHUNKS = 8
REL_CLIP = 256
C_GROUP = 16
C_STATE = 64
CONV_W = 3
N_BRANCH = 3
ATTN_SCALE = HEAD_DIM ** -0.5
IDX_SCALE = (IDX_DIM * IDX_HEADS) ** -0.5

LANES = 128
SUBLANES = 8
VMEM_LIMIT = 56 * 1024 * 1024

NEG_BIG = -1e30
KEY_NEG_INF = -2139095041
KEY_POS_INF = 2139095040
INT32_MIN = -2147483648

SSM_L = SUBLANES
SSM_GB = LANES // C_GROUP
DSA_TK = 512
DSA_TA = 512
DSA_VROWS = HEAD_DIM + 16
DSA_QB = 128
BAND_QB = 128


def _cparams(sem):
    return pltpu.CompilerParams(dimension_semantics=sem, vmem_limit_bytes=VMEM_LIMIT)


def _gelu(x):
    return 0.5 * x * (1.0 + jnp.tanh(math.sqrt(2.0 / math.pi) * (x + 0.044715 * (x * x * x))))


def _sigmoid(x):
    return 1.0 / (1.0 + jnp.exp(-x))


def _rms(x, g):
    ms = jnp.mean(x * x, axis=-1, keepdims=True)
    return (x * lax.rsqrt(ms + EPS)) * g


def _proj_kernel(x_ref, g_ref, w_ref, o_ref, og_ref, h_ref, *, n_main):
    j = pl.program_id(1)

    @pl.when(j == 0)
    def _():
        h_ref[...] = _rms(x_ref[...], g_ref[...]).astype(BF16)

    @pl.when(j < n_main)
    def _():
        o_ref[...] = jnp.dot(h_ref[...], w_ref[...], preferred_element_type=F32)

    @pl.when(j >= n_main)
    def _():
        og_ref[...] = jnp.dot(h_ref[...], w_ref[...], preferred_element_type=F32).astype(BF16)


def _proj(x, g, w, tm, tn, n_main):
    t, d = x.shape
    n = w.shape[1]
    nj = n // tn
    return pl.pallas_call(
        functools.partial(_proj_kernel, n_main=n_main),
        grid=(t // tm, nj),
        in_specs=[pl.BlockSpec((tm, d), lambda i, j: (i, 0)),
                  pl.BlockSpec((1, d), lambda i, j: (0, 0)),
                  pl.BlockSpec((d, tn), lambda i, j: (0, j))],
        out_specs=[pl.BlockSpec((tm, tn), lambda i, j: (i, jnp.minimum(j, n_main - 1))),
                   pl.BlockSpec((tm, tn), lambda i, j: (i, jnp.maximum(j - n_main, 0)))],
        out_shape=[jax.ShapeDtypeStruct((t, n_main * tn), F32),
                   jax.ShapeDtypeStruct((t, (nj - n_main) * tn), BF16)],
        scratch_shapes=[pltpu.VMEM((tm, d), BF16)],
        compiler_params=_cparams(("arbitrary", "arbitrary")),
        name="proj",
    )(x, g.reshape(1, d), w)


def _rot(x, c, sa, sb, half):
    n = x.shape[-1]
    return x * c + pltpu.roll(x, n - half, 1) * sa + pltpu.roll(x, half, 1) * sb


def _rope_kernel(qa_ref, ka_ref, va_ref, qi_ref, kw_ref, tab_ref,
                 qa_o, ka_o, kab_o, vab_o, qi_o, kw_o, kib_o):
    ca, saa, sba = tab_ref[0], tab_ref[1], tab_ref[2]
    ci, sai, sbi = tab_ref[3], tab_ref[4], tab_ref[5]
    ck, sak, sbk = tab_ref[6], tab_ref[7], tab_ref[8]
    for h in range(qa_ref.shape[1] // LANES):
        sl = slice(h * LANES, (h + 1) * LANES)
        qa_o[:, sl] = _rot(qa_ref[:, sl], ca, saa, sba, A_ROT // 2).astype(BF16)
    for h in range(ka_ref.shape[1] // LANES):
        sl = slice(h * LANES, (h + 1) * LANES)
        k = _rot(ka_ref[:, sl], ca, saa, sba, A_ROT // 2)
        ka_o[:, sl] = k
        kab_o[:, sl] = k.astype(BF16)
    vab_o[...] = va_ref[...].astype(BF16)
    for h in range(qi_ref.shape[1] // LANES):
        sl = slice(h * LANES, (h + 1) * LANES)
        qi_o[:, sl] = _rot(qi_ref[:, sl], ci, sai, sbi, IDX_ROT // 2).astype(BF16)
    kw = _rot(kw_ref[...], ck, sak, sbk, IDX_ROT // 2)
    kw_o[...] = kw
    kib_o[...] = kw.astype(BF16)


def _rope(p, tabs, cols, tm):
    t = p.shape[0]
    n_tab_blocks = tabs.shape[1] // tm
    d_qa, d_kv, d_qi = A_KV * A_GROUP * HEAD_DIM, A_KV * HEAD_DIM, IDX_HEADS * IDX_DIM

    def col(width, off):
        assert off % width == 0
        return pl.BlockSpec((tm, width), lambda i: (i, off // width))

    def out(width):
        return pl.BlockSpec((tm, width), lambda i: (i, 0))

    return pl.pallas_call(
        _rope_kernel,
        grid=(t // tm,),
        in_specs=[col(d_qa, cols["qa"]), col(d_kv, cols["ka"]), col(d_kv, cols["va"]),
                  col(d_qi, cols["qi"]), col(LANES, cols["kw"]),
                  pl.BlockSpec((9, tm, LANES), lambda i: (0, i % n_tab_blocks, 0))],
        out_specs=[out(d_qa), out(d_kv), out(d_kv), out(d_kv), out(d_qi), out(LANES), out(LANES)],
        out_shape=[jax.ShapeDtypeStruct((t, d_qa), BF16),
                   jax.ShapeDtypeStruct((t, d_kv), F32),
                   jax.ShapeDtypeStruct((t, d_kv), BF16),
                   jax.ShapeDtypeStruct((t, d_kv), BF16),
                   jax.ShapeDtypeStruct((t, d_qi), BF16),
                   jax.ShapeDtypeStruct((t, LANES), F32),
                   jax.ShapeDtypeStruct((t, LANES), BF16)],
        compiler_params=_cparams(("arbitrary",)),
        name="rope",
    )(p, p, p, p, p, tabs)


def _rope_tables(pos):
    pos = pos.astype(F32)[:, None]

    def cs(rot):
        half = rot // 2
        inv = jnp.float32(ROPE_THETA) ** (-jnp.arange(half, dtype=F32) / half)
        ang = pos * inv[None, :]
        return jnp.cos(ang), jnp.sin(ang)

    def tabs(rot, width):
        half = rot // 2
        c, s = cs(rot)
        n = pos.shape[0]
        one = jnp.ones((n, width - rot), F32)
        zero = jnp.zeros((n, width - rot), F32)
        zh = jnp.zeros((n, half), F32)
        return (jnp.concatenate([c, c, one], 1), jnp.concatenate([-s, zh, zero], 1),
                jnp.concatenate([zh, s, zero], 1))

    ca, saa, sba = tabs(A_ROT, HEAD_DIM)
    c64, sa64, sb64 = tabs(IDX_ROT, IDX_DIM)
    n = pos.shape[0]
    one64, zero64 = jnp.ones((n, IDX_DIM), F32), jnp.zeros((n, IDX_DIM), F32)
    return jnp.stack([ca, saa, sba,
                      jnp.concatenate([c64, c64], 1), jnp.concatenate([sa64, sa64], 1),
                      jnp.concatenate([sb64, sb64], 1),
                      jnp.concatenate([c64, one64], 1), jnp.concatenate([sa64, zero64], 1),
                      jnp.concatenate([sb64, zero64], 1)])


def _dsa_kernel(qaT_ref, qiT_ref, wiT_ref, k_ref, vT_ref, ki_ref, o_ref, key_ref, acc_ref,
                s0_ref, s1_ref, p0_ref, p1_ref, a0_ref, a1_ref,
                *, s_valid, q_pos0, topk):
    qb, tk = DSA_QB, DSA_TK
    j = pl.program_id(1)
    q_first = q_pos0 + j * qb
    qpos = q_first + lax.broadcasted_iota(I32, (1, qb), 1)
    assert CHUNK == 64 and tk == 512
    n_adm = jnp.minimum(((qpos >> 6) + 1) * CHUNK, s_valid)
    n_max = jnp.minimum((((q_first + qb - 1) >> 6) + 1) * CHUNK, s_valid)
    nkt = (n_max + tk - 1) >> 9

    qi = qiT_ref[0]
    rhs = jnp.concatenate([qi[h * IDX_DIM:(h + 1) * IDX_DIM, :] for h in range(IDX_HEADS)], axis=1)
    rhs = jnp.concatenate([rhs, jnp.zeros((LANES - IDX_DIM, IDX_HEADS * qb), BF16)], axis=0)
    wi = wiT_ref[0]

    def p1(kt, carry):
        off = pl.multiple_of(kt * tk, tk)
        kid = ki_ref[0, pl.ds(off, tk), :]
        s_all = jnp.dot(kid, rhs, preferred_element_type=F32)
        acc = wi[0:1, :] * jnp.maximum(s_all[:, 0:qb], 0.0)
        for h in range(1, IDX_HEADS):
            acc = acc + wi[h:h + 1, :] * jnp.maximum(s_all[:, h * qb:(h + 1) * qb], 0.0)
        score = acc * IDX_SCALE
        kpos = off + lax.broadcasted_iota(I32, (tk, 1), 0)
        score = jnp.where(kpos < n_adm, score, -jnp.inf)
        u = pltpu.bitcast(score, I32)
        key_ref[pl.ds(off, tk), :] = u ^ (lax.shift_right_arithmetic(u, 31) & 0x7FFFFFFF)
        return carry

    lax.fori_loop(0, nkt, p1, 0)

    pairs = key_ref.shape[0] % (2 * tk) == 0
    if pairs:
        @pl.when(nkt % 2 == 1)
        def _():
            key_ref[pl.ds(pl.multiple_of(nkt * tk, tk), tk), :] = jnp.full((tk, qb), INT32_MIN, I32)

    n_trips, per_trip = ((nkt + 1) // 2, 2) if pairs else (nkt, 1)

    def count_ge(cand):
        def body(kt, cnt):
            for half in range(per_trip):
                blk = key_ref[pl.ds(pl.multiple_of((kt * per_trip + half) * tk, tk), tk), :]
                m = jnp.where(blk >= cand, 1, 0).astype(I32)
                cnt = cnt + jnp.sum(m.reshape(tk // SUBLANES, SUBLANES, qb), axis=0)
            return cnt

        cnt8 = lax.fori_loop(0, n_trips, body, jnp.zeros((SUBLANES, qb), I32))
        return jnp.sum(cnt8, axis=0, keepdims=True)

    def bit_body(i, carry):
        prefix, n_ge = carry
        cand_u = prefix | lax.shift_left(jnp.int32(1), 31 - i)
        cnt = count_ge(cand_u ^ INT32_MIN)
        ok = cnt >= topk
        return jnp.where(ok, cand_u, prefix), jnp.where(ok, cnt, n_ge)

    prefix, n_ge = lax.fori_loop(0, 32, bit_body, (jnp.zeros((1, qb), I32), jnp.broadcast_to(nkt * tk, (1, qb))))
    thr = prefix ^ INT32_MIN
    cut_ties = jnp.max(jnp.where((n_ge > topk) & (thr > KEY_NEG_INF), 1, 0)) > 0

    @pl.when(cut_ties)
    def _():
        r = lax.broadcasted_iota(I32, (tk, tk), 0)
        c = lax.broadcasted_iota(I32, (tk, tk), 1)
        tri = jnp.where(c <= r, 1.0, 0.0).astype(BF16)
        take = (topk - count_ge(thr + 1)).astype(F32)

        def body(kt, seen):
            off = pl.multiple_of(kt * tk, tk)
            blk = key_ref[pl.ds(off, tk), :]
            eq = blk == thr
            incl = jnp.dot(tri, jnp.where(eq, 1.0, 0.0).astype(BF16), preferred_element_type=F32)
            drop = eq & (seen + incl > take)
            key_ref[pl.ds(off, tk), :] = jnp.where(drop, thr - 1, blk)
            return seen + incl[tk - 1:tk, :]

        lax.fori_loop(0, nkt, body, jnp.zeros((1, qb), F32))

    thr_lo = jnp.maximum(thr, KEY_NEG_INF + 1)
    qa = qaT_ref[0]
    ng = A_GROUP * qb
    qg = [jnp.concatenate([qa[(g * A_GROUP + hh) * HEAD_DIM:(g * A_GROUP + hh + 1) * HEAD_DIM, :]
                           for hh in range(A_GROUP)], axis=1) for g in range(A_KV)]
    acc_ref[...] = jnp.zeros(acc_ref.shape, F32)

    c_exp = ATTN_SCALE * math.log2(math.e)

    ta = DSA_TA
    n_steps = nkt * (tk // ta)
    last = n_steps - 1
    bufs = ((s0_ref, p0_ref, a0_ref), (s1_ref, p1_ref, a1_ref))

    def logits(step, s_ref):
        off = pl.multiple_of(jnp.minimum(step, last) * ta, ta)
        for g in range(A_KV):
            kg = k_ref[0, pl.ds(off, ta), g * HEAD_DIM:(g + 1) * HEAD_DIM]
            s_ref[g] = jnp.dot(kg, qg[g], preferred_element_type=F32)

    def numerators(step, s_ref, p_ref, a_ref, ms):
        off = pl.multiple_of(jnp.minimum(step, last) * ta, ta)
        blk = key_ref[pl.ds(off, ta), :]
        sel = (blk >= thr_lo) & (blk < KEY_POS_INF) & (step <= last)
        mask = jnp.concatenate([jnp.where(sel, 0.0, NEG_BIG)] * A_GROUP, axis=1)
        new = []
        for g in range(A_KV):
            s = s_ref[g] + mask
            m_new = jnp.maximum(ms[g], jnp.max(s, axis=0, keepdims=True))
            a_ref[g] = jnp.broadcast_to(jnp.exp2((ms[g] - m_new) * c_exp), (SUBLANES, ng))
            p_ref[g] = jnp.exp2((s - m_new).astype(BF16) * c_exp)
            new.append(m_new)
        return tuple(new)

    def weighted_values(step, p_ref, a_ref):
        off = pl.multiple_of(jnp.clip(step, 0, last) * ta, ta)
        for g in range(A_KV):
            vg = vT_ref[0, g * DSA_VROWS:(g + 1) * DSA_VROWS, pl.ds(off, ta)]
            acc_ref[g] = a_ref[g, 0:1, :] * acc_ref[g] + jnp.dot(vg, p_ref[g], preferred_element_type=F32)

    p1_ref[...] = jnp.zeros(p1_ref.shape, BF16)
    a1_ref[...] = jnp.ones(a1_ref.shape, F32)
    logits(0, s0_ref)

    def two_steps(i2, ms):
        i = 2 * i2
        for par in range(2):
            cur, nxt = bufs[par], bufs[1 - par]
            logits(i + par + 1, nxt[0])
            ms = numerators(i + par, cur[0], cur[1], cur[2], ms)
            weighted_values(i + par - 1, nxt[1], nxt[2])
        return ms

    init = (jnp.full((1, ng), NEG_BIG, F32),) * A_KV
    n_pairs = (n_steps + 1) // 2
    lax.fori_loop(0, n_pairs, two_steps, init)
    weighted_values(2 * n_pairs - 1, p1_ref, a1_ref)
    for g in range(A_KV):
        o = acc_ref[g, 0:HEAD_DIM, :] / acc_ref[g, HEAD_DIM:HEAD_DIM + 1, :]
        for hh in range(A_GROUP):
            h = g * A_GROUP + hh
            o_ref[0, h * HEAD_DIM:(h + 1) * HEAD_DIM, :] = o[:, hh * qb:(hh + 1) * qb].astype(o_ref.dtype)


def _dsa(qaT, qiT, wiT, k, vT, ki, *, s_valid, q_pos0, topk):
    b, dq, sq = qaT.shape
    s_pad = k.shape[1]
    assert s_pad % DSA_TK == 0 and s_pad % DSA_TA == 0 and sq % DSA_QB == 0 and topk <= DSA_TK
    kern = functools.partial(_dsa_kernel, s_valid=s_valid, q_pos0=q_pos0, topk=topk)
    return pl.pallas_call(
        kern,
        grid=(b, sq // DSA_QB),
        in_specs=[pl.BlockSpec((1, dq, DSA_QB), lambda i, j: (i, 0, j)),
                  pl.BlockSpec((1, qiT.shape[1], DSA_QB), lambda i, j: (i, 0, j)),
                  pl.BlockSpec((1, IDX_HEADS, DSA_QB), lambda i, j: (i, 0, j)),
                  pl.BlockSpec((1, s_pad, k.shape[2]), lambda i, j: (i, 0, 0)),
                  pl.BlockSpec((1, vT.shape[1], s_pad), lambda i, j: (i, 0, 0)),
                  pl.BlockSpec((1, s_pad, LANES), lambda i, j: (i, 0, 0))],
        out_specs=pl.BlockSpec((1, dq, DSA_QB), lambda i, j: (i, 0, j)),
        out_shape=jax.ShapeDtypeStruct((b, dq, sq), BF16),
        scratch_shapes=[pltpu.VMEM((s_pad, DSA_QB), I32),
                        pltpu.VMEM((A_KV, DSA_VROWS, A_GROUP * DSA_QB), F32)]
                       + [pltpu.VMEM((A_KV, DSA_TA, A_GROUP * DSA_QB), F32)] * 2
                       + [pltpu.VMEM((A_KV, DSA_TA, A_GROUP * DSA_QB), BF16)] * 2
                       + [pltpu.VMEM((A_KV, SUBLANES, A_GROUP * DSA_QB), F32)] * 2,
        compiler_params=_cparams(("arbitrary", "arbitrary")),
        name="dsa",
    )(qaT, qiT, wiT, k, vT, ki)


def _cache_prep_kernel(ck_ref, cv_ref, cki_ref, kn_ref, vn_ref, kin_ref, k_o, vT_o, ki_o, *, n_cache_tiles):
    kt = pl.program_id(1)
    tk = k_o.shape[1]
    extra = DSA_VROWS - HEAD_DIM
    ones_rows = jnp.where(lax.broadcasted_iota(I32, (extra, tk), 0) == 0, 1.0, 0.0).astype(BF16)

    def emit(k_head, v_head, ki):
        for g in range(A_KV):
            k_o[0, :, g * HEAD_DIM:(g + 1) * HEAD_DIM] = k_head(g).astype(BF16)
            vT_o[0, g * DSA_VROWS:g * DSA_VROWS + HEAD_DIM, :] = v_head(g).astype(F32).T.astype(BF16)
            vT_o[0, g * DSA_VROWS + HEAD_DIM:(g + 1) * DSA_VROWS, :] = ones_rows
        ki_o[0] = ki

    @pl.when(kt < n_cache_tiles)
    def _():
        ki = jnp.concatenate([cki_ref[0, 0], jnp.zeros((tk, LANES - IDX_DIM), F32)], axis=1)
        emit(lambda g: ck_ref[0, 0, :, g, :], lambda g: cv_ref[0, 0, :, g, :], ki.astype(BF16))

    @pl.when(kt >= n_cache_tiles)
    def _():
        t = kn_ref.shape[0]
        rows = lambda a: jnp.concatenate([a, jnp.zeros((tk - t, a.shape[1]), a.dtype)], axis=0)
        kn, vn = rows(kn_ref[...]), rows(vn_ref[...])
        head = lambda a: lambda g: a[:, g * HEAD_DIM:(g + 1) * HEAD_DIM]
        emit(head(kn), head(vn), rows(kin_ref[...]))


def _cache_prep(cache_k, cache_v, cache_ki, layer, kn, vn, kin, b, t):
    tk = DSA_TK
    past = cache_k.shape[2]
    assert past % tk == 0 and t <= tk
    nct = past // tk
    s_pad = past + tk
    cidx = lambda i, j: jnp.minimum(j, nct - 1)
    return pl.pallas_call(
        functools.partial(_cache_prep_kernel, n_cache_tiles=nct),
        grid=(b, nct + 1),
        in_specs=[pl.BlockSpec((1, 1, tk, A_KV, HEAD_DIM), lambda i, j: (layer, i, cidx(i, j), 0, 0)),
                  pl.BlockSpec((1, 1, tk, A_KV, HEAD_DIM), lambda i, j: (layer, i, cidx(i, j), 0, 0)),
                  pl.BlockSpec((1, 1, tk, IDX_DIM), lambda i, j: (layer, i, cidx(i, j), 0)),
                  pl.BlockSpec((t, A_KV * HEAD_DIM), lambda i, j: (i, 0)),
                  pl.BlockSpec((t, A_KV * HEAD_DIM), lambda i, j: (i, 0)),
                  pl.BlockSpec((t, LANES), lambda i, j: (i, 0))],
        out_specs=[pl.BlockSpec((1, tk, A_KV * HEAD_DIM), lambda i, j: (i, j, 0)),
                   pl.BlockSpec((1, A_KV * DSA_VROWS, tk), lambda i, j: (i, 0, j)),
                   pl.BlockSpec((1, tk, LANES), lambda i, j: (i, j, 0))],
        out_shape=[jax.ShapeDtypeStruct((b, s_pad, A_KV * HEAD_DIM), BF16),
                   jax.ShapeDtypeStruct((b, A_KV * DSA_VROWS, s_pad), BF16),
                   jax.ShapeDtypeStruct((b, s_pad, LANES), BF16)],
        compiler_params=_cparams(("arbitrary", "arbitrary")),
        name="cache_prep",
    )(cache_k, cache_v, cache_ki, kn, vn, kin)


def _band_heads(q, k_head, v_head, bias_ref, n_hidden, o_ref, s_ref):
    nh = q.shape[1] // HEAD_DIM
    tk = bias_ref.shape[2]
    hidden = None if n_hidden is None else lax.broadcasted_iota(I32, (1, tk), 1) < n_hidden
    for h in range(nh):
        sl = slice(h * HEAD_DIM, (h + 1) * HEAD_DIM)
        s = lax.dot_general(q[:, sl].astype(BF16), k_head(h), (((1,), (1,)), ((), ())),
                            preferred_element_type=F32) * ATTN_SCALE + bias_ref[h]
        if hidden is not None:
            s = jnp.where(hidden, NEG_BIG, s)
        s_ref[h] = s
    ms = [jnp.max(s_ref[h], axis=1, keepdims=True) for h in range(nh)]
    ls = []
    for h in range(nh):
        p = jnp.exp(s_ref[h] - ms[h])
        s_ref[h] = p
        ls.append(jnp.sum(p, axis=1, keepdims=True))
    for h in range(nh):
        sl = slice(h * HEAD_DIM, (h + 1) * HEAD_DIM)
        w = (s_ref[h] / ls[h]).astype(BF16)
        o_ref[:, sl] = jnp.dot(w, v_head(h), preferred_element_type=F32).astype(o_ref.dtype)


def _band_prompt_kernel(*refs, n_tiles):
    q_ref = refs[0]
    k_refs = refs[1:1 + n_tiles]
    v_refs = refs[1 + n_tiles:1 + 2 * n_tiles]
    bias_ref = refs[1 + 2 * n_tiles]
    o_ref = refs[2 + 2 * n_tiles]
    s_ref = refs[3 + 2 * n_tiles]
    j = pl.program_id(1)
    n_hidden = jnp.maximum(n_tiles - 1 - j, 0) * q_ref.shape[0]
    k_all = jnp.concatenate([r[...] for r in k_refs], axis=0).astype(BF16)
    v_all = jnp.concatenate([r[...] for r in v_refs], axis=0).astype(BF16)
    head = lambda a: lambda h: a[:, h * HEAD_DIM:(h + 1) * HEAD_DIM]
    _band_heads(q_ref[...], head(k_all), head(v_all), bias_ref, n_hidden, o_ref, s_ref)


def _band_prompt(p, bias, cols, batch, seq):
    qb = BAND_QB
    n_tiles = BAND_CHUNKS * CHUNK // qb + 1
    width = bias.shape[0] * HEAD_DIM
    nq = seq // qb
    cq, ck, cv = cols["qb"] // width, cols["kb"] // width, cols["vb"] // width
    assert cols["qb"] % width == 0 and cols["kb"] % width == 0 and cols["vb"] % width == 0

    def kv_spec(c, i):
        return pl.BlockSpec((qb, width), lambda b, j: (b * nq + jnp.maximum(j - (n_tiles - 1) + i, 0), c))

    return pl.pallas_call(
        functools.partial(_band_prompt_kernel, n_tiles=n_tiles),
        grid=(batch, nq),
        in_specs=([pl.BlockSpec((qb, width), lambda b, j: (b * nq + j, cq))]
                  + [kv_spec(ck, i) for i in range(n_tiles)]
                  + [kv_spec(cv, i) for i in range(n_tiles)]
                  + [pl.BlockSpec(bias.shape, lambda b, j: (0, 0, 0))]),
        out_specs=pl.BlockSpec((qb, width), lambda b, j: (b * nq + j, 0)),
        out_shape=jax.ShapeDtypeStruct((batch * seq, width), BF16),
        compiler_params=_cparams(("arbitrary", "arbitrary")),
        scratch_shapes=[pltpu.VMEM(bias.shape, F32)],
        name="band_prompt",
    )(*([p] * (1 + 2 * n_tiles)), bias)


def _band_sample_kernel(q_ref, k_ref, v_ref, ck_ref, cv_ref, bias_ref, o_ref, s_ref):
    def head(cache_ref, new_ref):
        return lambda h: jnp.concatenate(
            [cache_ref[0, 0, :, h, :], new_ref[:, h * HEAD_DIM:(h + 1) * HEAD_DIM]], axis=0).astype(BF16)

    _band_heads(q_ref[...], head(ck_ref, k_ref), head(cv_ref, v_ref), bias_ref, None, o_ref, s_ref)


def _band_sample(p, cache_k, cache_v, layer, bias, cols, batch, t):
    nh = bias.shape[0]
    width = nh * HEAD_DIM
    nbc = cache_k.shape[2]
    cq, ck, cv = cols["qb"] // width, cols["kb"] // width, cols["vb"] // width
    return pl.pallas_call(
        _band_sample_kernel,
        grid=(batch,),
        in_specs=[pl.BlockSpec((t, width), lambda b: (b, cq)),
                  pl.BlockSpec((t, width), lambda b: (b, ck)),
                  pl.BlockSpec((t, width), lambda b: (b, cv)),
                  pl.BlockSpec((1, 1, nbc, nh, HEAD_DIM), lambda b: (layer, b, 0, 0, 0)),
                  pl.BlockSpec((1, 1, nbc, nh, HEAD_DIM), lambda b: (layer, b, 0, 0, 0)),
                  pl.BlockSpec(bias.shape, lambda b: (0, 0, 0))],
        out_specs=pl.BlockSpec((t, width), lambda b: (b, 0)),
        out_shape=jax.ShapeDtypeStruct((batch * t, width), BF16),
        compiler_params=_cparams(("arbitrary",)),
        scratch_shapes=[pltpu.VMEM(bias.shape, F32)],
        name="band_sample",
    )(p, p, p, cache_k, cache_v, bias)


def _band_bias(table, q0, nq, k0, nk):
    n = nq + nk
    j = jnp.arange(n)
    g = table[:, jnp.clip(j + (q0 - k0 - nk + 1), -REL_CLIP, REL_CLIP) + REL_CLIP].astype(F32)
    hank = jnp.tile(g, (1, nq + 1))[:, :nq * (n + 1)].reshape(-1, nq, n + 1)[:, :, :nk]
    bias = hank[:, :, ::-1]
    qc = (q0 + jnp.arange(nq))[:, None] // CHUNK
    kc = (k0 + jnp.arange(nk))[None, :] // CHUNK
    vis = (kc <= qc) & (kc >= qc - BAND_CHUNKS)
    return jnp.where(vis[None], bias, NEG_BIG)


def _ssm_kernel(u_ref, t_ref, p_ref, q_ref, a_ref, h0_ref, y_ref, hre_o, him_o, sre_ref, sim_ref,
                *, seg, tiles_per_seq):
    L = SSM_L
    x = jnp.concatenate([u_ref[:, s, :] for s in range(L)], axis=1).astype(BF16)
    sre_ref[...] = jnp.dot(x, p_ref[0, 0], preferred_element_type=F32)
    sim_ref[...] = jnp.dot(x, p_ref[0, 1], preferred_element_type=F32)
    rows, w = sre_ref.shape
    kseg = lax.broadcasted_iota(I32, (SUBLANES, 1), 0) % seg
    apk_re, apk_im = a_ref[0, 0, 0:SUBLANES, :], a_ref[0, 1, 0:SUBLANES, :]

    def cmad(x_re, x_im, a_re, a_im, y_re, y_im):
        return x_re + a_re * y_re - a_im * y_im, x_im + a_re * y_im + a_im * y_re

    def tile(i, carry):
        r = pl.ds(pl.multiple_of(i * SUBLANES, SUBLANES), SUBLANES)
        x_re, x_im = sre_ref[r, :], sim_ref[r, :]
        for n, d in enumerate((1, 2, 4)):
            if d < seg:
                ad_re = a_ref[0, 0, SUBLANES + n:SUBLANES + n + 1, :]
                ad_im = a_ref[0, 1, SUBLANES + n:SUBLANES + n + 1, :]
                sh_re = jnp.where(kseg >= d, pltpu.roll(x_re, d, 0), 0.0)
                sh_im = jnp.where(kseg >= d, pltpu.roll(x_im, d, 0), 0.0)
                x_re, x_im = cmad(x_re, x_im, ad_re, ad_im, sh_re, sh_im)
        if tiles_per_seq > 1:
            first = (i % tiles_per_seq) == 0
            hin_re = jnp.where(first, h0_ref[0, 0], carry[0])
            hin_im = jnp.where(first, h0_ref[0, 1], carry[1])
        else:
            hin_re, hin_im = h0_ref[0, 0, r, :], h0_ref[0, 1, r, :]
        inc_re, inc_im = cmad(x_re, x_im, apk_re, apk_im, hin_re, hin_im)
        sre_ref[r, :] = jnp.where(kseg == 0, hin_re, pltpu.roll(inc_re, 1, 0))
        sim_ref[r, :] = jnp.where(kseg == 0, hin_im, pltpu.roll(inc_im, 1, 0))
        hre_o[0, r, :] = inc_re
        him_o[0, r, :] = inc_im
        return inc_re[SUBLANES - 1:SUBLANES, :], inc_im[SUBLANES - 1:SUBLANES, :]

    zero = jnp.zeros((1, w), F32)
    lax.fori_loop(0, rows // SUBLANES, tile, (zero, zero))
    y = (jnp.dot(x, t_ref[0], preferred_element_type=F32)
         + jnp.dot(sre_ref[...].astype(BF16), q_ref[0, 0], preferred_element_type=F32)
         + jnp.dot(sim_ref[...].astype(BF16), q_ref[0, 1], preferred_element_type=F32))
    for t in range(L):
        y_ref[:, t, :] = y[:, t * LANES:(t + 1) * LANES]


def _ssm(p3, uc_col, tmat, pmat, qmat, apow, h0, rows_blk, seg, tiles_per_seq):
    rows_total = p3.shape[0]
    nb, _, _, w = pmat.shape
    kl = SSM_L * LANES
    h0_rows = h0.shape[2] // (rows_total // rows_blk)
    hspec = pl.BlockSpec((1, rows_blk, w), lambda i, r: (i, r, 0))
    return pl.pallas_call(
        functools.partial(_ssm_kernel, seg=seg, tiles_per_seq=tiles_per_seq),
        grid=(nb, rows_total // rows_blk),
        in_specs=[pl.BlockSpec((rows_blk, SSM_L, LANES), lambda i, r: (r, 0, uc_col // LANES + i)),
                  pl.BlockSpec((1, kl, kl), lambda i, r: (i, 0, 0)),
                  pl.BlockSpec((1, 2, kl, w), lambda i, r: (i, 0, 0, 0)),
                  pl.BlockSpec((1, 2, w, kl), lambda i, r: (i, 0, 0, 0)),
                  pl.BlockSpec((1, 2, 2 * SUBLANES, w), lambda i, r: (i, 0, 0, 0)),
                  pl.BlockSpec((1, 2, h0_rows, w), lambda i, r: (i, 0, r, 0))],
        out_specs=[pl.BlockSpec((rows_blk, SSM_L, LANES), lambda i, r: (r, 0, i)), hspec, hspec],
        out_shape=[jax.ShapeDtypeStruct((rows_total, SSM_L, nb * LANES), F32),
                   jax.ShapeDtypeStruct((nb, rows_total, w), F32),
                   jax.ShapeDtypeStruct((nb, rows_total, w), F32)],
        scratch_shapes=[pltpu.VMEM((rows_blk, w), F32), pltpu.VMEM((rows_blk, w), F32)],
        compiler_params=_cparams(("arbitrary", "arbitrary")),
        name="ssm",
    )(p3, tmat, pmat, qmat, apow, h0)


def _ssm_weights(a_re, a_im, log_dt, b_re, b_im, c_re, c_im, d):
    hp = lax.Precision.HIGHEST
    g, p = a_re.shape
    gc = b_re.shape[2]
    L = SSM_L
    a_re, a_im = a_re.astype(F32), a_im.astype(F32)
    dt = jnp.exp(log_dt.astype(F32))[:, None]
    mag = jnp.exp(dt * a_re)
    ab_re, ab_im = mag * jnp.cos(dt * a_im), mag * jnp.sin(dt * a_im)
    n_re = ab_re - 1.0
    n_im = ab_im
    den = a_re * a_re + a_im * a_im
    cc_re = (n_re * a_re + n_im * a_im) / den
    cc_im = (n_im * a_re - n_re * a_im) / den
    b_re, b_im = b_re.astype(F32), b_im.astype(F32)
    bb_re = cc_re[..., None] * b_re - cc_im[..., None] * b_im
    bb_im = cc_re[..., None] * b_im + cc_im[..., None] * b_re
    c_re, c_im = c_re.astype(F32), c_im.astype(F32)

    def cmul(x, y):
        return x[0] * y[0] - x[1] * y[1], x[0] * y[1] + x[1] * y[0]

    rep = lambda a: jnp.broadcast_to(a[None], (L,) + a.shape)
    pw_re, pw_im = lax.associative_scan(cmul, (rep(ab_re), rep(ab_im)), axis=0)
    pw_re = jnp.concatenate([jnp.ones((1, g, p), F32), pw_re], axis=0)
    pw_im = jnp.concatenate([jnp.zeros((1, g, p), F32), pw_im], axis=0)
    ca_re = c_re[None] * pw_re[:, :, None, :] - c_im[None] * pw_im[:, :, None, :]
    ca_im = c_re[None] * pw_im[:, :, None, :] + c_im[None] * pw_re[:, :, None, :]
    taps = (jnp.einsum("tgop,gpi->tgoi", ca_re[:L], bb_re, precision=hp)
            - jnp.einsum("tgop,gpi->tgoi", ca_im[:L], bb_im, precision=hp))
    taps = taps.at[0].add(jax.vmap(jnp.diag)(d.astype(F32)))
    s_idx = jnp.arange(L)[:, None]
    t_idx = jnp.arange(L)[None, :]
    lag = t_idx - s_idx
    tt = jnp.where((lag >= 0)[:, :, None, None, None], taps[jnp.clip(lag, 0, L - 1)], 0.0)
    gb = SSM_GB
    nb = g // gb
    n_x = L * gb * gc
    x = jnp.arange(n_x)
    x_grp, x_loc = (x // gc) % gb, (x // (gb * gc)) * gc + x % gc
    place_x = ((x_grp[None, :, None] == jnp.arange(gb)[:, None, None])
               & (x_loc[None, :, None] == jnp.arange(L * gc)[None, None, :])).astype(F32)
    q = jnp.arange(gb * p)
    place_q = ((q[None, :, None] // p == jnp.arange(gb)[:, None, None])
               & (q[None, :, None] % p == jnp.arange(p)[None, None, :])).astype(F32)
    blk = lambda a: a.reshape((nb, gb) + a.shape[1:])
    t_grp = blk(tt.transpose(2, 0, 4, 1, 3).reshape(g, L * gc, L * gc))
    tmat = jnp.einsum("gxr,bgrc,gyc->bxy", place_x, t_grp, place_x)
    rev = L - 1 - jnp.arange(L)
    pin_re = pw_re[rev][:, :, :, None] * bb_re[None] - pw_im[rev][:, :, :, None] * bb_im[None]
    pin_im = pw_re[rev][:, :, :, None] * bb_im[None] + pw_im[rev][:, :, :, None] * bb_re[None]
    to_p = lambda a: jnp.einsum("gxr,bgrp,gqp->bxq", place_x,
                                blk(a.transpose(1, 0, 3, 2).reshape(g, L * gc, p)), place_q)
    to_q = lambda a: jnp.einsum("gqp,bgpc,gyc->bqy", place_q,
                                blk(a.transpose(1, 3, 0, 2).reshape(g, p, L * gc)), place_x)
    pmat = jnp.stack([to_p(pin_re), to_p(pin_im)], axis=1)
    qmat = jnp.stack([to_q(ca_re[1:]), -to_q(ca_im[1:])], axis=1)
    rep8 = lambda a: jnp.broadcast_to(a[None], (SUBLANES,) + a.shape)
    apl_re, apl_im = lax.associative_scan(cmul, (rep8(pw_re[L]), rep8(pw_im[L])), axis=0)
    apl = jnp.stack([apl_re, apl_im]).reshape(2, SUBLANES, nb, gb * p)
    return tmat.astype(BF16), pmat.astype(BF16), qmat.astype(BF16), apl


def _ssm_apply(p, uc_col, b, t, h0_re, h0_im, weights, one_seq_per_step):
    tmat, pmat, qmat, apl = weights
    nb, w = pmat.shape[0], pmat.shape[3]
    L = SSM_L
    nc = t // L
    assert t % L == 0 and (nc % SUBLANES == 0 or SUBLANES % nc == 0)
    seg = min(nc, SUBLANES)
    if one_seq_per_step:
        assert nc % SUBLANES == 0
        rows_blk, tiles_per_seq, reps = nc, nc // SUBLANES, SUBLANES
    else:
        rows_blk, tiles_per_seq, reps = b * nc, 1, nc
        assert nc <= SUBLANES and rows_blk % SUBLANES == 0
    pw_rows = [k % seg for k in range(SUBLANES)] + [0, 1, 3] + [0] * (SUBLANES - 3)
    apow = jnp.stack([apl[:, k] for k in pw_rows], axis=1).transpose(2, 0, 1, 3)

    def pack(h):
        h = h.astype(F32).reshape(b, nb, w).transpose(1, 0, 2)
        return jnp.repeat(h, reps, axis=1)

    h0 = jnp.stack([pack(h0_re), pack(h0_im)], axis=1)
    p3 = p.reshape(b * nc, L, p.shape[1])
    y, hre, him = _ssm(p3, uc_col, tmat, pmat, qmat, apow, h0, rows_blk, seg, tiles_per_seq)

    def last(h):
        return h.reshape(nb, b, nc, w)[:, :, nc - 1].transpose(1, 0, 2).reshape(b, nb * SSM_GB, C_STATE)

    return y.reshape(b * t, nb * LANES), last(hre), last(him)


def _glu_kernel(y_ref, w_ref, b_ref, o_ref):
    z = _gelu(y_ref[...])
    a = jnp.dot(z.astype(BF16), w_ref[...], preferred_element_type=F32) + b_ref[...]
    o_ref[...] = (z * _sigmoid(a)).astype(o_ref.dtype)


def _glu(y, w, b, tm):
    t, n = y.shape
    return pl.pallas_call(
        _glu_kernel,
        grid=(t // tm,),
        in_specs=[pl.BlockSpec((tm, n), lambda i: (i, 0)),
                  pl.BlockSpec((n, n), lambda i: (0, 0)),
                  pl.BlockSpec((1, n), lambda i: (0, 0))],
        out_specs=pl.BlockSpec((tm, n), lambda i: (i, 0)),
        out_shape=jax.ShapeDtypeStruct((t, n), BF16),
        compiler_params=_cparams(("arbitrary",)),
        name="glu",
    )(y, w, b.reshape(1, n))


def _merge_kernel(oa_ref, ob_ref, oc_ref, ga_ref, gb_ref, gc_ref, w_ref, o_ref):
    acc = None
    for n, (o, g) in enumerate(((oa_ref, ga_ref), (ob_ref, gb_ref), (oc_ref, gc_ref))):
        br = jnp.dot(o[...], w_ref[n], preferred_element_type=F32)
        t = _sigmoid(g[...].astype(F32)) * br
        acc = t if acc is None else acc + t
    o_ref[...] = acc.astype(o_ref.dtype)


def _merge(oa, ob, oc, gates, w, tm, tn):
    t, m = oa.shape
    d = w.shape[2]
    assert d % tn == 0

    def gspec(n):
        return pl.BlockSpec((tm, tn), lambda i, j: (i, (n * d) // tn + j))

    ospec = pl.BlockSpec((tm, m), lambda i, j: (i, 0))
    return pl.pallas_call(
        _merge_kernel,
        grid=(t // tm, d // tn),
        in_specs=[ospec, ospec, ospec, gspec(0), gspec(1), gspec(2),
                  pl.BlockSpec((N_BRANCH, m, tn), lambda i, j: (0, 0, j))],
        out_specs=pl.BlockSpec((tm, tn), lambda i, j: (i, j)),
        out_shape=jax.ShapeDtypeStruct((t, d), BF16),
        compiler_params=_cparams(("arbitrary", "arbitrary")),
        name="merge",
    )(oa, ob, oc, gates, gates, gates, w)


def _mmres_kernel(a_ref, w_ref, r_ref, o_ref):
    o_ref[...] = r_ref[...] + jnp.dot(a_ref[...], w_ref[...], preferred_element_type=F32)


def _mmres(a, w, res, tm, tn):
    t, k = a.shape
    n = w.shape[1]
    return pl.pallas_call(
        _mmres_kernel,
        grid=(t // tm, n // tn),
        in_specs=[pl.BlockSpec((tm, k), lambda i, j: (i, 0)),
                  pl.BlockSpec((k, tn), lambda i, j: (0, j)),
                  pl.BlockSpec((tm, tn), lambda i, j: (i, j))],
        out_specs=pl.BlockSpec((tm, tn), lambda i, j: (i, j)),
        out_shape=jax.ShapeDtypeStruct((t, n), F32),
        compiler_params=_cparams(("arbitrary", "arbitrary")),
        name="mmres",
    )(a, w, res)


def _ffn_in_kernel(x_ref, g_ref, wu_ref, wv_ref, wc_ref, bc_ref, st_ref, o_ref, tail_ref,
                   h_ref, up_ref, carry_ref, *, ns, ts, tiles_per_seq):
    i, j = pl.program_id(0), pl.program_id(1)
    pad = SUBLANES

    @pl.when(j == 0)
    def _():
        h_ref[...] = _rms(x_ref[...], g_ref[...]).astype(BF16)

    @pl.when((i == 0) & (j == 0))
    def _():
        carry_ref[...] = jnp.zeros(carry_ref.shape, F32)

    h = h_ref[...]
    u = jnp.dot(h, wu_ref[...], preferred_element_type=F32)
    tn = u.shape[1]
    up_ref[:, pad:, :] = u.reshape(ns, ts, tn)
    first = (i % tiles_per_seq) == 0
    up_ref[:, pad - 2:pad, :] = jnp.where(first, st_ref[...], carry_ref[j])
    last2 = up_ref[:, ts + pad - 2:ts + pad, :]
    carry_ref[j] = last2
    tail_ref[0] = last2
    c = bc_ref[...].reshape(1, 1, tn)
    for tap in range(CONV_W):
        c = c + up_ref[:, pad - 2 + tap:pad - 2 + tap + ts, :] * wc_ref[tap:tap + 1, :].reshape(1, 1, tn)
    gl = _gelu(c).reshape(ns * ts, tn)
    v = jnp.dot(h, wv_ref[...], preferred_element_type=F32)
    o_ref[...] = (gl * v).astype(o_ref.dtype)


def _ffn_in(x, g, w, wc, bc, state, ns, ts, tn):
    t, d = x.shape
    f = w.shape[1] // 2
    tm = ns * ts
    nj = f // tn
    seq_len = t // state.shape[0]
    tiles_per_seq = max(seq_len // tm, 1)
    kern = functools.partial(_ffn_in_kernel, ns=ns, ts=ts, tiles_per_seq=tiles_per_seq)
    return pl.pallas_call(
        kern,
        grid=(t // tm, nj),
        in_specs=[pl.BlockSpec((tm, d), lambda i, j: (i, 0)),
                  pl.BlockSpec((1, d), lambda i, j: (0, 0)),
                  pl.BlockSpec((d, tn), lambda i, j: (0, j)),
                  pl.BlockSpec((d, tn), lambda i, j: (0, nj + j)),
                  pl.BlockSpec((CONV_W, tn), lambda i, j: (0, j)),
                  pl.BlockSpec((1, tn), lambda i, j: (0, j)),
                  pl.BlockSpec((ns, CONV_W - 1, tn), lambda i, j: (i // tiles_per_seq, 0, j))],
        out_specs=[pl.BlockSpec((tm, tn), lambda i, j: (i, j)),
                   pl.BlockSpec((1, ns, CONV_W - 1, tn), lambda i, j: (i, 0, 0, j))],
        out_shape=[jax.ShapeDtypeStruct((t, f), BF16),
                   jax.ShapeDtypeStruct((t // tm, ns, CONV_W - 1, f), F32)],
        scratch_shapes=[pltpu.VMEM((tm, d), BF16),
                        pltpu.VMEM((ns, ts + SUBLANES, tn), F32),
                        pltpu.VMEM((nj, ns, CONV_W - 1, tn), F32)],
        compiler_params=_cparams(("arbitrary", "arbitrary")),
        name="ffn_in",
    )(x, g.reshape(1, d), w, w, wc, bc.reshape(1, f), state)


def _norm_kernel(x_ref, g_ref, o_ref):
    o_ref[...] = _rms(x_ref[...], g_ref[...])


def _norm(x, g, tm):
    t, d = x.shape
    return pl.pallas_call(
        _norm_kernel,
        grid=(t // tm,),
        in_specs=[pl.BlockSpec((tm, d), lambda i: (i, 0)), pl.BlockSpec((1, d), lambda i: (0, 0))],
        out_specs=pl.BlockSpec((tm, d), lambda i: (i, 0)),
        out_shape=jax.ShapeDtypeStruct((t, d), F32),
        compiler_params=_cparams(("arbitrary",)),
        name="final_norm",
    )(x, g.reshape(1, d))


def _row_tile(t, cap=1024):
    tm = min(t, cap)
    while t % tm:
        tm //= 2
    return tm


def _pack_w_in(w_in, d_model):
    mix = d_model // 2
    sizes = dict(qa=mix, ka=A_KV * HEAD_DIM, va=A_KV * HEAD_DIM, qi=IDX_HEADS * IDX_DIM, ki=IDX_DIM,
                 wi=IDX_HEADS, qb=mix, kb=mix, vb=mix, uc=mix, g=N_BRANCH * d_model)
    src_order = ["qa", "ka", "va", "qi", "ki", "wi", "qb", "kb", "vb", "uc", "g"]
    src, off = {}, 0
    for name in src_order:
        src[name] = (off, sizes[name])
        off += sizes[name]
    assert off == w_in.shape[1]
    tn = 1280
    dst_order = ["qa", "qb", "kb", "vb", "uc", "qi", "ka", "va", "ki", "wi"]
    cols, parts, off = {}, [], 0

    def pad_to_tile():
        nonlocal off
        total = -(-off // tn) * tn
        parts.append(jnp.zeros((w_in.shape[0], total - off), w_in.dtype))
        off = total

    for name in dst_order:
        s, n = src[name]
        cols[name] = off
        parts.append(w_in[:, s:s + n])
        off += n
    cols["kw"] = cols["ki"]
    pad_to_tile()
    n_main = off // tn
    s, n = src["g"]
    parts.append(w_in[:, s:s + n])
    off += n
    pad_to_tile()
    return jnp.concatenate(parts, axis=1).astype(BF16), cols, tn, n_main


def _to_heads_T(x, b, t):
    return x.reshape(b, t, x.shape[1]).transpose(0, 2, 1)


def _pad_axis(x, axis, size):
    pad = [(0, 0)] * x.ndim
    pad[axis] = (0, size - x.shape[axis])
    return jnp.pad(x, pad)


def _layer(x, pos, prm, cache, layer, is_prompt):
    b, t, d = x.shape
    mix = d // 2
    xt = x.reshape(b * t, d)
    tm = _row_tile(b * t)
    cols = prm["cols"]
    p, gates = _proj(xt, prm["norm1"], prm["w_in"], tm, prm["w_in_tn"], prm["w_in_main"])

    tabs = _rope_tables(pos)
    tm_r = _row_tile(t) if is_prompt else tm
    if not is_prompt:
        tabs = jnp.tile(tabs, (1, b, 1))
    qa, ka, kab, vab, qi, kw, kib = _rope(p, tabs, cols, tm_r)
    if is_prompt:
        sq, s_valid, q_pos0 = t, t, 0
        s_pad = -(-s_valid // DSA_TK) * DSA_TK
        k_all = _pad_axis(kab.reshape(b, t, -1), 1, s_pad)
        ki_all = _pad_axis(kib.reshape(b, t, LANES), 1, s_pad)
        vT = _pad_axis(vab.reshape(b, t, -1), 1, s_pad).transpose(0, 2, 1).reshape(b, A_KV, HEAD_DIM, s_pad)
        ones = jnp.broadcast_to((jnp.arange(DSA_VROWS - HEAD_DIM) == 0).astype(BF16)[None, None, :, None],
                                (b, A_KV, DSA_VROWS - HEAD_DIM, s_pad))
        vT = jnp.concatenate([vT, ones], axis=2).reshape(b, A_KV * DSA_VROWS, s_pad)
    else:
        past = cache["a_k"].shape[2]
        sq, s_valid, q_pos0 = DSA_QB, past + t, past
        k_all, vT, ki_all = _cache_prep(cache["a_k"], cache["a_v"], cache["a_kidx"], layer, kab, vab, kib, b, t)
    qaT = _pad_axis(_to_heads_T(qa, b, t), 2, sq)
    qiT = _pad_axis(_to_heads_T(qi, b, t), 2, sq)
    wiT = _pad_axis(_to_heads_T(kw[:, IDX_DIM:IDX_DIM + IDX_HEADS], b, t), 2, sq)
    topk = min(TOPK_MAX, s_valid // 4)
    oaT = _dsa(qaT, qiT, wiT, k_all, vT, ki_all, s_valid=s_valid, q_pos0=q_pos0, topk=topk)
    oa = oaT[:, :, :t].transpose(0, 2, 1).reshape(b * t, mix)

    nh = mix // HEAD_DIM
    if is_prompt:
        n_tiles = BAND_CHUNKS * CHUNK // BAND_QB + 1
        bias = _band_bias(prm["rel_bias"], BAND_CHUNKS * CHUNK, BAND_QB, 0, n_tiles * BAND_QB)
        ob = _band_prompt(p, bias, cols, b, t)
    else:
        nbc = cache["b_k"].shape[2]
        past = cache["a_k"].shape[2]
        bias = _band_bias(prm["rel_bias"], past, t, past - nbc, nbc + t)
        ob = _band_sample(p, cache["b_k"], cache["b_v"], layer, bias, cols, b, t)

    g_ssm = mix // C_GROUP
    if is_prompt:
        h0_re = jnp.zeros((b, g_ssm, C_STATE), F32)
        h0_im = h0_re
    else:
        h0_re, h0_im = cache["c_re"][layer], cache["c_im"][layer]
    yc, hr, hi = _ssm_apply(p, cols["uc"], b, t, h0_re, h0_im, prm["ssm"], is_prompt)
    oc = _glu(yc, prm["w_glu"], prm["b_glu"], tm)

    merged = _merge(oa, ob, oc, gates, prm["w_branch"], tm, 512)
    x1 = _mmres(merged, prm["w_out"], xt, _row_tile(b * t, 512), 2048)

    f = prm["w_conv"].shape[1]
    if is_prompt:
        state = jnp.zeros((b, CONV_W - 1, f), F32)
        ns, ts = 1, _row_tile(t)
    else:
        state = cache["ffn_conv"][layer].astype(F32)
        ns, ts = b, t
    act, tails = _ffn_in(x1, prm["norm2"], prm["w_ffn_in"], prm["w_conv"], prm["b_conv"], state, ns, ts, 512)
    x2 = _mmres(act, prm["w_down"], x1, tm, 512)
    if is_prompt:
        buf = tails.reshape(b, t // ts, CONV_W - 1, f)[:, -1]
    else:
        buf = tails[0]

    ka4 = ka.reshape(b, t, A_KV, HEAD_DIM)
    va4 = p[:, cols["va"]:cols["va"] + A_KV * HEAD_DIM].reshape(b, t, A_KV, HEAD_DIM)
    ki3 = kw[:, :IDX_DIM].reshape(b, t, IDX_DIM)
    nb = min(BAND_CHUNKS * CHUNK, t) if is_prompt else t
    pb = p.reshape(b, t, p.shape[1])[:, t - nb:]
    kb4 = pb[:, :, cols["kb"]:cols["kb"] + mix].reshape(b, nb, nh, HEAD_DIM)
    vb4 = pb[:, :, cols["vb"]:cols["vb"] + mix].reshape(b, nb, nh, HEAD_DIM)
    return x2.reshape(b, t, d), (ka4, va4, ki3, kb4, vb4, hr, hi, buf)


def kernel(x_prompt, x_sample, cache_a_k, cache_a_v, cache_a_kidx, cache_b_k, cache_b_v, state_c_re, state_c_im,
           state_ffn_conv, norm1_g, w_in, rel_bias, ssm_a_re, ssm_a_im, ssm_log_dt, ssm_b_re, ssm_b_im, ssm_c_re,
           ssm_c_im, ssm_d, w_glu, b_glu, w_branch, w_out, norm2_g, w_ffn_in, w_ffn_conv, b_ffn_conv, w_ffn_down,
           normf_g):
    depth = w_in.shape[0]
    d = x_prompt.shape[2]
    pos_p = jnp.arange(x_prompt.shape[1])
    pos_s = cache_a_k.shape[2] + jnp.arange(x_sample.shape[1])
    xp, xs = x_prompt, x_sample
    st_p, st_s = [], []
    caches = dict(a_k=cache_a_k, a_v=cache_a_v, a_kidx=cache_a_kidx, b_k=cache_b_k, b_v=cache_b_v,
                  c_re=state_c_re, c_im=state_c_im, ffn_conv=state_ffn_conv)
    for l in range(depth):
        w_in_l, cols, tn, n_main = _pack_w_in(w_in[l], d)
        prm = dict(norm1=norm1_g[l], w_in=w_in_l, cols=cols, w_in_tn=tn, w_in_main=n_main, rel_bias=rel_bias[l],
                   ssm=_ssm_weights(ssm_a_re[l], ssm_a_im[l], ssm_log_dt[l], ssm_b_re[l], ssm_b_im[l],
                                    ssm_c_re[l], ssm_c_im[l], ssm_d[l]),
                   w_glu=w_glu[l].astype(BF16), b_glu=b_glu[l], w_branch=w_branch[l].astype(BF16),
                   w_out=w_out[l].astype(BF16), norm2=norm2_g[l], w_ffn_in=w_ffn_in[l].astype(BF16),
                   w_conv=w_ffn_conv[l], b_conv=b_ffn_conv[l], w_down=w_ffn_down[l].astype(BF16))
        xp, sp = _layer(xp, pos_p, prm, None, l, True)
        xs, ss = _layer(xs, pos_s, prm, caches, l, False)
        st_p.append(sp)
        st_s.append(ss)
    bp, tp, _ = xp.shape
    bs, tsq, _ = xs.shape
    y_prompt = _norm(xp.reshape(bp * tp, d), normf_g, _row_tile(bp * tp)).reshape(bp, tp, d)
    y_sample = _norm(xs.reshape(bs * tsq, d), normf_g, _row_tile(bs * tsq)).reshape(bs, tsq, d)
    outs_p = [jnp.stack([s[i] for s in st_p]) for i in range(8)]
    outs_s = [jnp.stack([s[i] for s in st_s]) for i in range(8)]
    return (y_prompt, y_sample, *outs_p, *outs_s)
```

```python
import functools
import math

import jax
import jax.numpy as jnp
from jax import lax
from jax.experimental import pallas as pl
from jax.experimental.pallas import tpu as pltpu

F32 = jnp.float32
BF16 = jnp.bfloat16
I32 = jnp.int32

CHUNK = 64
EPS = 1e-6
ROPE_THETA = 500000.0
HEAD_DIM = 128
A_KV = 2
A_GROUP = 4
A_ROT = HEAD_DIM // 4
IDX_HEADS = 8
IDX_DIM = 64
IDX_ROT = IDX_DIM // 4
TOPK_MAX = 256
BAND_CHUNKS = 8
REL_CLIP = 256
C_GROUP = 16
C_STATE = 64
CONV_W = 3
N_BRANCH = 3
ATTN_SCALE = HEAD_DIM ** -0.5
IDX_SCALE = (IDX_DIM * IDX_HEADS) ** -0.5

LANES = 128
SUBLANES = 8
VMEM_LIMIT = 56 * 1024 * 1024

NEG_BIG = -1e30
KEY_NEG_INF = -2139095041
KEY_POS_INF = 2139095040
INT32_MIN = -2147483648

SSM_L = SUBLANES
SSM_GB = LANES // C_GROUP
DSA_TK = 512
DSA_TA = 512
DSA_VROWS = HEAD_DIM + 16
DSA_QB = 128
BAND_QB = 128


def _cparams(sem):
    return pltpu.CompilerParams(dimension_semantics=sem, vmem_limit_bytes=VMEM_LIMIT)


def _gelu(x):
    return 0.5 * x * (1.0 + jnp.tanh(math.sqrt(2.0 / math.pi) * (x + 0.044715 * (x * x * x))))


def _sigmoid(x):
    return 1.0 / (1.0 + jnp.exp(-x))


def _rms(x, g):
    ms = jnp.mean(x * x, axis=-1, keepdims=True)
    return (x * lax.rsqrt(ms + EPS)) * g


def _proj_kernel(x_ref, g_ref, w_ref, o_ref, og_ref, h_ref, *, n_main):
    j = pl.program_id(1)

    @pl.when(j == 0)
    def _():
        h_ref[...] = _rms(x_ref[...], g_ref[...]).astype(BF16)

    @pl.when(j < n_main)
    def _():
        o_ref[...] = jnp.dot(h_ref[...], w_ref[...], preferred_element_type=F32)

    @pl.when(j >= n_main)
    def _():
        og_ref[...] = jnp.dot(h_ref[...], w_ref[...], preferred_element_type=F32).astype(BF16)


def _proj(x, g, w, tm, tn, n_main):
    t, d = x.shape
    n = w.shape[1]
    nj = n // tn
    return pl.pallas_call(
        functools.partial(_proj_kernel, n_main=n_main),
        grid=(t // tm, nj),
        in_specs=[pl.BlockSpec((tm, d), lambda i, j: (i, 0)),
                  pl.BlockSpec((1, d), lambda i, j: (0, 0)),
                  pl.BlockSpec((d, tn), lambda i, j: (0, j))],
        out_specs=[pl.BlockSpec((tm, tn), lambda i, j: (i, jnp.minimum(j, n_main - 1))),
                   pl.BlockSpec((tm, tn), lambda i, j: (i, jnp.maximum(j - n_main, 0)))],
        out_shape=[jax.ShapeDtypeStruct((t, n_main * tn), F32),
                   jax.ShapeDtypeStruct((t, (nj - n_main) * tn), BF16)],
        scratch_shapes=[pltpu.VMEM((tm, d), BF16)],
        compiler_params=_cparams(("arbitrary", "arbitrary")),
        name="proj",
    )(x, g.reshape(1, d), w)


def _rot(x, c, sa, sb, half):
    n = x.shape[-1]
    return x * c + pltpu.roll(x, n - half, 1) * sa + pltpu.roll(x, half, 1) * sb


def _rope_kernel(qa_ref, ka_ref, va_ref, qi_ref, kw_ref, tab_ref,
                 qa_o, ka_o, kab_o, vab_o, qi_o, kw_o, kib_o):
    ca, saa, sba = tab_ref[0], tab_ref[1], tab_ref[2]
    ci, sai, sbi = tab_ref[3], tab_ref[4], tab_ref[5]
    ck, sak, sbk = tab_ref[6], tab_ref[7], tab_ref[8]
    for h in range(qa_ref.shape[1] // LANES):
        sl = slice(h * LANES, (h + 1) * LANES)
        qa_o[:, sl] = _rot(qa_ref[:, sl], ca, saa, sba, A_ROT // 2).astype(BF16)
    for h in range(ka_ref.shape[1] // LANES):
        sl = slice(h * LANES, (h + 1) * LANES)
        k = _rot(ka_ref[:, sl], ca, saa, sba, A_ROT // 2)
        ka_o[:, sl] = k
        kab_o[:, sl] = k.astype(BF16)
    vab_o[...] = va_ref[...].astype(BF16)
    for h in range(qi_ref.shape[1] // LANES):
        sl = slice(h * LANES, (h + 1) * LANES)
        qi_o[:, sl] = _rot(qi_ref[:, sl], ci, sai, sbi, IDX_ROT // 2).astype(BF16)
    kw = _rot(kw_ref[...], ck, sak, sbk, IDX_ROT // 2)
    kw_o[...] = kw
    kib_o[...] = kw.astype(BF16)


def _rope(p, tabs, cols, tm):
    t = p.shape[0]
    n_tab_blocks = tabs.shape[1] // tm
    d_qa, d_kv, d_qi = A_KV * A_GROUP * HEAD_DIM, A_KV * HEAD_DIM, IDX_HEADS * IDX_DIM

    def col(width, off):
        assert off % width == 0
        return pl.BlockSpec((tm, width), lambda i: (i, off // width))

    def out(width):
        return pl.BlockSpec((tm, width), lambda i: (i, 0))

    return pl.pallas_call(
        _rope_kernel,
        grid=(t // tm,),
        in_specs=[col(d_qa, cols["qa"]), col(d_kv, cols["ka"]), col(d_kv, cols["va"]),
                  col(d_qi, cols["qi"]), col(LANES, cols["kw"]),
                  pl.BlockSpec((9, tm, LANES), lambda i: (0, i % n_tab_blocks, 0))],
        out_specs=[out(d_qa), out(d_kv), out(d_kv), out(d_kv), out(d_qi), out(LANES), out(LANES)],
        out_shape=[jax.ShapeDtypeStruct((t, d_qa), BF16),
                   jax.ShapeDtypeStruct((t, d_kv), F32),
                   jax.ShapeDtypeStruct((t, d_kv), BF16),
                   jax.ShapeDtypeStruct((t, d_kv), BF16),
                   jax.ShapeDtypeStruct((t, d_qi), BF16),
                   jax.ShapeDtypeStruct((t, LANES), F32),
                   jax.ShapeDtypeStruct((t, LANES), BF16)],
        compiler_params=_cparams(("arbitrary",)),
        name="rope",
    )(p, p, p, p, p, tabs)


def _rope_tables(pos):
    pos = pos.astype(F32)[:, None]

    def cs(rot):
        half = rot // 2
        inv = jnp.float32(ROPE_THETA) ** (-jnp.arange(half, dtype=F32) / half)
        ang = pos * inv[None, :]
        return jnp.cos(ang), jnp.sin(ang)

    def tabs(rot, width):
        half = rot // 2
        c, s = cs(rot)
        n = pos.shape[0]
        one = jnp.ones((n, width - rot), F32)
        zero = jnp.zeros((n, width - rot), F32)
        zh = jnp.zeros((n, half), F32)
        return (jnp.concatenate([c, c, one], 1), jnp.concatenate([-s, zh, zero], 1),
                jnp.concatenate([zh, s, zero], 1))

    ca, saa, sba = tabs(A_ROT, HEAD_DIM)
    c64, sa64, sb64 = tabs(IDX_ROT, IDX_DIM)
    n = pos.shape[0]
    one64, zero64 = jnp.ones((n, IDX_DIM), F32), jnp.zeros((n, IDX_DIM), F32)
    return jnp.stack([ca, saa, sba,
                      jnp.concatenate([c64, c64], 1), jnp.concatenate([sa64, sa64], 1),
                      jnp.concatenate([sb64, sb64], 1),
                      jnp.concatenate([c64, one64], 1), jnp.concatenate([sa64, zero64], 1),
                      jnp.concatenate([sb64, zero64], 1)])


def _dsa_kernel(qaT_ref, qiT_ref, wiT_ref, k_ref, vT_ref, ki_ref, o_ref, key_ref, acc_ref,
                s0_ref, s1_ref, p0_ref, p1_ref, a0_ref, a1_ref,
                *, s_valid, q_pos0, topk):
    qb, tk = DSA_QB, DSA_TK
    j = pl.program_id(1)
    q_first = q_pos0 + j * qb
    qpos = q_first + lax.broadcasted_iota(I32, (1, qb), 1)
    assert CHUNK == 64 and tk == 512
    n_adm = jnp.minimum(((qpos >> 6) + 1) * CHUNK, s_valid)
    n_max = jnp.minimum((((q_first + qb - 1) >> 6) + 1) * CHUNK, s_valid)
    nkt = (n_max + tk - 1) >> 9

    qi = qiT_ref[0]
    rhs = jnp.concatenate([qi[h * IDX_DIM:(h + 1) * IDX_DIM, :] for h in range(IDX_HEADS)], axis=1)
    rhs = jnp.concatenate([rhs, jnp.zeros((LANES - IDX_DIM, IDX_HEADS * qb), BF16)], axis=0)
    wi = wiT_ref[0]

    def p1(kt, carry):
        off = pl.multiple_of(kt * tk, tk)
        kid = ki_ref[0, pl.ds(off, tk), :]
        s_all = jnp.dot(kid, rhs, preferred_element_type=F32)
        acc = wi[0:1, :] * jnp.maximum(s_all[:, 0:qb], 0.0)
        for h in range(1, IDX_HEADS):
            acc = acc + wi[h:h + 1, :] * jnp.maximum(s_all[:, h * qb:(h + 1) * qb], 0.0)
        score = acc * IDX_SCALE
        kpos = off + lax.broadcasted_iota(I32, (tk, 1), 0)
        score = jnp.where(kpos < n_adm, score, -jnp.inf)
        u = pltpu.bitcast(score, I32)
        key_ref[pl.ds(off, tk), :] = u ^ (lax.shift_right_arithmetic(u, 31) & 0x7FFFFFFF)
        return carry

    lax.fori_loop(0, nkt, p1, 0)

    pairs = key_ref.shape[0] % (2 * tk) == 0
    if pairs:
        @pl.when(nkt % 2 == 1)
        def _():
            key_ref[pl.ds(pl.multiple_of(nkt * tk, tk), tk), :] = jnp.full((tk, qb), INT32_MIN, I32)

    n_trips, per_trip = ((nkt + 1) // 2, 2) if pairs else (nkt, 1)

    def count_ge(cand):
        def body(kt, cnt):
            for half in range(per_trip):
                blk = key_ref[pl.ds(pl.multiple_of((kt * per_trip + half) * tk, tk), tk), :]
                m = jnp.where(blk >= cand, 1, 0).astype(I32)
                cnt = cnt + jnp.sum(m.reshape(tk // SUBLANES, SUBLANES, qb), axis=0)
            return cnt

        cnt8 = lax.fori_loop(0, n_trips, body, jnp.zeros((SUBLANES, qb), I32))
        return jnp.sum(cnt8, axis=0, keepdims=True)

    def bit_body(i, carry):
        prefix, n_ge = carry
        cand_u = prefix | lax.shift_left(jnp.int32(1), 31 - i)
        cnt = count_ge(cand_u ^ INT32_MIN)
        ok = cnt >= topk
        return jnp.where(ok, cand_u, prefix), jnp.where(ok, cnt, n_ge)

    prefix, n_ge = lax.fori_loop(0, 32, bit_body, (jnp.zeros((1, qb), I32), jnp.broadcast_to(nkt * tk, (1, qb))))
    thr = prefix ^ INT32_MIN
    cut_ties = jnp.max(jnp.where((n_ge > topk) & (thr > KEY_NEG_INF), 1, 0)) > 0

    @pl.when(cut_ties)
    def _():
        r = lax.broadcasted_iota(I32, (tk, tk), 0)
        c = lax.broadcasted_iota(I32, (tk, tk), 1)
        tri = jnp.where(c <= r, 1.0, 0.0).astype(BF16)
        take = (topk - count_ge(thr + 1)).astype(F32)

        def body(kt, seen):
            off = pl.multiple_of(kt * tk, tk)
            blk = key_ref[pl.ds(off, tk), :]
            eq = blk == thr
            incl = jnp.dot(tri, jnp.where(eq, 1.0, 0.0).astype(BF16), preferred_element_type=F32)
            drop = eq & (seen + incl > take)
            key_ref[pl.ds(off, tk), :] = jnp.where(drop, thr - 1, blk)
            return seen + incl[tk - 1:tk, :]

        lax.fori_loop(0, nkt, body, jnp.zeros((1, qb), F32))

    thr_lo = jnp.maximum(thr, KEY_NEG_INF + 1)
    qa = qaT_ref[0]
    ng = A_GROUP * qb
    qg = [jnp.concatenate([qa[(g * A_GROUP + hh) * HEAD_DIM:(g * A_GROUP + hh + 1) * HEAD_DIM, :]
                           for hh in range(A_GROUP)], axis=1) for g in range(A_KV)]
    acc_ref[...] = jnp.zeros(acc_ref.shape, F32)

    c_exp = ATTN_SCALE * math.log2(math.e)

    ta = DSA_TA
    n_steps = nkt * (tk // ta)
    last = n_steps - 1
    bufs = ((s0_ref, p0_ref, a0_ref), (s1_ref, p1_ref, a1_ref))

    def logits(step, s_ref):
        off = pl.multiple_of(jnp.minimum(step, last) * ta, ta)
        for g in range(A_KV):
            kg = k_ref[0, pl.ds(off, ta), g * HEAD_DIM:(g + 1) * HEAD_DIM]
            s_ref[g] = jnp.dot(kg, qg[g], preferred_element_type=F32)

    def numerators(step, s_ref, p_ref, a_ref, ms):
        off = pl.multiple_of(jnp.minimum(step, last) * ta, ta)
        blk = key_ref[pl.ds(off, ta), :]
        sel = (blk >= thr_lo) & (blk < KEY_POS_INF) & (step <= last)
        mask = jnp.concatenate([jnp.where(sel, 0.0, NEG_BIG)] * A_GROUP, axis=1)
        new = []
        for g in range(A_KV):
            s = s_ref[g] + mask
            m_new = jnp.maximum(ms[g], jnp.max(s, axis=0, keepdims=True))
            a_ref[g] = jnp.broadcast_to(jnp.exp2((ms[g] - m_new) * c_exp), (SUBLANES, ng))
            p_ref[g] = jnp.exp2((s - m_new).astype(BF16) * c_exp)
            new.append(m_new)
        return tuple(new)

    def weighted_values(step, p_ref, a_ref):
        off = pl.multiple_of(jnp.clip(step, 0, last) * ta, ta)
        for g in range(A_KV):
            vg = vT_ref[0, g * DSA_VROWS:(g + 1) * DSA_VROWS, pl.ds(off, ta)]
            acc_ref[g] = a_ref[g, 0:1, :] * acc_ref[g] + jnp.dot(vg, p_ref[g], preferred_element_type=F32)

    p1_ref[...] = jnp.zeros(p1_ref.shape, BF16)
    a1_ref[...] = jnp.ones(a1_ref.shape, F32)
    logits(0, s0_ref)

    def two_steps(i2, ms):
        i = 2 * i2
        for par in range(2):
            cur, nxt = bufs[par], bufs[1 - par]
            logits(i + par + 1, nxt[0])
            ms = numerators(i + par, cur[0], cur[1], cur[2], ms)
            weighted_values(i + par - 1, nxt[1], nxt[2])
        return ms

    init = (jnp.full((1, ng), NEG_BIG, F32),) * A_KV
    n_pairs = (n_steps + 1) // 2
    lax.fori_loop(0, n_pairs, two_steps, init)
    weighted_values(2 * n_pairs - 1, p1_ref, a1_ref)
    for g in range(A_KV):
        o = acc_ref[g, 0:HEAD_DIM, :] / acc_ref[g, HEAD_DIM:HEAD_DIM + 1, :]
        for hh in range(A_GROUP):
            h = g * A_GROUP + hh
            o_ref[0, h * HEAD_DIM:(h + 1) * HEAD_DIM, :] = o[:, hh * qb:(hh + 1) * qb].astype(o_ref.dtype)


def _dsa(qaT, qiT, wiT, k, vT, ki, *, s_valid, q_pos0, topk):
    b, dq, sq = qaT.shape
    s_pad = k.shape[1]
    assert s_pad % DSA_TK == 0 and s_pad % DSA_TA == 0 and sq % DSA_QB == 0 and topk <= DSA_TK
    kern = functools.partial(_dsa_kernel, s_valid=s_valid, q_pos0=q_pos0, topk=topk)
    return pl.pallas_call(
        kern,
        grid=(b, sq // DSA_QB),
        in_specs=[pl.BlockSpec((1, dq, DSA_QB), lambda i, j: (i, 0, j)),
                  pl.BlockSpec((1, qiT.shape[1], DSA_QB), lambda i, j: (i, 0, j)),
                  pl.BlockSpec((1, IDX_HEADS, DSA_QB), lambda i, j: (i, 0, j)),
                  pl.BlockSpec((1, s_pad, k.shape[2]), lambda i, j: (i, 0, 0)),
                  pl.BlockSpec((1, vT.shape[1], s_pad), lambda i, j: (i, 0, 0)),
                  pl.BlockSpec((1, s_pad, LANES), lambda i, j: (i, 0, 0))],
        out_specs=pl.BlockSpec((1, dq, DSA_QB), lambda i, j: (i, 0, j)),
        out_shape=jax.ShapeDtypeStruct((b, dq, sq), BF16),
        scratch_shapes=[pltpu.VMEM((s_pad, DSA_QB), I32),
                        pltpu.VMEM((A_KV, DSA_VROWS, A_GROUP * DSA_QB), F32)]
                       + [pltpu.VMEM((A_KV, DSA_TA, A_GROUP * DSA_QB), F32)] * 2
                       + [pltpu.VMEM((A_KV, DSA_TA, A_GROUP * DSA_QB), BF16)] * 2
                       + [pltpu.VMEM((A_KV, SUBLANES, A_GROUP * DSA_QB), F32)] * 2,
        compiler_params=_cparams(("arbitrary", "arbitrary")),
        name="dsa",
    )(qaT, qiT, wiT, k, vT, ki)


def _cache_prep_kernel(ck_ref, cv_ref, cki_ref, kn_ref, vn_ref, kin_ref, k_o, vT_o, ki_o, *, n_cache_tiles):
    kt = pl.program_id(1)
    tk = k_o.shape[1]
    extra = DSA_VROWS - HEAD_DIM
    ones_rows = jnp.where(lax.broadcasted_iota(I32, (extra, tk), 0) == 0, 1.0, 0.0).astype(BF16)

    def emit(k_head, v_head, ki):
        for g in range(A_KV):
            k_o[0, :, g * HEAD_DIM:(g + 1) * HEAD_DIM] = k_head(g).astype(BF16)
            vT_o[0, g * DSA_VROWS:g * DSA_VROWS + HEAD_DIM, :] = v_head(g).astype(F32).T.astype(BF16)
            vT_o[0, g * DSA_VROWS + HEAD_DIM:(g + 1) * DSA_VROWS, :] = ones_rows
        ki_o[0] = ki

    @pl.when(kt < n_cache_tiles)
    def _():
        ki = jnp.concatenate([cki_ref[0, 0], jnp.zeros((tk, LANES - IDX_DIM), F32)], axis=1)
        emit(lambda g: ck_ref[0, 0, :, g, :], lambda g: cv_ref[0, 0, :, g, :], ki.astype(BF16))

    @pl.when(kt >= n_cache_tiles)
    def _():
        t = kn_ref.shape[0]
        rows = lambda a: jnp.concatenate([a, jnp.zeros((tk - t, a.shape[1]), a.dtype)], axis=0)
        kn, vn = rows(kn_ref[...]), rows(vn_ref[...])
        head = lambda a: lambda g: a[:, g * HEAD_DIM:(g + 1) * HEAD_DIM]
        emit(head(kn), head(vn), rows(kin_ref[...]))


def _cache_prep(cache_k, cache_v, cache_ki, layer, kn, vn, kin, b, t):
    tk = DSA_TK
    past = cache_k.shape[2]
    assert past % tk == 0 and t <= tk
    nct = past // tk
    s_pad = past + tk
    cidx = lambda i, j: jnp.minimum(j, nct - 1)
    return pl.pallas_call(
        functools.partial(_cache_prep_kernel, n_cache_tiles=nct),
        grid=(b, nct + 1),
        in_specs=[pl.BlockSpec((1, 1, tk, A_KV, HEAD_DIM), lambda i, j: (layer, i, cidx(i, j), 0, 0)),
                  pl.BlockSpec((1, 1, tk, A_KV, HEAD_DIM), lambda i, j: (layer, i, cidx(i, j), 0, 0)),
                  pl.BlockSpec((1, 1, tk, IDX_DIM), lambda i, j: (layer, i, cidx(i, j), 0)),
                  pl.BlockSpec((t, A_KV * HEAD_DIM), lambda i, j: (i, 0)),
                  pl.BlockSpec((t, A_KV * HEAD_DIM), lambda i, j: (i, 0)),
                  pl.BlockSpec((t, LANES), lambda i, j: (i, 0))],
        out_specs=[pl.BlockSpec((1, tk, A_KV * HEAD_DIM), lambda i, j: (i, j, 0)),
                   pl.BlockSpec((1, A_KV * DSA_VROWS, tk), lambda i, j: (i, 0, j)),
                   pl.BlockSpec((1, tk, LANES), lambda i, j: (i, j, 0))],
        out_shape=[jax.ShapeDtypeStruct((b, s_pad, A_KV * HEAD_DIM), BF16),
                   jax.ShapeDtypeStruct((b, A_KV * DSA_VROWS, s_pad), BF16),
                   jax.ShapeDtypeStruct((b, s_pad, LANES), BF16)],
        compiler_params=_cparams(("arbitrary", "arbitrary")),
        name="cache_prep",
    )(cache_k, cache_v, cache_ki, kn, vn, kin)


def _band_heads(q, k_head, v_head, bias_ref, n_hidden, o_ref, s_ref):
    nh = q.shape[1] // HEAD_DIM
    tk = bias_ref.shape[2]
    hidden = None if n_hidden is None else lax.broadcasted_iota(I32, (1, tk), 1) < n_hidden
    for h in range(nh):
        sl = slice(h * HEAD_DIM, (h + 1) * HEAD_DIM)
        s = lax.dot_general(q[:, sl].astype(BF16), k_head(h), (((1,), (1,)), ((), ())),
                            preferred_element_type=F32) * ATTN_SCALE + bias_ref[h]
        if hidden is not None:
            s = jnp.where(hidden, NEG_BIG, s)
        s_ref[h] = s
    ms = [jnp.max(s_ref[h], axis=1, keepdims=True) for h in range(nh)]
    ls = []
    for h in range(nh):
        p = jnp.exp(s_ref[h] - ms[h])
        s_ref[h] = p
        ls.append(jnp.sum(p, axis=1, keepdims=True))
    for h in range(nh):
        sl = slice(h * HEAD_DIM, (h + 1) * HEAD_DIM)
        w = (s_ref[h] / ls[h]).astype(BF16)
        o_ref[:, sl] = jnp.dot(w, v_head(h), preferred_element_type=F32).astype(o_ref.dtype)


def _band_prompt_kernel(*refs, n_tiles):
    q_ref = refs[0]
    k_refs = refs[1:1 + n_tiles]
    v_refs = refs[1 + n_tiles:1 + 2 * n_tiles]
    bias_ref = refs[1 + 2 * n_tiles]
    o_ref = refs[2 + 2 * n_tiles]
    s_ref = refs[3 + 2 * n_tiles]
    j = pl.program_id(1)
    n_hidden = jnp.maximum(n_tiles - 1 - j, 0) * q_ref.shape[0]
    k_all = jnp.concatenate([r[...] for r in k_refs], axis=0).astype(BF16)
    v_all = jnp.concatenate([r[...] for r in v_refs], axis=0).astype(BF16)
    head = lambda a: lambda h: a[:, h * HEAD_DIM:(h + 1) * HEAD_DIM]
    _band_heads(q_ref[...], head(k_all), head(v_all), bias_ref, n_hidden, o_ref, s_ref)


def _band_prompt(p, bias, cols, batch, seq):
    qb = BAND_QB
    n_tiles = BAND_CHUNKS * CHUNK // qb + 1
    width = bias.shape[0] * HEAD_DIM
    nq = seq // qb
    cq, ck, cv = cols["qb"] // width, cols["kb"] // width, cols["vb"] // width
    assert cols["qb"] % width == 0 and cols["kb"] % width == 0 and cols["vb"] % width == 0

    def kv_spec(c, i):
        return pl.BlockSpec((qb, width), lambda b, j: (b * nq + jnp.maximum(j - (n_tiles - 1) + i, 0), c))

    return pl.pallas_call(
        functools.partial(_band_prompt_kernel, n_tiles=n_tiles),
        grid=(batch, nq),
        in_specs=([pl.BlockSpec((qb, width), lambda b, j: (b * nq + j, cq))]
                  + [kv_spec(ck, i) for i in range(n_tiles)]
                  + [kv_spec(cv, i) for i in range(n_tiles)]
                  + [pl.BlockSpec(bias.shape, lambda b, j: (0, 0, 0))]),
        out_specs=pl.BlockSpec((qb, width), lambda b, j: (b * nq + j, 0)),
        out_shape=jax.ShapeDtypeStruct((batch * seq, width), BF16),
        compiler_params=_cparams(("arbitrary", "arbitrary")),
        scratch_shapes=[pltpu.VMEM(bias.shape, F32)],
        name="band_prompt",
    )(*([p] * (1 + 2 * n_tiles)), bias)


def _band_sample_kernel(q_ref, k_ref, v_ref, ck_ref, cv_ref, bias_ref, o_ref, s_ref):
    def head(cache_ref, new_ref):
        return lambda h: jnp.concatenate(
            [cache_ref[0, 0, :, h, :], new_ref[:, h * HEAD_DIM:(h + 1) * HEAD_DIM]], axis=0).astype(BF16)

    _band_heads(q_ref[...], head(ck_ref, k_ref), head(cv_ref, v_ref), bias_ref, None, o_ref, s_ref)


def _band_sample(p, cache_k, cache_v, layer, bias, cols, batch, t):
    nh = bias.shape[0]
    width = nh * HEAD_DIM
    nbc = cache_k.shape[2]
    cq, ck, cv = cols["qb"] // width, cols["kb"] // width, cols["vb"] // width
    return pl.pallas_call(
        _band_sample_kernel,
        grid=(batch,),
        in_specs=[pl.BlockSpec((t, width), lambda b: (b, cq)),
                  pl.BlockSpec((t, width), lambda b: (b, ck)),
                  pl.BlockSpec((t, width), lambda b: (b, cv)),
                  pl.BlockSpec((1, 1, nbc, nh, HEAD_DIM), lambda b: (layer, b, 0, 0, 0)),
                  pl.BlockSpec((1, 1, nbc, nh, HEAD_DIM), lambda b: (layer, b, 0, 0, 0)),
                  pl.BlockSpec(bias.shape, lambda b: (0, 0, 0))],
        out_specs=pl.BlockSpec((t, width), lambda b: (b, 0)),
        out_shape=jax.ShapeDtypeStruct((batch * t, width), BF16),
        compiler_params=_cparams(("arbitrary",)),
        scratch_shapes=[pltpu.VMEM(bias.shape, F32)],
        name="band_sample",
    )(p, p, p, cache_k, cache_v, bias)


def _band_bias(table, q0, nq, k0, nk):
    n = nq + nk
    j = jnp.arange(n)
    g = table[:, jnp.clip(j + (q0 - k0 - nk + 1), -REL_CLIP, REL_CLIP) + REL_CLIP].astype(F32)
    hank = jnp.tile(g, (1, nq + 1))[:, :nq * (n + 1)].reshape(-1, nq, n + 1)[:, :, :nk]
    bias = hank[:, :, ::-1]
    qc = (q0 + jnp.arange(nq))[:, None] // CHUNK
    kc = (k0 + jnp.arange(nk))[None, :] // CHUNK
    vis = (kc <= qc) & (kc >= qc - BAND_CHUNKS)
    return jnp.where(vis[None], bias, NEG_BIG)


def _ssm_kernel(u_ref, t_ref, p_ref, q_ref, a_ref, h0_ref, y_ref, hre_o, him_o, sre_ref, sim_ref,
                *, seg, tiles_per_seq):
    L = SSM_L
    x = jnp.concatenate([u_ref[:, s, :] for s in range(L)], axis=1).astype(BF16)
    sre_ref[...] = jnp.dot(x, p_ref[0, 0], preferred_element_type=F32)
    sim_ref[...] = jnp.dot(x, p_ref[0, 1], preferred_element_type=F32)
    rows, w = sre_ref.shape
    kseg = lax.broadcasted_iota(I32, (SUBLANES, 1), 0) % seg
    apk_re, apk_im = a_ref[0, 0, 0:SUBLANES, :], a_ref[0, 1, 0:SUBLANES, :]

    def cmad(x_re, x_im, a_re, a_im, y_re, y_im):
        return x_re + a_re * y_re - a_im * y_im, x_im + a_re * y_im + a_im * y_re

    def tile(i, carry):
        r = pl.ds(pl.multiple_of(i * SUBLANES, SUBLANES), SUBLANES)
        x_re, x_im = sre_ref[r, :], sim_ref[r, :]
        for n, d in enumerate((1, 2, 4)):
            if d < seg:
                ad_re = a_ref[0, 0, SUBLANES + n:SUBLANES + n + 1, :]
                ad_im = a_ref[0, 1, SUBLANES + n:SUBLANES + n + 1, :]
                sh_re = jnp.where(kseg >= d, pltpu.roll(x_re, d, 0), 0.0)
                sh_im = jnp.where(kseg >= d, pltpu.roll(x_im, d, 0), 0.0)
                x_re, x_im = cmad(x_re, x_im, ad_re, ad_im, sh_re, sh_im)
        if tiles_per_seq > 1:
            first = (i % tiles_per_seq) == 0
            hin_re = jnp.where(first, h0_ref[0, 0], carry[0])
            hin_im = jnp.where(first, h0_ref[0, 1], carry[1])
        else:
            hin_re, hin_im = h0_ref[0, 0, r, :], h0_ref[0, 1, r, :]
        inc_re, inc_im = cmad(x_re, x_im, apk_re, apk_im, hin_re, hin_im)
        sre_ref[r, :] = jnp.where(kseg == 0, hin_re, pltpu.roll(inc_re, 1, 0))
        sim_ref[r, :] = jnp.where(kseg == 0, hin_im, pltpu.roll(inc_im, 1, 0))
        hre_o[0, r, :] = inc_re
        him_o[0, r, :] = inc_im
        return inc_re[SUBLANES - 1:SUBLANES, :], inc_im[SUBLANES - 1:SUBLANES, :]

    zero = jnp.zeros((1, w), F32)
    lax.fori_loop(0, rows // SUBLANES, tile, (zero, zero))
    y = (jnp.dot(x, t_ref[0], preferred_element_type=F32)
         + jnp.dot(sre_ref[...].astype(BF16), q_ref[0, 0], preferred_element_type=F32)
         + jnp.dot(sim_ref[...].astype(BF16), q_ref[0, 1], preferred_element_type=F32))
    for t in range(L):
        y_ref[:, t, :] = y[:, t * LANES:(t + 1) * LANES]


def _ssm(p3, uc_col, tmat, pmat, qmat, apow, h0, rows_blk, seg, tiles_per_seq):
    rows_total = p3.shape[0]
    nb, _, _, w = pmat.shape
    kl = SSM_L * LANES
    h0_rows = h0.shape[2] // (rows_total // rows_blk)
    hspec = pl.BlockSpec((1, rows_blk, w), lambda i, r: (i, r, 0))
    return pl.pallas_call(
        functools.partial(_ssm_kernel, seg=seg, tiles_per_seq=tiles_per_seq),
        grid=(nb, rows_total // rows_blk),
        in_specs=[pl.BlockSpec((rows_blk, SSM_L, LANES), lambda i, r: (r, 0, uc_col // LANES + i)),
                  pl.BlockSpec((1, kl, kl), lambda i, r: (i, 0, 0)),
                  pl.BlockSpec((1, 2, kl, w), lambda i, r: (i, 0, 0, 0)),
                  pl.BlockSpec((1, 2, w, kl), lambda i, r: (i, 0, 0, 0)),
                  pl.BlockSpec((1, 2, 2 * SUBLANES, w), lambda i, r: (i, 0, 0, 0)),
                  pl.BlockSpec((1, 2, h0_rows, w), lambda i, r: (i, 0, r, 0))],
        out_specs=[pl.BlockSpec((rows_blk, SSM_L, LANES), lambda i, r: (r, 0, i)), hspec, hspec],
        out_shape=[jax.ShapeDtypeStruct((rows_total, SSM_L, nb * LANES), F32),
                   jax.ShapeDtypeStruct((nb, rows_total, w), F32),
                   jax.ShapeDtypeStruct((nb, rows_total, w), F32)],
        scratch_shapes=[pltpu.VMEM((rows_blk, w), F32), pltpu.VMEM((rows_blk, w), F32)],
        compiler_params=_cparams(("arbitrary", "arbitrary")),
        name="ssm",
    )(p3, tmat, pmat, qmat, apow, h0)


def _ssm_weights(a_re, a_im, log_dt, b_re, b_im, c_re, c_im, d):
    hp = lax.Precision.HIGHEST
    g, p = a_re.shape
    gc = b_re.shape[2]
    L = SSM_L
    a_re, a_im = a_re.astype(F32), a_im.astype(F32)
    dt = jnp.exp(log_dt.astype(F32))[:, None]
    mag = jnp.exp(dt * a_re)
    ab_re, ab_im = mag * jnp.cos(dt * a_im), mag * jnp.sin(dt * a_im)
    n_re = ab_re - 1.0
    n_im = ab_im
    den = a_re * a_re + a_im * a_im
    cc_re = (n_re * a_re + n_im * a_im) / den
    cc_im = (n_im * a_re - n_re * a_im) / den
    b_re, b_im = b_re.astype(F32), b_im.astype(F32)
    bb_re = cc_re[..., None] * b_re - cc_im[..., None] * b_im
    bb_im = cc_re[..., None] * b_im + cc_im[..., None] * b_re
    c_re, c_im = c_re.astype(F32), c_im.astype(F32)

    def cmul(x, y):
        return x[0] * y[0] - x[1] * y[1], x[0] * y[1] + x[1] * y[0]

    rep = lambda a: jnp.broadcast_to(a[None], (L,) + a.shape)
    pw_re, pw_im = lax.associative_scan(cmul, (rep(ab_re), rep(ab_im)), axis=0)
    pw_re = jnp.concatenate([jnp.ones((1, g, p), F32), pw_re], axis=0)
    pw_im = jnp.concatenate([jnp.zeros((1, g, p), F32), pw_im], axis=0)
    ca_re = c_re[None] * pw_re[:, :, None, :] - c_im[None] * pw_im[:, :, None, :]
    ca_im = c_re[None] * pw_im[:, :, None, :] + c_im[None] * pw_re[:, :, None, :]
    taps = (jnp.einsum("tgop,gpi->tgoi", ca_re[:L], bb_re, precision=hp)
            - jnp.einsum("tgop,gpi->tgoi", ca_im[:L], bb_im, precision=hp))
    taps = taps.at[0].add(jax.vmap(jnp.diag)(d.astype(F32)))
    s_idx = jnp.arange(L)[:, None]
    t_idx = jnp.arange(L)[None, :]
    lag = t_idx - s_idx
    tt = jnp.where((lag >= 0)[:, :, None, None, None], taps[jnp.clip(lag, 0, L - 1)], 0.0)
    gb = SSM_GB
    nb = g // gb
    n_x = L * gb * gc
    x = jnp.arange(n_x)
    x_grp, x_loc = (x // gc) % gb, (x // (gb * gc)) * gc + x % gc
    place_x = ((x_grp[None, :, None] == jnp.arange(gb)[:, None, None])
               & (x_loc[None, :, None] == jnp.arange(L * gc)[None, None, :])).astype(F32)
    q = jnp.arange(gb * p)
    place_q = ((q[None, :, None] // p == jnp.arange(gb)[:, None, None])
               & (q[None, :, None] % p == jnp.arange(p)[None, None, :])).astype(F32)
    blk = lambda a: a.reshape((nb, gb) + a.shape[1:])
    t_grp = blk(tt.transpose(2, 0, 4, 1, 3).reshape(g, L * gc, L * gc))
    tmat = jnp.einsum("gxr,bgrc,gyc->bxy", place_x, t_grp, place_x)
    rev = L - 1 - jnp.arange(L)
    pin_re = pw_re[rev][:, :, :, None] * bb_re[None] - pw_im[rev][:, :, :, None] * bb_im[None]
    pin_im = pw_re[rev][:, :, :, None] * bb_im[None] + pw_im[rev][:, :, :, None] * bb_re[None]
    to_p = lambda a: jnp.einsum("gxr,bgrp,gqp->bxq", place_x,
                                blk(a.transpose(1, 0, 3, 2).reshape(g, L * gc, p)), place_q)
    to_q = lambda a: jnp.einsum("gqp,bgpc,gyc->bqy", place_q,
                                blk(a.transpose(1, 3, 0, 2).reshape(g, p, L * gc)), place_x)
    pmat = jnp.stack([to_p(pin_re), to_p(pin_im)], axis=1)
    qmat = jnp.stack([to_q(ca_re[1:]), -to_q(ca_im[1:])], axis=1)
    rep8 = lambda a: jnp.broadcast_to(a[None], (SUBLANES,) + a.shape)
    apl_re, apl_im = lax.associative_scan(cmul, (rep8(pw_re[L]), rep8(pw_im[L])), axis=0)
    apl = jnp.stack([apl_re, apl_im]).reshape(2, SUBLANES, nb, gb * p)
    return tmat.astype(BF16), pmat.astype(BF16), qmat.astype(BF16), apl


def _ssm_apply(p, uc_col, b, t, h0_re, h0_im, weights, one_seq_per_step):
    tmat, pmat, qmat, apl = weights
    nb, w = pmat.shape[0], pmat.shape[3]
    L = SSM_L
    nc = t // L
    assert t % L == 0 and (nc % SUBLANES == 0 or SUBLANES % nc == 0)
    seg = min(nc, SUBLANES)
    if one_seq_per_step:
        assert nc % SUBLANES == 0
        rows_blk, tiles_per_seq, reps = nc, nc // SUBLANES, SUBLANES
    else:
        rows_blk, tiles_per_seq, reps = b * nc, 1, nc
        assert nc <= SUBLANES and rows_blk % SUBLANES == 0
    pw_rows = [k % seg for k in range(SUBLANES)] + [0, 1, 3] + [0] * (SUBLANES - 3)
    apow = jnp.stack([apl[:, k] for k in pw_rows], axis=1).transpose(2, 0, 1, 3)

    def pack(h):
        h = h.astype(F32).reshape(b, nb, w).transpose(1, 0, 2)
        return jnp.repeat(h, reps, axis=1)

    h0 = jnp.stack([pack(h0_re), pack(h0_im)], axis=1)
    p3 = p.reshape(b * nc, L, p.shape[1])
    y, hre, him = _ssm(p3, uc_col, tmat, pmat, qmat, apow, h0, rows_blk, seg, tiles_per_seq)

    def last(h):
        return h.reshape(nb, b, nc, w)[:, :, nc - 1].transpose(1, 0, 2).reshape(b, nb * SSM_GB, C_STATE)

    return y.reshape(b * t, nb * LANES), last(hre), last(him)


def _glu_kernel(y_ref, w_ref, b_ref, o_ref):
    z = _gelu(y_ref[...])
    a = jnp.dot(z.astype(BF16), w_ref[...], preferred_element_type=F32) + b_ref[...]
    o_ref[...] = (z * _sigmoid(a)).astype(o_ref.dtype)


def _glu(y, w, b, tm):
    t, n = y.shape
    return pl.pallas_call(
        _glu_kernel,
        grid=(t // tm,),
        in_specs=[pl.BlockSpec((tm, n), lambda i: (i, 0)),
                  pl.BlockSpec((n, n), lambda i: (0, 0)),
                  pl.BlockSpec((1, n), lambda i: (0, 0))],
        out_specs=pl.BlockSpec((tm, n), lambda i: (i, 0)),
        out_shape=jax.ShapeDtypeStruct((t, n), BF16),
        compiler_params=_cparams(("arbitrary",)),
        name="glu",
    )(y, w, b.reshape(1, n))


def _merge_kernel(oa_ref, ob_ref, oc_ref, ga_ref, gb_ref, gc_ref, w_ref, o_ref):
    acc = None
    for n, (o, g) in enumerate(((oa_ref, ga_ref), (ob_ref, gb_ref), (oc_ref, gc_ref))):
        br = jnp.dot(o[...], w_ref[n], preferred_element_type=F32)
        t = _sigmoid(g[...].astype(F32)) * br
        acc = t if acc is None else acc + t
    o_ref[...] = acc.astype(o_ref.dtype)


def _merge(oa, ob, oc, gates, w, tm, tn):
    t, m = oa.shape
    d = w.shape[2]
    assert d % tn == 0

    def gspec(n):
        return pl.BlockSpec((tm, tn), lambda i, j: (i, (n * d) // tn + j))

    ospec = pl.BlockSpec((tm, m), lambda i, j: (i, 0))
    return pl.pallas_call(
        _merge_kernel,
        grid=(t // tm, d // tn),
        in_specs=[ospec, ospec, ospec, gspec(0), gspec(1), gspec(2),
                  pl.BlockSpec((N_BRANCH, m, tn), lambda i, j: (0, 0, j))],
        out_specs=pl.BlockSpec((tm, tn), lambda i, j: (i, j)),
        out_shape=jax.ShapeDtypeStruct((t, d), BF16),
        compiler_params=_cparams(("arbitrary", "arbitrary")),
        name="merge",
    )(oa, ob, oc, gates, gates, gates, w)


def _mmres_kernel(a_ref, w_ref, r_ref, o_ref):
    o_ref[...] = r_ref[...] + jnp.dot(a_ref[...], w_ref[...], preferred_element_type=F32)


def _mmres(a, w, res, tm, tn):
    t, k = a.shape
    n = w.shape[1]
    return pl.pallas_call(
        _mmres_kernel,
        grid=(t // tm, n // tn),
        in_specs=[pl.BlockSpec((tm, k), lambda i, j: (i, 0)),
                  pl.BlockSpec((k, tn), lambda i, j: (0, j)),
                  pl.BlockSpec((tm, tn), lambda i, j: (i, j))],
        out_specs=pl.BlockSpec((tm, tn), lambda i, j: (i, j)),
        out_shape=jax.ShapeDtypeStruct((t, n), F32),
        compiler_params=_cparams(("arbitrary", "arbitrary")),
        name="mmres",
    )(a, w, res)


def _ffn_in_kernel(x_ref, g_ref, wu_ref, wv_ref, wc_ref, bc_ref, st_ref, o_ref, tail_ref,
                   h_ref, up_ref, carry_ref, *, ns, ts, tiles_per_seq):
    i, j = pl.program_id(0), pl.program_id(1)
    pad = SUBLANES

    @pl.when(j == 0)
    def _():
        h_ref[...] = _rms(x_ref[...], g_ref[...]).astype(BF16)

    @pl.when((i == 0) & (j == 0))
    def _():
        carry_ref[...] = jnp.zeros(carry_ref.shape, F32)

    h = h_ref[...]
    u = jnp.dot(h, wu_ref[...], preferred_element_type=F32)
    tn = u.shape[1]
    up_ref[:, pad:, :] = u.reshape(ns, ts, tn)
    first = (i % tiles_per_seq) == 0
    up_ref[:, pad - 2:pad, :] = jnp.where(first, st_ref[...], carry_ref[j])
    last2 = up_ref[:, ts + pad - 2:ts + pad, :]
    carry_ref[j] = last2
    tail_ref[0] = last2
    c = bc_ref[...].reshape(1, 1, tn)
    for tap in range(CONV_W):
        c = c + up_ref[:, pad - 2 + tap:pad - 2 + tap + ts, :] * wc_ref[tap:tap + 1, :].reshape(1, 1, tn)
    gl = _gelu(c).reshape(ns * ts, tn)
    v = jnp.dot(h, wv_ref[...], preferred_element_type=F32)
    o_ref[...] = (gl * v).astype(o_ref.dtype)


def _ffn_in(x, g, w, wc, bc, state, ns, ts, tn):
    t, d = x.shape
    f = w.shape[1] // 2
    tm = ns * ts
    nj = f // tn
    seq_len = t // state.shape[0]
    tiles_per_seq = max(seq_len // tm, 1)
    kern = functools.partial(_ffn_in_kernel, ns=ns, ts=ts, tiles_per_seq=tiles_per_seq)
    return pl.pallas_call(
        kern,
        grid=(t // tm, nj),
        in_specs=[pl.BlockSpec((tm, d), lambda i, j: (i, 0)),
                  pl.BlockSpec((1, d), lambda i, j: (0, 0)),
                  pl.BlockSpec((d, tn), lambda i, j: (0, j)),
                  pl.BlockSpec((d, tn), lambda i, j: (0, nj + j)),
                  pl.BlockSpec((CONV_W, tn), lambda i, j: (0, j)),
                  pl.BlockSpec((1, tn), lambda i, j: (0, j)),
                  pl.BlockSpec((ns, CONV_W - 1, tn), lambda i, j: (i // tiles_per_seq, 0, j))],
        out_specs=[pl.BlockSpec((tm, tn), lambda i, j: (i, j)),
                   pl.BlockSpec((1, ns, CONV_W - 1, tn), lambda i, j: (i, 0, 0, j))],
        out_shape=[jax.ShapeDtypeStruct((t, f), BF16),
                   jax.ShapeDtypeStruct((t // tm, ns, CONV_W - 1, f), F32)],
        scratch_shapes=[pltpu.VMEM((tm, d), BF16),
                        pltpu.VMEM((ns, ts + SUBLANES, tn), F32),
                        pltpu.VMEM((nj, ns, CONV_W - 1, tn), F32)],
        compiler_params=_cparams(("arbitrary", "arbitrary")),
        name="ffn_in",
    )(x, g.reshape(1, d), w, w, wc, bc.reshape(1, f), state)


def _norm_kernel(x_ref, g_ref, o_ref):
    o_ref[...] = _rms(x_ref[...], g_ref[...])


def _norm(x, g, tm):
    t, d = x.shape
    return pl.pallas_call(
        _norm_kernel,
        grid=(t // tm,),
        in_specs=[pl.BlockSpec((tm, d), lambda i: (i, 0)), pl.BlockSpec((1, d), lambda i: (0, 0))],
        out_specs=pl.BlockSpec((tm, d), lambda i: (i, 0)),
        out_shape=jax.ShapeDtypeStruct((t, d), F32),
        compiler_params=_cparams(("arbitrary",)),
        name="final_norm",
    )(x, g.reshape(1, d))


def _row_tile(t, cap=1024):
    tm = min(t, cap)
    while t % tm:
        tm //= 2
    return tm


def _pack_w_in(w_in, d_model):
    mix = d_model // 2
    sizes = dict(qa=mix, ka=A_KV * HEAD_DIM, va=A_KV * HEAD_DIM, qi=IDX_HEADS * IDX_DIM, ki=IDX_DIM,
                 wi=IDX_HEADS, qb=mix, kb=mix, vb=mix, uc=mix, g=N_BRANCH * d_model)
    src_order = ["qa", "ka", "va", "qi", "ki", "wi", "qb", "kb", "vb", "uc", "g"]
    src, off = {}, 0
    for name in src_order:
        src[name] = (off, sizes[name])
        off += sizes[name]
    assert off == w_in.shape[1]
    tn = 1280
    dst_order = ["qa", "qb", "kb", "vb", "uc", "qi", "ka", "va", "ki", "wi"]
    cols, parts, off = {}, [], 0

    def pad_to_tile():
        nonlocal off
        total = -(-off // tn) * tn
        parts.append(jnp.zeros((w_in.shape[0], total - off), w_in.dtype))
        off = total

    for name in dst_order:
        s, n = src[name]
        cols[name] = off
        parts.append(w_in[:, s:s + n])
        off += n
    cols["kw"] = cols["ki"]
    pad_to_tile()
    n_main = off // tn
    s, n = src["g"]
    parts.append(w_in[:, s:s + n])
    off += n
    pad_to_tile()
    return jnp.concatenate(parts, axis=1).astype(BF16), cols, tn, n_main


def _to_heads_T(x, b, t):
    return x.reshape(b, t, x.shape[1]).transpose(0, 2, 1)


def _pad_axis(x, axis, size):
    pad = [(0, 0)] * x.ndim
    pad[axis] = (0, size - x.shape[axis])
    return jnp.pad(x, pad)


def _layer(x, pos, prm, cache, layer, is_prompt):
    b, t, d = x.shape
    mix = d // 2
    xt = x.reshape(b * t, d)
    tm = _row_tile(b * t)
    cols = prm["cols"]
    p, gates = _proj(xt, prm["norm1"], prm["w_in"], tm, prm["w_in_tn"], prm["w_in_main"])

    tabs = _rope_tables(pos)
    tm_r = _row_tile(t) if is_prompt else tm
    if not is_prompt:
        tabs = jnp.tile(tabs, (1, b, 1))
    qa, ka, kab, vab, qi, kw, kib = _rope(p, tabs, cols, tm_r)
    if is_prompt:
        sq, s_valid, q_pos0 = t, t, 0
        s_pad = -(-s_valid // DSA_TK) * DSA_TK
        k_all = _pad_axis(kab.reshape(b, t, -1), 1, s_pad)
        ki_all = _pad_axis(kib.reshape(b, t, LANES), 1, s_pad)
        vT = _pad_axis(vab.reshape(b, t, -1), 1, s_pad).transpose(0, 2, 1).reshape(b, A_KV, HEAD_DIM, s_pad)
        ones = jnp.broadcast_to((jnp.arange(DSA_VROWS - HEAD_DIM) == 0).astype(BF16)[None, None, :, None],
                                (b, A_KV, DSA_VROWS - HEAD_DIM, s_pad))
        vT = jnp.concatenate([vT, ones], axis=2).reshape(b, A_KV * DSA_VROWS, s_pad)
    else:
        past = cache["a_k"].shape[2]
        sq, s_valid, q_pos0 = DSA_QB, past + t, past
        k_all, vT, ki_all = _cache_prep(cache["a_k"], cache["a_v"], cache["a_kidx"], layer, kab, vab, kib, b, t)
    qaT = _pad_axis(_to_heads_T(qa, b, t), 2, sq)
    qiT = _pad_axis(_to_heads_T(qi, b, t), 2, sq)
    wiT = _pad_axis(_to_heads_T(kw[:, IDX_DIM:IDX_DIM + IDX_HEADS], b, t), 2, sq)
    topk = min(TOPK_MAX, s_valid // 4)
    oaT = _dsa(qaT, qiT, wiT, k_all, vT, ki_all, s_valid=s_valid, q_pos0=q_pos0, topk=topk)
    oa = oaT[:, :, :t].transpose(0, 2, 1).reshape(b * t, mix)

    nh = mix // HEAD_DIM
    if is_prompt:
        n_tiles = BAND_CHUNKS * CHUNK // BAND_QB + 1
        bias = _band_bias(prm["rel_bias"], BAND_CHUNKS * CHUNK, BAND_QB, 0, n_tiles * BAND_QB)
        ob = _band_prompt(p, bias, cols, b, t)
    else:
        nbc = cache["b_k"].shape[2]
        past = cache["a_k"].shape[2]
        bias = _band_bias(prm["rel_bias"], past, t, past - nbc, nbc + t)
        ob = _band_sample(p, cache["b_k"], cache["b_v"], layer, bias, cols, b, t)

    g_ssm = mix // C_GROUP
    if is_prompt:
        h0_re = jnp.zeros((b, g_ssm, C_STATE), F32)
        h0_im = h0_re
    else:
        h0_re, h0_im = cache["c_re"][layer], cache["c_im"][layer]
    yc, hr, hi = _ssm_apply(p, cols["uc"], b, t, h0_re, h0_im, prm["ssm"], is_prompt)
    oc = _glu(yc, prm["w_glu"], prm["b_glu"], tm)

    merged = _merge(oa, ob, oc, gates, prm["w_branch"], tm, 1024)
    x1 = _mmres(merged, prm["w_out"], xt, _row_tile(b * t, 512), 2048)

    f = prm["w_conv"].shape[1]
    if is_prompt:
        state = jnp.zeros((b, CONV_W - 1, f), F32)
        ns, ts = 1, _row_tile(t)
    else:
        state = cache["ffn_conv"][layer].astype(F32)
        ns, ts = b, t
    act, tails = _ffn_in(x1, prm["norm2"], prm["w_ffn_in"], prm["w_conv"], prm["b_conv"], state, ns, ts, 512)
    x2 = _mmres(act, prm["w_down"], x1, tm, 512)
    if is_prompt:
        buf = tails.reshape(b, t // ts, CONV_W - 1, f)[:, -1]
    else:
        buf = tails[0]

    ka4 = ka.reshape(b, t, A_KV, HEAD_DIM)
    va4 = p[:, cols["va"]:cols["va"] + A_KV * HEAD_DIM].reshape(b, t, A_KV, HEAD_DIM)
    ki3 = kw[:, :IDX_DIM].reshape(b, t, IDX_DIM)
    nb = min(BAND_CHUNKS * CHUNK, t) if is_prompt else t
    pb = p.reshape(b, t, p.shape[1])[:, t - nb:]
    kb4 = pb[:, :, cols["kb"]:cols["kb"] + mix].reshape(b, nb, nh, HEAD_DIM)
    vb4 = pb[:, :, cols["vb"]:cols["vb"] + mix].reshape(b, nb, nh, HEAD_DIM)
    return x2.reshape(b, t, d), (ka4, va4, ki3, kb4, vb4, hr, hi, buf)


def kernel(x_prompt, x_sample, cache_a_k, cache_a_v, cache_a_kidx, cache_b_k, cache_b_v, state_c_re, state_c_im,
           state_ffn_conv, norm1_g, w_in, rel_bias, ssm_a_re, ssm_a_im, ssm_log_dt, ssm_b_re, ssm_b_im, ssm_c_re,
           ssm_c_im, ssm_d, w_glu, b_glu, w_branch, w_out, norm2_g, w_ffn_in, w_ffn_conv, b_ffn_conv, w_ffn_down,
           normf_g):
    depth = w_in.shape[0]
    d = x_prompt.shape[2]
    pos_p = jnp.arange(x_prompt.shape[1])
    pos_s = cache_a_k.shape[2] + jnp.arange(x_sample.shape[1])
    xp, xs = x_prompt, x_sample
    st_p, st_s = [], []
    caches = dict(a_k=cache_a_k, a_v=cache_a_v, a_kidx=cache_a_kidx, b_k=cache_b_k, b_v=cache_b_v,
                  c_re=state_c_re, c_im=state_c_im, ffn_conv=state_ffn_conv)
    for l in range(depth):
        w_in_l, cols, tn, n_main = _pack_w_in(w_in[l], d)
        prm = dict(norm1=norm1_g[l], w_in=w_in_l, cols=cols, w_in_tn=tn, w_in_main=n_main, rel_bias=rel_bias[l],
                   ssm=_ssm_weights(ssm_a_re[l], ssm_a_im[l], ssm_log_dt[l], ssm_b_re[l], ssm_b_im[l],
                                    ssm_c_re[l], ssm_c_im[l], ssm_d[l]),
                   w_glu=w_glu[l].astype(BF16), b_glu=b_glu[l], w_branch=w_branch[l].astype(BF16),
                   w_out=w_out[l].astype(BF16), norm2=norm2_g[l], w_ffn_in=w_ffn_in[l].astype(BF16),
                   w_conv=w_ffn_conv[l], b_conv=b_ffn_conv[l], w_down=w_ffn_down[l].astype(BF16))
        xp, sp = _layer(xp, pos_p, prm, None, l, True)
        xs, ss = _layer(xs, pos_s, prm, caches, l, False)
        st_p.append(sp)
        st_s.append(ss)
    bp, tp, _ = xp.shape
    bs, tsq, _ = xs.shape
    y_prompt = _norm(xp.reshape(bp * tp, d), normf_g, _row_tile(bp * tp)).reshape(bp, tp, d)
    y_sample = _norm(xs.reshape(bs * tsq, d), normf_g, _row_tile(bs * tsq)).reshape(bs, tsq, d)
    outs_p = [jnp.stack([s[i] for s in st_p]) for i in range(8)]
    outs_s = [jnp.stack([s[i] for s in st_s]) for i in range(8)]
    return (y_prompt, y_sample, *outs_p, *outs_s)
```

```python
import functools
import math

import jax
import jax.numpy as jnp
from jax import lax
from jax.experimental import pallas as pl
from jax.experimental.pallas import tpu as pltpu

F32 = jnp.float32
BF16 = jnp.bfloat16
I32 = jnp.int32

CHUNK = 64
EPS = 1e-6
ROPE_THETA = 500000.0
HEAD_DIM = 128
A_KV = 2
A_GROUP = 4
A_ROT = HEAD_DIM // 4
IDX_HEADS = 8
IDX_DIM = 64
IDX_ROT = IDX_DIM // 4
TOPK_MAX = 256
BAND_CHUNKS = 8
REL_CLIP = 256
C_GROUP = 16
C_STATE = 64
CONV_W = 3
N_BRANCH = 3
ATTN_SCALE = HEAD_DIM ** -0.5
IDX_SCALE = (IDX_DIM * IDX_HEADS) ** -0.5

LANES = 128
SUBLANES = 8
VMEM_LIMIT = 56 * 1024 * 1024

NEG_BIG = -1e30
KEY_NEG_INF = -2139095041
KEY_POS_INF = 2139095040
INT32_MIN = -2147483648

SSM_L = SUBLANES
SSM_GB = LANES // C_GROUP
DSA_TK = 512
DSA_TA = 512
DSA_VROWS = HEAD_DIM + 16
DSA_QB = 128
BAND_QB = 256


def _cparams(sem):
    return pltpu.CompilerParams(dimension_semantics=sem, vmem_limit_bytes=VMEM_LIMIT)


def _gelu(x):
    return 0.5 * x * (1.0 + jnp.tanh(math.sqrt(2.0 / math.pi) * (x + 0.044715 * (x * x * x))))


def _sigmoid(x):
    return 1.0 / (1.0 + jnp.exp(-x))


def _rms(x, g):
    ms = jnp.mean(x * x, axis=-1, keepdims=True)
    return (x * lax.rsqrt(ms + EPS)) * g


def _proj_kernel(x_ref, g_ref, w_ref, o_ref, og_ref, h_ref, *, n_main):
    j = pl.program_id(1)

    @pl.when(j == 0)
    def _():
        h_ref[...] = _rms(x_ref[...], g_ref[...]).astype(BF16)

    @pl.when(j < n_main)
    def _():
        o_ref[...] = jnp.dot(h_ref[...], w_ref[...], preferred_element_type=F32)

    @pl.when(j >= n_main)
    def _():
        og_ref[...] = jnp.dot(h_ref[...], w_ref[...], preferred_element_type=F32).astype(BF16)


def _proj(x, g, w, tm, tn, n_main):
    t, d = x.shape
    n = w.shape[1]
    nj = n // tn
    return pl.pallas_call(
        functools.partial(_proj_kernel, n_main=n_main),
        grid=(t // tm, nj),
        in_specs=[pl.BlockSpec((tm, d), lambda i, j: (i, 0)),
                  pl.BlockSpec((1, d), lambda i, j: (0, 0)),
                  pl.BlockSpec((d, tn), lambda i, j: (0, j))],
        out_specs=[pl.BlockSpec((tm, tn), lambda i, j: (i, jnp.minimum(j, n_main - 1))),
                   pl.BlockSpec((tm, tn), lambda i, j: (i, jnp.maximum(j - n_main, 0)))],
        out_shape=[jax.ShapeDtypeStruct((t, n_main * tn), F32),
                   jax.ShapeDtypeStruct((t, (nj - n_main) * tn), BF16)],
        scratch_shapes=[pltpu.VMEM((tm, d), BF16)],
        compiler_params=_cparams(("arbitrary", "arbitrary")),
        name="proj",
    )(x, g.reshape(1, d), w)


def _rot(x, c, sa, sb, half):
    n = x.shape[-1]
    return x * c + pltpu.roll(x, n - half, 1) * sa + pltpu.roll(x, half, 1) * sb


def _rope_kernel(qa_ref, ka_ref, va_ref, qi_ref, kw_ref, tab_ref,
                 qa_o, ka_o, kab_o, vab_o, qi_o, kw_o, kib_o):
    ca, saa, sba = tab_ref[0], tab_ref[1], tab_ref[2]
    ci, sai, sbi = tab_ref[3], tab_ref[4], tab_ref[5]
    ck, sak, sbk = tab_ref[6], tab_ref[7], tab_ref[8]
    for h in range(qa_ref.shape[1] // LANES):
        sl = slice(h * LANES, (h + 1) * LANES)
        qa_o[:, sl] = _rot(qa_ref[:, sl], ca, saa, sba, A_ROT // 2).astype(BF16)
    for h in range(ka_ref.shape[1] // LANES):
        sl = slice(h * LANES, (h + 1) * LANES)
        k = _rot(ka_ref[:, sl], ca, saa, sba, A_ROT // 2)
        ka_o[:, sl] = k
        kab_o[:, sl] = k.astype(BF16)
    vab_o[...] = va_ref[...].astype(BF16)
    for h in range(qi_ref.shape[1] // LANES):
        sl = slice(h * LANES, (h + 1) * LANES)
        qi_o[:, sl] = _rot(qi_ref[:, sl], ci, sai, sbi, IDX_ROT // 2).astype(BF16)
    kw = _rot(kw_ref[...], ck, sak, sbk, IDX_ROT // 2)
    kw_o[...] = kw
    kib_o[...] = kw.astype(BF16)


def _rope(p, tabs, cols, tm):
    t = p.shape[0]
    n_tab_blocks = tabs.shape[1] // tm
    d_qa, d_kv, d_qi = A_KV * A_GROUP * HEAD_DIM, A_KV * HEAD_DIM, IDX_HEADS * IDX_DIM

    def col(width, off):
        assert off % width == 0
        return pl.BlockSpec((tm, width), lambda i: (i, off // width))

    def out(width):
        return pl.BlockSpec((tm, width), lambda i: (i, 0))

    return pl.pallas_call(
        _rope_kernel,
        grid=(t // tm,),
        in_specs=[col(d_qa, cols["qa"]), col(d_kv, cols["ka"]), col(d_kv, cols["va"]),
                  col(d_qi, cols["qi"]), col(LANES, cols["kw"]),
                  pl.BlockSpec((9, tm, LANES), lambda i: (0, i % n_tab_blocks, 0))],
        out_specs=[out(d_qa), out(d_kv), out(d_kv), out(d_kv), out(d_qi), out(LANES), out(LANES)],
        out_shape=[jax.ShapeDtypeStruct((t, d_qa), BF16),
                   jax.ShapeDtypeStruct((t, d_kv), F32),
                   jax.ShapeDtypeStruct((t, d_kv), BF16),
                   jax.ShapeDtypeStruct((t, d_kv), BF16),
                   jax.ShapeDtypeStruct((t, d_qi), BF16),
                   jax.ShapeDtypeStruct((t, LANES), F32),
                   jax.ShapeDtypeStruct((t, LANES), BF16)],
        compiler_params=_cparams(("arbitrary",)),
        name="rope",
    )(p, p, p, p, p, tabs)


def _rope_tables(pos):
    pos = pos.astype(F32)[:, None]

    def cs(rot):
        half = rot // 2
        inv = jnp.float32(ROPE_THETA) ** (-jnp.arange(half, dtype=F32) / half)
        ang = pos * inv[None, :]
        return jnp.cos(ang), jnp.sin(ang)

    def tabs(rot, width):
        half = rot // 2
        c, s = cs(rot)
        n = pos.shape[0]
        one = jnp.ones((n, width - rot), F32)
        zero = jnp.zeros((n, width - rot), F32)
        zh = jnp.zeros((n, half), F32)
        return (jnp.concatenate([c, c, one], 1), jnp.concatenate([-s, zh, zero], 1),
                jnp.concatenate([zh, s, zero], 1))

    ca, saa, sba = tabs(A_ROT, HEAD_DIM)
    c64, sa64, sb64 = tabs(IDX_ROT, IDX_DIM)
    n = pos.shape[0]
    one64, zero64 = jnp.ones((n, IDX_DIM), F32), jnp.zeros((n, IDX_DIM), F32)
    return jnp.stack([ca, saa, sba,
                      jnp.concatenate([c64, c64], 1), jnp.concatenate([sa64, sa64], 1),
                      jnp.concatenate([sb64, sb64], 1),
                      jnp.concatenate([c64, one64], 1), jnp.concatenate([sa64, zero64], 1),
                      jnp.concatenate([sb64, zero64], 1)])


def _dsa_kernel(qaT_ref, qiT_ref, wiT_ref, k_ref, vT_ref, ki_ref, o_ref, key_ref, acc_ref,
                s0_ref, s1_ref, p0_ref, p1_ref, a0_ref, a1_ref,
                *, s_valid, q_pos0, topk):
    qb, tk = DSA_QB, DSA_TK
    j = pl.program_id(1)
    q_first = q_pos0 + j * qb
    qpos = q_first + lax.broadcasted_iota(I32, (1, qb), 1)
    assert CHUNK == 64 and tk == 512
    n_adm = jnp.minimum(((qpos >> 6) + 1) * CHUNK, s_valid)
    n_max = jnp.minimum((((q_first + qb - 1) >> 6) + 1) * CHUNK, s_valid)
    nkt = (n_max + tk - 1) >> 9

    qi = qiT_ref[0]
    rhs = jnp.concatenate([qi[h * IDX_DIM:(h + 1) * IDX_DIM, :] for h in range(IDX_HEADS)], axis=1)
    rhs = jnp.concatenate([rhs, jnp.zeros((LANES - IDX_DIM, IDX_HEADS * qb), BF16)], axis=0)
    wi = wiT_ref[0]

    def p1(kt, carry):
        off = pl.multiple_of(kt * tk, tk)
        kid = ki_ref[0, pl.ds(off, tk), :]
        s_all = jnp.dot(kid, rhs, preferred_element_type=F32)
        acc = wi[0:1, :] * jnp.maximum(s_all[:, 0:qb], 0.0)
        for h in range(1, IDX_HEADS):
            acc = acc + wi[h:h + 1, :] * jnp.maximum(s_all[:, h * qb:(h + 1) * qb], 0.0)
        score = acc * IDX_SCALE
        kpos = off + lax.broadcasted_iota(I32, (tk, 1), 0)
        score = jnp.where(kpos < n_adm, score, -jnp.inf)
        u = pltpu.bitcast(score, I32)
        key_ref[pl.ds(off, tk), :] = u ^ (lax.shift_right_arithmetic(u, 31) & 0x7FFFFFFF)
        return carry

    lax.fori_loop(0, nkt, p1, 0)

    pairs = key_ref.shape[0] % (2 * tk) == 0
    if pairs:
        @pl.when(nkt % 2 == 1)
        def _():
            key_ref[pl.ds(pl.multiple_of(nkt * tk, tk), tk), :] = jnp.full((tk, qb), INT32_MIN, I32)

    n_trips, per_trip = ((nkt + 1) // 2, 2) if pairs else (nkt, 1)

    def count_ge(cand):
        def body(kt, cnt):
            for half in range(per_trip):
                blk = key_ref[pl.ds(pl.multiple_of((kt * per_trip + half) * tk, tk), tk), :]
                m = jnp.where(blk >= cand, 1, 0).astype(I32)
                cnt = cnt + jnp.sum(m.reshape(tk // SUBLANES, SUBLANES, qb), axis=0)
            return cnt

        cnt8 = lax.fori_loop(0, n_trips, body, jnp.zeros((SUBLANES, qb), I32))
        return jnp.sum(cnt8, axis=0, keepdims=True)

    def bit_body(i, carry):
        prefix, n_ge = carry
        cand_u = prefix | lax.shift_left(jnp.int32(1), 31 - i)
        cnt = count_ge(cand_u ^ INT32_MIN)
        ok = cnt >= topk
        return jnp.where(ok, cand_u, prefix), jnp.where(ok, cnt, n_ge)

    prefix, n_ge = lax.fori_loop(0, 32, bit_body, (jnp.zeros((1, qb), I32), jnp.broadcast_to(nkt * tk, (1, qb))))
    thr = prefix ^ INT32_MIN
    cut_ties = jnp.max(jnp.where((n_ge > topk) & (thr > KEY_NEG_INF), 1, 0)) > 0

    @pl.when(cut_ties)
    def _():
        r = lax.broadcasted_iota(I32, (tk, tk), 0)
        c = lax.broadcasted_iota(I32, (tk, tk), 1)
        tri = jnp.where(c <= r, 1.0, 0.0).astype(BF16)
        take = (topk - count_ge(thr + 1)).astype(F32)

        def body(kt, seen):
            off = pl.multiple_of(kt * tk, tk)
            blk = key_ref[pl.ds(off, tk), :]
            eq = blk == thr
            incl = jnp.dot(tri, jnp.where(eq, 1.0, 0.0).astype(BF16), preferred_element_type=F32)
            drop = eq & (seen + incl > take)
            key_ref[pl.ds(off, tk), :] = jnp.where(drop, thr - 1, blk)
            return seen + incl[tk - 1:tk, :]

        lax.fori_loop(0, nkt, body, jnp.zeros((1, qb), F32))

    thr_lo = jnp.maximum(thr, KEY_NEG_INF + 1)
    qa = qaT_ref[0]
    ng = A_GROUP * qb
    qg = [jnp.concatenate([qa[(g * A_GROUP + hh) * HEAD_DIM:(g * A_GROUP + hh + 1) * HEAD_DIM, :]
                           for hh in range(A_GROUP)], axis=1) for g in range(A_KV)]
    acc_ref[...] = jnp.zeros(acc_ref.shape, F32)

    c_exp = ATTN_SCALE * math.log2(math.e)

    ta = DSA_TA
    n_steps = nkt * (tk // ta)
    last = n_steps - 1
    bufs = ((s0_ref, p0_ref, a0_ref), (s1_ref, p1_ref, a1_ref))

    def logits(step, s_ref):
        off = pl.multiple_of(jnp.minimum(step, last) * ta, ta)
        for g in range(A_KV):
            kg = k_ref[0, pl.ds(off, ta), g * HEAD_DIM:(g + 1) * HEAD_DIM]
            s_ref[g] = jnp.dot(kg, qg[g], preferred_element_type=F32)

    def numerators(step, s_ref, p_ref, a_ref, ms):
        off = pl.multiple_of(jnp.minimum(step, last) * ta, ta)
        blk = key_ref[pl.ds(off, ta), :]
        sel = (blk >= thr_lo) & (blk < KEY_POS_INF) & (step <= last)
        mask = jnp.concatenate([jnp.where(sel, 0.0, NEG_BIG)] * A_GROUP, axis=1)
        new = []
        for g in range(A_KV):
            s = s_ref[g] + mask
            m_new = jnp.maximum(ms[g], jnp.max(s, axis=0, keepdims=True))
            a_ref[g] = jnp.broadcast_to(jnp.exp2((ms[g] - m_new) * c_exp), (SUBLANES, ng))
            p_ref[g] = jnp.exp2((s - m_new).astype(BF16) * c_exp)
            new.append(m_new)
        return tuple(new)

    def weighted_values(step, p_ref, a_ref):
        off = pl.multiple_of(jnp.clip(step, 0, last) * ta, ta)
        for g in range(A_KV):
            vg = vT_ref[0, g * DSA_VROWS:(g + 1) * DSA_VROWS, pl.ds(off, ta)]
            acc_ref[g] = a_ref[g, 0:1, :] * acc_ref[g] + jnp.dot(vg, p_ref[g], preferred_element_type=F32)

    p1_ref[...] = jnp.zeros(p1_ref.shape, BF16)
    a1_ref[...] = jnp.ones(a1_ref.shape, F32)
    logits(0, s0_ref)

    def two_steps(i2, ms):
        i = 2 * i2
        for par in range(2):
            cur, nxt = bufs[par], bufs[1 - par]
            logits(i + par + 1, nxt[0])
            ms = numerators(i + par, cur[0], cur[1], cur[2], ms)
            weighted_values(i + par - 1, nxt[1], nxt[2])
        return ms

    init = (jnp.full((1, ng), NEG_BIG, F32),) * A_KV
    n_pairs = (n_steps + 1) // 2
    lax.fori_loop(0, n_pairs, two_steps, init)
    weighted_values(2 * n_pairs - 1, p1_ref, a1_ref)
    for g in range(A_KV):
        o = acc_ref[g, 0:HEAD_DIM, :] / acc_ref[g, HEAD_DIM:HEAD_DIM + 1, :]
        for hh in range(A_GROUP):
            h = g * A_GROUP + hh
            o_ref[0, h * HEAD_DIM:(h + 1) * HEAD_DIM, :] = o[:, hh * qb:(hh + 1) * qb].astype(o_ref.dtype)


def _dsa(qaT, qiT, wiT, k, vT, ki, *, s_valid, q_pos0, topk):
    b, dq, sq = qaT.shape
    s_pad = k.shape[1]
    assert s_pad % DSA_TK == 0 and s_pad % DSA_TA == 0 and sq % DSA_QB == 0 and topk <= DSA_TK
    kern = functools.partial(_dsa_kernel, s_valid=s_valid, q_pos0=q_pos0, topk=topk)
    return pl.pallas_call(
        kern,
        grid=(b, sq // DSA_QB),
        in_specs=[pl.BlockSpec((1, dq, DSA_QB), lambda i, j: (i, 0, j)),
                  pl.BlockSpec((1, qiT.shape[1], DSA_QB), lambda i, j: (i, 0, j)),
                  pl.BlockSpec((1, IDX_HEADS, DSA_QB), lambda i, j: (i, 0, j)),
                  pl.BlockSpec((1, s_pad, k.shape[2]), lambda i, j: (i, 0, 0)),
                  pl.BlockSpec((1, vT.shape[1], s_pad), lambda i, j: (i, 0, 0)),
                  pl.BlockSpec((1, s_pad, LANES), lambda i, j: (i, 0, 0))],
        out_specs=pl.BlockSpec((1, dq, DSA_QB), lambda i, j: (i, 0, j)),
        out_shape=jax.ShapeDtypeStruct((b, dq, sq), BF16),
        scratch_shapes=[pltpu.VMEM((s_pad, DSA_QB), I32),
                        pltpu.VMEM((A_KV, DSA_VROWS, A_GROUP * DSA_QB), F32)]
                       + [pltpu.VMEM((A_KV, DSA_TA, A_GROUP * DSA_QB), F32)] * 2
                       + [pltpu.VMEM((A_KV, DSA_TA, A_GROUP * DSA_QB), BF16)] * 2
                       + [pltpu.VMEM((A_KV, SUBLANES, A_GROUP * DSA_QB), F32)] * 2,
        compiler_params=_cparams(("arbitrary", "arbitrary")),
        name="dsa",
    )(qaT, qiT, wiT, k, vT, ki)


def _cache_prep_kernel(ck_ref, cv_ref, cki_ref, kn_ref, vn_ref, kin_ref, k_o, vT_o, ki_o, *, n_cache_tiles):
    kt = pl.program_id(1)
    tk = k_o.shape[1]
    extra = DSA_VROWS - HEAD_DIM
    ones_rows = jnp.where(lax.broadcasted_iota(I32, (extra, tk), 0) == 0, 1.0, 0.0).astype(BF16)

    def emit(k_head, v_head, ki):
        for g in range(A_KV):
            k_o[0, :, g * HEAD_DIM:(g + 1) * HEAD_DIM] = k_head(g).astype(BF16)
            vT_o[0, g * DSA_VROWS:g * DSA_VROWS + HEAD_DIM, :] = v_head(g).astype(F32).T.astype(BF16)
            vT_o[0, g * DSA_VROWS + HEAD_DIM:(g + 1) * DSA_VROWS, :] = ones_rows
        ki_o[0] = ki

    @pl.when(kt < n_cache_tiles)
    def _():
        ki = jnp.concatenate([cki_ref[0, 0], jnp.zeros((tk, LANES - IDX_DIM), F32)], axis=1)
        emit(lambda g: ck_ref[0, 0, :, g, :], lambda g: cv_ref[0, 0, :, g, :], ki.astype(BF16))

    @pl.when(kt >= n_cache_tiles)
    def _():
        t = kn_ref.shape[0]
        rows = lambda a: jnp.concatenate([a, jnp.zeros((tk - t, a.shape[1]), a.dtype)], axis=0)
        kn, vn = rows(kn_ref[...]), rows(vn_ref[...])
        head = lambda a: lambda g: a[:, g * HEAD_DIM:(g + 1) * HEAD_DIM]
        emit(head(kn), head(vn), rows(kin_ref[...]))


def _cache_prep(cache_k, cache_v, cache_ki, layer, kn, vn, kin, b, t):
    tk = DSA_TK
    past = cache_k.shape[2]
    assert past % tk == 0 and t <= tk
    nct = past // tk
    s_pad = past + tk
    cidx = lambda i, j: jnp.minimum(j, nct - 1)
    return pl.pallas_call(
        functools.partial(_cache_prep_kernel, n_cache_tiles=nct),
        grid=(b, nct + 1),
        in_specs=[pl.BlockSpec((1, 1, tk, A_KV, HEAD_DIM), lambda i, j: (layer, i, cidx(i, j), 0, 0)),
                  pl.BlockSpec((1, 1, tk, A_KV, HEAD_DIM), lambda i, j: (layer, i, cidx(i, j), 0, 0)),
                  pl.BlockSpec((1, 1, tk, IDX_DIM), lambda i, j: (layer, i, cidx(i, j), 0)),
                  pl.BlockSpec((t, A_KV * HEAD_DIM), lambda i, j: (i, 0)),
                  pl.BlockSpec((t, A_KV * HEAD_DIM), lambda i, j: (i, 0)),
                  pl.BlockSpec((t, LANES), lambda i, j: (i, 0))],
        out_specs=[pl.BlockSpec((1, tk, A_KV * HEAD_DIM), lambda i, j: (i, j, 0)),
                   pl.BlockSpec((1, A_KV * DSA_VROWS, tk), lambda i, j: (i, 0, j)),
                   pl.BlockSpec((1, tk, LANES), lambda i, j: (i, j, 0))],
        out_shape=[jax.ShapeDtypeStruct((b, s_pad, A_KV * HEAD_DIM), BF16),
                   jax.ShapeDtypeStruct((b, A_KV * DSA_VROWS, s_pad), BF16),
                   jax.ShapeDtypeStruct((b, s_pad, LANES), BF16)],
        compiler_params=_cparams(("arbitrary", "arbitrary")),
        name="cache_prep",
    )(cache_k, cache_v, cache_ki, kn, vn, kin)


def _band_heads(q, k_head, v_head, bias_ref, n_hidden, o_ref, s_ref):
    nh = q.shape[1] // HEAD_DIM
    tk = bias_ref.shape[2]
    hidden = None if n_hidden is None else lax.broadcasted_iota(I32, (1, tk), 1) < n_hidden
    for h in range(nh):
        sl = slice(h * HEAD_DIM, (h + 1) * HEAD_DIM)
        s = lax.dot_general(q[:, sl].astype(BF16), k_head(h), (((1,), (1,)), ((), ())),
                            preferred_element_type=F32) * ATTN_SCALE + bias_ref[h]
        if hidden is not None:
            s = jnp.where(hidden, NEG_BIG, s)
        s_ref[h] = s
    ms = [jnp.max(s_ref[h], axis=1, keepdims=True) for h in range(nh)]
    ls = []
    for h in range(nh):
        p = jnp.exp(s_ref[h] - ms[h])
        s_ref[h] = p
        ls.append(jnp.sum(p, axis=1, keepdims=True))
    for h in range(nh):
        sl = slice(h * HEAD_DIM, (h + 1) * HEAD_DIM)
        w = (s_ref[h] / ls[h]).astype(BF16)
        o_ref[:, sl] = jnp.dot(w, v_head(h), preferred_element_type=F32).astype(o_ref.dtype)


def _band_prompt_kernel(*refs, n_tiles):
    q_ref = refs[0]
    k_refs = refs[1:1 + n_tiles]
    v_refs = refs[1 + n_tiles:1 + 2 * n_tiles]
    bias_ref = refs[1 + 2 * n_tiles]
    o_ref = refs[2 + 2 * n_tiles]
    s_ref = refs[3 + 2 * n_tiles]
    j = pl.program_id(1)
    n_hidden = jnp.maximum(n_tiles - 1 - j, 0) * q_ref.shape[0]
    k_all = jnp.concatenate([r[...] for r in k_refs], axis=0).astype(BF16)
    v_all = jnp.concatenate([r[...] for r in v_refs], axis=0).astype(BF16)
    head = lambda a: lambda h: a[:, h * HEAD_DIM:(h + 1) * HEAD_DIM]
    _band_heads(q_ref[...], head(k_all), head(v_all), bias_ref, n_hidden, o_ref, s_ref)


def _band_prompt(p, bias, cols, batch, seq):
    qb = BAND_QB
    n_tiles = BAND_CHUNKS * CHUNK // qb + 1
    width = bias.shape[0] * HEAD_DIM
    nq = seq // qb
    cq, ck, cv = cols["qb"] // width, cols["kb"] // width, cols["vb"] // width
    assert cols["qb"] % width == 0 and cols["kb"] % width == 0 and cols["vb"] % width == 0

    def kv_spec(c, i):
        return pl.BlockSpec((qb, width), lambda b, j: (b * nq + jnp.maximum(j - (n_tiles - 1) + i, 0), c))

    return pl.pallas_call(
        functools.partial(_band_prompt_kernel, n_tiles=n_tiles),
        grid=(batch, nq),
        in_specs=([pl.BlockSpec((qb, width), lambda b, j: (b * nq + j, cq))]
                  + [kv_spec(ck, i) for i in range(n_tiles)]
                  + [kv_spec(cv, i) for i in range(n_tiles)]
                  + [pl.BlockSpec(bias.shape, lambda b, j: (0, 0, 0))]),
        out_specs=pl.BlockSpec((qb, width), lambda b, j: (b * nq + j, 0)),
        out_shape=jax.ShapeDtypeStruct((batch * seq, width), BF16),
        compiler_params=_cparams(("arbitrary", "arbitrary")),
        scratch_shapes=[pltpu.VMEM(bias.shape, F32)],
        name="band_prompt",
    )(*([p] * (1 + 2 * n_tiles)), bias)


def _band_sample_kernel(q_ref, k_ref, v_ref, ck_ref, cv_ref, bias_ref, o_ref, s_ref):
    def head(cache_ref, new_ref):
        return lambda h: jnp.concatenate(
            [cache_ref[0, 0, :, h, :], new_ref[:, h * HEAD_DIM:(h + 1) * HEAD_DIM]], axis=0).astype(BF16)

    _band_heads(q_ref[...], head(ck_ref, k_ref), head(cv_ref, v_ref), bias_ref, None, o_ref, s_ref)


def _band_sample(p, cache_k, cache_v, layer, bias, cols, batch, t):
    nh = bias.shape[0]
    width = nh * HEAD_DIM
    nbc = cache_k.shape[2]
    cq, ck, cv = cols["qb"] // width, cols["kb"] // width, cols["vb"] // width
    return pl.pallas_call(
        _band_sample_kernel,
        grid=(batch,),
        in_specs=[pl.BlockSpec((t, width), lambda b: (b, cq)),
                  pl.BlockSpec((t, width), lambda b: (b, ck)),
                  pl.BlockSpec((t, width), lambda b: (b, cv)),
                  pl.BlockSpec((1, 1, nbc, nh, HEAD_DIM), lambda b: (layer, b, 0, 0, 0)),
                  pl.BlockSpec((1, 1, nbc, nh, HEAD_DIM), lambda b: (layer, b, 0, 0, 0)),
                  pl.BlockSpec(bias.shape, lambda b: (0, 0, 0))],
        out_specs=pl.BlockSpec((t, width), lambda b: (b, 0)),
        out_shape=jax.ShapeDtypeStruct((batch * t, width), BF16),
        compiler_params=_cparams(("arbitrary",)),
        scratch_shapes=[pltpu.VMEM(bias.shape, F32)],
        name="band_sample",
    )(p, p, p, cache_k, cache_v, bias)


def _band_bias(table, q0, nq, k0, nk):
    n = nq + nk
    j = jnp.arange(n)
    g = table[:, jnp.clip(j + (q0 - k0 - nk + 1), -REL_CLIP, REL_CLIP) + REL_CLIP].astype(F32)
    hank = jnp.tile(g, (1, nq + 1))[:, :nq * (n + 1)].reshape(-1, nq, n + 1)[:, :, :nk]
    bias = hank[:, :, ::-1]
    qc = (q0 + jnp.arange(nq))[:, None] // CHUNK
    kc = (k0 + jnp.arange(nk))[None, :] // CHUNK
    vis = (kc <= qc) & (kc >= qc - BAND_CHUNKS)
    return jnp.where(vis[None], bias, NEG_BIG)


def _ssm_kernel(u_ref, t_ref, p_ref, q_ref, a_ref, h0_ref, y_ref, hre_o, him_o, sre_ref, sim_ref,
                *, seg, tiles_per_seq):
    L = SSM_L
    x = jnp.concatenate([u_ref[:, s, :] for s in range(L)], axis=1).astype(BF16)
    sre_ref[...] = jnp.dot(x, p_ref[0, 0], preferred_element_type=F32)
    sim_ref[...] = jnp.dot(x, p_ref[0, 1], preferred_element_type=F32)
    rows, w = sre_ref.shape
    kseg = lax.broadcasted_iota(I32, (SUBLANES, 1), 0) % seg
    apk_re, apk_im = a_ref[0, 0, 0:SUBLANES, :], a_ref[0, 1, 0:SUBLANES, :]

    def cmad(x_re, x_im, a_re, a_im, y_re, y_im):
        return x_re + a_re * y_re - a_im * y_im, x_im + a_re * y_im + a_im * y_re

    def tile(i, carry):
        r = pl.ds(pl.multiple_of(i * SUBLANES, SUBLANES), SUBLANES)
        x_re, x_im = sre_ref[r, :], sim_ref[r, :]
        for n, d in enumerate((1, 2, 4)):
            if d < seg:
                ad_re = a_ref[0, 0, SUBLANES + n:SUBLANES + n + 1, :]
                ad_im = a_ref[0, 1, SUBLANES + n:SUBLANES + n + 1, :]
                sh_re = jnp.where(kseg >= d, pltpu.roll(x_re, d, 0), 0.0)
                sh_im = jnp.where(kseg >= d, pltpu.roll(x_im, d, 0), 0.0)
                x_re, x_im = cmad(x_re, x_im, ad_re, ad_im, sh_re, sh_im)
        if tiles_per_seq > 1:
            first = (i % tiles_per_seq) == 0
            hin_re = jnp.where(first, h0_ref[0, 0], carry[0])
            hin_im = jnp.where(first, h0_ref[0, 1], carry[1])
        else:
            hin_re, hin_im = h0_ref[0, 0, r, :], h0_ref[0, 1, r, :]
        inc_re, inc_im = cmad(x_re, x_im, apk_re, apk_im, hin_re, hin_im)
        sre_ref[r, :] = jnp.where(kseg == 0, hin_re, pltpu.roll(inc_re, 1, 0))
        sim_ref[r, :] = jnp.where(kseg == 0, hin_im, pltpu.roll(inc_im, 1, 0))
        hre_o[0, r, :] = inc_re
        him_o[0, r, :] = inc_im
        return inc_re[SUBLANES - 1:SUBLANES, :], inc_im[SUBLANES - 1:SUBLANES, :]

    zero = jnp.zeros((1, w), F32)
    lax.fori_loop(0, rows // SUBLANES, tile, (zero, zero))
    y = (jnp.dot(x, t_ref[0], preferred_element_type=F32)
         + jnp.dot(sre_ref[...].astype(BF16), q_ref[0, 0], preferred_element_type=F32)
         + jnp.dot(sim_ref[...].astype(BF16), q_ref[0, 1], preferred_element_type=F32))
    for t in range(L):
        y_ref[:, t, :] = y[:, t * LANES:(t + 1) * LANES]


def _ssm(p3, uc_col, tmat, pmat, qmat, apow, h0, rows_blk, seg, tiles_per_seq):
    rows_total = p3.shape[0]
    nb, _, _, w = pmat.shape
    kl = SSM_L * LANES
    h0_rows = h0.shape[2] // (rows_total // rows_blk)
    hspec = pl.BlockSpec((1, rows_blk, w), lambda i, r: (i, r, 0))
    return pl.pallas_call(
        functools.partial(_ssm_kernel, seg=seg, tiles_per_seq=tiles_per_seq),
        grid=(nb, rows_total // rows_blk),
        in_specs=[pl.BlockSpec((rows_blk, SSM_L, LANES), lambda i, r: (r, 0, uc_col // LANES + i)),
                  pl.BlockSpec((1, kl, kl), lambda i, r: (i, 0, 0)),
                  pl.BlockSpec((1, 2, kl, w), lambda i, r: (i, 0, 0, 0)),
                  pl.BlockSpec((1, 2, w, kl), lambda i, r: (i, 0, 0, 0)),
                  pl.BlockSpec((1, 2, 2 * SUBLANES, w), lambda i, r: (i, 0, 0, 0)),
                  pl.BlockSpec((1, 2, h0_rows, w), lambda i, r: (i, 0, r, 0))],
        out_specs=[pl.BlockSpec((rows_blk, SSM_L, LANES), lambda i, r: (r, 0, i)), hspec, hspec],
        out_shape=[jax.ShapeDtypeStruct((rows_total, SSM_L, nb * LANES), F32),
                   jax.ShapeDtypeStruct((nb, rows_total, w), F32),
                   jax.ShapeDtypeStruct((nb, rows_total, w), F32)],
        scratch_shapes=[pltpu.VMEM((rows_blk, w), F32), pltpu.VMEM((rows_blk, w), F32)],
        compiler_params=_cparams(("arbitrary", "arbitrary")),
        name="ssm",
    )(p3, tmat, pmat, qmat, apow, h0)


def _ssm_weights(a_re, a_im, log_dt, b_re, b_im, c_re, c_im, d):
    hp = lax.Precision.HIGHEST
    g, p = a_re.shape
    gc = b_re.shape[2]
    L = SSM_L
    a_re, a_im = a_re.astype(F32), a_im.astype(F32)
    dt = jnp.exp(log_dt.astype(F32))[:, None]
    mag = jnp.exp(dt * a_re)
    ab_re, ab_im = mag * jnp.cos(dt * a_im), mag * jnp.sin(dt * a_im)
    n_re = ab_re - 1.0
    n_im = ab_im
    den = a_re * a_re + a_im * a_im
    cc_re = (n_re * a_re + n_im * a_im) / den
    cc_im = (n_im * a_re - n_re * a_im) / den
    b_re, b_im = b_re.astype(F32), b_im.astype(F32)
    bb_re = cc_re[..., None] * b_re - cc_im[..., None] * b_im
    bb_im = cc_re[..., None] * b_im + cc_im[..., None] * b_re
    c_re, c_im = c_re.astype(F32), c_im.astype(F32)

    def cmul(x, y):
        return x[0] * y[0] - x[1] * y[1], x[0] * y[1] + x[1] * y[0]

    rep = lambda a: jnp.broadcast_to(a[None], (L,) + a.shape)
    pw_re, pw_im = lax.associative_scan(cmul, (rep(ab_re), rep(ab_im)), axis=0)
    pw_re = jnp.concatenate([jnp.ones((1, g, p), F32), pw_re], axis=0)
    pw_im = jnp.concatenate([jnp.zeros((1, g, p), F32), pw_im], axis=0)
    ca_re = c_re[None] * pw_re[:, :, None, :] - c_im[None] * pw_im[:, :, None, :]
    ca_im = c_re[None] * pw_im[:, :, None, :] + c_im[None] * pw_re[:, :, None, :]
    taps = (jnp.einsum("tgop,gpi->tgoi", ca_re[:L], bb_re, precision=hp)
            - jnp.einsum("tgop,gpi->tgoi", ca_im[:L], bb_im, precision=hp))
    taps = taps.at[0].add(jax.vmap(jnp.diag)(d.astype(F32)))
    s_idx = jnp.arange(L)[:, None]
    t_idx = jnp.arange(L)[None, :]
    lag = t_idx - s_idx
    tt = jnp.where((lag >= 0)[:, :, None, None, None], taps[jnp.clip(lag, 0, L - 1)], 0.0)
    gb = SSM_GB
    nb = g // gb
    n_x = L * gb * gc
    x = jnp.arange(n_x)
    x_grp, x_loc = (x // gc) % gb, (x // (gb * gc)) * gc + x % gc
    place_x = ((x_grp[None, :, None] == jnp.arange(gb)[:, None, None])
               & (x_loc[None, :, None] == jnp.arange(L * gc)[None, None, :])).astype(F32)
    q = jnp.arange(gb * p)
    place_q = ((q[None, :, None] // p == jnp.arange(gb)[:, None, None])
               & (q[None, :, None] % p == jnp.arange(p)[None, None, :])).astype(F32)
    blk = lambda a: a.reshape((nb, gb) + a.shape[1:])
    t_grp = blk(tt.transpose(2, 0, 4, 1, 3).reshape(g, L * gc, L * gc))
    tmat = jnp.einsum("gxr,bgrc,gyc->bxy", place_x, t_grp, place_x)
    rev = L - 1 - jnp.arange(L)
    pin_re = pw_re[rev][:, :, :, None] * bb_re[None] - pw_im[rev][:, :, :, None] * bb_im[None]
    pin_im = pw_re[rev][:, :, :, None] * bb_im[None] + pw_im[rev][:, :, :, None] * bb_re[None]
    to_p = lambda a: jnp.einsum("gxr,bgrp,gqp->bxq", place_x,
                                blk(a.transpose(1, 0, 3, 2).reshape(g, L * gc, p)), place_q)
    to_q = lambda a: jnp.einsum("gqp,bgpc,gyc->bqy", place_q,
                                blk(a.transpose(1, 3, 0, 2).reshape(g, p, L * gc)), place_x)
    pmat = jnp.stack([to_p(pin_re), to_p(pin_im)], axis=1)
    qmat = jnp.stack([to_q(ca_re[1:]), -to_q(ca_im[1:])], axis=1)
    rep8 = lambda a: jnp.broadcast_to(a[None], (SUBLANES,) + a.shape)
    apl_re, apl_im = lax.associative_scan(cmul, (rep8(pw_re[L]), rep8(pw_im[L])), axis=0)
    apl = jnp.stack([apl_re, apl_im]).reshape(2, SUBLANES, nb, gb * p)
    return tmat.astype(BF16), pmat.astype(BF16), qmat.astype(BF16), apl


def _ssm_apply(p, uc_col, b, t, h0_re, h0_im, weights, one_seq_per_step):
    tmat, pmat, qmat, apl = weights
    nb, w = pmat.shape[0], pmat.shape[3]
    L = SSM_L
    nc = t // L
    assert t % L == 0 and (nc % SUBLANES == 0 or SUBLANES % nc == 0)
    seg = min(nc, SUBLANES)
    if one_seq_per_step:
        assert nc % SUBLANES == 0
        rows_blk, tiles_per_seq, reps = nc, nc // SUBLANES, SUBLANES
    else:
        rows_blk, tiles_per_seq, reps = b * nc, 1, nc
        assert nc <= SUBLANES and rows_blk % SUBLANES == 0
    pw_rows = [k % seg for k in range(SUBLANES)] + [0, 1, 3] + [0] * (SUBLANES - 3)
    apow = jnp.stack([apl[:, k] for k in pw_rows], axis=1).transpose(2, 0, 1, 3)

    def pack(h):
        h = h.astype(F32).reshape(b, nb, w).transpose(1, 0, 2)
        return jnp.repeat(h, reps, axis=1)

    h0 = jnp.stack([pack(h0_re), pack(h0_im)], axis=1)
    p3 = p.reshape(b * nc, L, p.shape[1])
    y, hre, him = _ssm(p3, uc_col, tmat, pmat, qmat, apow, h0, rows_blk, seg, tiles_per_seq)

    def last(h):
        return h.reshape(nb, b, nc, w)[:, :, nc - 1].transpose(1, 0, 2).reshape(b, nb * SSM_GB, C_STATE)

    return y.reshape(b * t, nb * LANES), last(hre), last(him)


def _glu_kernel(y_ref, w_ref, b_ref, o_ref):
    z = _gelu(y_ref[...])
    a = jnp.dot(z.astype(BF16), w_ref[...], preferred_element_type=F32) + b_ref[...]
    o_ref[...] = (z * _sigmoid(a)).astype(o_ref.dtype)


def _glu(y, w, b, tm):
    t, n = y.shape
    return pl.pallas_call(
        _glu_kernel,
        grid=(t // tm,),
        in_specs=[pl.BlockSpec((tm, n), lambda i: (i, 0)),
                  pl.BlockSpec((n, n), lambda i: (0, 0)),
                  pl.BlockSpec((1, n), lambda i: (0, 0))],
        out_specs=pl.BlockSpec((tm, n), lambda i: (i, 0)),
        out_shape=jax.ShapeDtypeStruct((t, n), BF16),
        compiler_params=_cparams(("arbitrary",)),
        name="glu",
    )(y, w, b.reshape(1, n))


def _merge_kernel(oa_ref, ob_ref, oc_ref, ga_ref, gb_ref, gc_ref, w_ref, o_ref):
    acc = None
    for n, (o, g) in enumerate(((oa_ref, ga_ref), (ob_ref, gb_ref), (oc_ref, gc_ref))):
        br = jnp.dot(o[...], w_ref[n], preferred_element_type=F32)
        t = _sigmoid(g[...].astype(F32)) * br
        acc = t if acc is None else acc + t
    o_ref[...] = acc.astype(o_ref.dtype)


def _merge(oa, ob, oc, gates, w, tm, tn):
    t, m = oa.shape
    d = w.shape[2]
    assert d % tn == 0

    def gspec(n):
        return pl.BlockSpec((tm, tn), lambda i, j: (i, (n * d) // tn + j))

    ospec = pl.BlockSpec((tm, m), lambda i, j: (i, 0))
    return pl.pallas_call(
        _merge_kernel,
        grid=(t // tm, d // tn),
        in_specs=[ospec, ospec, ospec, gspec(0), gspec(1), gspec(2),
                  pl.BlockSpec((N_BRANCH, m, tn), lambda i, j: (0, 0, j))],
        out_specs=pl.BlockSpec((tm, tn), lambda i, j: (i, j)),
        out_shape=jax.ShapeDtypeStruct((t, d), BF16),
        compiler_params=_cparams(("arbitrary", "arbitrary")),
        name="merge",
    )(oa, ob, oc, gates, gates, gates, w)


def _mmres_kernel(a_ref, w_ref, r_ref, o_ref):
    o_ref[...] = r_ref[...] + jnp.dot(a_ref[...], w_ref[...], preferred_element_type=F32)


def _mmres(a, w, res, tm, tn):
    t, k = a.shape
    n = w.shape[1]
    return pl.pallas_call(
        _mmres_kernel,
        grid=(t // tm, n // tn),
        in_specs=[pl.BlockSpec((tm, k), lambda i, j: (i, 0)),
                  pl.BlockSpec((k, tn), lambda i, j: (0, j)),
                  pl.BlockSpec((tm, tn), lambda i, j: (i, j))],
        out_specs=pl.BlockSpec((tm, tn), lambda i, j: (i, j)),
        out_shape=jax.ShapeDtypeStruct((t, n), F32),
        compiler_params=_cparams(("arbitrary", "arbitrary")),
        name="mmres",
    )(a, w, res)


def _ffn_in_kernel(x_ref, g_ref, wu_ref, wv_ref, wc_ref, bc_ref, st_ref, o_ref, tail_ref,
                   h_ref, up_ref, carry_ref, *, ns, ts, tiles_per_seq):
    i, j = pl.program_id(0), pl.program_id(1)
    pad = SUBLANES

    @pl.when(j == 0)
    def _():
        h_ref[...] = _rms(x_ref[...], g_ref[...]).astype(BF16)

    @pl.when((i == 0) & (j == 0))
    def _():
        carry_ref[...] = jnp.zeros(carry_ref.shape, F32)

    h = h_ref[...]
    u = jnp.dot(h, wu_ref[...], preferred_element_type=F32)
    tn = u.shape[1]
    up_ref[:, pad:, :] = u.reshape(ns, ts, tn)
    first = (i % tiles_per_seq) == 0
    up_ref[:, pad - 2:pad, :] = jnp.where(first, st_ref[...], carry_ref[j])
    last2 = up_ref[:, ts + pad - 2:ts + pad, :]
    carry_ref[j] = last2
    tail_ref[0] = last2
    c = bc_ref[...].reshape(1, 1, tn)
    for tap in range(CONV_W):
        c = c + up_ref[:, pad - 2 + tap:pad - 2 + tap + ts, :] * wc_ref[tap:tap + 1, :].reshape(1, 1, tn)
    gl = _gelu(c).reshape(ns * ts, tn)
    v = jnp.dot(h, wv_ref[...], preferred_element_type=F32)
    o_ref[...] = (gl * v).astype(o_ref.dtype)


def _ffn_in(x, g, w, wc, bc, state, ns, ts, tn):
    t, d = x.shape
    f = w.shape[1] // 2
    tm = ns * ts
    nj = f // tn
    seq_len = t // state.shape[0]
    tiles_per_seq = max(seq_len // tm, 1)
    kern = functools.partial(_ffn_in_kernel, ns=ns, ts=ts, tiles_per_seq=tiles_per_seq)
    return pl.pallas_call(
        kern,
        grid=(t // tm, nj),
        in_specs=[pl.BlockSpec((tm, d), lambda i, j: (i, 0)),
                  pl.BlockSpec((1, d), lambda i, j: (0, 0)),
                  pl.BlockSpec((d, tn), lambda i, j: (0, j)),
                  pl.BlockSpec((d, tn), lambda i, j: (0, nj + j)),
                  pl.BlockSpec((CONV_W, tn), lambda i, j: (0, j)),
                  pl.BlockSpec((1, tn), lambda i, j: (0, j)),
                  pl.BlockSpec((ns, CONV_W - 1, tn), lambda i, j: (i // tiles_per_seq, 0, j))],
        out_specs=[pl.BlockSpec((tm, tn), lambda i, j: (i, j)),
                   pl.BlockSpec((1, ns, CONV_W - 1, tn), lambda i, j: (i, 0, 0, j))],
        out_shape=[jax.ShapeDtypeStruct((t, f), BF16),
                   jax.ShapeDtypeStruct((t // tm, ns, CONV_W - 1, f), F32)],
        scratch_shapes=[pltpu.VMEM((tm, d), BF16),
                        pltpu.VMEM((ns, ts + SUBLANES, tn), F32),
                        pltpu.VMEM((nj, ns, CONV_W - 1, tn), F32)],
        compiler_params=_cparams(("arbitrary", "arbitrary")),
        name="ffn_in",
    )(x, g.reshape(1, d), w, w, wc, bc.reshape(1, f), state)


def _norm_kernel(x_ref, g_ref, o_ref):
    o_ref[...] = _rms(x_ref[...], g_ref[...])


def _norm(x, g, tm):
    t, d = x.shape
    return pl.pallas_call(
        _norm_kernel,
        grid=(t // tm,),
        in_specs=[pl.BlockSpec((tm, d), lambda i: (i, 0)), pl.BlockSpec((1, d), lambda i: (0, 0))],
        out_specs=pl.BlockSpec((tm, d), lambda i: (i, 0)),
        out_shape=jax.ShapeDtypeStruct((t, d), F32),
        compiler_params=_cparams(("arbitrary",)),
        name="final_norm",
    )(x, g.reshape(1, d))


def _row_tile(t, cap=1024):
    tm = min(t, cap)
    while t % tm:
        tm //= 2
    return tm


def _pack_w_in(w_in, d_model):
    mix = d_model // 2
    sizes = dict(qa=mix, ka=A_KV * HEAD_DIM, va=A_KV * HEAD_DIM, qi=IDX_HEADS * IDX_DIM, ki=IDX_DIM,
                 wi=IDX_HEADS, qb=mix, kb=mix, vb=mix, uc=mix, g=N_BRANCH * d_model)
    src_order = ["qa", "ka", "va", "qi", "ki", "wi", "qb", "kb", "vb", "uc", "g"]
    src, off = {}, 0
    for name in src_order:
        src[name] = (off, sizes[name])
        off += sizes[name]
    assert off == w_in.shape[1]
    tn = 1280
    dst_order = ["qa", "qb", "kb", "vb", "uc", "qi", "ka", "va", "ki", "wi"]
    cols, parts, off = {}, [], 0

    def pad_to_tile():
        nonlocal off
        total = -(-off // tn) * tn
        parts.append(jnp.zeros((w_in.shape[0], total - off), w_in.dtype))
        off = total

    for name in dst_order:
        s, n = src[name]
        cols[name] = off
        parts.append(w_in[:, s:s + n])
        off += n
    cols["kw"] = cols["ki"]
    pad_to_tile()
    n_main = off // tn
    s, n = src["g"]
    parts.append(w_in[:, s:s + n])
    off += n
    pad_to_tile()
    return jnp.concatenate(parts, axis=1).astype(BF16), cols, tn, n_main


def _to_heads_T(x, b, t):
    return x.reshape(b, t, x.shape[1]).transpose(0, 2, 1)


def _pad_axis(x, axis, size):
    pad = [(0, 0)] * x.ndim
    pad[axis] = (0, size - x.shape[axis])
    return jnp.pad(x, pad)


def _layer(x, pos, prm, cache, layer, is_prompt):
    b, t, d = x.shape
    mix = d // 2
    xt = x.reshape(b * t, d)
    tm = _row_tile(b * t)
    cols = prm["cols"]
    p, gates = _proj(xt, prm["norm1"], prm["w_in"], tm, prm["w_in_tn"], prm["w_in_main"])

    tabs = _rope_tables(pos)
    tm_r = _row_tile(t) if is_prompt else tm
    if not is_prompt:
        tabs = jnp.tile(tabs, (1, b, 1))
    qa, ka, kab, vab, qi, kw, kib = _rope(p, tabs, cols, tm_r)
    if is_prompt:
        sq, s_valid, q_pos0 = t, t, 0
        s_pad = -(-s_valid // DSA_TK) * DSA_TK
        k_all = _pad_axis(kab.reshape(b, t, -1), 1, s_pad)
        ki_all = _pad_axis(kib.reshape(b, t, LANES), 1, s_pad)
        vT = _pad_axis(vab.reshape(b, t, -1), 1, s_pad).transpose(0, 2, 1).reshape(b, A_KV, HEAD_DIM, s_pad)
        ones = jnp.broadcast_to((jnp.arange(DSA_VROWS - HEAD_DIM) == 0).astype(BF16)[None, None, :, None],
                                (b, A_KV, DSA_VROWS - HEAD_DIM, s_pad))
        vT = jnp.concatenate([vT, ones], axis=2).reshape(b, A_KV * DSA_VROWS, s_pad)
    else:
        past = cache["a_k"].shape[2]
        sq, s_valid, q_pos0 = DSA_QB, past + t, past
        k_all, vT, ki_all = _cache_prep(cache["a_k"], cache["a_v"], cache["a_kidx"], layer, kab, vab, kib, b, t)
    qaT = _pad_axis(_to_heads_T(qa, b, t), 2, sq)
    qiT = _pad_axis(_to_heads_T(qi, b, t), 2, sq)
    wiT = _pad_axis(_to_heads_T(kw[:, IDX_DIM:IDX_DIM + IDX_HEADS], b, t), 2, sq)
    topk = min(TOPK_MAX, s_valid // 4)
    oaT = _dsa(qaT, qiT, wiT, k_all, vT, ki_all, s_valid=s_valid, q_pos0=q_pos0, topk=topk)
    oa = oaT[:, :, :t].transpose(0, 2, 1).reshape(b * t, mix)

    nh = mix // HEAD_DIM
    if is_prompt:
        n_tiles = BAND_CHUNKS * CHUNK // BAND_QB + 1
        bias = _band_bias(prm["rel_bias"], BAND_CHUNKS * CHUNK, BAND_QB, 0, n_tiles * BAND_QB)
        ob = _band_prompt(p, bias, cols, b, t)
    else:
        nbc = cache["b_k"].shape[2]
        past = cache["a_k"].shape[2]
        bias = _band_bias(prm["rel_bias"], past, t, past - nbc, nbc + t)
        ob = _band_sample(p, cache["b_k"], cache["b_v"], layer, bias, cols, b, t)

    g_ssm = mix // C_GROUP
    if is_prompt:
        h0_re = jnp.zeros((b, g_ssm, C_STATE), F32)
        h0_im = h0_re
    else:
        h0_re, h0_im = cache["c_re"][layer], cache["c_im"][layer]
    yc, hr, hi = _ssm_apply(p, cols["uc"], b, t, h0_re, h0_im, prm["ssm"], is_prompt)
    oc = _glu(yc, prm["w_glu"], prm["b_glu"], tm)

    merged = _merge(oa, ob, oc, gates, prm["w_branch"], tm, 1024)
    x1 = _mmres(merged, prm["w_out"], xt, _row_tile(b * t, 512), 2048)

    f = prm["w_conv"].shape[1]
    if is_prompt:
        state = jnp.zeros((b, CONV_W - 1, f), F32)
        ns, ts = 1, _row_tile(t)
    else:
        state = cache["ffn_conv"][layer].astype(F32)
        ns, ts = b, t
    act, tails = _ffn_in(x1, prm["norm2"], prm["w_ffn_in"], prm["w_conv"], prm["b_conv"], state, ns, ts, 512)
    x2 = _mmres(act, prm["w_down"], x1, tm, 512)
    if is_prompt:
        buf = tails.reshape(b, t // ts, CONV_W - 1, f)[:, -1]
    else:
        buf = tails[0]

    ka4 = ka.reshape(b, t, A_KV, HEAD_DIM)
    va4 = p[:, cols["va"]:cols["va"] + A_KV * HEAD_DIM].reshape(b, t, A_KV, HEAD_DIM)
    ki3 = kw[:, :IDX_DIM].reshape(b, t, IDX_DIM)
    nb = min(BAND_CHUNKS * CHUNK, t) if is_prompt else t
    pb = p.reshape(b, t, p.shape[1])[:, t - nb:]
    kb4 = pb[:, :, cols["kb"]:cols["kb"] + mix].reshape(b, nb, nh, HEAD_DIM)
    vb4 = pb[:, :, cols["vb"]:cols["vb"] + mix].reshape(b, nb, nh, HEAD_DIM)
    return x2.reshape(b, t, d), (ka4, va4, ki3, kb4, vb4, hr, hi, buf)


def kernel(x_prompt, x_sample, cache_a_k, cache_a_v, cache_a_kidx, cache_b_k, cache_b_v, state_c_re, state_c_im,
           state_ffn_conv, norm1_g, w_in, rel_bias, ssm_a_re, ssm_a_im, ssm_log_dt, ssm_b_re, ssm_b_im, ssm_c_re,
           ssm_c_im, ssm_d, w_glu, b_glu, w_branch, w_out, norm2_g, w_ffn_in, w_ffn_conv, b_ffn_conv, w_ffn_down,
           normf_g):
    depth = w_in.shape[0]
    d = x_prompt.shape[2]
    pos_p = jnp.arange(x_prompt.shape[1])
    pos_s = cache_a_k.shape[2] + jnp.arange(x_sample.shape[1])
    xp, xs = x_prompt, x_sample
    st_p, st_s = [], []
    caches = dict(a_k=cache_a_k, a_v=cache_a_v, a_kidx=cache_a_kidx, b_k=cache_b_k, b_v=cache_b_v,
                  c_re=state_c_re, c_im=state_c_im, ffn_conv=state_ffn_conv)
    for l in range(depth):
        w_in_l, cols, tn, n_main = _pack_w_in(w_in[l], d)
        prm = dict(norm1=norm1_g[l], w_in=w_in_l, cols=cols, w_in_tn=tn, w_in_main=n_main, rel_bias=rel_bias[l],
                   ssm=_ssm_weights(ssm_a_re[l], ssm_a_im[l], ssm_log_dt[l], ssm_b_re[l], ssm_b_im[l],
                                    ssm_c_re[l], ssm_c_im[l], ssm_d[l]),
                   w_glu=w_glu[l].astype(BF16), b_glu=b_glu[l], w_branch=w_branch[l].astype(BF16),
                   w_out=w_out[l].astype(BF16), norm2=norm2_g[l], w_ffn_in=w_ffn_in[l].astype(BF16),
                   w_conv=w_ffn_conv[l], b_conv=b_ffn_conv[l], w_down=w_ffn_down[l].astype(BF16))
        xp, sp = _layer(xp, pos_p, prm, None, l, True)
        xs, ss = _layer(xs, pos_s, prm, caches, l, False)
        st_p.append(sp)
        st_s.append(ss)
    bp, tp, _ = xp.shape
    bs, tsq, _ = xs.shape
    y_prompt = _norm(xp.reshape(bp * tp, d), normf_g, _row_tile(bp * tp)).reshape(bp, tp, d)
    y_sample = _norm(xs.reshape(bs * tsq, d), normf_g, _row_tile(bs * tsq)).reshape(bs, tsq, d)
    outs_p = [jnp.stack([s[i] for s in st_p]) for i in range(8)]
    outs_s = [jnp.stack([s[i] for s in st_s]) for i in range(8)]
    return (y_prompt, y_sample, *outs_p, *outs_s)
```

```python
import functools
import math

import jax
import jax.numpy as jnp
from jax import lax
from jax.experimental import pallas as pl
from jax.experimental.pallas import tpu as pltpu

F32 = jnp.float32
BF16 = jnp.bfloat16
I32 = jnp.int32

CHUNK = 64
EPS = 1e-6
ROPE_THETA = 500000.0
HEAD_DIM = 128
A_KV = 2
A_GROUP = 4
A_ROT = HEAD_DIM // 4
IDX_HEADS = 8
IDX_DIM = 64
IDX_ROT = IDX_DIM // 4
TOPK_MAX = 256
BAND_CHUNKS = 8
REL_CLIP = 256
C_GROUP = 16
C_STATE = 64
CONV_W = 3
N_BRANCH = 3
ATTN_SCALE = HEAD_DIM ** -0.5
IDX_SCALE = (IDX_DIM * IDX_HEADS) ** -0.5

LANES = 128
SUBLANES = 8
VMEM_LIMIT = 56 * 1024 * 1024

NEG_BIG = -1e30
KEY_NEG_INF = -2139095041
KEY_POS_INF = 2139095040
INT32_MIN = -2147483648

SSM_L = SUBLANES
SSM_GB = LANES // C_GROUP
DSA_TK = 512
DSA_TA = 512
DSA_VROWS = HEAD_DIM + 16
DSA_QB = 128
BAND_QB = 256


def _cparams(sem):
    return pltpu.CompilerParams(dimension_semantics=sem, vmem_limit_bytes=VMEM_LIMIT)


def _gelu(x):
    return 0.5 * x * (1.0 + jnp.tanh(math.sqrt(2.0 / math.pi) * (x + 0.044715 * (x * x * x))))


def _sigmoid(x):
    return 1.0 / (1.0 + jnp.exp(-x))


def _rms(x, g):
    ms = jnp.mean(x * x, axis=-1, keepdims=True)
    return (x * lax.rsqrt(ms + EPS)) * g


def _proj_kernel(x_ref, g_ref, w_ref, o_ref, og_ref, h_ref, *, n_main):
    j = pl.program_id(1)

    @pl.when(j == 0)
    def _():
        h_ref[...] = _rms(x_ref[...], g_ref[...]).astype(BF16)

    @pl.when(j < n_main)
    def _():
        o_ref[...] = jnp.dot(h_ref[...], w_ref[...], preferred_element_type=F32)

    @pl.when(j >= n_main)
    def _():
        og_ref[...] = jnp.dot(h_ref[...], w_ref[...], preferred_element_type=F32).astype(BF16)


def _proj(x, g, w, tm, tn, n_main):
    t, d = x.shape
    n = w.shape[1]
    nj = n // tn
    return pl.pallas_call(
        functools.partial(_proj_kernel, n_main=n_main),
        grid=(t // tm, nj),
        in_specs=[pl.BlockSpec((tm, d), lambda i, j: (i, 0)),
                  pl.BlockSpec((1, d), lambda i, j: (0, 0)),
                  pl.BlockSpec((d, tn), lambda i, j: (0, j))],
        out_specs=[pl.BlockSpec((tm, tn), lambda i, j: (i, jnp.minimum(j, n_main - 1))),
                   pl.BlockSpec((tm, tn), lambda i, j: (i, jnp.maximum(j - n_main, 0)))],
        out_shape=[jax.ShapeDtypeStruct((t, n_main * tn), F32),
                   jax.ShapeDtypeStruct((t, (nj - n_main) * tn), BF16)],
        scratch_shapes=[pltpu.VMEM((tm, d), BF16)],
        compiler_params=_cparams(("arbitrary", "arbitrary")),
        name="proj",
    )(x, g.reshape(1, d), w)


def _rot(x, c, sa, sb, half):
    n = x.shape[-1]
    return x * c + pltpu.roll(x, n - half, 1) * sa + pltpu.roll(x, half, 1) * sb


def _rope_kernel(qa_ref, ka_ref, va_ref, qi_ref, kw_ref, tab_ref,
                 qa_o, ka_o, kab_o, vab_o, qi_o, kw_o, kib_o):
    ca, saa, sba = tab_ref[0], tab_ref[1], tab_ref[2]
    ci, sai, sbi = tab_ref[3], tab_ref[4], tab_ref[5]
    ck, sak, sbk = tab_ref[6], tab_ref[7], tab_ref[8]
    for h in range(qa_ref.shape[1] // LANES):
        sl = slice(h * LANES, (h + 1) * LANES)
        qa_o[:, sl] = _rot(qa_ref[:, sl], ca, saa, sba, A_ROT // 2).astype(BF16)
    for h in range(ka_ref.shape[1] // LANES):
        sl = slice(h * LANES, (h + 1) * LANES)
        k = _rot(ka_ref[:, sl], ca, saa, sba, A_ROT // 2)
        ka_o[:, sl] = k
        kab_o[:, sl] = k.astype(BF16)
    vab_o[...] = va_ref[...].astype(BF16)
    for h in range(qi_ref.shape[1] // LANES):
        sl = slice(h * LANES, (h + 1) * LANES)
        qi_o[:, sl] = _rot(qi_ref[:, sl], ci, sai, sbi, IDX_ROT // 2).astype(BF16)
    kw = _rot(kw_ref[...], ck, sak, sbk, IDX_ROT // 2)
    kw_o[...] = kw
    kib_o[...] = kw.astype(BF16)


def _rope(p, tabs, cols, tm):
    t = p.shape[0]
    n_tab_blocks = tabs.shape[1] // tm
    d_qa, d_kv, d_qi = A_KV * A_GROUP * HEAD_DIM, A_KV * HEAD_DIM, IDX_HEADS * IDX_DIM

    def col(width, off):
        assert off % width == 0
        return pl.BlockSpec((tm, width), lambda i: (i, off // width))

    def out(width):
        return pl.BlockSpec((tm, width), lambda i: (i, 0))

    return pl.pallas_call(
        _rope_kernel,
        grid=(t // tm,),
        in_specs=[col(d_qa, cols["qa"]), col(d_kv, cols["ka"]), col(d_kv, cols["va"]),
                  col(d_qi, cols["qi"]), col(LANES, cols["kw"]),
                  pl.BlockSpec((9, tm, LANES), lambda i: (0, i % n_tab_blocks, 0))],
        out_specs=[out(d_qa), out(d_kv), out(d_kv), out(d_kv), out(d_qi), out(LANES), out(LANES)],
        out_shape=[jax.ShapeDtypeStruct((t, d_qa), BF16),
                   jax.ShapeDtypeStruct((t, d_kv), F32),
                   jax.ShapeDtypeStruct((t, d_kv), BF16),
                   jax.ShapeDtypeStruct((t, d_kv), BF16),
                   jax.ShapeDtypeStruct((t, d_qi), BF16),
                   jax.ShapeDtypeStruct((t, LANES), F32),
                   jax.ShapeDtypeStruct((t, LANES), BF16)],
        compiler_params=_cparams(("arbitrary",)),
        name="rope",
    )(p, p, p, p, p, tabs)


def _rope_tables(pos):
    pos = pos.astype(F32)[:, None]

    def cs(rot):
        half = rot // 2
        inv = jnp.float32(ROPE_THETA) ** (-jnp.arange(half, dtype=F32) / half)
        ang = pos * inv[None, :]
        return jnp.cos(ang), jnp.sin(ang)

    def tabs(rot, width):
        half = rot // 2
        c, s = cs(rot)
        n = pos.shape[0]
        one = jnp.ones((n, width - rot), F32)
        zero = jnp.zeros((n, width - rot), F32)
        zh = jnp.zeros((n, half), F32)
        return (jnp.concatenate([c, c, one], 1), jnp.concatenate([-s, zh, zero], 1),
                jnp.concatenate([zh, s, zero], 1))

    ca, saa, sba = tabs(A_ROT, HEAD_DIM)
    c64, sa64, sb64 = tabs(IDX_ROT, IDX_DIM)
    n = pos.shape[0]
    one64, zero64 = jnp.ones((n, IDX_DIM), F32), jnp.zeros((n, IDX_DIM), F32)
    return jnp.stack([ca, saa, sba,
                      jnp.concatenate([c64, c64], 1), jnp.concatenate([sa64, sa64], 1),
                      jnp.concatenate([sb64, sb64], 1),
                      jnp.concatenate([c64, one64], 1), jnp.concatenate([sa64, zero64], 1),
                      jnp.concatenate([sb64, zero64], 1)])


def _dsa_kernel(qaT_ref, qiT_ref, wiT_ref, k_ref, vT_ref, ki_ref, o_ref, key_ref, acc_ref,
                s0_ref, s1_ref, p0_ref, p1_ref, a0_ref, a1_ref,
                *, s_valid, q_pos0, topk):
    qb, tk = DSA_QB, DSA_TK
    j = pl.program_id(1)
    q_first = q_pos0 + j * qb
    qpos = q_first + lax.broadcasted_iota(I32, (1, qb), 1)
    assert CHUNK == 64 and tk == 512
    n_adm = jnp.minimum(((qpos >> 6) + 1) * CHUNK, s_valid)
    n_max = jnp.minimum((((q_first + qb - 1) >> 6) + 1) * CHUNK, s_valid)
    nkt = (n_max + tk - 1) >> 9

    qi = qiT_ref[0]
    rhs = jnp.concatenate([qi[h * IDX_DIM:(h + 1) * IDX_DIM, :] for h in range(IDX_HEADS)], axis=1)
    rhs = jnp.concatenate([rhs, jnp.zeros((LANES - IDX_DIM, IDX_HEADS * qb), BF16)], axis=0)
    wi = wiT_ref[0]

    def p1(kt, carry):
        off = pl.multiple_of(kt * tk, tk)
        kid = ki_ref[0, pl.ds(off, tk), :]
        s_all = jnp.dot(kid, rhs, preferred_element_type=F32)
        acc = wi[0:1, :] * jnp.maximum(s_all[:, 0:qb], 0.0)
        for h in range(1, IDX_HEADS):
            acc = acc + wi[h:h + 1, :] * jnp.maximum(s_all[:, h * qb:(h + 1) * qb], 0.0)
        score = acc * IDX_SCALE
        kpos = off + lax.broadcasted_iota(I32, (tk, 1), 0)
        score = jnp.where(kpos < n_adm, score, -jnp.inf)
        u = pltpu.bitcast(score, I32)
        key_ref[pl.ds(off, tk), :] = u ^ (lax.shift_right_arithmetic(u, 31) & 0x7FFFFFFF)
        return carry

    lax.fori_loop(0, nkt, p1, 0)

    pairs = key_ref.shape[0] % (2 * tk) == 0
    if pairs:
        @pl.when(nkt % 2 == 1)
        def _():
            key_ref[pl.ds(pl.multiple_of(nkt * tk, tk), tk), :] = jnp.full((tk, qb), INT32_MIN, I32)

    n_trips, per_trip = ((nkt + 1) // 2, 2) if pairs else (nkt, 1)

    def count_ge(cand):
        def body(kt, cnt):
            for half in range(per_trip):
                blk = key_ref[pl.ds(pl.multiple_of((kt * per_trip + half) * tk, tk), tk), :]
                m = jnp.where(blk >= cand, 1, 0).astype(I32)
                cnt = cnt + jnp.sum(m.reshape(tk // SUBLANES, SUBLANES, qb), axis=0)
            return cnt

        cnt8 = lax.fori_loop(0, n_trips, body, jnp.zeros((SUBLANES, qb), I32))
        return jnp.sum(cnt8, axis=0, keepdims=True)

    def bit_body(i, carry):
        prefix, n_ge = carry
        cand_u = prefix | lax.shift_left(jnp.int32(1), 31 - i)
        cnt = count_ge(cand_u ^ INT32_MIN)
        ok = cnt >= topk
        return jnp.where(ok, cand_u, prefix), jnp.where(ok, cnt, n_ge)

    prefix, n_ge = lax.fori_loop(0, 32, bit_body, (jnp.zeros((1, qb), I32), jnp.broadcast_to(nkt * tk, (1, qb))))
    thr = prefix ^ INT32_MIN
    cut_ties = jnp.max(jnp.where((n_ge > topk) & (thr > KEY_NEG_INF), 1, 0)) > 0

    @pl.when(cut_ties)
    def _():
        r = lax.broadcasted_iota(I32, (tk, tk), 0)
        c = lax.broadcasted_iota(I32, (tk, tk), 1)
        tri = jnp.where(c <= r, 1.0, 0.0).astype(BF16)
        take = (topk - count_ge(thr + 1)).astype(F32)

        def body(kt, seen):
            off = pl.multiple_of(kt * tk, tk)
            blk = key_ref[pl.ds(off, tk), :]
            eq = blk == thr
            incl = jnp.dot(tri, jnp.where(eq, 1.0, 0.0).astype(BF16), preferred_element_type=F32)
            drop = eq & (seen + incl > take)
            key_ref[pl.ds(off, tk), :] = jnp.where(drop, thr - 1, blk)
            return seen + incl[tk - 1:tk, :]

        lax.fori_loop(0, nkt, body, jnp.zeros((1, qb), F32))

    thr_lo = jnp.maximum(thr, KEY_NEG_INF + 1)
    qa = qaT_ref[0]
    ng = A_GROUP * qb
    qg = [jnp.concatenate([qa[(g * A_GROUP + hh) * HEAD_DIM:(g * A_GROUP + hh + 1) * HEAD_DIM, :]
                           for hh in range(A_GROUP)], axis=1) for g in range(A_KV)]
    acc_ref[...] = jnp.zeros(acc_ref.shape, F32)

    c_exp = ATTN_SCALE * math.log2(math.e)

    ta = DSA_TA
    n_steps = nkt * (tk // ta)
    last = n_steps - 1
    bufs = ((s0_ref, p0_ref, a0_ref), (s1_ref, p1_ref, a1_ref))

    def logits(step, s_ref):
        off = pl.multiple_of(jnp.minimum(step, last) * ta, ta)
        for g in range(A_KV):
            kg = k_ref[0, pl.ds(off, ta), g * HEAD_DIM:(g + 1) * HEAD_DIM]
            s_ref[g] = jnp.dot(kg, qg[g], preferred_element_type=F32)

    def numerators(step, s_ref, p_ref, a_ref, ms):
        off = pl.multiple_of(jnp.minimum(step, last) * ta, ta)
        blk = key_ref[pl.ds(off, ta), :]
        sel = (blk >= thr_lo) & (blk < KEY_POS_INF) & (step <= last)
        mask = jnp.concatenate([jnp.where(sel, 0.0, NEG_BIG)] * A_GROUP, axis=1)
        new = []
        for g in range(A_KV):
            s = s_ref[g] + mask
            m_new = jnp.maximum(ms[g], jnp.max(s, axis=0, keepdims=True))
            a_ref[g] = jnp.broadcast_to(jnp.exp2((ms[g] - m_new) * c_exp), (SUBLANES, ng))
            p_ref[g] = jnp.exp2((s - m_new).astype(BF16) * c_exp)
            new.append(m_new)
        return tuple(new)

    def weighted_values(step, p_ref, a_ref):
        off = pl.multiple_of(jnp.clip(step, 0, last) * ta, ta)
        for g in range(A_KV):
            vg = vT_ref[0, g * DSA_VROWS:(g + 1) * DSA_VROWS, pl.ds(off, ta)]
            acc_ref[g] = a_ref[g, 0:1, :] * acc_ref[g] + jnp.dot(vg, p_ref[g], preferred_element_type=F32)

    p1_ref[...] = jnp.zeros(p1_ref.shape, BF16)
    a1_ref[...] = jnp.ones(a1_ref.shape, F32)
    logits(0, s0_ref)

    def two_steps(i2, ms):
        i = 2 * i2
        for par in range(2):
            cur, nxt = bufs[par], bufs[1 - par]
            logits(i + par + 1, nxt[0])
            ms = numerators(i + par, cur[0], cur[1], cur[2], ms)
            weighted_values(i + par - 1, nxt[1], nxt[2])
        return ms

    init = (jnp.full((1, ng), NEG_BIG, F32),) * A_KV
    n_pairs = (n_steps + 1) // 2
    lax.fori_loop(0, n_pairs, two_steps, init)
    weighted_values(2 * n_pairs - 1, p1_ref, a1_ref)
    for g in range(A_KV):
        o = acc_ref[g, 0:HEAD_DIM, :] / acc_ref[g, HEAD_DIM:HEAD_DIM + 1, :]
        for hh in range(A_GROUP):
            h = g * A_GROUP + hh
            o_ref[0, h * HEAD_DIM:(h + 1) * HEAD_DIM, :] = o[:, hh * qb:(hh + 1) * qb].astype(o_ref.dtype)


def _dsa(qaT, qiT, wiT, k, vT, ki, *, s_valid, q_pos0, topk):
    b, dq, sq = qaT.shape
    s_pad = k.shape[1]
    assert s_pad % DSA_TK == 0 and s_pad % DSA_TA == 0 and sq % DSA_QB == 0 and topk <= DSA_TK
    kern = functools.partial(_dsa_kernel, s_valid=s_valid, q_pos0=q_pos0, topk=topk)
    return pl.pallas_call(
        kern,
        grid=(b, sq // DSA_QB),
        in_specs=[pl.BlockSpec((1, dq, DSA_QB), lambda i, j: (i, 0, j)),
                  pl.BlockSpec((1, qiT.shape[1], DSA_QB), lambda i, j: (i, 0, j)),
                  pl.BlockSpec((1, IDX_HEADS, DSA_QB), lambda i, j: (i, 0, j)),
                  pl.BlockSpec((1, s_pad, k.shape[2]), lambda i, j: (i, 0, 0)),
                  pl.BlockSpec((1, vT.shape[1], s_pad), lambda i, j: (i, 0, 0)),
                  pl.BlockSpec((1, s_pad, LANES), lambda i, j: (i, 0, 0))],
        out_specs=pl.BlockSpec((1, dq, DSA_QB), lambda i, j: (i, 0, j)),
        out_shape=jax.ShapeDtypeStruct((b, dq, sq), BF16),
        scratch_shapes=[pltpu.VMEM((s_pad, DSA_QB), I32),
                        pltpu.VMEM((A_KV, DSA_VROWS, A_GROUP * DSA_QB), F32)]
                       + [pltpu.VMEM((A_KV, DSA_TA, A_GROUP * DSA_QB), F32)] * 2
                       + [pltpu.VMEM((A_KV, DSA_TA, A_GROUP * DSA_QB), BF16)] * 2
                       + [pltpu.VMEM((A_KV, SUBLANES, A_GROUP * DSA_QB), F32)] * 2,
        compiler_params=_cparams(("arbitrary", "arbitrary")),
        name="dsa",
    )(qaT, qiT, wiT, k, vT, ki)


def _cache_prep_kernel(ck_ref, cv_ref, cki_ref, kn_ref, vn_ref, kin_ref, k_o, vT_o, ki_o, *, n_cache_tiles):
    kt = pl.program_id(1)
    tk = k_o.shape[1]
    extra = DSA_VROWS - HEAD_DIM
    ones_rows = jnp.where(lax.broadcasted_iota(I32, (extra, tk), 0) == 0, 1.0, 0.0).astype(BF16)

    def emit(k_head, v_head, ki):
        for g in range(A_KV):
            k_o[0, :, g * HEAD_DIM:(g + 1) * HEAD_DIM] = k_head(g).astype(BF16)
            vT_o[0, g * DSA_VROWS:g * DSA_VROWS + HEAD_DIM, :] = v_head(g).astype(F32).T.astype(BF16)
            vT_o[0, g * DSA_VROWS + HEAD_DIM:(g + 1) * DSA_VROWS, :] = ones_rows
        ki_o[0] = ki

    @pl.when(kt < n_cache_tiles)
    def _():
        ki = jnp.concatenate([cki_ref[0, 0], jnp.zeros((tk, LANES - IDX_DIM), F32)], axis=1)
        emit(lambda g: ck_ref[0, 0, :, g, :], lambda g: cv_ref[0, 0, :, g, :], ki.astype(BF16))

    @pl.when(kt >= n_cache_tiles)
    def _():
        t = kn_ref.shape[0]
        rows = lambda a: jnp.concatenate([a, jnp.zeros((tk - t, a.shape[1]), a.dtype)], axis=0)
        kn, vn = rows(kn_ref[...]), rows(vn_ref[...])
        head = lambda a: lambda g: a[:, g * HEAD_DIM:(g + 1) * HEAD_DIM]
        emit(head(kn), head(vn), rows(kin_ref[...]))


def _cache_prep(cache_k, cache_v, cache_ki, layer, kn, vn, kin, b, t):
    tk = 2 * DSA_TK
    past = cache_k.shape[2]
    assert past % tk == 0 and t <= tk
    nct = past // tk
    s_pad = past + tk
    cidx = lambda i, j: jnp.minimum(j, nct - 1)
    return pl.pallas_call(
        functools.partial(_cache_prep_kernel, n_cache_tiles=nct),
        grid=(b, nct + 1),
        in_specs=[pl.BlockSpec((1, 1, tk, A_KV, HEAD_DIM), lambda i, j: (layer, i, cidx(i, j), 0, 0)),
                  pl.BlockSpec((1, 1, tk, A_KV, HEAD_DIM), lambda i, j: (layer, i, cidx(i, j), 0, 0)),
                  pl.BlockSpec((1, 1, tk, IDX_DIM), lambda i, j: (layer, i, cidx(i, j), 0)),
                  pl.BlockSpec((t, A_KV * HEAD_DIM), lambda i, j: (i, 0)),
                  pl.BlockSpec((t, A_KV * HEAD_DIM), lambda i, j: (i, 0)),
                  pl.BlockSpec((t, LANES), lambda i, j: (i, 0))],
        out_specs=[pl.BlockSpec((1, tk, A_KV * HEAD_DIM), lambda i, j: (i, j, 0)),
                   pl.BlockSpec((1, A_KV * DSA_VROWS, tk), lambda i, j: (i, 0, j)),
                   pl.BlockSpec((1, tk, LANES), lambda i, j: (i, j, 0))],
        out_shape=[jax.ShapeDtypeStruct((b, s_pad, A_KV * HEAD_DIM), BF16),
                   jax.ShapeDtypeStruct((b, A_KV * DSA_VROWS, s_pad), BF16),
                   jax.ShapeDtypeStruct((b, s_pad, LANES), BF16)],
        compiler_params=_cparams(("arbitrary", "arbitrary")),
        name="cache_prep",
    )(cache_k, cache_v, cache_ki, kn, vn, kin)


def _band_heads(q, k_head, v_head, bias_ref, n_hidden, o_ref, s_ref):
    nh = q.shape[1] // HEAD_DIM
    tk = bias_ref.shape[2]
    hidden = None if n_hidden is None else lax.broadcasted_iota(I32, (1, tk), 1) < n_hidden
    for h in range(nh):
        sl = slice(h * HEAD_DIM, (h + 1) * HEAD_DIM)
        s = lax.dot_general(q[:, sl].astype(BF16), k_head(h), (((1,), (1,)), ((), ())),
                            preferred_element_type=F32) * ATTN_SCALE + bias_ref[h]
        if hidden is not None:
            s = jnp.where(hidden, NEG_BIG, s)
        s_ref[h] = s
    ms = [jnp.max(s_ref[h], axis=1, keepdims=True) for h in range(nh)]
    ls = []
    for h in range(nh):
        p = jnp.exp(s_ref[h] - ms[h])
        s_ref[h] = p
        ls.append(jnp.sum(p, axis=1, keepdims=True))
    for h in range(nh):
        sl = slice(h * HEAD_DIM, (h + 1) * HEAD_DIM)
        w = (s_ref[h] / ls[h]).astype(BF16)
        o_ref[:, sl] = jnp.dot(w, v_head(h), preferred_element_type=F32).astype(o_ref.dtype)


def _band_prompt_kernel(*refs, n_tiles):
    q_ref = refs[0]
    k_refs = refs[1:1 + n_tiles]
    v_refs = refs[1 + n_tiles:1 + 2 * n_tiles]
    bias_ref = refs[1 + 2 * n_tiles]
    o_ref = refs[2 + 2 * n_tiles]
    s_ref = refs[3 + 2 * n_tiles]
    j = pl.program_id(1)
    n_hidden = jnp.maximum(n_tiles - 1 - j, 0) * q_ref.shape[0]
    k_all = jnp.concatenate([r[...] for r in k_refs], axis=0).astype(BF16)
    v_all = jnp.concatenate([r[...] for r in v_refs], axis=0).astype(BF16)
    head = lambda a: lambda h: a[:, h * HEAD_DIM:(h + 1) * HEAD_DIM]
    _band_heads(q_ref[...], head(k_all), head(v_all), bias_ref, n_hidden, o_ref, s_ref)


def _band_prompt(p, bias, cols, batch, seq):
    qb = BAND_QB
    n_tiles = BAND_CHUNKS * CHUNK // qb + 1
    width = bias.shape[0] * HEAD_DIM
    nq = seq // qb
    cq, ck, cv = cols["qb"] // width, cols["kb"] // width, cols["vb"] // width
    assert cols["qb"] % width == 0 and cols["kb"] % width == 0 and cols["vb"] % width == 0

    def kv_spec(c, i):
        return pl.BlockSpec((qb, width), lambda b, j: (b * nq + jnp.maximum(j - (n_tiles - 1) + i, 0), c))

    return pl.pallas_call(
        functools.partial(_band_prompt_kernel, n_tiles=n_tiles),
        grid=(batch, nq),
        in_specs=([pl.BlockSpec((qb, width), lambda b, j: (b * nq + j, cq))]
                  + [kv_spec(ck, i) for i in range(n_tiles)]
                  + [kv_spec(cv, i) for i in range(n_tiles)]
                  + [pl.BlockSpec(bias.shape, lambda b, j: (0, 0, 0))]),
        out_specs=pl.BlockSpec((qb, width), lambda b, j: (b * nq + j, 0)),
        out_shape=jax.ShapeDtypeStruct((batch * seq, width), BF16),
        compiler_params=_cparams(("arbitrary", "arbitrary")),
        scratch_shapes=[pltpu.VMEM(bias.shape, F32)],
        name="band_prompt",
    )(*([p] * (1 + 2 * n_tiles)), bias)


def _band_sample_kernel(q_ref, k_ref, v_ref, ck_ref, cv_ref, bias_ref, o_ref, s_ref):
    def head(cache_ref, new_ref):
        return lambda h: jnp.concatenate(
            [cache_ref[0, 0, :, h, :], new_ref[:, h * HEAD_DIM:(h + 1) * HEAD_DIM]], axis=0).astype(BF16)

    _band_heads(q_ref[...], head(ck_ref, k_ref), head(cv_ref, v_ref), bias_ref, None, o_ref, s_ref)


def _band_sample(p, cache_k, cache_v, layer, bias, cols, batch, t):
    nh = bias.shape[0]
    width = nh * HEAD_DIM
    nbc = cache_k.shape[2]
    cq, ck, cv = cols["qb"] // width, cols["kb"] // width, cols["vb"] // width
    return pl.pallas_call(
        _band_sample_kernel,
        grid=(batch,),
        in_specs=[pl.BlockSpec((t, width), lambda b: (b, cq)),
                  pl.BlockSpec((t, width), lambda b: (b, ck)),
                  pl.BlockSpec((t, width), lambda b: (b, cv)),
                  pl.BlockSpec((1, 1, nbc, nh, HEAD_DIM), lambda b: (layer, b, 0, 0, 0)),
                  pl.BlockSpec((1, 1, nbc, nh, HEAD_DIM), lambda b: (layer, b, 0, 0, 0)),
                  pl.BlockSpec(bias.shape, lambda b: (0, 0, 0))],
        out_specs=pl.BlockSpec((t, width), lambda b: (b, 0)),
        out_shape=jax.ShapeDtypeStruct((batch * t, width), BF16),
        compiler_params=_cparams(("arbitrary",)),
        scratch_shapes=[pltpu.VMEM(bias.shape, F32)],
        name="band_sample",
    )(p, p, p, cache_k, cache_v, bias)


def _band_bias(table, q0, nq, k0, nk):
    n = nq + nk
    j = jnp.arange(n)
    g = table[:, jnp.clip(j + (q0 - k0 - nk + 1), -REL_CLIP, REL_CLIP) + REL_CLIP].astype(F32)
    hank = jnp.tile(g, (1, nq + 1))[:, :nq * (n + 1)].reshape(-1, nq, n + 1)[:, :, :nk]
    bias = hank[:, :, ::-1]
    qc = (q0 + jnp.arange(nq))[:, None] // CHUNK
    kc = (k0 + jnp.arange(nk))[None, :] // CHUNK
    vis = (kc <= qc) & (kc >= qc - BAND_CHUNKS)
    return jnp.where(vis[None], bias, NEG_BIG)


def _ssm_kernel(u_ref, t_ref, p_ref, q_ref, a_ref, h0_ref, y_ref, hre_o, him_o, sre_ref, sim_ref,
                *, seg, tiles_per_seq):
    L = SSM_L
    x = jnp.concatenate([u_ref[:, s, :] for s in range(L)], axis=1).astype(BF16)
    sre_ref[...] = jnp.dot(x, p_ref[0, 0], preferred_element_type=F32)
    sim_ref[...] = jnp.dot(x, p_ref[0, 1], preferred_element_type=F32)
    rows, w = sre_ref.shape
    kseg = lax.broadcasted_iota(I32, (SUBLANES, 1), 0) % seg
    apk_re, apk_im = a_ref[0, 0, 0:SUBLANES, :], a_ref[0, 1, 0:SUBLANES, :]

    def cmad(x_re, x_im, a_re, a_im, y_re, y_im):
        return x_re + a_re * y_re - a_im * y_im, x_im + a_re * y_im + a_im * y_re

    def tile(i, carry):
        r = pl.ds(pl.multiple_of(i * SUBLANES, SUBLANES), SUBLANES)
        x_re, x_im = sre_ref[r, :], sim_ref[r, :]
        for n, d in enumerate((1, 2, 4)):
            if d < seg:
                ad_re = a_ref[0, 0, SUBLANES + n:SUBLANES + n + 1, :]
                ad_im = a_ref[0, 1, SUBLANES + n:SUBLANES + n + 1, :]
                sh_re = jnp.where(kseg >= d, pltpu.roll(x_re, d, 0), 0.0)
                sh_im = jnp.where(kseg >= d, pltpu.roll(x_im, d, 0), 0.0)
                x_re, x_im = cmad(x_re, x_im, ad_re, ad_im, sh_re, sh_im)
        if tiles_per_seq > 1:
            first = (i % tiles_per_seq) == 0
            hin_re = jnp.where(first, h0_ref[0, 0], carry[0])
            hin_im = jnp.where(first, h0_ref[0, 1], carry[1])
        else:
            hin_re, hin_im = h0_ref[0, 0, r, :], h0_ref[0, 1, r, :]
        inc_re, inc_im = cmad(x_re, x_im, apk_re, apk_im, hin_re, hin_im)
        sre_ref[r, :] = jnp.where(kseg == 0, hin_re, pltpu.roll(inc_re, 1, 0))
        sim_ref[r, :] = jnp.where(kseg == 0, hin_im, pltpu.roll(inc_im, 1, 0))
        hre_o[0, r, :] = inc_re
        him_o[0, r, :] = inc_im
        return inc_re[SUBLANES - 1:SUBLANES, :], inc_im[SUBLANES - 1:SUBLANES, :]

    zero = jnp.zeros((1, w), F32)
    lax.fori_loop(0, rows // SUBLANES, tile, (zero, zero))
    y = (jnp.dot(x, t_ref[0], preferred_element_type=F32)
         + jnp.dot(sre_ref[...].astype(BF16), q_ref[0, 0], preferred_element_type=F32)
         + jnp.dot(sim_ref[...].astype(BF16), q_ref[0, 1], preferred_element_type=F32))
    for t in range(L):
        y_ref[:, t, :] = y[:, t * LANES:(t + 1) * LANES]


def _ssm(p3, uc_col, tmat, pmat, qmat, apow, h0, rows_blk, seg, tiles_per_seq):
    rows_total = p3.shape[0]
    nb, _, _, w = pmat.shape
    kl = SSM_L * LANES
    h0_rows = h0.shape[2] // (rows_total // rows_blk)
    hspec = pl.BlockSpec((1, rows_blk, w), lambda i, r: (i, r, 0))
    return pl.pallas_call(
        functools.partial(_ssm_kernel, seg=seg, tiles_per_seq=tiles_per_seq),
        grid=(nb, rows_total // rows_blk),
        in_specs=[pl.BlockSpec((rows_blk, SSM_L, LANES), lambda i, r: (r, 0, uc_col // LANES + i)),
                  pl.BlockSpec((1, kl, kl), lambda i, r: (i, 0, 0)),
                  pl.BlockSpec((1, 2, kl, w), lambda i, r: (i, 0, 0, 0)),
                  pl.BlockSpec((1, 2, w, kl), lambda i, r: (i, 0, 0, 0)),
                  pl.BlockSpec((1, 2, 2 * SUBLANES, w), lambda i, r: (i, 0, 0, 0)),
                  pl.BlockSpec((1, 2, h0_rows, w), lambda i, r: (i, 0, r, 0))],
        out_specs=[pl.BlockSpec((rows_blk, SSM_L, LANES), lambda i, r: (r, 0, i)), hspec, hspec],
        out_shape=[jax.ShapeDtypeStruct((rows_total, SSM_L, nb * LANES), F32),
                   jax.ShapeDtypeStruct((nb, rows_total, w), F32),
                   jax.ShapeDtypeStruct((nb, rows_total, w), F32)],
        scratch_shapes=[pltpu.VMEM((rows_blk, w), F32), pltpu.VMEM((rows_blk, w), F32)],
        compiler_params=_cparams(("arbitrary", "arbitrary")),
        name="ssm",
    )(p3, tmat, pmat, qmat, apow, h0)


def _ssm_weights(a_re, a_im, log_dt, b_re, b_im, c_re, c_im, d):
    hp = lax.Precision.HIGHEST
    g, p = a_re.shape
    gc = b_re.shape[2]
    L = SSM_L
    a_re, a_im = a_re.astype(F32), a_im.astype(F32)
    dt = jnp.exp(log_dt.astype(F32))[:, None]
    mag = jnp.exp(dt * a_re)
    ab_re, ab_im = mag * jnp.cos(dt * a_im), mag * jnp.sin(dt * a_im)
    n_re = ab_re - 1.0
    n_im = ab_im
    den = a_re * a_re + a_im * a_im
    cc_re = (n_re * a_re + n_im * a_im) / den
    cc_im = (n_im * a_re - n_re * a_im) / den
    b_re, b_im = b_re.astype(F32), b_im.astype(F32)
    bb_re = cc_re[..., None] * b_re - cc_im[..., None] * b_im
    bb_im = cc_re[..., None] * b_im + cc_im[..., None] * b_re
    c_re, c_im = c_re.astype(F32), c_im.astype(F32)

    def cmul(x, y):
        return x[0] * y[0] - x[1] * y[1], x[0] * y[1] + x[1] * y[0]

    rep = lambda a: jnp.broadcast_to(a[None], (L,) + a.shape)
    pw_re, pw_im = lax.associative_scan(cmul, (rep(ab_re), rep(ab_im)), axis=0)
    pw_re = jnp.concatenate([jnp.ones((1, g, p), F32), pw_re], axis=0)
    pw_im = jnp.concatenate([jnp.zeros((1, g, p), F32), pw_im], axis=0)
    ca_re = c_re[None] * pw_re[:, :, None, :] - c_im[None] * pw_im[:, :, None, :]
    ca_im = c_re[None] * pw_im[:, :, None, :] + c_im[None] * pw_re[:, :, None, :]
    taps = (jnp.einsum("tgop,gpi->tgoi", ca_re[:L], bb_re, precision=hp)
            - jnp.einsum("tgop,gpi->tgoi", ca_im[:L], bb_im, precision=hp))
    taps = taps.at[0].add(jax.vmap(jnp.diag)(d.astype(F32)))
    s_idx = jnp.arange(L)[:, None]
    t_idx = jnp.arange(L)[None, :]
    lag = t_idx - s_idx
    tt = jnp.where((lag >= 0)[:, :, None, None, None], taps[jnp.clip(lag, 0, L - 1)], 0.0)
    gb = SSM_GB
    nb = g // gb
    n_x = L * gb * gc
    x = jnp.arange(n_x)
    x_grp, x_loc = (x // gc) % gb, (x // (gb * gc)) * gc + x % gc
    place_x = ((x_grp[None, :, None] == jnp.arange(gb)[:, None, None])
               & (x_loc[None, :, None] == jnp.arange(L * gc)[None, None, :])).astype(F32)
    q = jnp.arange(gb * p)
    place_q = ((q[None, :, None] // p == jnp.arange(gb)[:, None, None])
               & (q[None, :, None] % p == jnp.arange(p)[None, None, :])).astype(F32)
    blk = lambda a: a.reshape((nb, gb) + a.shape[1:])
    t_grp = blk(tt.transpose(2, 0, 4, 1, 3).reshape(g, L * gc, L * gc))
    tmat = jnp.einsum("gxr,bgrc,gyc->bxy", place_x, t_grp, place_x)
    rev = L - 1 - jnp.arange(L)
    pin_re = pw_re[rev][:, :, :, None] * bb_re[None] - pw_im[rev][:, :, :, None] * bb_im[None]
    pin_im = pw_re[rev][:, :, :, None] * bb_im[None] + pw_im[rev][:, :, :, None] * bb_re[None]
    to_p = lambda a: jnp.einsum("gxr,bgrp,gqp->bxq", place_x,
                                blk(a.transpose(1, 0, 3, 2).reshape(g, L * gc, p)), place_q)
    to_q = lambda a: jnp.einsum("gqp,bgpc,gyc->bqy", place_q,
                                blk(a.transpose(1, 3, 0, 2).reshape(g, p, L * gc)), place_x)
    pmat = jnp.stack([to_p(pin_re), to_p(pin_im)], axis=1)
    qmat = jnp.stack([to_q(ca_re[1:]), -to_q(ca_im[1:])], axis=1)
    rep8 = lambda a: jnp.broadcast_to(a[None], (SUBLANES,) + a.shape)
    apl_re, apl_im = lax.associative_scan(cmul, (rep8(pw_re[L]), rep8(pw_im[L])), axis=0)
    apl = jnp.stack([apl_re, apl_im]).reshape(2, SUBLANES, nb, gb * p)
    return tmat.astype(BF16), pmat.astype(BF16), qmat.astype(BF16), apl


def _ssm_apply(p, uc_col, b, t, h0_re, h0_im, weights, one_seq_per_step):
    tmat, pmat, qmat, apl = weights
    nb, w = pmat.shape[0], pmat.shape[3]
    L = SSM_L
    nc = t // L
    assert t % L == 0 and (nc % SUBLANES == 0 or SUBLANES % nc == 0)
    seg = min(nc, SUBLANES)
    if one_seq_per_step:
        assert nc % SUBLANES == 0
        rows_blk, tiles_per_seq, reps = nc, nc // SUBLANES, SUBLANES
    else:
        rows_blk, tiles_per_seq, reps = b * nc, 1, nc
        assert nc <= SUBLANES and rows_blk % SUBLANES == 0
    pw_rows = [k % seg for k in range(SUBLANES)] + [0, 1, 3] + [0] * (SUBLANES - 3)
    apow = jnp.stack([apl[:, k] for k in pw_rows], axis=1).transpose(2, 0, 1, 3)

    def pack(h):
        h = h.astype(F32).reshape(b, nb, w).transpose(1, 0, 2)
        return jnp.repeat(h, reps, axis=1)

    h0 = jnp.stack([pack(h0_re), pack(h0_im)], axis=1)
    p3 = p.reshape(b * nc, L, p.shape[1])
    y, hre, him = _ssm(p3, uc_col, tmat, pmat, qmat, apow, h0, rows_blk, seg, tiles_per_seq)

    def last(h):
        return h.reshape(nb, b, nc, w)[:, :, nc - 1].transpose(1, 0, 2).reshape(b, nb * SSM_GB, C_STATE)

    return y.reshape(b * t, nb * LANES), last(hre), last(him)


def _glu_kernel(y_ref, w_ref, b_ref, o_ref):
    z = _gelu(y_ref[...])
    a = jnp.dot(z.astype(BF16), w_ref[...], preferred_element_type=F32) + b_ref[...]
    o_ref[...] = (z * _sigmoid(a)).astype(o_ref.dtype)


def _glu(y, w, b, tm):
    t, n = y.shape
    return pl.pallas_call(
        _glu_kernel,
        grid=(t // tm,),
        in_specs=[pl.BlockSpec((tm, n), lambda i: (i, 0)),
                  pl.BlockSpec((n, n), lambda i: (0, 0)),
                  pl.BlockSpec((1, n), lambda i: (0, 0))],
        out_specs=pl.BlockSpec((tm, n), lambda i: (i, 0)),
        out_shape=jax.ShapeDtypeStruct((t, n), BF16),
        compiler_params=_cparams(("arbitrary",)),
        name="glu",
    )(y, w, b.reshape(1, n))


def _merge_kernel(oa_ref, ob_ref, oc_ref, ga_ref, gb_ref, gc_ref, w_ref, o_ref):
    acc = None
    for n, (o, g) in enumerate(((oa_ref, ga_ref), (ob_ref, gb_ref), (oc_ref, gc_ref))):
        br = jnp.dot(o[...], w_ref[n], preferred_element_type=F32)
        t = _sigmoid(g[...].astype(F32)) * br
        acc = t if acc is None else acc + t
    o_ref[...] = acc.astype(o_ref.dtype)


def _merge(oa, ob, oc, gates, w, tm, tn):
    t, m = oa.shape
    d = w.shape[2]
    assert d % tn == 0

    def gspec(n):
        return pl.BlockSpec((tm, tn), lambda i, j: (i, (n * d) // tn + j))

    ospec = pl.BlockSpec((tm, m), lambda i, j: (i, 0))
    return pl.pallas_call(
        _merge_kernel,
        grid=(t // tm, d // tn),
        in_specs=[ospec, ospec, ospec, gspec(0), gspec(1), gspec(2),
                  pl.BlockSpec((N_BRANCH, m, tn), lambda i, j: (0, 0, j))],
        out_specs=pl.BlockSpec((tm, tn), lambda i, j: (i, j)),
        out_shape=jax.ShapeDtypeStruct((t, d), BF16),
        compiler_params=_cparams(("arbitrary", "arbitrary")),
        name="merge",
    )(oa, ob, oc, gates, gates, gates, w)


def _mmres_kernel(a_ref, w_ref, r_ref, o_ref):
    o_ref[...] = r_ref[...] + jnp.dot(a_ref[...], w_ref[...], preferred_element_type=F32)


def _mmres(a, w, res, tm, tn):
    t, k = a.shape
    n = w.shape[1]
    return pl.pallas_call(
        _mmres_kernel,
        grid=(t // tm, n // tn),
        in_specs=[pl.BlockSpec((tm, k), lambda i, j: (i, 0)),
                  pl.BlockSpec((k, tn), lambda i, j: (0, j)),
                  pl.BlockSpec((tm, tn), lambda i, j: (i, j))],
        out_specs=pl.BlockSpec((tm, tn), lambda i, j: (i, j)),
        out_shape=jax.ShapeDtypeStruct((t, n), F32),
        compiler_params=_cparams(("arbitrary", "arbitrary")),
        name="mmres",
    )(a, w, res)


def _ffn_in_kernel(x_ref, g_ref, wu_ref, wv_ref, wc_ref, bc_ref, st_ref, o_ref, tail_ref,
                   h_ref, up_ref, carry_ref, *, ns, ts, tiles_per_seq):
    i, j = pl.program_id(0), pl.program_id(1)
    pad = SUBLANES

    @pl.when(j == 0)
    def _():
        h_ref[...] = _rms(x_ref[...], g_ref[...]).astype(BF16)

    @pl.when((i == 0) & (j == 0))
    def _():
        carry_ref[...] = jnp.zeros(carry_ref.shape, F32)

    h = h_ref[...]
    u = jnp.dot(h, wu_ref[...], preferred_element_type=F32)
    tn = u.shape[1]
    up_ref[:, pad:, :] = u.reshape(ns, ts, tn)
    first = (i % tiles_per_seq) == 0
    up_ref[:, pad - 2:pad, :] = jnp.where(first, st_ref[...], carry_ref[j])
    last2 = up_ref[:, ts + pad - 2:ts + pad, :]
    carry_ref[j] = last2
    tail_ref[0] = last2
    c = bc_ref[...].reshape(1, 1, tn)
    for tap in range(CONV_W):
        c = c + up_ref[:, pad - 2 + tap:pad - 2 + tap + ts, :] * wc_ref[tap:tap + 1, :].reshape(1, 1, tn)
    gl = _gelu(c).reshape(ns * ts, tn)
    v = jnp.dot(h, wv_ref[...], preferred_element_type=F32)
    o_ref[...] = (gl * v).astype(o_ref.dtype)


def _ffn_in(x, g, w, wc, bc, state, ns, ts, tn):
    t, d = x.shape
    f = w.shape[1] // 2
    tm = ns * ts
    nj = f // tn
    seq_len = t // state.shape[0]
    tiles_per_seq = max(seq_len // tm, 1)
    kern = functools.partial(_ffn_in_kernel, ns=ns, ts=ts, tiles_per_seq=tiles_per_seq)
    return pl.pallas_call(
        kern,
        grid=(t // tm, nj),
        in_specs=[pl.BlockSpec((tm, d), lambda i, j: (i, 0)),
                  pl.BlockSpec((1, d), lambda i, j: (0, 0)),
                  pl.BlockSpec((d, tn), lambda i, j: (0, j)),
                  pl.BlockSpec((d, tn), lambda i, j: (0, nj + j)),
                  pl.BlockSpec((CONV_W, tn), lambda i, j: (0, j)),
                  pl.BlockSpec((1, tn), lambda i, j: (0, j)),
                  pl.BlockSpec((ns, CONV_W - 1, tn), lambda i, j: (i // tiles_per_seq, 0, j))],
        out_specs=[pl.BlockSpec((tm, tn), lambda i, j: (i, j)),
                   pl.BlockSpec((1, ns, CONV_W - 1, tn), lambda i, j: (i, 0, 0, j))],
        out_shape=[jax.ShapeDtypeStruct((t, f), BF16),
                   jax.ShapeDtypeStruct((t // tm, ns, CONV_W - 1, f), F32)],
        scratch_shapes=[pltpu.VMEM((tm, d), BF16),
                        pltpu.VMEM((ns, ts + SUBLANES, tn), F32),
                        pltpu.VMEM((nj, ns, CONV_W - 1, tn), F32)],
        compiler_params=_cparams(("arbitrary", "arbitrary")),
        name="ffn_in",
    )(x, g.reshape(1, d), w, w, wc, bc.reshape(1, f), state)


def _norm_kernel(x_ref, g_ref, o_ref):
    o_ref[...] = _rms(x_ref[...], g_ref[...])


def _norm(x, g, tm):
    t, d = x.shape
    return pl.pallas_call(
        _norm_kernel,
        grid=(t // tm,),
        in_specs=[pl.BlockSpec((tm, d), lambda i: (i, 0)), pl.BlockSpec((1, d), lambda i: (0, 0))],
        out_specs=pl.BlockSpec((tm, d), lambda i: (i, 0)),
        out_shape=jax.ShapeDtypeStruct((t, d), F32),
        compiler_params=_cparams(("arbitrary",)),
        name="final_norm",
    )(x, g.reshape(1, d))


def _row_tile(t, cap=1024):
    tm = min(t, cap)
    while t % tm:
        tm //= 2
    return tm


def _pack_w_in(w_in, d_model):
    mix = d_model // 2
    sizes = dict(qa=mix, ka=A_KV * HEAD_DIM, va=A_KV * HEAD_DIM, qi=IDX_HEADS * IDX_DIM, ki=IDX_DIM,
                 wi=IDX_HEADS, qb=mix, kb=mix, vb=mix, uc=mix, g=N_BRANCH * d_model)
    src_order = ["qa", "ka", "va", "qi", "ki", "wi", "qb", "kb", "vb", "uc", "g"]
    src, off = {}, 0
    for name in src_order:
        src[name] = (off, sizes[name])
        off += sizes[name]
    assert off == w_in.shape[1]
    tn = 1280
    dst_order = ["qa", "qb", "kb", "vb", "uc", "qi", "ka", "va", "ki", "wi"]
    cols, parts, off = {}, [], 0

    def pad_to_tile():
        nonlocal off
        total = -(-off // tn) * tn
        parts.append(jnp.zeros((w_in.shape[0], total - off), w_in.dtype))
        off = total

    for name in dst_order:
        s, n = src[name]
        cols[name] = off
        parts.append(w_in[:, s:s + n])
        off += n
    cols["kw"] = cols["ki"]
    pad_to_tile()
    n_main = off // tn
    s, n = src["g"]
    parts.append(w_in[:, s:s + n])
    off += n
    pad_to_tile()
    return jnp.concatenate(parts, axis=1).astype(BF16), cols, tn, n_main


def _to_heads_T(x, b, t):
    return x.reshape(b, t, x.shape[1]).transpose(0, 2, 1)


def _pad_axis(x, axis, size):
    pad = [(0, 0)] * x.ndim
    pad[axis] = (0, size - x.shape[axis])
    return jnp.pad(x, pad)


def _layer(x, pos, prm, cache, layer, is_prompt):
    b, t, d = x.shape
    mix = d // 2
    xt = x.reshape(b * t, d)
    tm = _row_tile(b * t)
    cols = prm["cols"]
    p, gates = _proj(xt, prm["norm1"], prm["w_in"], tm, prm["w_in_tn"], prm["w_in_main"])

    tabs = _rope_tables(pos)
    tm_r = _row_tile(t) if is_prompt else tm
    if not is_prompt:
        tabs = jnp.tile(tabs, (1, b, 1))
    qa, ka, kab, vab, qi, kw, kib = _rope(p, tabs, cols, tm_r)
    if is_prompt:
        sq, s_valid, q_pos0 = t, t, 0
        s_pad = -(-s_valid // DSA_TK) * DSA_TK
        k_all = _pad_axis(kab.reshape(b, t, -1), 1, s_pad)
        ki_all = _pad_axis(kib.reshape(b, t, LANES), 1, s_pad)
        vT = _pad_axis(vab.reshape(b, t, -1), 1, s_pad).transpose(0, 2, 1).reshape(b, A_KV, HEAD_DIM, s_pad)
        ones = jnp.broadcast_to((jnp.arange(DSA_VROWS - HEAD_DIM) == 0).astype(BF16)[None, None, :, None],
                                (b, A_KV, DSA_VROWS - HEAD_DIM, s_pad))
        vT = jnp.concatenate([vT, ones], axis=2).reshape(b, A_KV * DSA_VROWS, s_pad)
    else:
        past = cache["a_k"].shape[2]
        sq, s_valid, q_pos0 = DSA_QB, past + t, past
        k_all, vT, ki_all = _cache_prep(cache["a_k"], cache["a_v"], cache["a_kidx"], layer, kab, vab, kib, b, t)
    qaT = _pad_axis(_to_heads_T(qa, b, t), 2, sq)
    qiT = _pad_axis(_to_heads_T(qi, b, t), 2, sq)
    wiT = _pad_axis(_to_heads_T(kw[:, IDX_DIM:IDX_DIM + IDX_HEADS], b, t), 2, sq)
    topk = min(TOPK_MAX, s_valid // 4)
    oaT = _dsa(qaT, qiT, wiT, k_all, vT, ki_all, s_valid=s_valid, q_pos0=q_pos0, topk=topk)
    oa = oaT[:, :, :t].transpose(0, 2, 1).reshape(b * t, mix)

    nh = mix // HEAD_DIM
    if is_prompt:
        n_tiles = BAND_CHUNKS * CHUNK // BAND_QB + 1
        bias = _band_bias(prm["rel_bias"], BAND_CHUNKS * CHUNK, BAND_QB, 0, n_tiles * BAND_QB)
        ob = _band_prompt(p, bias, cols, b, t)
    else:
        nbc = cache["b_k"].shape[2]
        past = cache["a_k"].shape[2]
        bias = _band_bias(prm["rel_bias"], past, t, past - nbc, nbc + t)
        ob = _band_sample(p, cache["b_k"], cache["b_v"], layer, bias, cols, b, t)

    g_ssm = mix // C_GROUP
    if is_prompt:
        h0_re = jnp.zeros((b, g_ssm, C_STATE), F32)
        h0_im = h0_re
    else:
        h0_re, h0_im = cache["c_re"][layer], cache["c_im"][layer]
    yc, hr, hi = _ssm_apply(p, cols["uc"], b, t, h0_re, h0_im, prm["ssm"], is_prompt)
    oc = _glu(yc, prm["w_glu"], prm["b_glu"], tm)

    merged = _merge(oa, ob, oc, gates, prm["w_branch"], tm, 1024)
    x1 = _mmres(merged, prm["w_out"], xt, _row_tile(b * t, 512), 2048)

    f = prm["w_conv"].shape[1]
    if is_prompt:
        state = jnp.zeros((b, CONV_W - 1, f), F32)
        ns, ts = 1, _row_tile(t)
    else:
        state = cache["ffn_conv"][layer].astype(F32)
        ns, ts = b, t
    act, tails = _ffn_in(x1, prm["norm2"], prm["w_ffn_in"], prm["w_conv"], prm["b_conv"], state, ns, ts, 512)
    x2 = _mmres(act, prm["w_down"], x1, tm, 512)
    if is_prompt:
        buf = tails.reshape(b, t // ts, CONV_W - 1, f)[:, -1]
    else:
        buf = tails[0]

    ka4 = ka.reshape(b, t, A_KV, HEAD_DIM)
    va4 = p[:, cols["va"]:cols["va"] + A_KV * HEAD_DIM].reshape(b, t, A_KV, HEAD_DIM)
    ki3 = kw[:, :IDX_DIM].reshape(b, t, IDX_DIM)
    nb = min(BAND_CHUNKS * CHUNK, t) if is_prompt else t
    pb = p.reshape(b, t, p.shape[1])[:, t - nb:]
    kb4 = pb[:, :, cols["kb"]:cols["kb"] + mix].reshape(b, nb, nh, HEAD_DIM)
    vb4 = pb[:, :, cols["vb"]:cols["vb"] + mix].reshape(b, nb, nh, HEAD_DIM)
    return x2.reshape(b, t, d), (ka4, va4, ki3, kb4, vb4, hr, hi, buf)


def kernel(x_prompt, x_sample, cache_a_k, cache_a_v, cache_a_kidx, cache_b_k, cache_b_v, state_c_re, state_c_im,
           state_ffn_conv, norm1_g, w_in, rel_bias, ssm_a_re, ssm_a_im, ssm_log_dt, ssm_b_re, ssm_b_im, ssm_c_re,
           ssm_c_im, ssm_d, w_glu, b_glu, w_branch, w_out, norm2_g, w_ffn_in, w_ffn_conv, b_ffn_conv, w_ffn_down,
           normf_g):
    depth = w_in.shape[0]
    d = x_prompt.shape[2]
    pos_p = jnp.arange(x_prompt.shape[1])
    pos_s = cache_a_k.shape[2] + jnp.arange(x_sample.shape[1])
    xp, xs = x_prompt, x_sample
    st_p, st_s = [], []
    caches = dict(a_k=cache_a_k, a_v=cache_a_v, a_kidx=cache_a_kidx, b_k=cache_b_k, b_v=cache_b_v,
                  c_re=state_c_re, c_im=state_c_im, ffn_conv=state_ffn_conv)
    for l in range(depth):
        w_in_l, cols, tn, n_main = _pack_w_in(w_in[l], d)
        prm = dict(norm1=norm1_g[l], w_in=w_in_l, cols=cols, w_in_tn=tn, w_in_main=n_main, rel_bias=rel_bias[l],
                   ssm=_ssm_weights(ssm_a_re[l], ssm_a_im[l], ssm_log_dt[l], ssm_b_re[l], ssm_b_im[l],
                                    ssm_c_re[l], ssm_c_im[l], ssm_d[l]),
                   w_glu=w_glu[l].astype(BF16), b_glu=b_glu[l], w_branch=w_branch[l].astype(BF16),
                   w_out=w_out[l].astype(BF16), norm2=norm2_g[l], w_ffn_in=w_ffn_in[l].astype(BF16),
                   w_conv=w_ffn_conv[l], b_conv=b_ffn_conv[l], w_down=w_ffn_down[l].astype(BF16))
        xp, sp = _layer(xp, pos_p, prm, None, l, True)
        xs, ss = _layer(xs, pos_s, prm, caches, l, False)
        st_p.append(sp)
        st_s.append(ss)
    bp, tp, _ = xp.shape
    bs, tsq, _ = xs.shape
    y_prompt = _norm(xp.reshape(bp * tp, d), normf_g, _row_tile(bp * tp)).reshape(bp, tp, d)
    y_sample = _norm(xs.reshape(bs * tsq, d), normf_g, _row_tile(bs * tsq)).reshape(bs, tsq, d)
    outs_p = [jnp.stack([s[i] for s in st_p]) for i in range(8)]
    outs_s = [jnp.stack([s[i] for s in st_s]) for i in range(8)]
    return (y_prompt, y_sample, *outs_p, *outs_s)
```
